```python
import jax, jax.numpy as jnp
from jax import lax
import numpy as np

D_MODEL = 1024
BATCH = 2
SEQ = 8192
DEPTH = 2

N_MIXERS = 2
MIX_WIDTH = D_MODEL
MEM_TOKENS = 256
MEM_HEADS = 4
MEM_WIDTH = MIX_WIDTH // 4
MEM_HEAD_DIM = MEM_WIDTH // MEM_HEADS
MAIN_WIDTH = MIX_WIDTH - MEM_WIDTH
ML_HEADS = 4
ML_HEAD_DIM = MAIN_WIDTH // ML_HEADS
ML_CHUNK = 64
CONV_WIDTH = 4
MLA_HEADS = 6
MLA_NOPE = 128
MLA_ROPE = 64
MLA_QK = MLA_NOPE + MLA_ROPE
MLA_V = MAIN_WIDTH // MLA_HEADS
Q_LORA = 384
KV_LORA = 128
ROPE_THETA = 10000.0
Q_BLOCK = 128
D_FF = 3584
N_EXPERTS = 8
TOP_K = 2
EPS = 1e-6
ML_IN = 4 * MAIN_WIDTH + 2 * ML_HEADS + MEM_WIDTH
MLA_IN = Q_LORA + KV_LORA + MLA_ROPE + MEM_WIDTH

kernel_name = "hybrid_mlstm_mla_memory_moe"


def split_cols(t, sizes):
    idx = np.cumsum(sizes)[:-1].tolist()
    return jnp.split(t, idx, axis=-1)


def rms_norm(x, g):
    xf = x.astype(jnp.float32)
    y = xf * lax.rsqrt(jnp.mean(xf * xf, axis=-1, keepdims=True) + EPS)
    return (y * g.astype(jnp.float32)).astype(x.dtype)


def swiglu(x, w_gate, w_up, w_down):
    return (jax.nn.silu(x @ w_gate) * (x @ w_up)) @ w_down


def causal_depthwise_conv(u, w):
    C = u.shape[-1]
    return lax.conv_general_dilated(
        u, w[:, None, :].astype(u.dtype), window_strides=(1,),
        padding=[(w.shape[0] - 1, 0)], dimension_numbers=('NWC', 'WIO', 'NWC'),
        feature_group_count=C)


def rope_tail(t, positions):
    half = MLA_ROPE // 2
    inv_freq = ROPE_THETA ** (-jnp.arange(half, dtype=jnp.float32) / half)
    ang = positions.astype(jnp.float32)[..., None] * inv_freq
    cos = jnp.cos(ang)[:, :, None, :]
    sin = jnp.sin(ang)[:, :, None, :]
    t_pass, t_rot = t[..., :-MLA_ROPE], t[..., -MLA_ROPE:]
    x1 = t_rot[..., :half].astype(jnp.float32)
    x2 = t_rot[..., half:].astype(jnp.float32)
    rot = jnp.concatenate([x1 * cos - x2 * sin, x2 * cos + x1 * sin], axis=-1).astype(t.dtype)
    return jnp.concatenate([t_pass, rot], axis=-1)


def memory_attention(mq, mem_k, mem_v, q_gain, k_gain):
    B, S, _ = mq.shape
    q = rms_norm(mq.reshape(B, S, MEM_HEADS, MEM_HEAD_DIM), q_gain)
    k = rms_norm(mem_k, k_gain)
    s = jnp.einsum('bshd,bmhd->bhsm', q, k).astype(jnp.float32) * MEM_HEAD_DIM ** -0.5
    p = jax.nn.softmax(s, axis=-1).astype(mem_v.dtype)
    return jnp.einsum('bhsm,bmhd->bshd', p, mem_v).reshape(B, S, MEM_WIDTH)


def mlstm_chunkwise(q, k, v, i_pre, log_f):
    B, S, H, Dh = q.shape
    L = ML_CHUNK
    NC = S // L
    f32 = jnp.float32

    def chunks(t):
        t = t.astype(f32).reshape((B, NC, L, H) + t.shape[3:])
        return jnp.moveaxis(t, (1, 3), (0, 2))

    xs = (chunks(q), chunks(k) * Dh ** -0.5, chunks(v), chunks(i_pre), chunks(log_f))
    causal = jnp.tril(jnp.ones((L, L), dtype=bool))

    def step(carry, inp):
        C, n, m = carry
        qc, kc, vc, ig, lf = inp
        b = jnp.cumsum(lf, axis=-1)
        log_d = jnp.where(causal, b[..., :, None] - b[..., None, :] + ig[..., None, :], -jnp.inf)
        inter = b + m[..., None]
        m_row = jnp.maximum(inter, jnp.max(log_d, axis=-1))
        d = jnp.exp(log_d - m_row[..., None])
        s_inter = jnp.exp(inter - m_row)
        qk = jnp.einsum('bhld,bhsd->bhls', qc, kc) * d
        num = jnp.einsum('bhls,bhsd->bhld', qk, vc) + s_inter[..., None] * jnp.einsum('bhvk,bhlk->bhlv', C, qc)
        den = jnp.sum(qk, axis=-1) + s_inter * jnp.einsum('bhk,bhlk->bhl', n, qc)
        h = num / jnp.maximum(jnp.abs(den), jnp.exp(-m_row))[..., None]
        b_last = b[..., -1]
        g = b_last[..., None] - b + ig
        m_new = jnp.maximum(b_last + m, jnp.max(g, axis=-1))
        w = jnp.exp(g - m_new[..., None])
        decay = jnp.exp(b_last + m - m_new)
        C_new = decay[..., None, None] * C + jnp.einsum('bhs,bhsv,bhsk->bhvk', w, vc, kc)
        n_new = decay[..., None] * n + jnp.einsum('bhs,bhsk->bhk', w, kc)
        return (C_new, n_new, m_new), h

    init = (jnp.zeros((B, H, Dh, Dh), f32), jnp.zeros((B, H, Dh), f32), jnp.zeros((B, H), f32))
    _, hs = lax.scan(step, init, xs)
    return jnp.moveaxis(hs, (0, 2), (1, 3)).reshape(B, S, H, Dh).astype(q.dtype)


def causal_block_attention(q, k, v):
    B, S, H, Dq = q.shape
    qh = q.transpose(0, 2, 1, 3)
    kh = k.transpose(0, 2, 1, 3)
    vh = v.transpose(0, 2, 1, 3)
    kpos = jnp.arange(S)
    scale = Dq ** -0.5

    def block(i):
        start = i * Q_BLOCK
        qb = lax.dynamic_slice_in_dim(qh, start, Q_BLOCK, axis=2)
        s = jnp.einsum('bhqd,bhkd->bhqk', qb, kh).astype(jnp.float32) * scale
        qpos = start + jnp.arange(Q_BLOCK)
        s = jnp.where(kpos[None, :] <= qpos[:, None], s, -jnp.inf)
        p = jax.nn.softmax(s, axis=-1).astype(vh.dtype)
        return jnp.einsum('bhqk,bhkd->bqhd', p, vh)

    out = lax.map(block, jnp.arange(S // Q_BLOCK))
    return jnp.moveaxis(out, 0, 1).reshape(B, S, H, v.shape[-1])


def mlstm_layer(xn, mem_k, mem_v, w_in, conv_w, b_i, b_f, h_norm, mq_norm, mk_norm, w_out):
    B, S, _ = xn.shape
    qk, v, o, ig, fg, mq = split_cols(xn @ w_in, [2 * MAIN_WIDTH, MAIN_WIDTH, MAIN_WIDTH, ML_HEADS, ML_HEADS, MEM_WIDTH])
    qk = jax.nn.silu(causal_depthwise_conv(qk, conv_w))
    q, k = jnp.split(qk, 2, axis=-1)
    heads = lambda t: t.reshape(B, S, ML_HEADS, ML_HEAD_DIM)
    i_pre = (ig + b_i).astype(jnp.float32)
    log_f = jax.nn.log_sigmoid((fg + b_f).astype(jnp.float32))
    h_tilde = mlstm_chunkwise(heads(q), heads(k), heads(v), i_pre, log_f)
    h = rms_norm(h_tilde, h_norm) * jax.nn.sigmoid(heads(o))
    y_mem = memory_attention(mq, mem_k, mem_v, mq_norm, mk_norm)
    return jnp.concatenate([h.reshape(B, S, MAIN_WIDTH), y_mem], axis=-1) @ w_out


def mla_layer(xn, positions, mem_k, mem_v, w_in, cq_norm, ckv_norm, w_uq, w_ukv, q_norm, k_norm, mq_norm, mk_norm, w_out):
    B, S, _ = xn.shape
    c_q, c_kv, k_r, mq = split_cols(xn @ w_in, [Q_LORA, KV_LORA, MLA_ROPE, MEM_WIDTH])
    q = (rms_norm(c_q, cq_norm) @ w_uq).reshape(B, S, MLA_HEADS, MLA_QK)
    kv = (rms_norm(c_kv, ckv_norm) @ w_ukv).reshape(B, S, MLA_HEADS, MLA_NOPE + MLA_V)
    k_nope, v = jnp.split(kv, [MLA_NOPE], axis=-1)
    k = jnp.concatenate([k_nope, jnp.broadcast_to(k_r[:, :, None, :], (B, S, MLA_HEADS, MLA_ROPE))], axis=-1)
    q = rope_tail(rms_norm(q, q_norm), positions)
    k = rope_tail(rms_norm(k, k_norm), positions)
    o = causal_block_attention(q, k, v)
    y_mem = memory_attention(mq, mem_k, mem_v, mq_norm, mk_norm)
    return jnp.concatenate([o.reshape(B, S, MAIN_WIDTH), y_mem], axis=-1) @ w_out


def moe_swiglu(xn, w_router, we_gate, we_up, we_down):
    logits = (xn @ w_router).astype(jnp.float32)
    top_v, top_i = lax.top_k(logits, TOP_K)
    top_w = jax.nn.softmax(top_v, axis=-1)
    gates = jnp.sum(jax.nn.one_hot(top_i, N_EXPERTS, dtype=jnp.float32) * top_w[..., None], axis=-2)
    y = jnp.zeros_like(xn)
    for e in range(N_EXPERTS):
        y = y + gates[..., e:e + 1].astype(xn.dtype) * swiglu(xn, we_gate[e], we_up[e], we_down[e])
    return y


def setup_inputs(seed: int = 0) -> dict:
    key = jax.random.key(seed)
    ks = iter(jax.random.split(key, 48))
    f32 = jnp.float32
    nrm = lambda shape, fan_in: jax.random.normal(next(ks), shape, f32) * fan_in ** -0.5
    gain = lambda n: 1.0 + 0.02 * jax.random.normal(next(ks), (n,), f32)
    x = jax.random.normal(next(ks), (BATCH, SEQ, D_MODEL), f32)
    mem = jax.random.normal(next(ks), (BATCH, MEM_TOKENS, D_MODEL), f32)
    offset = jax.random.randint(next(ks), (BATCH, 1), 0, 1024, dtype=jnp.int32)
    positions = offset + jnp.arange(SEQ, dtype=jnp.int32)[None, :]
    return {
        "x": x, "mem": mem, "positions": positions,
        "mem_norm": gain(D_MODEL), "w_mem_kv": nrm((D_MODEL, 2 * MEM_WIDTH), D_MODEL),
        "ln_mix0": gain(D_MODEL), "w_in0": nrm((D_MODEL, ML_IN), D_MODEL),
        "conv_w0": nrm((CONV_WIDTH, 2 * MAIN_WIDTH), CONV_WIDTH),
        "b_i0": 0.1 * jax.random.normal(next(ks), (ML_HEADS,), f32),
        "b_f0": 3.0 + 0.1 * jax.random.normal(next(ks), (ML_HEADS,), f32),
        "h_norm0": gain(ML_HEAD_DIM), "mq_norm0": gain(MEM_HEAD_DIM), "mk_norm0": gain(MEM_HEAD_DIM),
        "w_out0": nrm((MIX_WIDTH, D_MODEL), MIX_WIDTH),
        "ln_ffn0": gain(D_MODEL), "w_gate0": nrm((D_MODEL, D_FF), D_MODEL),
        "w_up0": nrm((D_MODEL, D_FF), D_MODEL), "w_down0": nrm((D_FF, D_MODEL), D_FF),
        "ln_mix1": gain(D_MODEL), "w_in1": nrm((D_MODEL, MLA_IN), D_MODEL),
        "cq_norm1": gain(Q_LORA), "ckv_norm1": gain(KV_LORA),
        "w_uq1": nrm((Q_LORA, MLA_HEADS * MLA_QK), Q_LORA),
        "w_ukv1": nrm((KV_LORA, MLA_HEADS * (MLA_NOPE + MLA_V)), KV_LORA),
        "q_norm1": gain(MLA_QK), "k_norm1": gain(MLA_QK),
        "mq_norm1": gain(MEM_HEAD_DIM), "mk_norm1": gain(MEM_HEAD_DIM),
        "w_out1": nrm((MIX_WIDTH, D_MODEL), MIX_WIDTH),
        "ln_ffn1": gain(D_MODEL), "w_router1": nrm((D_MODEL, N_EXPERTS), D_MODEL),
        "we_gate1": nrm((N_EXPERTS, D_MODEL, D_FF), D_MODEL),
        "we_up1": nrm((N_EXPERTS, D_MODEL, D_FF), D_MODEL),
        "we_down1": nrm((N_EXPERTS, D_FF, D_MODEL), D_FF),
    }


def reference(x, mem, positions, mem_norm, w_mem_kv,
              ln_mix0, w_in0, conv_w0, b_i0, b_f0, h_norm0, mq_norm0, mk_norm0, w_out0,
              ln_ffn0, w_gate0, w_up0, w_down0,
              ln_mix1, w_in1, cq_norm1, ckv_norm1, w_uq1, w_ukv1, q_norm1, k_norm1, mq_norm1, mk_norm1, w_out1,
              ln_ffn1, w_router1, we_gate1, we_up1, we_down1):
    B, M, _ = mem.shape
    mem_k, mem_v = jnp.split(rms_norm(mem, mem_norm) @ w_mem_kv, 2, axis=-1)
    mem_k = mem_k.reshape(B, M, MEM_HEADS, MEM_HEAD_DIM)
    mem_v = mem_v.reshape(B, M, MEM_HEADS, MEM_HEAD_DIM)
    h = x
    for layer in range(DEPTH):
        if layer % N_MIXERS == 0:
            h = h + mlstm_layer(rms_norm(h, ln_mix0), mem_k, mem_v, w_in0, conv_w0, b_i0, b_f0,
                                h_norm0, mq_norm0, mk_norm0, w_out0)
        else:
            h = h + mla_layer(rms_norm(h, ln_mix1), positions, mem_k, mem_v, w_in1, cq_norm1, ckv_norm1,
                              w_uq1, w_ukv1, q_norm1, k_norm1, mq_norm1, mk_norm1, w_out1)
        if layer % 2 == 0:
            h = h + swiglu(rms_norm(h, ln_ffn0), w_gate0, w_up0, w_down0)
        else:
            h = h + moe_swiglu(rms_norm(h, ln_ffn1), w_router1, we_gate1, we_up1, we_down1)
    return h
```

```python
import functools

import jax
import jax.numpy as jnp
import numpy as np
from jax import lax
from jax.experimental import pallas as pl
from jax.experimental.pallas import tpu as pltpu

F32 = jnp.float32
BF16 = jnp.bfloat16

D_MODEL = 1024
MEM_TOKENS = 256
MEM_HEADS = 4
MEM_WIDTH = 256
MEM_HEAD_DIM = 64
MAIN_WIDTH = 768
ML_HEADS = 4
ML_HEAD_DIM = 192
CONV_WIDTH = 4
MLA_HEADS = 6
MLA_NOPE = 128
MLA_ROPE = 64
MLA_QK = 192
MLA_V = 128
Q_LORA = 384
KV_LORA = 128
ROPE_THETA = 10000.0
D_FF = 3584
N_EXPERTS = 8
EPS = 1e-6

LANE = 128
HEAD_PAD = 256
ML_PAD = ML_HEADS * HEAD_PAD
MLA_PAD = MLA_HEADS * HEAD_PAD
GATE_PAD = 128
ML_CHUNK = 256
TM_PROJ = 256
TM_FFN = 512
FF_CHUNK = 512
TM_MOE = 1024
TQ = 1024
TK = 1024
NEG = -1e30
VMEM_LIMIT = 56 * 1024 * 1024


def _cparams(sem, vmem=VMEM_LIMIT):
    return pltpu.CompilerParams(dimension_semantics=sem, vmem_limit_bytes=vmem)


def _rms(x, g):
    return x * lax.rsqrt(jnp.mean(x * x, axis=-1, keepdims=True) + EPS) * g


def _split_dot(x, w_bf16):
    hi = x.astype(BF16)
    lo = (x - hi.astype(F32)).astype(BF16)
    return (jnp.dot(hi, w_bf16, preferred_element_type=F32)
            + jnp.dot(lo, w_bf16, preferred_element_type=F32))


def _block_diag_ones(n, blk):
    r = lax.broadcasted_iota(jnp.int32, (n, n), 0) // blk
    c = lax.broadcasted_iota(jnp.int32, (n, n), 1) // blk
    return jnp.where(r == c, 1.0, 0.0).astype(BF16)


def _const_spec(shape):
    nd = len(shape)
    return pl.BlockSpec(shape, lambda *_: (0,) * nd)


def _memkv_kernel(mem_ref, g_ref, w_ref, gk0_ref, gk1_ref, k0_ref, k1_ref, v_ref):
    xn = _rms(mem_ref[...], g_ref[...]).astype(BF16)
    kv = jnp.dot(xn, w_ref[...], preferred_element_type=F32)
    k = kv[:, :MEM_WIDTH]
    v = kv[:, MEM_WIDTH:]
    bd = _block_diag_ones(MEM_WIDTH, MEM_HEAD_DIM)
    ms = _split_dot(k * k, bd) * (1.0 / MEM_HEAD_DIM)
    kn = k * lax.rsqrt(ms + EPS)
    lane_head = lax.broadcasted_iota(jnp.int32, (MEM_TOKENS, MEM_WIDTH), 1) // MEM_HEAD_DIM
    for h in range(MEM_HEADS):
        sel = lane_head == h
        k0_ref[h] = jnp.where(sel, kn * gk0_ref[...], 0.0).astype(BF16)
        k1_ref[h] = jnp.where(sel, kn * gk1_ref[...], 0.0).astype(BF16)
        v_ref[h] = jnp.where(sel, v, 0.0).astype(BF16)


def _memkv(mem2d, mem_norm, w_kv, gk0, gk1, batch):
    out = jax.ShapeDtypeStruct((batch * MEM_HEADS, MEM_TOKENS, MEM_WIDTH), BF16)
    hspec = pl.BlockSpec((MEM_HEADS, MEM_TOKENS, MEM_WIDTH), lambda b: (b, 0, 0))
    return pl.pallas_call(
        _memkv_kernel,
        out_shape=(out, out, out),
        grid=(batch,),
        in_specs=[pl.BlockSpec((MEM_TOKENS, D_MODEL), lambda b: (b, 0)),
                  _const_spec((1, D_MODEL)), _const_spec((D_MODEL, 2 * MEM_WIDTH)),
                  _const_spec((1, MEM_WIDTH)), _const_spec((1, MEM_WIDTH))],
        out_specs=(hspec, hspec, hspec),
        compiler_params=_cparams(("parallel",)),
        name="memkv",
    )(mem2d, mem_norm, w_kv, gk0, gk1)


def _in0_kernel(tiles_per_seq, h_ref, hp_ref, g_ref, wqk_ref, wvo_ref, wmq_ref, wg_ref, wgt_ref,
                cw_ref, qs_ref, bias_ref, biast_ref,
                qk_ref, v_ref, so_ref, mq_ref, gates_ref, gatest_ref):
    i = pl.program_id(0)
    tm = h_ref.shape[0]
    g = g_ref[...]
    xn = _rms(h_ref[...], g).astype(BF16)
    first = (i % tiles_per_seq) == 0
    xpn = _rms(hp_ref[...], g).astype(BF16)
    row8 = lax.broadcasted_iota(jnp.int32, (8, FF_CHUNK), 0)
    ncol = wqk_ref.shape[1]
    for c in range(ncol // FF_CHUNK):
        cs = slice(c * FF_CHUNK, (c + 1) * FF_CHUNK)
        w = wqk_ref[:, cs]
        u = jnp.dot(xn, w, preferred_element_type=F32)
        up = jnp.dot(xpn, w, preferred_element_type=F32)
        up = jnp.where(first, 0.0, up)
        acc = u * cw_ref[CONV_WIDTH - 1:CONV_WIDTH, cs]
        for k in range(1, CONV_WIDTH):
            rolled = pltpu.roll(u, k, 0)
            prev = pltpu.roll(up, k, 0)
            head = jnp.where(row8 < k, prev, rolled[:8])
            shifted = jnp.concatenate([head, rolled[8:]], axis=0)
            acc = acc + shifted * cw_ref[CONV_WIDTH - 1 - k:CONV_WIDTH - k, cs]
        y = acc * jax.nn.sigmoid(acc) * qs_ref[:, cs]
        qk_ref[:, cs] = y.astype(BF16)
    zvo = jnp.dot(xn, wvo_ref[...], preferred_element_type=F32)
    v_ref[...] = zvo[:, :ML_PAD].astype(BF16)
    so_ref[...] = jax.nn.sigmoid(zvo[:, ML_PAD:]).astype(BF16)
    mq_ref[...] = jnp.dot(xn, wmq_ref[...], preferred_element_type=F32).astype(BF16)

    def gate_act(z, is_input_gate):
        logsig = jnp.minimum(z, 0.0) - jnp.log(1.0 + jnp.exp(-jnp.abs(z)))
        return jnp.where(is_input_gate, z, logsig)

    zg = jnp.dot(xn, wg_ref[...], preferred_element_type=F32) + bias_ref[...]
    lane = lax.broadcasted_iota(jnp.int32, (tm, GATE_PAD), 1)
    gates_ref[...] = gate_act(zg, lane < ML_HEADS)
    zgt = lax.dot_general(wgt_ref[...], xn, (((1,), (1,)), ((), ())),
                          preferred_element_type=F32) + biast_ref[...]
    row = lax.broadcasted_iota(jnp.int32, (8, tm), 0)
    gatest_ref[...] = gate_act(zgt, row < ML_HEADS)


def _in0(h2d, ln, wqk, wvo, wmq, wg, wgt, cw, qscale, bias, biast, seq):
    t = h2d.shape[0]
    tm = TM_PROJ
    tiles_per_seq = seq // tm
    tok = lambda n, dt: jax.ShapeDtypeStruct((t, n), dt)
    row = lambda n: pl.BlockSpec((tm, n), lambda i: (i, 0))
    return pl.pallas_call(
        functools.partial(_in0_kernel, tiles_per_seq),
        out_shape=(tok(2 * ML_PAD, BF16), tok(ML_PAD, BF16), tok(ML_PAD, BF16), tok(MEM_WIDTH, BF16),
                   tok(GATE_PAD, F32), jax.ShapeDtypeStruct((8, t), F32)),
        grid=(t // tm,),
        in_specs=[row(D_MODEL),
                  pl.BlockSpec((8, D_MODEL), lambda i: (jnp.maximum(i * (tm // 8) - 1, 0), 0)),
                  _const_spec((1, D_MODEL)), _const_spec(wqk.shape), _const_spec(wvo.shape),
                  _const_spec(wmq.shape), _const_spec(wg.shape), _const_spec(wgt.shape),
                  _const_spec(cw.shape), _const_spec(qscale.shape), _const_spec(bias.shape),
                  _const_spec(biast.shape)],
        out_specs=(row(2 * ML_PAD), row(ML_PAD), row(ML_PAD), row(MEM_WIDTH), row(GATE_PAD),
                   pl.BlockSpec((8, tm), lambda i: (0, i))),
        compiler_params=_cparams(("parallel",)),
        name="in_proj0",
    )(h2d, h2d, ln, wqk, wvo, wmq, wg, wgt, cw, qscale, bias, biast)


def _mlstm_kernel(qk_ref, v_ref, so_ref, gates_ref, gatest_ref, hg_ref, out_ref, c_ref, m_ref):
    L = qk_ref.shape[0]

    @pl.when(pl.program_id(1) == 0)
    def _():
        c_ref[...] = jnp.zeros_like(c_ref)
        m_ref[...] = jnp.zeros_like(m_ref)

    gts = gates_ref[...]
    gtt = gatest_ref[...]
    r = lax.broadcasted_iota(jnp.int32, (L, L), 0)
    c = lax.broadcasted_iota(jnp.int32, (L, L), 1)
    causal = r >= c
    tri_lo = jnp.where(causal, 1.0, 0.0).astype(BF16)
    tri_up = jnp.where(r <= c, 1.0, 0.0).astype(BF16)
    hi = gts.astype(BF16)
    lo = (gts - hi.astype(F32)).astype(BF16)
    b_cols = (jnp.dot(tri_lo, hi, preferred_element_type=F32)
              + jnp.dot(tri_lo, lo, preferred_element_type=F32))
    b_rows = _split_dot(gtt, tri_up)
    lane = lax.broadcasted_iota(jnp.int32, (L, HEAD_PAD), 1)
    hg = hg_ref[...]
    for h in range(ML_HEADS):
        m_h = m_ref[h][0:1, 0:1]
        b_c = b_cols[:, ML_HEADS + h:ML_HEADS + h + 1]
        ig_c = gts[:, h:h + 1]
        b_r = b_rows[ML_HEADS + h:ML_HEADS + h + 1, :]
        ig_r = gtt[h:h + 1, :]
        c_r = b_r - ig_r
        log_d = jnp.where(causal, b_c - c_r, NEG)
        inter = b_c + m_h
        m_row = jnp.maximum(inter, jnp.max(log_d, axis=-1, keepdims=True))
        d = jnp.exp(log_d - m_row)
        s_inter = jnp.exp(inter - m_row)
        q_h = qk_ref[:, h * HEAD_PAD:(h + 1) * HEAD_PAD]
        k_h = qk_ref[:, ML_PAD + h * HEAD_PAD:ML_PAD + (h + 1) * HEAD_PAD]
        v_aug = jnp.where(lane == ML_HEAD_DIM, 1.0, v_ref[:, h * HEAD_PAD:(h + 1) * HEAD_PAD].astype(F32)).astype(BF16)
        s = lax.dot_general(q_h, k_h, (((1,), (1,)), ((), ())), preferred_element_type=F32)
        qkd = (s * d).astype(BF16)
        c_old = c_ref[h]
        num = (jnp.dot(qkd, v_aug, preferred_element_type=F32)
               + s_inter * jnp.dot(q_h, c_old.astype(BF16), preferred_element_type=F32))
        den = jnp.sum(jnp.where(lane == ML_HEAD_DIM, num, 0.0), axis=-1, keepdims=True)
        ht = num * (1.0 / jnp.maximum(jnp.abs(den), jnp.exp(-m_row)))
        ht = jnp.where(lane < ML_HEAD_DIM, ht, 0.0)
        ms = jnp.sum(ht * ht, axis=-1, keepdims=True) * (1.0 / ML_HEAD_DIM)
        hn = ht * lax.rsqrt(ms + EPS) * hg * so_ref[:, h * HEAD_PAD:(h + 1) * HEAD_PAD].astype(F32)
        out_ref[:, h * HEAD_PAD:(h + 1) * HEAD_PAD] = hn.astype(BF16)
        b_last = b_c[L - 1:L, :]
        g_r = b_last - c_r
        m_new = jnp.maximum(b_last + m_h, jnp.max(g_r, axis=-1, keepdims=True))
        w_c = jnp.exp(b_last - b_c + ig_c - m_new)
        decay = jnp.exp(b_last + m_h - m_new)
        kw = (k_h.astype(F32) * w_c).astype(BF16)
        upd = lax.dot_general(kw, v_aug, (((0,), (0,)), ((), ())), preferred_element_type=F32)
        c_ref[h] = decay * c_old + upd
        m_ref[h] = jnp.broadcast_to(m_new, m_ref.shape[1:])


def _mlstm(qk, v, so, gates, gatest, hgain, batch, seq):
    t = qk.shape[0]
    L = ML_CHUNK
    nc = seq // L
    row = lambda n: pl.BlockSpec((L, n), lambda b, c: (b * nc + c, 0))
    return pl.pallas_call(
        _mlstm_kernel,
        out_shape=jax.ShapeDtypeStruct((t, ML_PAD), BF16),
        grid=(batch, nc),
        in_specs=[row(2 * ML_PAD), row(ML_PAD), row(ML_PAD), row(GATE_PAD),
                  pl.BlockSpec((8, L), lambda b, c: (0, b * nc + c)),
                  _const_spec((1, HEAD_PAD))],
        out_specs=row(ML_PAD),
        scratch_shapes=[pltpu.VMEM((ML_HEADS, HEAD_PAD, HEAD_PAD), F32),
                        pltpu.VMEM((ML_HEADS, 8, LANE), F32)],
        compiler_params=_cparams(("parallel", "arbitrary")),
        name="mlstm",
    )(qk, v, so, gates, gatest, hgain)


def _out_kernel(main_ref, mq_ref, h_ref, k4_ref, v4_ref, gq_ref, wm_ref, wmem_ref, out_ref):
    q = mq_ref[...].astype(F32)
    bd = _block_diag_ones(MEM_WIDTH, MEM_HEAD_DIM)
    ms = _split_dot(q * q, bd) * (1.0 / MEM_HEAD_DIM)
    qn = (q * lax.rsqrt(ms + EPS) * gq_ref[...] * (MEM_HEAD_DIM ** -0.5)).astype(BF16)
    ymem = jnp.zeros(q.shape, F32)
    for h in range(MEM_HEADS):
        s = lax.dot_general(qn, k4_ref[h], (((1,), (1,)), ((), ())), preferred_element_type=F32)
        e = jnp.exp(s - jnp.max(s, axis=-1, keepdims=True))
        inv = 1.0 / jnp.sum(e, axis=-1, keepdims=True)
        ymem = ymem + jnp.dot(e.astype(BF16), v4_ref[h], preferred_element_type=F32) * inv
    y = (jnp.dot(main_ref[...], wm_ref[...], preferred_element_type=F32)
         + jnp.dot(ymem.astype(BF16), wmem_ref[...], preferred_element_type=F32))
    out_ref[...] = h_ref[...] + y


def _out_proj(main, mq, h2d, k4, v4, gq, wm, wmem, seq):
    t = h2d.shape[0]
    tm = TM_PROJ
    tiles_per_seq = seq // tm
    row = lambda n: pl.BlockSpec((tm, n), lambda i: (i, 0))
    memspec = pl.BlockSpec((MEM_HEADS, MEM_TOKENS, MEM_WIDTH), lambda i: (i // tiles_per_seq, 0, 0))
    return pl.pallas_call(
        _out_kernel,
        out_shape=jax.ShapeDtypeStruct((t, D_MODEL), F32),
        grid=(t // tm,),
        in_specs=[row(main.shape[1]), row(MEM_WIDTH), row(D_MODEL), memspec, memspec,
                  _const_spec((1, MEM_WIDTH)), _const_spec(wm.shape), _const_spec(wmem.shape)],
        out_specs=row(D_MODEL),
        compiler_params=_cparams(("parallel",)),
        name="out_proj",
    )(main, mq, h2d, k4, v4, gq, wm, wmem)


def _ffn_kernel(h_ref, g_ref, wg_ref, wu_ref, wd_ref, out_ref):
    x = h_ref[...]
    xn = _rms(x, g_ref[...]).astype(BF16)
    acc = x
    for c in range(D_FF // FF_CHUNK):
        cs = slice(c * FF_CHUNK, (c + 1) * FF_CHUNK)
        gt = jnp.dot(xn, wg_ref[:, cs], preferred_element_type=F32)
        up = jnp.dot(xn, wu_ref[:, cs], preferred_element_type=F32)
        a = (gt * jax.nn.sigmoid(gt) * up).astype(BF16)
        acc = acc + jnp.dot(a, wd_ref[cs, :], preferred_element_type=F32)
    out_ref[...] = acc


def _ffn(h2d, ln, wg, wu, wd):
    t = h2d.shape[0]
    tm = TM_FFN
    row = pl.BlockSpec((tm, D_MODEL), lambda i: (i, 0))
    res = lambda shape: pl.BlockSpec(shape, lambda i: (0, 0), pipeline_mode=pl.Buffered(1))
    return pl.pallas_call(
        _ffn_kernel,
        out_shape=jax.ShapeDtypeStruct((t, D_MODEL), F32),
        grid=(t // tm,),
        in_specs=[row, _const_spec((1, D_MODEL)), res(wg.shape), res(wu.shape), res(wd.shape)],
        out_specs=row,
        compiler_params=_cparams(("parallel",)),
        name="ffn_dense",
    )(h2d, ln, wg, wu, wd)


def _in1_kernel(h_ref, pos_ref, g_ref, w1_ref, gcq_ref, gckv_ref, wuq_ref, wuk_ref, wuv_ref,
                gq_ref, gkn_ref, gkr_ref, invf_ref,
                q_ref, k_ref, v_ref, mq_ref):
    tm = h_ref.shape[0]
    xn = _rms(h_ref[...], g_ref[...]).astype(BF16)
    z = jnp.dot(xn, w1_ref[...], preferred_element_type=F32)
    cq = z[:, :Q_LORA]
    ckv = z[:, Q_LORA:Q_LORA + KV_LORA]
    kr = z[:, Q_LORA + KV_LORA:Q_LORA + KV_LORA + LANE]
    mq_ref[...] = z[:, Q_LORA + KV_LORA + LANE:].astype(BF16)
    cqn = _rms(cq, gcq_ref[...]).astype(BF16)
    ckvn = _rms(ckv, gckv_ref[...]).astype(BF16)
    q = jnp.dot(cqn, wuq_ref[...], preferred_element_type=F32)
    kn = jnp.dot(ckvn, wuk_ref[...], preferred_element_type=F32)
    v_ref[...] = jnp.dot(ckvn, wuv_ref[...], preferred_element_type=F32).astype(BF16)

    half = MLA_ROPE // 2
    ang = pos_ref[...].astype(F32) * invf_ref[...]
    lane = lax.broadcasted_iota(jnp.int32, (tm, LANE), 1)
    cos = jnp.where(lane < MLA_ROPE, jnp.cos(ang), 0.0)
    sin = jnp.sin(ang)
    sin_hi = jnp.where((lane >= half) & (lane < MLA_ROPE), sin, 0.0)
    sin_lo = jnp.where(lane < half, -sin, 0.0)

    def rope(t):
        return t * cos + pltpu.roll(t, half, 1) * sin_hi + pltpu.roll(t, LANE - half, 1) * sin_lo

    scale = MLA_QK ** -0.5
    gq = gq_ref[...]
    for h in range(MLA_HEADS):
        qh = q[:, h * HEAD_PAD:(h + 1) * HEAD_PAD]
        rs = lax.rsqrt(jnp.sum(qh * qh, axis=-1, keepdims=True) * (1.0 / MLA_QK) + EPS) * scale
        qn = qh * rs * gq
        q_ref[:, h * HEAD_PAD:h * HEAD_PAD + LANE] = qn[:, :LANE].astype(BF16)
        q_ref[:, h * HEAD_PAD + LANE:(h + 1) * HEAD_PAD] = rope(qn[:, LANE:]).astype(BF16)
    ss_r = jnp.sum(kr * kr, axis=-1, keepdims=True)
    krr = rope(kr * gkr_ref[...])
    gkn = gkn_ref[...]
    for h in range(MLA_HEADS):
        kh = kn[:, h * MLA_NOPE:(h + 1) * MLA_NOPE]
        rs = lax.rsqrt((jnp.sum(kh * kh, axis=-1, keepdims=True) + ss_r) * (1.0 / MLA_QK) + EPS)
        k_ref[:, h * HEAD_PAD:h * HEAD_PAD + LANE] = (kh * rs * gkn).astype(BF16)
        k_ref[:, h * HEAD_PAD + LANE:(h + 1) * HEAD_PAD] = (krr * rs).astype(BF16)


def _in1(h2d, pos2d, ln, w1, gcq, gckv, wuq, wuk, wuv, gq, gkn, gkr, invf):
    t = h2d.shape[0]
    tm = TM_PROJ
    tok = lambda n: jax.ShapeDtypeStruct((t, n), BF16)
    row = lambda n: pl.BlockSpec((tm, n), lambda i: (i, 0))
    consts = [ln, w1, gcq, gckv, wuq, wuk, wuv, gq, gkn, gkr, invf]
    return pl.pallas_call(
        _in1_kernel,
        out_shape=(tok(MLA_PAD), tok(MLA_PAD), tok(MAIN_WIDTH), tok(MEM_WIDTH)),
        grid=(t // tm,),
        in_specs=[row(D_MODEL), row(1)] + [_const_spec(a.shape) for a in consts],
        out_specs=(row(MLA_PAD), row(MLA_PAD), row(MAIN_WIDTH), row(MEM_WIDTH)),
        compiler_params=_cparams(("parallel",)),
        name="in_proj1",
    )(h2d, pos2d, *consts)


def _attn_kernel(qi_ref, kj_ref, q_ref, k_ref, v_ref, o_ref, m_ref, l_ref, acc_ref):
    p = pl.program_id(2)
    qi = qi_ref[p]
    kj = kj_ref[p]

    @pl.when(kj == 0)
    def _():
        m_ref[...] = jnp.full_like(m_ref, NEG)
        l_ref[...] = jnp.zeros_like(l_ref)
        acc_ref[...] = jnp.zeros_like(acc_ref)

    s = lax.dot_general(q_ref[...], k_ref[...], (((1,), (1,)), ((), ())), preferred_element_type=F32)

    def update(s):
        m_old = m_ref[...]
        m_new = jnp.maximum(m_old, jnp.max(s, axis=-1, keepdims=True))
        alpha = jnp.exp(m_old - m_new)
        e = jnp.exp(s - m_new)
        l_ref[...] = alpha * l_ref[...] + jnp.sum(e, axis=-1, keepdims=True)
        acc_ref[...] = alpha * acc_ref[...] + jnp.dot(e.astype(BF16), v_ref[...], preferred_element_type=F32)
        m_ref[...] = m_new

    last = (kj + 1) * TK >= (qi + 1) * TQ
    crosses_diagonal = (kj + 1) * TK - 1 > qi * TQ

    @pl.when(jnp.logical_not(crosses_diagonal))
    def _():
        update(s)

    @pl.when(crosses_diagonal)
    def _():
        rows = qi * TQ + lax.broadcasted_iota(jnp.int32, s.shape, 0)
        cols = kj * TK + lax.broadcasted_iota(jnp.int32, s.shape, 1)
        update(jnp.where(cols <= rows, s, NEG))

    @pl.when(last)
    def _():
        o_ref[...] = (acc_ref[...] / l_ref[...]).astype(o_ref.dtype)


def _attention(q, k, v, batch, seq):
    t = q.shape[0]
    nq = seq // TQ
    nk_per_q = TQ // TK
    qi_l, kj_l = [], []
    for i in range(nq):
        for j in range((i + 1) * nk_per_q):
            qi_l.append(i)
            kj_l.append(j)
    qi = jnp.asarray(np.array(qi_l, np.int32))
    kj = jnp.asarray(np.array(kj_l, np.int32))
    npairs = len(qi_l)
    nkb = seq // TK
    grid_spec = pltpu.PrefetchScalarGridSpec(
        num_scalar_prefetch=2,
        grid=(batch, MLA_HEADS, npairs),
        in_specs=[pl.BlockSpec((TQ, HEAD_PAD), lambda b, h, p, qi, kj: (b * nq + qi[p], h)),
                  pl.BlockSpec((TK, HEAD_PAD), lambda b, h, p, qi, kj: (b * nkb + kj[p], h)),
                  pl.BlockSpec((TK, MLA_V), lambda b, h, p, qi, kj: (b * nkb + kj[p], h))],
        out_specs=pl.BlockSpec((TQ, MLA_V), lambda b, h, p, qi, kj: (b * nq + qi[p], h)),
        scratch_shapes=[pltpu.VMEM((TQ, 1), F32), pltpu.VMEM((TQ, 1), F32), pltpu.VMEM((TQ, MLA_V), F32)],
    )
    return pl.pallas_call(
        _attn_kernel,
        out_shape=jax.ShapeDtypeStruct((t, MAIN_WIDTH), BF16),
        grid_spec=grid_spec,
        compiler_params=_cparams(("parallel", "parallel", "arbitrary")),
        name="mla_attention",
    )(qi, kj, q, k, v)


def _router_kernel(h_ref, g_ref, whi_ref, wlo_ref, xn_ref, gates_ref):
    xn = _rms(h_ref[...], g_ref[...])
    hi = xn.astype(BF16)
    lo = (xn - hi.astype(F32)).astype(BF16)
    xn_ref[...] = hi
    whi = whi_ref[...]
    logits = (jnp.dot(hi, whi, preferred_element_type=F32) + jnp.dot(lo, whi, preferred_element_type=F32)
              + jnp.dot(hi, wlo_ref[...], preferred_element_type=F32))
    lane = lax.broadcasted_iota(jnp.int32, logits.shape, 1).astype(F32)
    logits = jnp.where(lane < N_EXPERTS, logits, NEG)
    m1 = jnp.max(logits, axis=-1, keepdims=True)
    i1 = jnp.min(jnp.where(logits == m1, lane, float(LANE)), axis=-1, keepdims=True)
    rest = jnp.where(lane == i1, NEG, logits)
    m2 = jnp.max(rest, axis=-1, keepdims=True)
    i2 = jnp.min(jnp.where(rest == m2, lane, float(LANE)), axis=-1, keepdims=True)
    e2 = jnp.exp(m2 - m1)
    w1 = 1.0 / (1.0 + e2)
    w2 = e2 * w1
    gates_ref[...] = jnp.where(lane == i1, w1, jnp.where(lane == i2, w2, 0.0))


def _router(h2d, ln, whi, wlo):
    t = h2d.shape[0]
    tm = TM_PROJ
    row = lambda n: pl.BlockSpec((tm, n), lambda i: (i, 0))
    return pl.pallas_call(
        _router_kernel,
        out_shape=(jax.ShapeDtypeStruct((t, D_MODEL), BF16), jax.ShapeDtypeStruct((t, LANE), F32)),
        grid=(t // tm,),
        in_specs=[row(D_MODEL), _const_spec((1, D_MODEL)), _const_spec(whi.shape), _const_spec(wlo.shape)],
        out_specs=(row(D_MODEL), row(LANE)),
        compiler_params=_cparams(("parallel",)),
        name="moe_router",
    )(h2d, ln, whi, wlo)


def _expert_kernel(te_ref, ts_ref, x_ref, wg_ref, wu_ref, wd_ref, out_ref, acc_ref):
    c = pl.program_id(1)
    x = x_ref[...]
    gt = jnp.dot(x, wg_ref[0], preferred_element_type=F32)
    up = jnp.dot(x, wu_ref[0], preferred_element_type=F32)
    a = (gt * jax.nn.sigmoid(gt) * up).astype(BF16)
    y = jnp.dot(a, wd_ref[0], preferred_element_type=F32)

    @pl.when(c == 0)
    def _():
        acc_ref[...] = y

    @pl.when(c > 0)
    def _():
        acc_ref[...] += y

    @pl.when(c == pl.num_programs(1) - 1)
    def _():
        out_ref[...] = acc_ref[...].astype(out_ref.dtype)


def _experts(x, tile_expert, tile_src, wg, wu, wd):
    n_tiles = tile_expert.shape[0]
    tm = TM_MOE
    grid_spec = pltpu.PrefetchScalarGridSpec(
        num_scalar_prefetch=2,
        grid=(n_tiles, D_FF // FF_CHUNK),
        in_specs=[pl.BlockSpec((tm, D_MODEL), lambda j, c, te, ts: (ts[j], 0)),
                  pl.BlockSpec((1, D_MODEL, FF_CHUNK), lambda j, c, te, ts: (te[j], 0, c)),
                  pl.BlockSpec((1, D_MODEL, FF_CHUNK), lambda j, c, te, ts: (te[j], 0, c)),
                  pl.BlockSpec((1, FF_CHUNK, D_MODEL), lambda j, c, te, ts: (te[j], c, 0))],
        out_specs=pl.BlockSpec((tm, D_MODEL), lambda j, c, te, ts: (j, 0)),
        scratch_shapes=[pltpu.VMEM((tm, D_MODEL), F32)],
    )
    return pl.pallas_call(
        _expert_kernel,
        out_shape=jax.ShapeDtypeStruct((n_tiles * tm, D_MODEL), BF16),
        grid_spec=grid_spec,
        compiler_params=_cparams(("parallel", "arbitrary")),
        name="moe_experts",
    )(tile_expert, tile_src, x, wg, wu, wd)


def _combine_kernel(h_ref, gates_ref, y_ref, out_ref):
    e = pl.program_id(1)
    lane = lax.broadcasted_iota(jnp.int32, gates_ref.shape, 1)
    w = jnp.sum(jnp.where(lane == e, gates_ref[...], 0.0), axis=-1, keepdims=True)
    contrib = w * y_ref[0].astype(F32)

    @pl.when(e == 0)
    def _():
        out_ref[...] = h_ref[...] + contrib

    @pl.when(e > 0)
    def _():
        out_ref[...] += contrib


def _combine_dense(h2d, gates, y_all):
    t = h2d.shape[0]
    tm = TM_FFN
    return pl.pallas_call(
        _combine_kernel,
        out_shape=jax.ShapeDtypeStruct((t, D_MODEL), F32),
        grid=(t // tm, N_EXPERTS),
        in_specs=[pl.BlockSpec((tm, D_MODEL), lambda i, e: (i, 0)),
                  pl.BlockSpec((tm, LANE), lambda i, e: (i, 0)),
                  pl.BlockSpec((1, tm, D_MODEL), lambda i, e: (e, i, 0))],
        out_specs=pl.BlockSpec((tm, D_MODEL), lambda i, e: (i, 0)),
        compiler_params=_cparams(("parallel", "arbitrary")),
        name="moe_combine",
    )(h2d, gates, y_all)


def _pad_heads_cols(w, heads, dim, pad):
    lead = w.shape[:-1]
    w = w.reshape(lead + (heads, dim))
    w = jnp.pad(w, [(0, 0)] * len(lead) + [(0, 0), (0, pad - dim)])
    return w.reshape(lead + (heads * pad,))


def _row(v):
    return v.reshape(1, -1).astype(F32)


def kernel(x, mem, positions, mem_norm, w_mem_kv, ln_mix0, w_in0, conv_w0, b_i0, b_f0, h_norm0, mq_norm0, mk_norm0, w_out0, ln_ffn0, w_gate0, w_up0, w_down0, ln_mix1, w_in1, cq_norm1, ckv_norm1, w_uq1, w_ukv1, q_norm1, k_norm1, mq_norm1, mk_norm1, w_out1, ln_ffn1, w_router1, we_gate1, we_up1, we_down1):
    batch, seq, _ = x.shape
    t = batch * seq
    h = x.reshape(t, D_MODEL)
    tile4 = lambda g: _row(jnp.tile(g, MEM_HEADS))

    k4_0, k4_1, v4 = _memkv(mem.reshape(batch * MEM_TOKENS, D_MODEL), _row(mem_norm), w_mem_kv.astype(BF16),
                            tile4(mk_norm0), tile4(mk_norm1), batch)

    mw = MAIN_WIDTH
    w_q, w_k, w_v, w_o = (w_in0[:, i * mw:(i + 1) * mw] for i in range(4))
    w_gates = w_in0[:, 4 * mw:4 * mw + 2 * ML_HEADS]
    w_mq = w_in0[:, 4 * mw + 2 * ML_HEADS:]
    padh = lambda w: _pad_heads_cols(w, ML_HEADS, ML_HEAD_DIM, HEAD_PAD)
    wqk = jnp.concatenate([padh(w_q), padh(w_k)], axis=1).astype(BF16)
    wvo = jnp.concatenate([padh(w_v), padh(w_o)], axis=1).astype(BF16)
    wg = jnp.pad(w_gates, ((0, 0), (0, GATE_PAD - 2 * ML_HEADS))).astype(BF16)
    wgt = w_gates.T.astype(BF16)
    cw = jnp.concatenate([padh(conv_w0[:, :mw]), padh(conv_w0[:, mw:])], axis=1).astype(F32)
    qscale = jnp.concatenate([jnp.full((1, ML_PAD), ML_HEAD_DIM ** -0.5, F32), jnp.ones((1, ML_PAD), F32)], axis=1)
    bias = jnp.concatenate([b_i0, b_f0]).astype(F32)
    bias_row = jnp.pad(bias, (0, GATE_PAD - 2 * ML_HEADS)).reshape(1, GATE_PAD)
    bias_col = bias.reshape(2 * ML_HEADS, 1)
    qk, v0, so, mq0, gates, gatest = _in0(h, _row(ln_mix0), wqk, wvo, w_mq.astype(BF16), wg, wgt, cw, qscale,
                                          bias_row, bias_col, seq)
    hgain = jnp.pad(h_norm0.astype(F32), (0, HEAD_PAD - ML_HEAD_DIM)).reshape(1, HEAD_PAD)
    hn = _mlstm(qk, v0, so, gates, gatest, hgain, batch, seq)
    wm0 = jnp.pad(w_out0[:mw].reshape(ML_HEADS, ML_HEAD_DIM, D_MODEL),
                  ((0, 0), (0, HEAD_PAD - ML_HEAD_DIM), (0, 0))).reshape(ML_PAD, D_MODEL).astype(BF16)
    h = _out_proj(hn, mq0, h, k4_0, v4, tile4(mq_norm0), wm0, w_out0[mw:].astype(BF16), seq)
    h = _ffn(h, _row(ln_ffn0), w_gate0.astype(BF16), w_up0.astype(BF16), w_down0.astype(BF16))

    w_cq = w_in1[:, :Q_LORA]
    w_ckv = w_in1[:, Q_LORA:Q_LORA + KV_LORA]
    w_kr = jnp.pad(w_in1[:, Q_LORA + KV_LORA:Q_LORA + KV_LORA + MLA_ROPE], ((0, 0), (0, LANE - MLA_ROPE)))
    w_mq1 = w_in1[:, Q_LORA + KV_LORA + MLA_ROPE:]
    w1 = jnp.concatenate([w_cq, w_ckv, w_kr, w_mq1], axis=1).astype(BF16)
    wuq = _pad_heads_cols(w_uq1, MLA_HEADS, MLA_QK, HEAD_PAD).astype(BF16)
    wukv = w_ukv1.reshape(KV_LORA, MLA_HEADS, MLA_NOPE + MLA_V)
    wuk = wukv[:, :, :MLA_NOPE].reshape(KV_LORA, MLA_HEADS * MLA_NOPE).astype(BF16)
    wuv = wukv[:, :, MLA_NOPE:].reshape(KV_LORA, MLA_HEADS * MLA_V).astype(BF16)
    gq = jnp.pad(q_norm1.astype(F32), (0, HEAD_PAD - MLA_QK)).reshape(1, HEAD_PAD)
    gkn = _row(k_norm1[:MLA_NOPE])
    gkr = jnp.pad(k_norm1[MLA_NOPE:].astype(F32), (0, LANE - MLA_ROPE)).reshape(1, LANE)
    half = MLA_ROPE // 2
    inv_freq = ROPE_THETA ** (-jnp.arange(half, dtype=F32) / half)
    invf = jnp.concatenate([inv_freq, inv_freq, jnp.zeros((LANE - MLA_ROPE,), F32)]).reshape(1, LANE)
    q1, k1, v1, mq1 = _in1(h, positions.reshape(t, 1).astype(jnp.int32), _row(ln_mix1), w1, _row(cq_norm1),
                           _row(ckv_norm1), wuq, wuk, wuv, gq, gkn, gkr, invf)
    o1 = _attention(q1, k1, v1, batch, seq)
    h = _out_proj(o1, mq1, h, k4_1, v4, tile4(mq_norm1), w_out1[:mw].astype(BF16), w_out1[mw:].astype(BF16), seq)

    wr = jnp.pad(w_router1.astype(F32), ((0, 0), (0, LANE - N_EXPERTS)))
    wr_hi = wr.astype(BF16)
    wr_lo = (wr - wr_hi.astype(F32)).astype(BF16)
    xn, gate_w = _router(h, _row(ln_ffn1), wr_hi, wr_lo)
    tiles_per_expert = t // TM_MOE
    tile_expert = jnp.asarray(np.repeat(np.arange(N_EXPERTS, dtype=np.int32), tiles_per_expert))
    tile_src = jnp.asarray(np.tile(np.arange(tiles_per_expert, dtype=np.int32), N_EXPERTS))
    y_all = _experts(xn, tile_expert, tile_src, we_gate1.astype(BF16), we_up1.astype(BF16), we_down1.astype(BF16))
    h = _combine_dense(h, gate_w, y_all.reshape(N_EXPERTS, t, D_MODEL))
    return h.reshape(batch, seq, D_MODEL)
```

```python
import functools

import jax
import jax.numpy as jnp
import numpy as np
from jax import lax
from jax.experimental import pallas as pl
from jax.experimental.pallas import tpu as pltpu

F32 = jnp.float32
BF16 = jnp.bfloat16

D_MODEL = 1024
MEM_TOKENS = 256
MEM_HEADS = 4
MEM_WIDTH = 256
MEM_HEAD_DIM = 64
MAIN_WIDTH = 768
ML_HEADS = 4
ML_HEAD_DIM = 192
CONV_WIDTH = 4
MLA_HEADS = 6
MLA_NOPE = 128
MLA_ROPE = 64
MLA_QK = 192
MLA_V = 128
Q_LORA = 384
KV_LORA = 128
ROPE_THETA = 10000.0
D_FF = 3584
N_EXPERTS = 8
EPS = 1e-6

LANE = 128
HEAD_PAD = 256
ML_PAD = ML_HEADS * HEAD_PAD
MLA_PAD = MLA_HEADS * HEAD_PAD
GATE_PAD = 128
ML_CHUNK = 256
TM_PROJ = 256
TM_FFN = 512
FF_CHUNK = 512
TM_MOE = 512
SLOT_GROUP = 256
SLOT_ALIGN = 16
COMBINE_WIN = 256
TQ = 1024
TK = 1024
NEG = -1e30
VMEM_LIMIT = 56 * 1024 * 1024


def _cparams(sem, vmem=VMEM_LIMIT):
    return pltpu.CompilerParams(dimension_semantics=sem, vmem_limit_bytes=vmem)


def _rms(x, g):
    return x * lax.rsqrt(jnp.mean(x * x, axis=-1, keepdims=True) + EPS) * g


def _split_dot(x, w_bf16):
    hi = x.astype(BF16)
    lo = (x - hi.astype(F32)).astype(BF16)
    return (jnp.dot(hi, w_bf16, preferred_element_type=F32)
            + jnp.dot(lo, w_bf16, preferred_element_type=F32))


def _block_diag_ones(n, blk):
    r = lax.broadcasted_iota(jnp.int32, (n, n), 0) // blk
    c = lax.broadcasted_iota(jnp.int32, (n, n), 1) // blk
    return jnp.where(r == c, 1.0, 0.0).astype(BF16)


def _const_spec(shape):
    nd = len(shape)
    return pl.BlockSpec(shape, lambda *_: (0,) * nd)


def _memkv_kernel(mem_ref, g_ref, w_ref, gk0_ref, gk1_ref, k0_ref, k1_ref, v_ref):
    xn = _rms(mem_ref[...], g_ref[...]).astype(BF16)
    kv = jnp.dot(xn, w_ref[...], preferred_element_type=F32)
    k = kv[:, :MEM_WIDTH]
    v = kv[:, MEM_WIDTH:]
    bd = _block_diag_ones(MEM_WIDTH, MEM_HEAD_DIM)
    ms = _split_dot(k * k, bd) * (1.0 / MEM_HEAD_DIM)
    kn = k * lax.rsqrt(ms + EPS)
    lane_head = lax.broadcasted_iota(jnp.int32, (MEM_TOKENS, MEM_WIDTH), 1) // MEM_HEAD_DIM
    for h in range(MEM_HEADS):
        sel = lane_head == h
        k0_ref[h] = jnp.where(sel, kn * gk0_ref[...], 0.0).astype(BF16)
        k1_ref[h] = jnp.where(sel, kn * gk1_ref[...], 0.0).astype(BF16)
        v_ref[h] = jnp.where(sel, v, 0.0).astype(BF16)


def _memkv(mem2d, mem_norm, w_kv, gk0, gk1, batch):
    out = jax.ShapeDtypeStruct((batch * MEM_HEADS, MEM_TOKENS, MEM_WIDTH), BF16)
    hspec = pl.BlockSpec((MEM_HEADS, MEM_TOKENS, MEM_WIDTH), lambda b: (b, 0, 0))
    return pl.pallas_call(
        _memkv_kernel,
        out_shape=(out, out, out),
        grid=(batch,),
        in_specs=[pl.BlockSpec((MEM_TOKENS, D_MODEL), lambda b: (b, 0)),
                  _const_spec((1, D_MODEL)), _const_spec((D_MODEL, 2 * MEM_WIDTH)),
                  _const_spec((1, MEM_WIDTH)), _const_spec((1, MEM_WIDTH))],
        out_specs=(hspec, hspec, hspec),
        compiler_params=_cparams(("parallel",)),
        name="memkv",
    )(mem2d, mem_norm, w_kv, gk0, gk1)


def _in0_kernel(tiles_per_seq, h_ref, hp_ref, g_ref, wqk_ref, wvo_ref, wmq_ref, wg_ref, wgt_ref,
                cw_ref, qs_ref, bias_ref, biast_ref,
                qk_ref, v_ref, so_ref, mq_ref, gates_ref, gatest_ref):
    i = pl.program_id(0)
    tm = h_ref.shape[0]
    g = g_ref[...]
    xn = _rms(h_ref[...], g).astype(BF16)
    first = (i % tiles_per_seq) == 0
    xpn = _rms(hp_ref[...], g).astype(BF16)
    row8 = lax.broadcasted_iota(jnp.int32, (8, FF_CHUNK), 0)
    ncol = wqk_ref.shape[1]
    for c in range(ncol // FF_CHUNK):
        cs = slice(c * FF_CHUNK, (c + 1) * FF_CHUNK)
        w = wqk_ref[:, cs]
        u = jnp.dot(xn, w, preferred_element_type=F32)
        up = jnp.dot(xpn, w, preferred_element_type=F32)
        up = jnp.where(first, 0.0, up)
        acc = u * cw_ref[CONV_WIDTH - 1:CONV_WIDTH, cs]
        for k in range(1, CONV_WIDTH):
            rolled = pltpu.roll(u, k, 0)
            prev = pltpu.roll(up, k, 0)
            head = jnp.where(row8 < k, prev, rolled[:8])
            shifted = jnp.concatenate([head, rolled[8:]], axis=0)
            acc = acc + shifted * cw_ref[CONV_WIDTH - 1 - k:CONV_WIDTH - k, cs]
        y = acc * jax.nn.sigmoid(acc) * qs_ref[:, cs]
        qk_ref[:, cs] = y.astype(BF16)
    zvo = jnp.dot(xn, wvo_ref[...], preferred_element_type=F32)
    v_ref[...] = zvo[:, :ML_PAD].astype(BF16)
    so_ref[...] = jax.nn.sigmoid(zvo[:, ML_PAD:]).astype(BF16)
    mq_ref[...] = jnp.dot(xn, wmq_ref[...], preferred_element_type=F32).astype(BF16)

    def gate_act(z, is_input_gate):
        logsig = jnp.minimum(z, 0.0) - jnp.log(1.0 + jnp.exp(-jnp.abs(z)))
        return jnp.where(is_input_gate, z, logsig)

    zg = jnp.dot(xn, wg_ref[...], preferred_element_type=F32) + bias_ref[...]
    lane = lax.broadcasted_iota(jnp.int32, (tm, GATE_PAD), 1)
    gates_ref[...] = gate_act(zg, lane < ML_HEADS)
    zgt = lax.dot_general(wgt_ref[...], xn, (((1,), (1,)), ((), ())),
                          preferred_element_type=F32) + biast_ref[...]
    row = lax.broadcasted_iota(jnp.int32, (8, tm), 0)
    gatest_ref[...] = gate_act(zgt, row < ML_HEADS)


def _in0(h2d, ln, wqk, wvo, wmq, wg, wgt, cw, qscale, bias, biast, seq):
    t = h2d.shape[0]
    tm = TM_PROJ
    tiles_per_seq = seq // tm
    tok = lambda n, dt: jax.ShapeDtypeStruct((t, n), dt)
    row = lambda n: pl.BlockSpec((tm, n), lambda i: (i, 0))
    return pl.pallas_call(
        functools.partial(_in0_kernel, tiles_per_seq),
        out_shape=(tok(2 * ML_PAD, BF16), tok(ML_PAD, BF16), tok(ML_PAD, BF16), tok(MEM_WIDTH, BF16),
                   tok(GATE_PAD, F32), jax.ShapeDtypeStruct((8, t), F32)),
        grid=(t // tm,),
        in_specs=[row(D_MODEL),
                  pl.BlockSpec((8, D_MODEL), lambda i: (jnp.maximum(i * (tm // 8) - 1, 0), 0)),
                  _const_spec((1, D_MODEL)), _const_spec(wqk.shape), _const_spec(wvo.shape),
                  _const_spec(wmq.shape), _const_spec(wg.shape), _const_spec(wgt.shape),
                  _const_spec(cw.shape), _const_spec(qscale.shape), _const_spec(bias.shape),
                  _const_spec(biast.shape)],
        out_specs=(row(2 * ML_PAD), row(ML_PAD), row(ML_PAD), row(MEM_WIDTH), row(GATE_PAD),
                   pl.BlockSpec((8, tm), lambda i: (0, i))),
        compiler_params=_cparams(("parallel",)),
        name="in_proj0",
    )(h2d, h2d, ln, wqk, wvo, wmq, wg, wgt, cw, qscale, bias, biast)


def _mlstm_kernel(qk_ref, v_ref, so_ref, gates_ref, gatest_ref, hg_ref, out_ref, c_ref, m_ref):
    L = qk_ref.shape[0]

    @pl.when(pl.program_id(1) == 0)
    def _():
        c_ref[...] = jnp.zeros_like(c_ref)
        m_ref[...] = jnp.zeros_like(m_ref)

    gts = gates_ref[...]
    gtt = gatest_ref[...]
    r = lax.broadcasted_iota(jnp.int32, (L, L), 0)
    c = lax.broadcasted_iota(jnp.int32, (L, L), 1)
    causal = r >= c
    tri_lo = jnp.where(causal, 1.0, 0.0).astype(BF16)
    tri_up = jnp.where(r <= c, 1.0, 0.0).astype(BF16)
    hi = gts.astype(BF16)
    lo = (gts - hi.astype(F32)).astype(BF16)
    b_cols = (jnp.dot(tri_lo, hi, preferred_element_type=F32)
              + jnp.dot(tri_lo, lo, preferred_element_type=F32))
    b_rows = _split_dot(gtt, tri_up)
    lane = lax.broadcasted_iota(jnp.int32, (L, HEAD_PAD), 1)
    hg = hg_ref[...]
    for h in range(ML_HEADS):
        m_h = m_ref[h][0:1, 0:1]
        b_c = b_cols[:, ML_HEADS + h:ML_HEADS + h + 1]
        ig_c = gts[:, h:h + 1]
        b_r = b_rows[ML_HEADS + h:ML_HEADS + h + 1, :]
        ig_r = gtt[h:h + 1, :]
        c_r = b_r - ig_r
        log_d = jnp.where(causal, b_c - c_r, NEG)
        inter = b_c + m_h
        m_row = jnp.maximum(inter, jnp.max(log_d, axis=-1, keepdims=True))
        d = jnp.exp(log_d - m_row)
        s_inter = jnp.exp(inter - m_row)
        q_h = qk_ref[:, h * HEAD_PAD:(h + 1) * HEAD_PAD]
        k_h = qk_ref[:, ML_PAD + h * HEAD_PAD:ML_PAD + (h + 1) * HEAD_PAD]
        v_aug = jnp.where(lane == ML_HEAD_DIM, 1.0, v_ref[:, h * HEAD_PAD:(h + 1) * HEAD_PAD].astype(F32)).astype(BF16)
        s = lax.dot_general(q_h, k_h, (((1,), (1,)), ((), ())), preferred_element_type=F32)
        qkd = (s * d).astype(BF16)
        c_old = c_ref[h]
        num = (jnp.dot(qkd, v_aug, preferred_element_type=F32)
               + s_inter * jnp.dot(q_h, c_old.astype(BF16), preferred_element_type=F32))
        den = jnp.sum(jnp.where(lane == ML_HEAD_DIM, num, 0.0), axis=-1, keepdims=True)
        ht = num * (1.0 / jnp.maximum(jnp.abs(den), jnp.exp(-m_row)))
        ht = jnp.where(lane < ML_HEAD_DIM, ht, 0.0)
        ms = jnp.sum(ht * ht, axis=-1, keepdims=True) * (1.0 / ML_HEAD_DIM)
        hn = ht * lax.rsqrt(ms + EPS) * hg * so_ref[:, h * HEAD_PAD:(h + 1) * HEAD_PAD].astype(F32)
        out_ref[:, h * HEAD_PAD:(h + 1) * HEAD_PAD] = hn.astype(BF16)
        b_last = b_c[L - 1:L, :]
        g_r = b_last - c_r
        m_new = jnp.maximum(b_last + m_h, jnp.max(g_r, axis=-1, keepdims=True))
        w_c = jnp.exp(b_last - b_c + ig_c - m_new)
        decay = jnp.exp(b_last + m_h - m_new)
        kw = (k_h.astype(F32) * w_c).astype(BF16)
        upd = lax.dot_general(kw, v_aug, (((0,), (0,)), ((), ())), preferred_element_type=F32)
        c_ref[h] = decay * c_old + upd
        m_ref[h] = jnp.broadcast_to(m_new, m_ref.shape[1:])


def _mlstm(qk, v, so, gates, gatest, hgain, batch, seq):
    t = qk.shape[0]
    L = ML_CHUNK
    nc = seq // L
    row = lambda n: pl.BlockSpec((L, n), lambda b, c: (b * nc + c, 0))
    return pl.pallas_call(
        _mlstm_kernel,
        out_shape=jax.ShapeDtypeStruct((t, ML_PAD), BF16),
        grid=(batch, nc),
        in_specs=[row(2 * ML_PAD), row(ML_PAD), row(ML_PAD), row(GATE_PAD),
                  pl.BlockSpec((8, L), lambda b, c: (0, b * nc + c)),
                  _const_spec((1, HEAD_PAD))],
        out_specs=row(ML_PAD),
        scratch_shapes=[pltpu.VMEM((ML_HEADS, HEAD_PAD, HEAD_PAD), F32),
                        pltpu.VMEM((ML_HEADS, 8, LANE), F32)],
        compiler_params=_cparams(("parallel", "arbitrary")),
        name="mlstm",
    )(qk, v, so, gates, gatest, hgain)


def _out_kernel(main_ref, mq_ref, h_ref, k4_ref, v4_ref, gq_ref, wm_ref, wmem_ref, out_ref):
    q = mq_ref[...].astype(F32)
    bd = _block_diag_ones(MEM_WIDTH, MEM_HEAD_DIM)
    ms = _split_dot(q * q, bd) * (1.0 / MEM_HEAD_DIM)
    qn = (q * lax.rsqrt(ms + EPS) * gq_ref[...] * (MEM_HEAD_DIM ** -0.5)).astype(BF16)
    ymem = jnp.zeros(q.shape, F32)
    for h in range(MEM_HEADS):
        s = lax.dot_general(qn, k4_ref[h], (((1,), (1,)), ((), ())), preferred_element_type=F32)
        e = jnp.exp(s - jnp.max(s, axis=-1, keepdims=True))
        inv = 1.0 / jnp.sum(e, axis=-1, keepdims=True)
        ymem = ymem + jnp.dot(e.astype(BF16), v4_ref[h], preferred_element_type=F32) * inv
    y = (jnp.dot(main_ref[...], wm_ref[...], preferred_element_type=F32)
         + jnp.dot(ymem.astype(BF16), wmem_ref[...], preferred_element_type=F32))
    out_ref[...] = h_ref[...] + y


def _out_proj(main, mq, h2d, k4, v4, gq, wm, wmem, seq):
    t = h2d.shape[0]
    tm = TM_PROJ
    tiles_per_seq = seq // tm
    row = lambda n: pl.BlockSpec((tm, n), lambda i: (i, 0))
    memspec = pl.BlockSpec((MEM_HEADS, MEM_TOKENS, MEM_WIDTH), lambda i: (i // tiles_per_seq, 0, 0))
    return pl.pallas_call(
        _out_kernel,
        out_shape=jax.ShapeDtypeStruct((t, D_MODEL), F32),
        grid=(t // tm,),
        in_specs=[row(main.shape[1]), row(MEM_WIDTH), row(D_MODEL), memspec, memspec,
                  _const_spec((1, MEM_WIDTH)), _const_spec(wm.shape), _const_spec(wmem.shape)],
        out_specs=row(D_MODEL),
        compiler_params=_cparams(("parallel",)),
        name="out_proj",
    )(main, mq, h2d, k4, v4, gq, wm, wmem)


def _ffn_kernel(h_ref, g_ref, wg_ref, wu_ref, wd_ref, out_ref):
    x = h_ref[...]
    xn = _rms(x, g_ref[...]).astype(BF16)
    acc = x
    for c in range(D_FF // FF_CHUNK):
        cs = slice(c * FF_CHUNK, (c + 1) * FF_CHUNK)
        gt = jnp.dot(xn, wg_ref[:, cs], preferred_element_type=F32)
        up = jnp.dot(xn, wu_ref[:, cs], preferred_element_type=F32)
        a = (gt * jax.nn.sigmoid(gt) * up).astype(BF16)
        acc = acc + jnp.dot(a, wd_ref[cs, :], preferred_element_type=F32)
    out_ref[...] = acc


def _ffn(h2d, ln, wg, wu, wd):
    t = h2d.shape[0]
    tm = TM_FFN
    row = pl.BlockSpec((tm, D_MODEL), lambda i: (i, 0))
    res = lambda shape: pl.BlockSpec(shape, lambda i: (0, 0), pipeline_mode=pl.Buffered(1))
    return pl.pallas_call(
        _ffn_kernel,
        out_shape=jax.ShapeDtypeStruct((t, D_MODEL), F32),
        grid=(t // tm,),
        in_specs=[row, _const_spec((1, D_MODEL)), res(wg.shape), res(wu.shape), res(wd.shape)],
        out_specs=row,
        compiler_params=_cparams(("parallel",)),
        name="ffn_dense",
    )(h2d, ln, wg, wu, wd)


def _in1_kernel(h_ref, pos_ref, g_ref, w1_ref, gcq_ref, gckv_ref, wuq_ref, wuk_ref, wuv_ref,
                gq_ref, gkn_ref, gkr_ref, invf_ref,
                q_ref, k_ref, v_ref, mq_ref):
    tm = h_ref.shape[0]
    xn = _rms(h_ref[...], g_ref[...]).astype(BF16)
    z = jnp.dot(xn, w1_ref[...], preferred_element_type=F32)
    cq = z[:, :Q_LORA]
    ckv = z[:, Q_LORA:Q_LORA + KV_LORA]
    kr = z[:, Q_LORA + KV_LORA:Q_LORA + KV_LORA + LANE]
    mq_ref[...] = z[:, Q_LORA + KV_LORA + LANE:].astype(BF16)
    cqn = _rms(cq, gcq_ref[...]).astype(BF16)
    ckvn = _rms(ckv, gckv_ref[...]).astype(BF16)
    q = jnp.dot(cqn, wuq_ref[...], preferred_element_type=F32)
    kn = jnp.dot(ckvn, wuk_ref[...], preferred_element_type=F32)
    v_ref[...] = jnp.dot(ckvn, wuv_ref[...], preferred_element_type=F32).astype(BF16)

    half = MLA_ROPE // 2
    ang = pos_ref[...].astype(F32) * invf_ref[...]
    lane = lax.broadcasted_iota(jnp.int32, (tm, LANE), 1)
    cos = jnp.where(lane < MLA_ROPE, jnp.cos(ang), 0.0)
    sin = jnp.sin(ang)
    sin_hi = jnp.where((lane >= half) & (lane < MLA_ROPE), sin, 0.0)
    sin_lo = jnp.where(lane < half, -sin, 0.0)

    def rope(t):
        return t * cos + pltpu.roll(t, half, 1) * sin_hi + pltpu.roll(t, LANE - half, 1) * sin_lo

    scale = MLA_QK ** -0.5
    gq = gq_ref[...]
    for h in range(MLA_HEADS):
        qh = q[:, h * HEAD_PAD:(h + 1) * HEAD_PAD]
        rs = lax.rsqrt(jnp.sum(qh * qh, axis=-1, keepdims=True) * (1.0 / MLA_QK) + EPS) * scale
        qn = qh * rs * gq
        q_ref[:, h * HEAD_PAD:h * HEAD_PAD + LANE] = qn[:, :LANE].astype(BF16)
        q_ref[:, h * HEAD_PAD + LANE:(h + 1) * HEAD_PAD] = rope(qn[:, LANE:]).astype(BF16)
    ss_r = jnp.sum(kr * kr, axis=-1, keepdims=True)
    krr = rope(kr * gkr_ref[...])
    gkn = gkn_ref[...]
    for h in range(MLA_HEADS):
        kh = kn[:, h * MLA_NOPE:(h + 1) * MLA_NOPE]
        rs = lax.rsqrt((jnp.sum(kh * kh, axis=-1, keepdims=True) + ss_r) * (1.0 / MLA_QK) + EPS)
        k_ref[:, h * HEAD_PAD:h * HEAD_PAD + LANE] = (kh * rs * gkn).astype(BF16)
        k_ref[:, h * HEAD_PAD + LANE:(h + 1) * HEAD_PAD] = (krr * rs).astype(BF16)


def _in1(h2d, pos2d, ln, w1, gcq, gckv, wuq, wuk, wuv, gq, gkn, gkr, invf):
    t = h2d.shape[0]
    tm = TM_PROJ
    tok = lambda n: jax.ShapeDtypeStruct((t, n), BF16)
    row = lambda n: pl.BlockSpec((tm, n), lambda i: (i, 0))
    consts = [ln, w1, gcq, gckv, wuq, wuk, wuv, gq, gkn, gkr, invf]
    return pl.pallas_call(
        _in1_kernel,
        out_shape=(tok(MLA_PAD), tok(MLA_PAD), tok(MAIN_WIDTH), tok(MEM_WIDTH)),
        grid=(t // tm,),
        in_specs=[row(D_MODEL), row(1)] + [_const_spec(a.shape) for a in consts],
        out_specs=(row(MLA_PAD), row(MLA_PAD), row(MAIN_WIDTH), row(MEM_WIDTH)),
        compiler_params=_cparams(("parallel",)),
        name="in_proj1",
    )(h2d, pos2d, *consts)


def _attn_kernel(qi_ref, kj_ref, q_ref, k_ref, v_ref, o_ref, m_ref, l_ref, acc_ref):
    p = pl.program_id(2)
    qi = qi_ref[p]
    kj = kj_ref[p]

    @pl.when(kj == 0)
    def _():
        m_ref[...] = jnp.full_like(m_ref, NEG)
        l_ref[...] = jnp.zeros_like(l_ref)
        acc_ref[...] = jnp.zeros_like(acc_ref)

    s = lax.dot_general(q_ref[...], k_ref[...], (((1,), (1,)), ((), ())), preferred_element_type=F32)

    def update(s):
        m_old = m_ref[...]
        m_new = jnp.maximum(m_old, jnp.max(s, axis=-1, keepdims=True))
        alpha = jnp.exp(m_old - m_new)
        e = jnp.exp(s - m_new)
        l_ref[...] = alpha * l_ref[...] + jnp.sum(e, axis=-1, keepdims=True)
        acc_ref[...] = alpha * acc_ref[...] + jnp.dot(e.astype(BF16), v_ref[...], preferred_element_type=F32)
        m_ref[...] = m_new

    last = (kj + 1) * TK >= (qi + 1) * TQ
    crosses_diagonal = (kj + 1) * TK - 1 > qi * TQ

    @pl.when(jnp.logical_not(crosses_diagonal))
    def _():
        update(s)

    @pl.when(crosses_diagonal)
    def _():
        rows = qi * TQ + lax.broadcasted_iota(jnp.int32, s.shape, 0)
        cols = kj * TK + lax.broadcasted_iota(jnp.int32, s.shape, 1)
        update(jnp.where(cols <= rows, s, NEG))

    @pl.when(last)
    def _():
        o_ref[...] = (acc_ref[...] / l_ref[...]).astype(o_ref.dtype)


def _attention(q, k, v, batch, seq):
    t = q.shape[0]
    nq = seq // TQ
    nk_per_q = TQ // TK
    qi_l, kj_l = [], []
    for i in range(nq):
        for j in range((i + 1) * nk_per_q):
            qi_l.append(i)
            kj_l.append(j)
    qi = jnp.asarray(np.array(qi_l, np.int32))
    kj = jnp.asarray(np.array(kj_l, np.int32))
    npairs = len(qi_l)
    nkb = seq // TK
    grid_spec = pltpu.PrefetchScalarGridSpec(
        num_scalar_prefetch=2,
        grid=(batch, MLA_HEADS, npairs),
        in_specs=[pl.BlockSpec((TQ, HEAD_PAD), lambda b, h, p, qi, kj: (b * nq + qi[p], h)),
                  pl.BlockSpec((TK, HEAD_PAD), lambda b, h, p, qi, kj: (b * nkb + kj[p], h)),
                  pl.BlockSpec((TK, MLA_V), lambda b, h, p, qi, kj: (b * nkb + kj[p], h))],
        out_specs=pl.BlockSpec((TQ, MLA_V), lambda b, h, p, qi, kj: (b * nq + qi[p], h)),
        scratch_shapes=[pltpu.VMEM((TQ, 1), F32), pltpu.VMEM((TQ, 1), F32), pltpu.VMEM((TQ, MLA_V), F32)],
    )
    return pl.pallas_call(
        _attn_kernel,
        out_shape=jax.ShapeDtypeStruct((t, MAIN_WIDTH), BF16),
        grid_spec=grid_spec,
        compiler_params=_cparams(("parallel", "parallel", "arbitrary")),
        name="mla_attention",
    )(qi, kj, q, k, v)


def _router_kernel(h_ref, g_ref, whi_ref, wlo_ref,
                   xn_ref, gates_ref, rank_ref, rankt_ref, starts_ref, counts_ref, run_ref):
    @pl.when(pl.program_id(0) == 0)
    def _():
        run_ref[...] = jnp.zeros_like(run_ref)

    tm = h_ref.shape[0]
    xn = _rms(h_ref[...], g_ref[...])
    hi = xn.astype(BF16)
    lo = (xn - hi.astype(F32)).astype(BF16)
    xn_ref[...] = hi
    whi = whi_ref[...]
    logits = (jnp.dot(hi, whi, preferred_element_type=F32) + jnp.dot(lo, whi, preferred_element_type=F32)
              + jnp.dot(hi, wlo_ref[...], preferred_element_type=F32))
    lane = lax.broadcasted_iota(jnp.int32, logits.shape, 1).astype(F32)
    logits = jnp.where(lane < N_EXPERTS, logits, NEG)
    m1 = jnp.max(logits, axis=-1, keepdims=True)
    i1 = jnp.min(jnp.where(logits == m1, lane, float(LANE)), axis=-1, keepdims=True)
    rest = jnp.where(lane == i1, NEG, logits)
    m2 = jnp.max(rest, axis=-1, keepdims=True)
    i2 = jnp.min(jnp.where(rest == m2, lane, float(LANE)), axis=-1, keepdims=True)
    e2 = jnp.exp(m2 - m1)
    w1 = 1.0 / (1.0 + e2)
    w2 = e2 * w1
    gates_ref[...] = jnp.where(lane == i1, w1, jnp.where(lane == i2, w2, 0.0))
    routed = (lane == i1) | (lane == i2)
    oh = jnp.where(routed, 1.0, 0.0)
    r = lax.broadcasted_iota(jnp.int32, (tm, tm), 0)
    c = lax.broadcasted_iota(jnp.int32, (tm, tm), 1)
    before = jnp.where(r > c, 1.0, 0.0).astype(BF16)
    prefix = jnp.dot(before, oh.astype(BF16), preferred_element_type=F32)
    run = run_ref[0:1, :]
    rank = jnp.where(routed, run + prefix, -1.0)
    rank_ref[...] = rank
    rankt_ref[...] = rank.T[:N_EXPERTS, :]
    starts_ref[...] = run_ref[...]
    total = jnp.broadcast_to(run + jnp.sum(oh, axis=0, keepdims=True), run_ref.shape)
    run_ref[...] = total
    counts_ref[...] = total


def _router(h2d, ln, whi, wlo):
    t = h2d.shape[0]
    tm = TM_PROJ
    nt = t // tm
    row = lambda n: pl.BlockSpec((tm, n), lambda i: (i, 0))
    return pl.pallas_call(
        _router_kernel,
        out_shape=(jax.ShapeDtypeStruct((t, D_MODEL), BF16), jax.ShapeDtypeStruct((t, LANE), F32),
                   jax.ShapeDtypeStruct((t, LANE), F32), jax.ShapeDtypeStruct((N_EXPERTS, t), F32),
                   jax.ShapeDtypeStruct((nt * 8, LANE), F32), jax.ShapeDtypeStruct((8, LANE), F32)),
        grid=(nt,),
        in_specs=[row(D_MODEL), _const_spec((1, D_MODEL)), _const_spec(whi.shape), _const_spec(wlo.shape)],
        out_specs=(row(D_MODEL), row(LANE), row(LANE), pl.BlockSpec((N_EXPERTS, tm), lambda i: (0, i)),
                   pl.BlockSpec((8, LANE), lambda i: (i, 0)), _const_spec((8, LANE))),
        scratch_shapes=[pltpu.VMEM((8, LANE), F32)],
        compiler_params=_cparams(("arbitrary",)),
        name="moe_router",
    )(h2d, ln, whi, wlo)


def _dispatch_kernel(ge_ref, gr0_ref, gclo_ref, gn_ref, xn_ref, rankt_ref, xs_ref, acc_ref):
    g = pl.program_id(0)
    e = ge_ref[g]
    clo = gclo_ref[g]
    sg = xs_ref.shape[0]
    want = (gr0_ref[g] + lax.broadcasted_iota(jnp.int32, (sg, TM_PROJ), 0)).astype(F32)
    acc_ref[...] = jnp.zeros_like(acc_ref)

    def body(k, carry):
        start = pl.multiple_of((clo + k) * TM_PROJ, TM_PROJ)
        ranks = rankt_ref[pl.ds(e, 1), pl.ds(start, TM_PROJ)]
        onehot = jnp.where(ranks == want, 1.0, 0.0).astype(BF16)
        acc_ref[...] += jnp.dot(onehot, xn_ref[pl.ds(start, TM_PROJ), :], preferred_element_type=F32)
        return carry

    lax.fori_loop(0, gn_ref[g], body, 0)
    xs_ref[...] = acc_ref[...].astype(BF16)


def _dispatch(xn, rankt, g_expert, g_rank0, g_chunk_lo, g_nchunks):
    t = xn.shape[0]
    n_groups = g_expert.shape[0]
    res = lambda shape: pl.BlockSpec(shape, lambda g, *_: (0, 0), pipeline_mode=pl.Buffered(1))
    grid_spec = pltpu.PrefetchScalarGridSpec(
        num_scalar_prefetch=4,
        grid=(n_groups,),
        in_specs=[res((t, D_MODEL)), res((N_EXPERTS, t))],
        out_specs=pl.BlockSpec((SLOT_GROUP, D_MODEL), lambda g, *_: (g, 0)),
        scratch_shapes=[pltpu.VMEM((SLOT_GROUP, D_MODEL), F32)],
    )
    return pl.pallas_call(
        _dispatch_kernel,
        out_shape=jax.ShapeDtypeStruct((n_groups * SLOT_GROUP, D_MODEL), BF16),
        grid_spec=grid_spec,
        compiler_params=_cparams(("arbitrary",)),
        name="moe_dispatch",
    )(g_expert, g_rank0, g_chunk_lo, g_nchunks, xn, rankt)


def _expert_kernel(te_ref, ta_ref, x_ref, wg_ref, wu_ref, wd_ref, out_ref, acc_ref):
    j = pl.program_id(0)
    c = pl.program_id(1)
    active = ta_ref[j] > 0

    @pl.when(active)
    def _():
        x = x_ref[...]
        gt = jnp.dot(x, wg_ref[0], preferred_element_type=F32)
        up = jnp.dot(x, wu_ref[0], preferred_element_type=F32)
        a = (gt * jax.nn.sigmoid(gt) * up).astype(BF16)
        y = jnp.dot(a, wd_ref[0], preferred_element_type=F32)

        @pl.when(c == 0)
        def _():
            acc_ref[...] = y

        @pl.when(c > 0)
        def _():
            acc_ref[...] += y

        @pl.when(c == pl.num_programs(1) - 1)
        def _():
            out_ref[...] = acc_ref[...].astype(out_ref.dtype)

    @pl.when(jnp.logical_not(active) & (c == pl.num_programs(1) - 1))
    def _():
        out_ref[...] = jnp.zeros_like(out_ref)


def _experts(x, tile_expert, tile_active, wg, wu, wd):
    n_tiles = tile_expert.shape[0]
    tm = TM_MOE
    xmap = lambda j, c, te, ta: (jnp.where(ta[j] > 0, j, 0), 0)
    cidx = lambda j, c, ta: jnp.where(ta[j] > 0, c, D_FF // FF_CHUNK - 1)
    grid_spec = pltpu.PrefetchScalarGridSpec(
        num_scalar_prefetch=2,
        grid=(n_tiles, D_FF // FF_CHUNK),
        in_specs=[pl.BlockSpec((tm, D_MODEL), xmap),
                  pl.BlockSpec((1, D_MODEL, FF_CHUNK), lambda j, c, te, ta: (te[j], 0, cidx(j, c, ta))),
                  pl.BlockSpec((1, D_MODEL, FF_CHUNK), lambda j, c, te, ta: (te[j], 0, cidx(j, c, ta))),
                  pl.BlockSpec((1, FF_CHUNK, D_MODEL), lambda j, c, te, ta: (te[j], cidx(j, c, ta), 0))],
        out_specs=pl.BlockSpec((tm, D_MODEL), lambda j, c, te, ta: (j, 0)),
        scratch_shapes=[pltpu.VMEM((tm, D_MODEL), F32)],
    )
    return pl.pallas_call(
        _expert_kernel,
        out_shape=jax.ShapeDtypeStruct((n_tiles * tm, D_MODEL), BF16),
        grid_spec=grid_spec,
        compiler_params=_cparams(("arbitrary", "arbitrary")),
        name="moe_experts",
    )(tile_expert, tile_active, x, wg, wu, wd)


def _combine_window_copy(ys_ref, buf_ref, sem_ref, start, slot, e):
    return pltpu.make_async_copy(ys_ref.at[pl.ds(pl.multiple_of(start, SLOT_ALIGN), COMBINE_WIN), :],
                                 buf_ref.at[slot, e], sem_ref.at[slot, e])


def _combine_kernel(ws_ref, np_ref, h_ref, gates_ref, rank_ref, base_ref, ys_ref, out_ref,
                    buf_ref, sem_ref, buf2_ref, sem2_ref):
    i = pl.program_id(0)
    n = pl.num_programs(0)
    tm = h_ref.shape[0]

    def start_tile(tile, slot):
        for e in range(N_EXPERTS):
            _combine_window_copy(ys_ref, buf_ref, sem_ref, ws_ref[tile * N_EXPERTS + e], slot, e).start()

    @pl.when(i == 0)
    def _():
        start_tile(0, 0)

    @pl.when(i + 1 < n)
    def _():
        start_tile(i + 1, (i + 1) % 2)

    slot = i % 2
    rank = rank_ref[...]
    slots = jnp.where(rank >= 0.0, rank + base_ref[...], -1.0)
    gates = gates_ref[...]
    lane_iota = lax.broadcasted_iota(jnp.int32, (tm, COMBINE_WIN), 1)
    y = h_ref[...]
    for e in range(N_EXPERTS):
        ws = ws_ref[i * N_EXPERTS + e]
        slot_e = slots[:, e:e + 1]
        _combine_window_copy(ys_ref, buf_ref, sem_ref, ws, slot, e).wait()
        onehot = jnp.where(slot_e == (ws + lane_iota).astype(F32), 1.0, 0.0).astype(BF16)
        z = jnp.dot(onehot, buf_ref[slot, e], preferred_element_type=F32)
        y = y + gates[:, e:e + 1] * z

        def second(ws=ws, slot_e=slot_e, e=e):
            cp = pltpu.make_async_copy(ys_ref.at[pl.ds(pl.multiple_of(ws + COMBINE_WIN, SLOT_ALIGN), COMBINE_WIN), :],
                                       buf2_ref, sem2_ref)
            cp.start()
            cp.wait()
            oh2 = jnp.where(slot_e == (ws + COMBINE_WIN + lane_iota).astype(F32), 1.0, 0.0).astype(BF16)
            return gates[:, e:e + 1] * jnp.dot(oh2, buf2_ref[...], preferred_element_type=F32)

        y = y + lax.cond(np_ref[i * N_EXPERTS + e] > 1, second, lambda: jnp.zeros((tm, D_MODEL), F32))
    out_ref[...] = y


def _combine(h2d, gates, rank, base_row, ys, win_start, win_pieces):
    t = h2d.shape[0]
    tm = TM_PROJ
    row = lambda n: pl.BlockSpec((tm, n), lambda i, *_: (i, 0))
    grid_spec = pltpu.PrefetchScalarGridSpec(
        num_scalar_prefetch=2,
        grid=(t // tm,),
        in_specs=[row(D_MODEL), row(LANE), row(LANE), pl.BlockSpec((1, LANE), lambda i, *_: (0, 0)),
                  pl.BlockSpec(memory_space=pl.ANY)],
        out_specs=row(D_MODEL),
        scratch_shapes=[pltpu.VMEM((2, N_EXPERTS, COMBINE_WIN, D_MODEL), BF16),
                        pltpu.SemaphoreType.DMA((2, N_EXPERTS)),
                        pltpu.VMEM((COMBINE_WIN, D_MODEL), BF16),
                        pltpu.SemaphoreType.DMA(())],
    )
    return pl.pallas_call(
        _combine_kernel,
        out_shape=jax.ShapeDtypeStruct((t, D_MODEL), F32),
        grid_spec=grid_spec,
        compiler_params=_cparams(("arbitrary",)),
        name="moe_combine",
    )(win_start, win_pieces, h2d, gates, rank, base_row, ys)


def _moe_plan(counts, starts, t):
    n_tok_tiles = t // TM_PROJ
    n_tiles = 2 * t // TM_MOE + N_EXPERTS + 1
    n_slots = n_tiles * TM_MOE
    counts = counts.astype(jnp.int32)
    cum = jnp.concatenate([starts.astype(jnp.int32), counts[None, :]], axis=0)
    padded = (counts + TM_MOE - 1) // TM_MOE * TM_MOE
    ends = jnp.cumsum(padded)
    base = ends - padded
    total = ends[-1]
    tile_first = jnp.arange(n_tiles, dtype=jnp.int32) * TM_MOE
    tile_expert = jnp.minimum(jnp.searchsorted(ends, tile_first, side="right"), N_EXPERTS - 1).astype(jnp.int32)
    tile_active = (tile_first < total).astype(jnp.int32)
    n_groups = n_slots // SLOT_GROUP
    g_first = jnp.arange(n_groups, dtype=jnp.int32) * SLOT_GROUP
    g_expert = tile_expert[g_first // TM_MOE]
    g_r0 = g_first - base[g_expert]
    cum_e = cum.T[g_expert]
    lo = jnp.sum(cum_e <= g_r0[:, None], axis=1) - 1
    hi = jnp.sum(cum_e < (g_r0 + SLOT_GROUP)[:, None], axis=1)
    g_valid = (g_first < total) & (g_r0 < counts[g_expert])
    lo = jnp.clip(lo, 0, n_tok_tiles - 1)
    hi = jnp.clip(hi, 0, n_tok_tiles)
    g_n = jnp.where(g_valid, jnp.maximum(hi - lo, 0), 0)
    first = base[None, :] + cum[:-1]
    cnt = cum[1:] - cum[:-1]
    ws = first // SLOT_ALIGN * SLOT_ALIGN
    pieces = jnp.where(cnt > 0, (first + cnt - ws + COMBINE_WIN - 1) // COMBINE_WIN, 0)
    return dict(tile_expert=tile_expert, tile_active=tile_active, g_expert=g_expert, g_r0=g_r0.astype(jnp.int32),
                g_lo=lo.astype(jnp.int32), g_n=g_n.astype(jnp.int32), base=base,
                ws=ws.reshape(-1).astype(jnp.int32), pieces=pieces.reshape(-1).astype(jnp.int32))


def _pad_heads_cols(w, heads, dim, pad):
    lead = w.shape[:-1]
    w = w.reshape(lead + (heads, dim))
    w = jnp.pad(w, [(0, 0)] * len(lead) + [(0, 0), (0, pad - dim)])
    return w.reshape(lead + (heads * pad,))


def _row(v):
    return v.reshape(1, -1).astype(F32)


def kernel(x, mem, positions, mem_norm, w_mem_kv, ln_mix0, w_in0, conv_w0, b_i0, b_f0, h_norm0, mq_norm0, mk_norm0, w_out0, ln_ffn0, w_gate0, w_up0, w_down0, ln_mix1, w_in1, cq_norm1, ckv_norm1, w_uq1, w_ukv1, q_norm1, k_norm1, mq_norm1, mk_norm1, w_out1, ln_ffn1, w_router1, we_gate1, we_up1, we_down1):
    batch, seq, _ = x.shape
    t = batch * seq
    h = x.reshape(t, D_MODEL)
    tile4 = lambda g: _row(jnp.tile(g, MEM_HEADS))

    k4_0, k4_1, v4 = _memkv(mem.reshape(batch * MEM_TOKENS, D_MODEL), _row(mem_norm), w_mem_kv.astype(BF16),
                            tile4(mk_norm0), tile4(mk_norm1), batch)

    mw = MAIN_WIDTH
    w_q, w_k, w_v, w_o = (w_in0[:, i * mw:(i + 1) * mw] for i in range(4))
    w_gates = w_in0[:, 4 * mw:4 * mw + 2 * ML_HEADS]
    w_mq = w_in0[:, 4 * mw + 2 * ML_HEADS:]
    padh = lambda w: _pad_heads_cols(w, ML_HEADS, ML_HEAD_DIM, HEAD_PAD)
    wqk = jnp.concatenate([padh(w_q), padh(w_k)], axis=1).astype(BF16)
    wvo = jnp.concatenate([padh(w_v), padh(w_o)], axis=1).astype(BF16)
    wg = jnp.pad(w_gates, ((0, 0), (0, GATE_PAD - 2 * ML_HEADS))).astype(BF16)
    wgt = w_gates.T.astype(BF16)
    cw = jnp.concatenate([padh(conv_w0[:, :mw]), padh(conv_w0[:, mw:])], axis=1).astype(F32)
    qscale = jnp.concatenate([jnp.full((1, ML_PAD), ML_HEAD_DIM ** -0.5, F32), jnp.ones((1, ML_PAD), F32)], axis=1)
    bias = jnp.concatenate([b_i0, b_f0]).astype(F32)
    bias_row = jnp.pad(bias, (0, GATE_PAD - 2 * ML_HEADS)).reshape(1, GATE_PAD)
    bias_col = bias.reshape(2 * ML_HEADS, 1)
    qk, v0, so, mq0, gates, gatest = _in0(h, _row(ln_mix0), wqk, wvo, w_mq.astype(BF16), wg, wgt, cw, qscale,
                                          bias_row, bias_col, seq)
    hgain = jnp.pad(h_norm0.astype(F32), (0, HEAD_PAD - ML_HEAD_DIM)).reshape(1, HEAD_PAD)
    hn = _mlstm(qk, v0, so, gates, gatest, hgain, batch, seq)
    wm0 = jnp.pad(w_out0[:mw].reshape(ML_HEADS, ML_HEAD_DIM, D_MODEL),
                  ((0, 0), (0, HEAD_PAD - ML_HEAD_DIM), (0, 0))).reshape(ML_PAD, D_MODEL).astype(BF16)
    h = _out_proj(hn, mq0, h, k4_0, v4, tile4(mq_norm0), wm0, w_out0[mw:].astype(BF16), seq)
    h = _ffn(h, _row(ln_ffn0), w_gate0.astype(BF16), w_up0.astype(BF16), w_down0.astype(BF16))

    w_cq = w_in1[:, :Q_LORA]
    w_ckv = w_in1[:, Q_LORA:Q_LORA + KV_LORA]
    w_kr = jnp.pad(w_in1[:, Q_LORA + KV_LORA:Q_LORA + KV_LORA + MLA_ROPE], ((0, 0), (0, LANE - MLA_ROPE)))
    w_mq1 = w_in1[:, Q_LORA + KV_LORA + MLA_ROPE:]
    w1 = jnp.concatenate([w_cq, w_ckv, w_kr, w_mq1], axis=1).astype(BF16)
    wuq = _pad_heads_cols(w_uq1, MLA_HEADS, MLA_QK, HEAD_PAD).astype(BF16)
    wukv = w_ukv1.reshape(KV_LORA, MLA_HEADS, MLA_NOPE + MLA_V)
    wuk = wukv[:, :, :MLA_NOPE].reshape(KV_LORA, MLA_HEADS * MLA_NOPE).astype(BF16)
    wuv = wukv[:, :, MLA_NOPE:].reshape(KV_LORA, MLA_HEADS * MLA_V).astype(BF16)
    gq = jnp.pad(q_norm1.astype(F32), (0, HEAD_PAD - MLA_QK)).reshape(1, HEAD_PAD)
    gkn = _row(k_norm1[:MLA_NOPE])
    gkr = jnp.pad(k_norm1[MLA_NOPE:].astype(F32), (0, LANE - MLA_ROPE)).reshape(1, LANE)
    half = MLA_ROPE // 2
    inv_freq = ROPE_THETA ** (-jnp.arange(half, dtype=F32) / half)
    invf = jnp.concatenate([inv_freq, inv_freq, jnp.zeros((LANE - MLA_ROPE,), F32)]).reshape(1, LANE)
    q1, k1, v1, mq1 = _in1(h, positions.reshape(t, 1).astype(jnp.int32), _row(ln_mix1), w1, _row(cq_norm1),
                           _row(ckv_norm1), wuq, wuk, wuv, gq, gkn, gkr, invf)
    o1 = _attention(q1, k1, v1, batch, seq)
    h = _out_proj(o1, mq1, h, k4_1, v4, tile4(mq_norm1), w_out1[:mw].astype(BF16), w_out1[mw:].astype(BF16), seq)

    wr = jnp.pad(w_router1.astype(F32), ((0, 0), (0, LANE - N_EXPERTS)))
    wr_hi = wr.astype(BF16)
    wr_lo = (wr - wr_hi.astype(F32)).astype(BF16)
    xn, gate_w, rank, rankt, starts, counts = _router(h, _row(ln_ffn1), wr_hi, wr_lo)
    plan = _moe_plan(counts[0, :N_EXPERTS], starts[::8, :N_EXPERTS], t)
    xs = _dispatch(xn, rankt, plan["g_expert"], plan["g_r0"], plan["g_lo"], plan["g_n"])
    ys = _experts(xs, plan["tile_expert"], plan["tile_active"],
                  we_gate1.astype(BF16), we_up1.astype(BF16), we_down1.astype(BF16))
    base_row = jnp.pad(plan["base"].astype(F32), (0, LANE - N_EXPERTS)).reshape(1, LANE)
    h = _combine(h, gate_w, rank, base_row, ys, plan["ws"], plan["pieces"])
    return h.reshape(batch, seq, D_MODEL)
```

```python
import functools

import jax
import jax.numpy as jnp
import numpy as np
from jax import lax
from jax.experimental import pallas as pl
from jax.experimental.pallas import tpu as pltpu

F32 = jnp.float32
BF16 = jnp.bfloat16

D_MODEL = 1024
MEM_TOKENS = 256
MEM_HEADS = 4
MEM_WIDTH = 256
MEM_HEAD_DIM = 64
MAIN_WIDTH = 768
ML_HEADS = 4
ML_HEAD_DIM = 192
CONV_WIDTH = 4
MLA_HEADS = 6
MLA_NOPE = 128
MLA_ROPE = 64
MLA_QK = 192
MLA_V = 128
Q_LORA = 384
KV_LORA = 128
ROPE_THETA = 10000.0
D_FF = 3584
N_EXPERTS = 8
EPS = 1e-6

LANE = 128
HEAD_PAD = 256
ML_PAD = ML_HEADS * HEAD_PAD
MLA_PAD = MLA_HEADS * HEAD_PAD
GATE_PAD = 128
ML_CHUNK = 256
TM_PROJ = 256
TM_FFN = 512
FF_CHUNK = 512
TM_MOE = 512
SLOT_GROUP = 256
SLOT_ALIGN = 16
COMBINE_WIN = 256
TQ = 1024
TK = 1024
NEG = -1e30
LOG2E = 1.4426950408889634
VMEM_LIMIT = 56 * 1024 * 1024


def _cparams(sem, vmem=VMEM_LIMIT):
    return pltpu.CompilerParams(dimension_semantics=sem, vmem_limit_bytes=vmem)


def _rms(x, g):
    return x * lax.rsqrt(jnp.mean(x * x, axis=-1, keepdims=True) + EPS) * g


def _split_dot(x, w_bf16):
    hi = x.astype(BF16)
    lo = (x - hi.astype(F32)).astype(BF16)
    return (jnp.dot(hi, w_bf16, preferred_element_type=F32)
            + jnp.dot(lo, w_bf16, preferred_element_type=F32))


def _block_diag_ones(n, blk):
    r = lax.broadcasted_iota(jnp.int32, (n, n), 0) // blk
    c = lax.broadcasted_iota(jnp.int32, (n, n), 1) // blk
    return jnp.where(r == c, 1.0, 0.0).astype(BF16)


def _const_spec(shape):
    nd = len(shape)
    return pl.BlockSpec(shape, lambda *_: (0,) * nd)


def _memkv_kernel(mem_ref, g_ref, w_ref, gk0_ref, gk1_ref, k0_ref, k1_ref, v_ref):
    xn = _rms(mem_ref[...], g_ref[...]).astype(BF16)
    kv = jnp.dot(xn, w_ref[...], preferred_element_type=F32)
    k = kv[:, :MEM_WIDTH]
    v = kv[:, MEM_WIDTH:]
    bd = _block_diag_ones(MEM_WIDTH, MEM_HEAD_DIM)
    ms = _split_dot(k * k, bd) * (1.0 / MEM_HEAD_DIM)
    kn = k * lax.rsqrt(ms + EPS)
    lane_head = lax.broadcasted_iota(jnp.int32, (MEM_TOKENS, MEM_WIDTH), 1) // MEM_HEAD_DIM
    for h in range(MEM_HEADS):
        sel = lane_head == h
        k0_ref[h] = jnp.where(sel, kn * gk0_ref[...], 0.0).astype(BF16)
        k1_ref[h] = jnp.where(sel, kn * gk1_ref[...], 0.0).astype(BF16)
        v_ref[h] = jnp.where(sel, v, 0.0).astype(BF16)


def _memkv(mem2d, mem_norm, w_kv, gk0, gk1, batch):
    out = jax.ShapeDtypeStruct((batch * MEM_HEADS, MEM_TOKENS, MEM_WIDTH), BF16)
    hspec = pl.BlockSpec((MEM_HEADS, MEM_TOKENS, MEM_WIDTH), lambda b: (b, 0, 0))
    return pl.pallas_call(
        _memkv_kernel,
        out_shape=(out, out, out),
        grid=(batch,),
        in_specs=[pl.BlockSpec((MEM_TOKENS, D_MODEL), lambda b: (b, 0)),
                  _const_spec((1, D_MODEL)), _const_spec((D_MODEL, 2 * MEM_WIDTH)),
                  _const_spec((1, MEM_WIDTH)), _const_spec((1, MEM_WIDTH))],
        out_specs=(hspec, hspec, hspec),
        compiler_params=_cparams(("parallel",)),
        name="memkv",
    )(mem2d, mem_norm, w_kv, gk0, gk1)


def _in0_kernel(tiles_per_seq, h_ref, hp_ref, g_ref, wqk_ref, wvo_ref, wmq_ref, wg_ref, wgt_ref,
                cw_ref, qs_ref, bias_ref, biast_ref,
                qk_ref, v_ref, so_ref, mq_ref, gates_ref, gatest_ref):
    i = pl.program_id(0)
    tm = h_ref.shape[0]
    g = g_ref[...]
    xn = _rms(h_ref[...], g).astype(BF16)
    first = (i % tiles_per_seq) == 0
    xpn = _rms(hp_ref[...], g).astype(BF16)
    row8 = lax.broadcasted_iota(jnp.int32, (8, FF_CHUNK), 0)
    ncol = wqk_ref.shape[1]
    for c in range(ncol // FF_CHUNK):
        cs = slice(c * FF_CHUNK, (c + 1) * FF_CHUNK)
        w = wqk_ref[:, cs]
        u = jnp.dot(xn, w, preferred_element_type=F32)
        up = jnp.dot(xpn, w, preferred_element_type=F32)
        up = jnp.where(first, 0.0, up)
        acc = u * cw_ref[CONV_WIDTH - 1:CONV_WIDTH, cs]
        for k in range(1, CONV_WIDTH):
            rolled = pltpu.roll(u, k, 0)
            prev = pltpu.roll(up, k, 0)
            head = jnp.where(row8 < k, prev, rolled[:8])
            shifted = jnp.concatenate([head, rolled[8:]], axis=0)
            acc = acc + shifted * cw_ref[CONV_WIDTH - 1 - k:CONV_WIDTH - k, cs]
        y = acc * jax.nn.sigmoid(acc) * qs_ref[:, cs]
        qk_ref[:, cs] = y.astype(BF16)
    zvo = jnp.dot(xn, wvo_ref[...], preferred_element_type=F32)
    v_ref[...] = zvo[:, :ML_PAD].astype(BF16)
    so_ref[...] = jax.nn.sigmoid(zvo[:, ML_PAD:]).astype(BF16)
    mq_ref[...] = jnp.dot(xn, wmq_ref[...], preferred_element_type=F32).astype(BF16)

    def gate_act(z, is_input_gate):
        logsig = jnp.minimum(z, 0.0) - jnp.log(1.0 + jnp.exp(-jnp.abs(z)))
        return jnp.where(is_input_gate, z, logsig)

    zg = jnp.dot(xn, wg_ref[...], preferred_element_type=F32) + bias_ref[...]
    lane = lax.broadcasted_iota(jnp.int32, (tm, GATE_PAD), 1)
    gates_ref[...] = gate_act(zg, lane < ML_HEADS)
    zgt = lax.dot_general(wgt_ref[...], xn, (((1,), (1,)), ((), ())),
                          preferred_element_type=F32) + biast_ref[...]
    row = lax.broadcasted_iota(jnp.int32, (8, tm), 0)
    gatest_ref[...] = gate_act(zgt, row < ML_HEADS)


def _in0(h2d, ln, wqk, wvo, wmq, wg, wgt, cw, qscale, bias, biast, seq):
    t = h2d.shape[0]
    tm = TM_PROJ
    tiles_per_seq = seq // tm
    tok = lambda n, dt: jax.ShapeDtypeStruct((t, n), dt)
    row = lambda n: pl.BlockSpec((tm, n), lambda i: (i, 0))
    return pl.pallas_call(
        functools.partial(_in0_kernel, tiles_per_seq),
        out_shape=(tok(2 * ML_PAD, BF16), tok(ML_PAD, BF16), tok(ML_PAD, BF16), tok(MEM_WIDTH, BF16),
                   tok(GATE_PAD, F32), jax.ShapeDtypeStruct((8, t), F32)),
        grid=(t // tm,),
        in_specs=[row(D_MODEL),
                  pl.BlockSpec((8, D_MODEL), lambda i: (jnp.maximum(i * (tm // 8) - 1, 0), 0)),
                  _const_spec((1, D_MODEL)), _const_spec(wqk.shape), _const_spec(wvo.shape),
                  _const_spec(wmq.shape), _const_spec(wg.shape), _const_spec(wgt.shape),
                  _const_spec(cw.shape), _const_spec(qscale.shape), _const_spec(bias.shape),
                  _const_spec(biast.shape)],
        out_specs=(row(2 * ML_PAD), row(ML_PAD), row(ML_PAD), row(MEM_WIDTH), row(GATE_PAD),
                   pl.BlockSpec((8, tm), lambda i: (0, i))),
        compiler_params=_cparams(("parallel",)),
        name="in_proj0",
    )(h2d, h2d, ln, wqk, wvo, wmq, wg, wgt, cw, qscale, bias, biast)


def _mlstm_kernel(qk_ref, v_ref, so_ref, gates_ref, gatest_ref, hg_ref, out_ref, c_ref, m_ref):
    L = qk_ref.shape[0]

    @pl.when(pl.program_id(1) == 0)
    def _():
        c_ref[...] = jnp.zeros_like(c_ref)
        m_ref[...] = jnp.zeros_like(m_ref)

    gts = gates_ref[...]
    gtt = gatest_ref[...]
    r = lax.broadcasted_iota(jnp.int32, (L, L), 0)
    c = lax.broadcasted_iota(jnp.int32, (L, L), 1)
    causal = r >= c
    tri_lo = jnp.where(causal, 1.0, 0.0).astype(BF16)
    tri_up = jnp.where(r <= c, 1.0, 0.0).astype(BF16)
    hi = gts.astype(BF16)
    lo = (gts - hi.astype(F32)).astype(BF16)
    b_cols = (jnp.dot(tri_lo, hi, preferred_element_type=F32)
              + jnp.dot(tri_lo, lo, preferred_element_type=F32))
    b_rows = _split_dot(gtt, tri_up)
    lane = lax.broadcasted_iota(jnp.int32, (L, HEAD_PAD), 1)
    hg = hg_ref[...]
    for h in range(ML_HEADS):
        m_h = m_ref[h][0:1, 0:1]
        b_c = b_cols[:, ML_HEADS + h:ML_HEADS + h + 1]
        ig_c = gts[:, h:h + 1]
        b_r = b_rows[ML_HEADS + h:ML_HEADS + h + 1, :]
        ig_r = gtt[h:h + 1, :]
        c_r = b_r - ig_r
        log_d = jnp.where(causal, b_c - c_r, NEG)
        inter = b_c + m_h
        m_row = jnp.maximum(inter, jnp.max(log_d, axis=-1, keepdims=True))
        d = jnp.exp(log_d - m_row)
        s_inter = jnp.exp(inter - m_row)
        q_h = qk_ref[:, h * HEAD_PAD:(h + 1) * HEAD_PAD]
        k_h = qk_ref[:, ML_PAD + h * HEAD_PAD:ML_PAD + (h + 1) * HEAD_PAD]
        v_aug = jnp.where(lane == ML_HEAD_DIM, 1.0, v_ref[:, h * HEAD_PAD:(h + 1) * HEAD_PAD].astype(F32)).astype(BF16)
        s = lax.dot_general(q_h, k_h, (((1,), (1,)), ((), ())), preferred_element_type=F32)
        qkd = (s * d).astype(BF16)
        c_old = c_ref[h]
        num = (jnp.dot(qkd, v_aug, preferred_element_type=F32)
               + s_inter * jnp.dot(q_h, c_old.astype(BF16), preferred_element_type=F32))
        den = jnp.sum(jnp.where(lane == ML_HEAD_DIM, num, 0.0), axis=-1, keepdims=True)
        ht = num * (1.0 / jnp.maximum(jnp.abs(den), jnp.exp(-m_row)))
        ht = jnp.where(lane < ML_HEAD_DIM, ht, 0.0)
        ms = jnp.sum(ht * ht, axis=-1, keepdims=True) * (1.0 / ML_HEAD_DIM)
        hn = ht * lax.rsqrt(ms + EPS) * hg * so_ref[:, h * HEAD_PAD:(h + 1) * HEAD_PAD].astype(F32)
        out_ref[:, h * HEAD_PAD:(h + 1) * HEAD_PAD] = hn.astype(BF16)
        b_last = b_c[L - 1:L, :]
        g_r = b_last - c_r
        m_new = jnp.maximum(b_last + m_h, jnp.max(g_r, axis=-1, keepdims=True))
        w_c = jnp.exp(b_last - b_c + ig_c - m_new)
        decay = jnp.exp(b_last + m_h - m_new)
        kw = (k_h.astype(F32) * w_c).astype(BF16)
        upd = lax.dot_general(kw, v_aug, (((0,), (0,)), ((), ())), preferred_element_type=F32)
        c_ref[h] = decay * c_old + upd
        m_ref[h] = jnp.broadcast_to(m_new, m_ref.shape[1:])


def _mlstm(qk, v, so, gates, gatest, hgain, batch, seq):
    t = qk.shape[0]
    L = ML_CHUNK
    nc = seq // L
    row = lambda n: pl.BlockSpec((L, n), lambda b, c: (b * nc + c, 0))
    return pl.pallas_call(
        _mlstm_kernel,
        out_shape=jax.ShapeDtypeStruct((t, ML_PAD), BF16),
        grid=(batch, nc),
        in_specs=[row(2 * ML_PAD), row(ML_PAD), row(ML_PAD), row(GATE_PAD),
                  pl.BlockSpec((8, L), lambda b, c: (0, b * nc + c)),
                  _const_spec((1, HEAD_PAD))],
        out_specs=row(ML_PAD),
        scratch_shapes=[pltpu.VMEM((ML_HEADS, HEAD_PAD, HEAD_PAD), F32),
                        pltpu.VMEM((ML_HEADS, 8, LANE), F32)],
        compiler_params=_cparams(("parallel", "arbitrary")),
        name="mlstm",
    )(qk, v, so, gates, gatest, hgain)


def _out_kernel(main_ref, mq_ref, h_ref, k4_ref, v4_ref, gq_ref, wm_ref, wmem_ref, out_ref):
    q = mq_ref[...].astype(F32)
    bd = _block_diag_ones(MEM_WIDTH, MEM_HEAD_DIM)
    ms = _split_dot(q * q, bd) * (1.0 / MEM_HEAD_DIM)
    qn = (q * lax.rsqrt(ms + EPS) * gq_ref[...] * (MEM_HEAD_DIM ** -0.5)).astype(BF16)
    ymem = jnp.zeros(q.shape, F32)
    for h in range(MEM_HEADS):
        s = lax.dot_general(qn, k4_ref[h], (((1,), (1,)), ((), ())), preferred_element_type=F32)
        e = jnp.exp(s - jnp.max(s, axis=-1, keepdims=True))
        inv = 1.0 / jnp.sum(e, axis=-1, keepdims=True)
        ymem = ymem + jnp.dot(e.astype(BF16), v4_ref[h], preferred_element_type=F32) * inv
    y = (jnp.dot(main_ref[...], wm_ref[...], preferred_element_type=F32)
         + jnp.dot(ymem.astype(BF16), wmem_ref[...], preferred_element_type=F32))
    out_ref[...] = h_ref[...] + y


def _out_proj(main, mq, h2d, k4, v4, gq, wm, wmem, seq):
    t = h2d.shape[0]
    tm = TM_PROJ
    tiles_per_seq = seq // tm
    row = lambda n: pl.BlockSpec((tm, n), lambda i: (i, 0))
    memspec = pl.BlockSpec((MEM_HEADS, MEM_TOKENS, MEM_WIDTH), lambda i: (i // tiles_per_seq, 0, 0))
    return pl.pallas_call(
        _out_kernel,
        out_shape=jax.ShapeDtypeStruct((t, D_MODEL), F32),
        grid=(t // tm,),
        in_specs=[row(main.shape[1]), row(MEM_WIDTH), row(D_MODEL), memspec, memspec,
                  _const_spec((1, MEM_WIDTH)), _const_spec(wm.shape), _const_spec(wmem.shape)],
        out_specs=row(D_MODEL),
        compiler_params=_cparams(("parallel",)),
        name="out_proj",
    )(main, mq, h2d, k4, v4, gq, wm, wmem)


def _ffn_kernel(h_ref, g_ref, wg_ref, wu_ref, wd_ref, out_ref):
    x = h_ref[...]
    xn = _rms(x, g_ref[...]).astype(BF16)
    acc = x
    for c in range(D_FF // FF_CHUNK):
        cs = slice(c * FF_CHUNK, (c + 1) * FF_CHUNK)
        gt = jnp.dot(xn, wg_ref[:, cs], preferred_element_type=F32)
        up = jnp.dot(xn, wu_ref[:, cs], preferred_element_type=F32)
        a = (gt * jax.nn.sigmoid(gt) * up).astype(BF16)
        acc = acc + jnp.dot(a, wd_ref[cs, :], preferred_element_type=F32)
    out_ref[...] = acc


def _ffn(h2d, ln, wg, wu, wd):
    t = h2d.shape[0]
    tm = TM_FFN
    row = pl.BlockSpec((tm, D_MODEL), lambda i: (i, 0))
    res = lambda shape: pl.BlockSpec(shape, lambda i: (0, 0), pipeline_mode=pl.Buffered(1))
    return pl.pallas_call(
        _ffn_kernel,
        out_shape=jax.ShapeDtypeStruct((t, D_MODEL), F32),
        grid=(t // tm,),
        in_specs=[row, _const_spec((1, D_MODEL)), res(wg.shape), res(wu.shape), res(wd.shape)],
        out_specs=row,
        compiler_params=_cparams(("parallel",)),
        name="ffn_dense",
    )(h2d, ln, wg, wu, wd)


def _in1_kernel(h_ref, pos_ref, g_ref, w1_ref, gcq_ref, gckv_ref, wuq_ref, wuk_ref, wuvt_ref,
                gq_ref, gkn_ref, gkr_ref, invf_ref,
                q_ref, k_ref, vt_ref, mq_ref):
    tm = h_ref.shape[0]
    xn = _rms(h_ref[...], g_ref[...]).astype(BF16)
    z = jnp.dot(xn, w1_ref[...], preferred_element_type=F32)
    cq = z[:, :Q_LORA]
    ckv = z[:, Q_LORA:Q_LORA + KV_LORA]
    kr = z[:, Q_LORA + KV_LORA:Q_LORA + KV_LORA + LANE]
    mq_ref[...] = z[:, Q_LORA + KV_LORA + LANE:].astype(BF16)
    cqn = _rms(cq, gcq_ref[...]).astype(BF16)
    ckvn = _rms(ckv, gckv_ref[...]).astype(BF16)
    q = jnp.dot(cqn, wuq_ref[...], preferred_element_type=F32)
    kn = jnp.dot(ckvn, wuk_ref[...], preferred_element_type=F32)
    vt_ref[...] = lax.dot_general(wuvt_ref[...], ckvn, (((1,), (1,)), ((), ())),
                                  preferred_element_type=F32).astype(BF16)

    half = MLA_ROPE // 2
    ang = pos_ref[...].astype(F32) * invf_ref[...]
    lane = lax.broadcasted_iota(jnp.int32, (tm, LANE), 1)
    cos = jnp.where(lane < MLA_ROPE, jnp.cos(ang), 0.0)
    sin = jnp.sin(ang)
    sin_hi = jnp.where((lane >= half) & (lane < MLA_ROPE), sin, 0.0)
    sin_lo = jnp.where(lane < half, -sin, 0.0)

    def rope(t):
        return t * cos + pltpu.roll(t, half, 1) * sin_hi + pltpu.roll(t, LANE - half, 1) * sin_lo

    scale = MLA_QK ** -0.5 * LOG2E
    gq = gq_ref[...]
    for h in range(MLA_HEADS):
        qh = q[:, h * HEAD_PAD:(h + 1) * HEAD_PAD]
        rs = lax.rsqrt(jnp.sum(qh * qh, axis=-1, keepdims=True) * (1.0 / MLA_QK) + EPS) * scale
        qn = qh * rs * gq
        q_ref[:, h * HEAD_PAD:h * HEAD_PAD + LANE] = qn[:, :LANE].astype(BF16)
        q_ref[:, h * HEAD_PAD + LANE:(h + 1) * HEAD_PAD] = rope(qn[:, LANE:]).astype(BF16)
    ss_r = jnp.sum(kr * kr, axis=-1, keepdims=True)
    krr = rope(kr * gkr_ref[...])
    gkn = gkn_ref[...]
    for h in range(MLA_HEADS):
        kh = kn[:, h * MLA_NOPE:(h + 1) * MLA_NOPE]
        rs = lax.rsqrt((jnp.sum(kh * kh, axis=-1, keepdims=True) + ss_r) * (1.0 / MLA_QK) + EPS)
        k_ref[:, h * HEAD_PAD:h * HEAD_PAD + LANE] = (kh * rs * gkn).astype(BF16)
        k_ref[:, h * HEAD_PAD + LANE:(h + 1) * HEAD_PAD] = (krr * rs).astype(BF16)


def _in1(h2d, pos2d, ln, w1, gcq, gckv, wuq, wuk, wuvt, gq, gkn, gkr, invf):
    t = h2d.shape[0]
    tm = TM_PROJ
    tok = lambda n: jax.ShapeDtypeStruct((t, n), BF16)
    row = lambda n: pl.BlockSpec((tm, n), lambda i: (i, 0))
    consts = [ln, w1, gcq, gckv, wuq, wuk, wuvt, gq, gkn, gkr, invf]
    return pl.pallas_call(
        _in1_kernel,
        out_shape=(tok(MLA_PAD), tok(MLA_PAD), jax.ShapeDtypeStruct((MAIN_WIDTH, t), BF16), tok(MEM_WIDTH)),
        grid=(t // tm,),
        in_specs=[row(D_MODEL), row(1)] + [_const_spec(a.shape) for a in consts],
        out_specs=(row(MLA_PAD), row(MLA_PAD), pl.BlockSpec((MAIN_WIDTH, tm), lambda i: (0, i)), row(MEM_WIDTH)),
        compiler_params=_cparams(("parallel",)),
        name="in_proj1",
    )(h2d, pos2d, *consts)


def _attn_kernel(qi_ref, kj_ref, q_ref, k_ref, vt_ref, o_ref, m_ref, l_ref, acc_ref):
    p = pl.program_id(2)
    qi = qi_ref[p]
    kj = kj_ref[p]

    @pl.when(kj == 0)
    def _():
        m_ref[...] = jnp.full_like(m_ref, NEG)
        l_ref[...] = jnp.zeros_like(l_ref)
        acc_ref[...] = jnp.zeros_like(acc_ref)

    st = lax.dot_general(k_ref[...], q_ref[...], (((1,), (1,)), ((), ())), preferred_element_type=F32)

    def update(st):
        m_old = m_ref[...]
        m_new = jnp.maximum(m_old, jnp.max(st, axis=0, keepdims=True))
        alpha = jnp.exp2(m_old - m_new)
        e = jnp.exp2(st - m_new)
        l_ref[...] = alpha * l_ref[...] + jnp.sum(e, axis=0, keepdims=True)
        acc_ref[...] = alpha * acc_ref[...] + jnp.dot(vt_ref[...], e.astype(BF16), preferred_element_type=F32)
        m_ref[...] = m_new

    last = (kj + 1) * TK >= (qi + 1) * TQ
    crosses_diagonal = (kj + 1) * TK - 1 > qi * TQ

    @pl.when(jnp.logical_not(crosses_diagonal))
    def _():
        update(st)

    @pl.when(crosses_diagonal)
    def _():
        kpos = kj * TK + lax.broadcasted_iota(jnp.int32, st.shape, 0)
        qpos = qi * TQ + lax.broadcasted_iota(jnp.int32, st.shape, 1)
        update(jnp.where(kpos <= qpos, st, NEG))

    @pl.when(last)
    def _():
        o_ref[...] = (acc_ref[...] * (1.0 / l_ref[...])).T.astype(o_ref.dtype)


def _attention(q, k, v, batch, seq):
    t = q.shape[0]
    nq = seq // TQ
    nk_per_q = TQ // TK
    qi_l, kj_l = [], []
    for i in range(nq):
        for j in range((i + 1) * nk_per_q):
            qi_l.append(i)
            kj_l.append(j)
    qi = jnp.asarray(np.array(qi_l, np.int32))
    kj = jnp.asarray(np.array(kj_l, np.int32))
    npairs = len(qi_l)
    nkb = seq // TK
    grid_spec = pltpu.PrefetchScalarGridSpec(
        num_scalar_prefetch=2,
        grid=(batch, MLA_HEADS, npairs),
        in_specs=[pl.BlockSpec((TQ, HEAD_PAD), lambda b, h, p, qi, kj: (b * nq + qi[p], h)),
                  pl.BlockSpec((TK, HEAD_PAD), lambda b, h, p, qi, kj: (b * nkb + kj[p], h)),
                  pl.BlockSpec((MLA_V, TK), lambda b, h, p, qi, kj: (h, b * nkb + kj[p]))],
        out_specs=pl.BlockSpec((TQ, MLA_V), lambda b, h, p, qi, kj: (b * nq + qi[p], h)),
        scratch_shapes=[pltpu.VMEM((1, TQ), F32), pltpu.VMEM((1, TQ), F32), pltpu.VMEM((MLA_V, TQ), F32)],
    )
    return pl.pallas_call(
        _attn_kernel,
        out_shape=jax.ShapeDtypeStruct((t, MAIN_WIDTH), BF16),
        grid_spec=grid_spec,
        compiler_params=_cparams(("parallel", "parallel", "arbitrary")),
        name="mla_attention",
    )(qi, kj, q, k, v)


def _router_kernel(h_ref, g_ref, whi_ref, wlo_ref,
                   xn_ref, gates_ref, rank_ref, rankt_ref, starts_ref, counts_ref, run_ref):
    @pl.when(pl.program_id(0) == 0)
    def _():
        run_ref[...] = jnp.zeros_like(run_ref)

    tm = h_ref.shape[0]
    xn = _rms(h_ref[...], g_ref[...])
    hi = xn.astype(BF16)
    lo = (xn - hi.astype(F32)).astype(BF16)
    xn_ref[...] = hi
    whi = whi_ref[...]
    logits = (jnp.dot(hi, whi, preferred_element_type=F32) + jnp.dot(lo, whi, preferred_element_type=F32)
              + jnp.dot(hi, wlo_ref[...], preferred_element_type=F32))
    lane = lax.broadcasted_iota(jnp.int32, logits.shape, 1).astype(F32)
    logits = jnp.where(lane < N_EXPERTS, logits, NEG)
    m1 = jnp.max(logits, axis=-1, keepdims=True)
    i1 = jnp.min(jnp.where(logits == m1, lane, float(LANE)), axis=-1, keepdims=True)
    rest = jnp.where(lane == i1, NEG, logits)
    m2 = jnp.max(rest, axis=-1, keepdims=True)
    i2 = jnp.min(jnp.where(rest == m2, lane, float(LANE)), axis=-1, keepdims=True)
    e2 = jnp.exp(m2 - m1)
    w1 = 1.0 / (1.0 + e2)
    w2 = e2 * w1
    gates_ref[...] = jnp.where(lane == i1, w1, jnp.where(lane == i2, w2, 0.0))
    routed = (lane == i1) | (lane == i2)
    oh = jnp.where(routed, 1.0, 0.0)
    r = lax.broadcasted_iota(jnp.int32, (tm, tm), 0)
    c = lax.broadcasted_iota(jnp.int32, (tm, tm), 1)
    before = jnp.where(r > c, 1.0, 0.0).astype(BF16)
    prefix = jnp.dot(before, oh.astype(BF16), preferred_element_type=F32)
    run = run_ref[0:1, :]
    rank = jnp.where(routed, run + prefix, -1.0)
    rank_ref[...] = rank
    rankt_ref[...] = rank.T[:N_EXPERTS, :]
    starts_ref[...] = run_ref[...]
    total = jnp.broadcast_to(run + jnp.sum(oh, axis=0, keepdims=True), run_ref.shape)
    run_ref[...] = total
    counts_ref[...] = total


def _router(h2d, ln, whi, wlo):
    t = h2d.shape[0]
    tm = TM_PROJ
    nt = t // tm
    row = lambda n: pl.BlockSpec((tm, n), lambda i: (i, 0))
    return pl.pallas_call(
        _router_kernel,
        out_shape=(jax.ShapeDtypeStruct((t, D_MODEL), BF16), jax.ShapeDtypeStruct((t, LANE), F32),
                   jax.ShapeDtypeStruct((t, LANE), F32), jax.ShapeDtypeStruct((N_EXPERTS, t), F32),
                   jax.ShapeDtypeStruct((nt * 8, LANE), F32), jax.ShapeDtypeStruct((8, LANE), F32)),
        grid=(nt,),
        in_specs=[row(D_MODEL), _const_spec((1, D_MODEL)), _const_spec(whi.shape), _const_spec(wlo.shape)],
        out_specs=(row(D_MODEL), row(LANE), row(LANE), pl.BlockSpec((N_EXPERTS, tm), lambda i: (0, i)),
                   pl.BlockSpec((8, LANE), lambda i: (i, 0)), _const_spec((8, LANE))),
        scratch_shapes=[pltpu.VMEM((8, LANE), F32)],
        compiler_params=_cparams(("arbitrary",)),
        name="moe_router",
    )(h2d, ln, whi, wlo)


def _dispatch_kernel(ge_ref, gr0_ref, gclo_ref, gn_ref, xn_ref, rankt_ref, xs_ref, acc_ref):
    g = pl.program_id(0)
    e = ge_ref[g]
    clo = gclo_ref[g]
    sg = xs_ref.shape[0]
    want = (gr0_ref[g] + lax.broadcasted_iota(jnp.int32, (sg, TM_PROJ), 0)).astype(F32)
    acc_ref[...] = jnp.zeros_like(acc_ref)

    def body(k, carry):
        start = pl.multiple_of((clo + k) * TM_PROJ, TM_PROJ)
        ranks = rankt_ref[pl.ds(e, 1), pl.ds(start, TM_PROJ)]
        onehot = jnp.where(ranks == want, 1.0, 0.0).astype(BF16)
        acc_ref[...] += jnp.dot(onehot, xn_ref[pl.ds(start, TM_PROJ), :], preferred_element_type=F32)
        return carry

    lax.fori_loop(0, gn_ref[g], body, 0)
    xs_ref[...] = acc_ref[...].astype(BF16)


def _dispatch(xn, rankt, g_expert, g_rank0, g_chunk_lo, g_nchunks):
    t = xn.shape[0]
    n_groups = g_expert.shape[0]
    res = lambda shape: pl.BlockSpec(shape, lambda g, *_: (0, 0), pipeline_mode=pl.Buffered(1))
    grid_spec = pltpu.PrefetchScalarGridSpec(
        num_scalar_prefetch=4,
        grid=(n_groups,),
        in_specs=[res((t, D_MODEL)), res((N_EXPERTS, t))],
        out_specs=pl.BlockSpec((SLOT_GROUP, D_MODEL), lambda g, *_: (g, 0)),
        scratch_shapes=[pltpu.VMEM((SLOT_GROUP, D_MODEL), F32)],
    )
    return pl.pallas_call(
        _dispatch_kernel,
        out_shape=jax.ShapeDtypeStruct((n_groups * SLOT_GROUP, D_MODEL), BF16),
        grid_spec=grid_spec,
        compiler_params=_cparams(("arbitrary",)),
        name="moe_dispatch",
    )(g_expert, g_rank0, g_chunk_lo, g_nchunks, xn, rankt)


def _expert_kernel(te_ref, ta_ref, x_ref, wg_ref, wu_ref, wd_ref, out_ref, acc_ref):
    j = pl.program_id(0)
    c = pl.program_id(1)
    active = ta_ref[j] > 0

    @pl.when(active)
    def _():
        x = x_ref[...]
        gt = jnp.dot(x, wg_ref[0], preferred_element_type=F32)
        up = jnp.dot(x, wu_ref[0], preferred_element_type=F32)
        a = (gt * jax.nn.sigmoid(gt) * up).astype(BF16)
        y = jnp.dot(a, wd_ref[0], preferred_element_type=F32)

        @pl.when(c == 0)
        def _():
            acc_ref[...] = y

        @pl.when(c > 0)
        def _():
            acc_ref[...] += y

        @pl.when(c == pl.num_programs(1) - 1)
        def _():
            out_ref[...] = acc_ref[...].astype(out_ref.dtype)

    @pl.when(jnp.logical_not(active) & (c == pl.num_programs(1) - 1))
    def _():
        out_ref[...] = jnp.zeros_like(out_ref)


def _experts(x, tile_expert, tile_active, wg, wu, wd):
    n_tiles = tile_expert.shape[0]
    tm = TM_MOE
    xmap = lambda j, c, te, ta: (jnp.where(ta[j] > 0, j, 0), 0)
    cidx = lambda j, c, ta: jnp.where(ta[j] > 0, c, D_FF // FF_CHUNK - 1)
    grid_spec = pltpu.PrefetchScalarGridSpec(
        num_scalar_prefetch=2,
        grid=(n_tiles, D_FF // FF_CHUNK),
        in_specs=[pl.BlockSpec((tm, D_MODEL), xmap),
                  pl.BlockSpec((1, D_MODEL, FF_CHUNK), lambda j, c, te, ta: (te[j], 0, cidx(j, c, ta))),
                  pl.BlockSpec((1, D_MODEL, FF_CHUNK), lambda j, c, te, ta: (te[j], 0, cidx(j, c, ta))),
                  pl.BlockSpec((1, FF_CHUNK, D_MODEL), lambda j, c, te, ta: (te[j], cidx(j, c, ta), 0))],
        out_specs=pl.BlockSpec((tm, D_MODEL), lambda j, c, te, ta: (j, 0)),
        scratch_shapes=[pltpu.VMEM((tm, D_MODEL), F32)],
    )
    return pl.pallas_call(
        _expert_kernel,
        out_shape=jax.ShapeDtypeStruct((n_tiles * tm, D_MODEL), BF16),
        grid_spec=grid_spec,
        compiler_params=_cparams(("arbitrary", "arbitrary")),
        name="moe_experts",
    )(tile_expert, tile_active, x, wg, wu, wd)


def _combine_window_copy(ys_ref, buf_ref, sem_ref, start, slot, e):
    return pltpu.make_async_copy(ys_ref.at[pl.ds(pl.multiple_of(start, SLOT_ALIGN), COMBINE_WIN), :],
                                 buf_ref.at[slot, e], sem_ref.at[slot, e])


def _combine_kernel(ws_ref, np_ref, h_ref, gates_ref, rank_ref, base_ref, ys_ref, out_ref,
                    buf_ref, sem_ref, buf2_ref, sem2_ref):
    i = pl.program_id(0)
    n = pl.num_programs(0)
    tm = h_ref.shape[0]

    def start_tile(tile, slot):
        for e in range(N_EXPERTS):
            _combine_window_copy(ys_ref, buf_ref, sem_ref, ws_ref[tile * N_EXPERTS + e], slot, e).start()

    @pl.when(i == 0)
    def _():
        start_tile(0, 0)

    @pl.when(i + 1 < n)
    def _():
        start_tile(i + 1, (i + 1) % 2)

    slot = i % 2
    rank = rank_ref[...]
    slots = jnp.where(rank >= 0.0, rank + base_ref[...], -1.0)
    gates = gates_ref[...]
    lane_iota = lax.broadcasted_iota(jnp.int32, (tm, COMBINE_WIN), 1)
    y = h_ref[...]
    for e in range(N_EXPERTS):
        ws = ws_ref[i * N_EXPERTS + e]
        slot_e = slots[:, e:e + 1]
        _combine_window_copy(ys_ref, buf_ref, sem_ref, ws, slot, e).wait()
        onehot = jnp.where(slot_e == (ws + lane_iota).astype(F32), 1.0, 0.0).astype(BF16)
        z = jnp.dot(onehot, buf_ref[slot, e], preferred_element_type=F32)
        y = y + gates[:, e:e + 1] * z

        def second(ws=ws, slot_e=slot_e, e=e):
            cp = pltpu.make_async_copy(ys_ref.at[pl.ds(pl.multiple_of(ws + COMBINE_WIN, SLOT_ALIGN), COMBINE_WIN), :],
                                       buf2_ref, sem2_ref)
            cp.start()
            cp.wait()
            oh2 = jnp.where(slot_e == (ws + COMBINE_WIN + lane_iota).astype(F32), 1.0, 0.0).astype(BF16)
            return gates[:, e:e + 1] * jnp.dot(oh2, buf2_ref[...], preferred_element_type=F32)

        y = y + lax.cond(np_ref[i * N_EXPERTS + e] > 1, second, lambda: jnp.zeros((tm, D_MODEL), F32))
    out_ref[...] = y


def _combine(h2d, gates, rank, base_row, ys, win_start, win_pieces):
    t = h2d.shape[0]
    tm = TM_PROJ
    row = lambda n: pl.BlockSpec((tm, n), lambda i, *_: (i, 0))
    grid_spec = pltpu.PrefetchScalarGridSpec(
        num_scalar_prefetch=2,
        grid=(t // tm,),
        in_specs=[row(D_MODEL), row(LANE), row(LANE), pl.BlockSpec((1, LANE), lambda i, *_: (0, 0)),
                  pl.BlockSpec(memory_space=pl.ANY)],
        out_specs=row(D_MODEL),
        scratch_shapes=[pltpu.VMEM((2, N_EXPERTS, COMBINE_WIN, D_MODEL), BF16),
                        pltpu.SemaphoreType.DMA((2, N_EXPERTS)),
                        pltpu.VMEM((COMBINE_WIN, D_MODEL), BF16),
                        pltpu.SemaphoreType.DMA(())],
    )
    return pl.pallas_call(
        _combine_kernel,
        out_shape=jax.ShapeDtypeStruct((t, D_MODEL), F32),
        grid_spec=grid_spec,
        compiler_params=_cparams(("arbitrary",)),
        name="moe_combine",
    )(win_start, win_pieces, h2d, gates, rank, base_row, ys)


def _moe_plan(counts, starts, t):
    n_tok_tiles = t // TM_PROJ
    n_tiles = 2 * t // TM_MOE + N_EXPERTS + 1
    n_slots = n_tiles * TM_MOE
    counts = counts.astype(jnp.int32)
    cum = jnp.concatenate([starts.astype(jnp.int32), counts[None, :]], axis=0)
    padded = (counts + TM_MOE - 1) // TM_MOE * TM_MOE
    ends = jnp.cumsum(padded)
    base = ends - padded
    total = ends[-1]
    tile_first = jnp.arange(n_tiles, dtype=jnp.int32) * TM_MOE
    tile_expert = jnp.minimum(jnp.searchsorted(ends, tile_first, side="right"), N_EXPERTS - 1).astype(jnp.int32)
    tile_active = (tile_first < total).astype(jnp.int32)
    n_groups = n_slots // SLOT_GROUP
    g_first = jnp.arange(n_groups, dtype=jnp.int32) * SLOT_GROUP
    g_expert = tile_expert[g_first // TM_MOE]
    g_r0 = g_first - base[g_expert]
    cum_e = cum.T[g_expert]
    lo = jnp.sum(cum_e <= g_r0[:, None], axis=1) - 1
    hi = jnp.sum(cum_e < (g_r0 + SLOT_GROUP)[:, None], axis=1)
    g_valid = (g_first < total) & (g_r0 < counts[g_expert])
    lo = jnp.clip(lo, 0, n_tok_tiles - 1)
    hi = jnp.clip(hi, 0, n_tok_tiles)
    g_n = jnp.where(g_valid, jnp.maximum(hi - lo, 0), 0)
    first = base[None, :] + cum[:-1]
    cnt = cum[1:] - cum[:-1]
    ws = first // SLOT_ALIGN * SLOT_ALIGN
    pieces = jnp.where(cnt > 0, (first + cnt - ws + COMBINE_WIN - 1) // COMBINE_WIN, 0)
    return dict(tile_expert=tile_expert, tile_active=tile_active, g_expert=g_expert, g_r0=g_r0.astype(jnp.int32),
                g_lo=lo.astype(jnp.int32), g_n=g_n.astype(jnp.int32), base=base,
                ws=ws.reshape(-1).astype(jnp.int32), pieces=pieces.reshape(-1).astype(jnp.int32))


def _pad_heads_cols(w, heads, dim, pad):
    lead = w.shape[:-1]
    w = w.reshape(lead + (heads, dim))
    w = jnp.pad(w, [(0, 0)] * len(lead) + [(0, 0), (0, pad - dim)])
    return w.reshape(lead + (heads * pad,))


def _row(v):
    return v.reshape(1, -1).astype(F32)


def kernel(x, mem, positions, mem_norm, w_mem_kv, ln_mix0, w_in0, conv_w0, b_i0, b_f0, h_norm0, mq_norm0, mk_norm0, w_out0, ln_ffn0, w_gate0, w_up0, w_down0, ln_mix1, w_in1, cq_norm1, ckv_norm1, w_uq1, w_ukv1, q_norm1, k_norm1, mq_norm1, mk_norm1, w_out1, ln_ffn1, w_router1, we_gate1, we_up1, we_down1):
    batch, seq, _ = x.shape
    t = batch * seq
    h = x.reshape(t, D_MODEL)
    tile4 = lambda g: _row(jnp.tile(g, MEM_HEADS))

    k4_0, k4_1, v4 = _memkv(mem.reshape(batch * MEM_TOKENS, D_MODEL), _row(mem_norm), w_mem_kv.astype(BF16),
                            tile4(mk_norm0), tile4(mk_norm1), batch)

    mw = MAIN_WIDTH
    w_q, w_k, w_v, w_o = (w_in0[:, i * mw:(i + 1) * mw] for i in range(4))
    w_gates = w_in0[:, 4 * mw:4 * mw + 2 * ML_HEADS]
    w_mq = w_in0[:, 4 * mw + 2 * ML_HEADS:]
    padh = lambda w: _pad_heads_cols(w, ML_HEADS, ML_HEAD_DIM, HEAD_PAD)
    wqk = jnp.concatenate([padh(w_q), padh(w_k)], axis=1).astype(BF16)
    wvo = jnp.concatenate([padh(w_v), padh(w_o)], axis=1).astype(BF16)
    wg = jnp.pad(w_gates, ((0, 0), (0, GATE_PAD - 2 * ML_HEADS))).astype(BF16)
    wgt = w_gates.T.astype(BF16)
    cw = jnp.concatenate([padh(conv_w0[:, :mw]), padh(conv_w0[:, mw:])], axis=1).astype(F32)
    qscale = jnp.concatenate([jnp.full((1, ML_PAD), ML_HEAD_DIM ** -0.5, F32), jnp.ones((1, ML_PAD), F32)], axis=1)
    bias = jnp.concatenate([b_i0, b_f0]).astype(F32)
    bias_row = jnp.pad(bias, (0, GATE_PAD - 2 * ML_HEADS)).reshape(1, GATE_PAD)
    bias_col = bias.reshape(2 * ML_HEADS, 1)
    qk, v0, so, mq0, gates, gatest = _in0(h, _row(ln_mix0), wqk, wvo, w_mq.astype(BF16), wg, wgt, cw, qscale,
                                          bias_row, bias_col, seq)
    hgain = jnp.pad(h_norm0.astype(F32), (0, HEAD_PAD - ML_HEAD_DIM)).reshape(1, HEAD_PAD)
    hn = _mlstm(qk, v0, so, gates, gatest, hgain, batch, seq)
    wm0 = jnp.pad(w_out0[:mw].reshape(ML_HEADS, ML_HEAD_DIM, D_MODEL),
                  ((0, 0), (0, HEAD_PAD - ML_HEAD_DIM), (0, 0))).reshape(ML_PAD, D_MODEL).astype(BF16)
    h = _out_proj(hn, mq0, h, k4_0, v4, tile4(mq_norm0), wm0, w_out0[mw:].astype(BF16), seq)
    h = _ffn(h, _row(ln_ffn0), w_gate0.astype(BF16), w_up0.astype(BF16), w_down0.astype(BF16))

    w_cq = w_in1[:, :Q_LORA]
    w_ckv = w_in1[:, Q_LORA:Q_LORA + KV_LORA]
    w_kr = jnp.pad(w_in1[:, Q_LORA + KV_LORA:Q_LORA + KV_LORA + MLA_ROPE], ((0, 0), (0, LANE - MLA_ROPE)))
    w_mq1 = w_in1[:, Q_LORA + KV_LORA + MLA_ROPE:]
    w1 = jnp.concatenate([w_cq, w_ckv, w_kr, w_mq1], axis=1).astype(BF16)
    wuq = _pad_heads_cols(w_uq1, MLA_HEADS, MLA_QK, HEAD_PAD).astype(BF16)
    wukv = w_ukv1.reshape(KV_LORA, MLA_HEADS, MLA_NOPE + MLA_V)
    wuk = wukv[:, :, :MLA_NOPE].reshape(KV_LORA, MLA_HEADS * MLA_NOPE).astype(BF16)
    wuvt = wukv[:, :, MLA_NOPE:].reshape(KV_LORA, MLA_HEADS * MLA_V).T.astype(BF16)
    gq = jnp.pad(q_norm1.astype(F32), (0, HEAD_PAD - MLA_QK)).reshape(1, HEAD_PAD)
    gkn = _row(k_norm1[:MLA_NOPE])
    gkr = jnp.pad(k_norm1[MLA_NOPE:].astype(F32), (0, LANE - MLA_ROPE)).reshape(1, LANE)
    half = MLA_ROPE // 2
    inv_freq = ROPE_THETA ** (-jnp.arange(half, dtype=F32) / half)
    invf = jnp.concatenate([inv_freq, inv_freq, jnp.zeros((LANE - MLA_ROPE,), F32)]).reshape(1, LANE)
    q1, k1, v1, mq1 = _in1(h, positions.reshape(t, 1).astype(jnp.int32), _row(ln_mix1), w1, _row(cq_norm1),
                           _row(ckv_norm1), wuq, wuk, wuvt, gq, gkn, gkr, invf)
    o1 = _attention(q1, k1, v1, batch, seq)
    h = _out_proj(o1, mq1, h, k4_1, v4, tile4(mq_norm1), w_out1[:mw].astype(BF16), w_out1[mw:].astype(BF16), seq)

    wr = jnp.pad(w_router1.astype(F32), ((0, 0), (0, LANE - N_EXPERTS)))
    wr_hi = wr.astype(BF16)
    wr_lo = (wr - wr_hi.astype(F32)).astype(BF16)
    xn, gate_w, rank, rankt, starts, counts = _router(h, _row(ln_ffn1), wr_hi, wr_lo)
    plan = _moe_plan(counts[0, :N_EXPERTS], starts[::8, :N_EXPERTS], t)
    xs = _dispatch(xn, rankt, plan["g_expert"], plan["g_r0"], plan["g_lo"], plan["g_n"])
    ys = _experts(xs, plan["tile_expert"], plan["tile_active"],
                  we_gate1.astype(BF16), we_up1.astype(BF16), we_down1.astype(BF16))
    base_row = jnp.pad(plan["base"].astype(F32), (0, LANE - N_EXPERTS)).reshape(1, LANE)
    h = _combine(h, gate_w, rank, base_row, ys, plan["ws"], plan["pieces"])
    return h.reshape(batch, seq, D_MODEL)
```

```python
import functools

import jax
import jax.numpy as jnp
import numpy as np
from jax import lax
from jax.experimental import pallas as pl
from jax.experimental.pallas import tpu as pltpu

F32 = jnp.float32
BF16 = jnp.bfloat16

D_MODEL = 1024
MEM_TOKENS = 256
MEM_HEADS = 4
MEM_WIDTH = 256
MEM_HEAD_DIM = 64
MAIN_WIDTH = 768
ML_HEADS = 4
ML_HEAD_DIM = 192
CONV_WIDTH = 4
MLA_HEADS = 6
MLA_NOPE = 128
MLA_ROPE = 64
MLA_QK = 192
MLA_V = 128
Q_LORA = 384
KV_LORA = 128
ROPE_THETA = 10000.0
D_FF = 3584
N_EXPERTS = 8
EPS = 1e-6

LANE = 128
HEAD_PAD = 256
ML_PAD = ML_HEADS * HEAD_PAD
MLA_PAD = MLA_HEADS * HEAD_PAD
GATE_PAD = 128
ML_CHUNK = 256
TM_PROJ = 256
TM_FFN = 512
FF_CHUNK = 512
TM_MOE = 512
SLOT_GROUP = 256
SLOT_ALIGN = 16
COMBINE_WIN = 256
TQ = 2048
TK = 512
NEG = -1e30
LOG2E = 1.4426950408889634
VMEM_LIMIT = 56 * 1024 * 1024
EXPERT_VMEM_LIMIT = 62 * 1024 * 1024


def _cparams(sem, vmem=VMEM_LIMIT):
    return pltpu.CompilerParams(dimension_semantics=sem, vmem_limit_bytes=vmem)


def _rms(x, g):
    return x * lax.rsqrt(jnp.mean(x * x, axis=-1, keepdims=True) + EPS) * g


def _split_dot(x, w_bf16):
    hi = x.astype(BF16)
    lo = (x - hi.astype(F32)).astype(BF16)
    return (jnp.dot(hi, w_bf16, preferred_element_type=F32)
            + jnp.dot(lo, w_bf16, preferred_element_type=F32))


def _block_diag_ones(n, blk):
    r = lax.broadcasted_iota(jnp.int32, (n, n), 0) // blk
    c = lax.broadcasted_iota(jnp.int32, (n, n), 1) // blk
    return jnp.where(r == c, 1.0, 0.0).astype(BF16)


def _const_spec(shape):
    nd = len(shape)
    return pl.BlockSpec(shape, lambda *_: (0,) * nd)


def _memkv_kernel(mem_ref, g_ref, w_ref, gk0_ref, gk1_ref, k0_ref, k1_ref, v_ref):
    xn = _rms(mem_ref[...], g_ref[...]).astype(BF16)
    kv = jnp.dot(xn, w_ref[...], preferred_element_type=F32)
    k = kv[:, :MEM_WIDTH]
    v = kv[:, MEM_WIDTH:]
    bd = _block_diag_ones(MEM_WIDTH, MEM_HEAD_DIM)
    ms = _split_dot(k * k, bd) * (1.0 / MEM_HEAD_DIM)
    kn = k * lax.rsqrt(ms + EPS)
    lane_head = lax.broadcasted_iota(jnp.int32, (MEM_TOKENS, MEM_WIDTH), 1) // MEM_HEAD_DIM
    for h in range(MEM_HEADS):
        sel = lane_head == h
        k0_ref[h] = jnp.where(sel, kn * gk0_ref[...], 0.0).astype(BF16)
        k1_ref[h] = jnp.where(sel, kn * gk1_ref[...], 0.0).astype(BF16)
        v_ref[h] = jnp.where(sel, v, 0.0).astype(BF16)


def _memkv(mem2d, mem_norm, w_kv, gk0, gk1, batch):
    out = jax.ShapeDtypeStruct((batch * MEM_HEADS, MEM_TOKENS, MEM_WIDTH), BF16)
    hspec = pl.BlockSpec((MEM_HEADS, MEM_TOKENS, MEM_WIDTH), lambda b: (b, 0, 0))
    return pl.pallas_call(
        _memkv_kernel,
        out_shape=(out, out, out),
        grid=(batch,),
        in_specs=[pl.BlockSpec((MEM_TOKENS, D_MODEL), lambda b: (b, 0)),
                  _const_spec((1, D_MODEL)), _const_spec((D_MODEL, 2 * MEM_WIDTH)),
                  _const_spec((1, MEM_WIDTH)), _const_spec((1, MEM_WIDTH))],
        out_specs=(hspec, hspec, hspec),
        compiler_params=_cparams(("parallel",)),
        name="memkv",
    )(mem2d, mem_norm, w_kv, gk0, gk1)


def _in0_kernel(tiles_per_seq, h_ref, hp_ref, g_ref, wqk_ref, wvo_ref, wmq_ref, wg_ref, wgt_ref,
                cw_ref, qs_ref, bias_ref, biast_ref,
                qk_ref, v_ref, so_ref, mq_ref, gates_ref, gatest_ref):
    i = pl.program_id(0)
    tm = h_ref.shape[0]
    g = g_ref[...]
    xn = _rms(h_ref[...], g).astype(BF16)
    first = (i % tiles_per_seq) == 0
    xpn = _rms(hp_ref[...], g).astype(BF16)
    row8 = lax.broadcasted_iota(jnp.int32, (8, FF_CHUNK), 0)
    ncol = wqk_ref.shape[1]
    for c in range(ncol // FF_CHUNK):
        cs = slice(c * FF_CHUNK, (c + 1) * FF_CHUNK)
        w = wqk_ref[:, cs]
        u = jnp.dot(xn, w, preferred_element_type=F32)
        up = jnp.dot(xpn, w, preferred_element_type=F32)
        up = jnp.where(first, 0.0, up)
        acc = u * cw_ref[CONV_WIDTH - 1:CONV_WIDTH, cs]
        for k in range(1, CONV_WIDTH):
            rolled = pltpu.roll(u, k, 0)
            prev = pltpu.roll(up, k, 0)
            head = jnp.where(row8 < k, prev, rolled[:8])
            shifted = jnp.concatenate([head, rolled[8:]], axis=0)
            acc = acc + shifted * cw_ref[CONV_WIDTH - 1 - k:CONV_WIDTH - k, cs]
        y = acc * jax.nn.sigmoid(acc) * qs_ref[:, cs]
        qk_ref[:, cs] = y.astype(BF16)
    zvo = jnp.dot(xn, wvo_ref[...], preferred_element_type=F32)
    v_ref[...] = zvo[:, :ML_PAD].astype(BF16)
    so_ref[...] = jax.nn.sigmoid(zvo[:, ML_PAD:]).astype(BF16)
    mq_ref[...] = jnp.dot(xn, wmq_ref[...], preferred_element_type=F32).astype(BF16)

    def gate_act(z, is_input_gate):
        logsig = jnp.minimum(z, 0.0) - jnp.log(1.0 + jnp.exp(-jnp.abs(z)))
        return jnp.where(is_input_gate, z, logsig)

    zg = jnp.dot(xn, wg_ref[...], preferred_element_type=F32) + bias_ref[...]
    lane = lax.broadcasted_iota(jnp.int32, (tm, GATE_PAD), 1)
    gates_ref[...] = gate_act(zg, lane < ML_HEADS)
    zgt = lax.dot_general(wgt_ref[...], xn, (((1,), (1,)), ((), ())),
                          preferred_element_type=F32) + biast_ref[...]
    row = lax.broadcasted_iota(jnp.int32, (8, tm), 0)
    gatest_ref[...] = gate_act(zgt, row < ML_HEADS)


def _in0(h2d, ln, wqk, wvo, wmq, wg, wgt, cw, qscale, bias, biast, seq):
    t = h2d.shape[0]
    tm = TM_PROJ
    tiles_per_seq = seq // tm
    tok = lambda n, dt: jax.ShapeDtypeStruct((t, n), dt)
    row = lambda n: pl.BlockSpec((tm, n), lambda i: (i, 0))
    return pl.pallas_call(
        functools.partial(_in0_kernel, tiles_per_seq),
        out_shape=(tok(2 * ML_PAD, BF16), tok(ML_PAD, BF16), tok(ML_PAD, BF16), tok(MEM_WIDTH, BF16),
                   tok(GATE_PAD, F32), jax.ShapeDtypeStruct((8, t), F32)),
        grid=(t // tm,),
        in_specs=[row(D_MODEL),
                  pl.BlockSpec((8, D_MODEL), lambda i: (jnp.maximum(i * (tm // 8) - 1, 0), 0)),
                  _const_spec((1, D_MODEL)), _const_spec(wqk.shape), _const_spec(wvo.shape),
                  _const_spec(wmq.shape), _const_spec(wg.shape), _const_spec(wgt.shape),
                  _const_spec(cw.shape), _const_spec(qscale.shape), _const_spec(bias.shape),
                  _const_spec(biast.shape)],
        out_specs=(row(2 * ML_PAD), row(ML_PAD), row(ML_PAD), row(MEM_WIDTH), row(GATE_PAD),
                   pl.BlockSpec((8, tm), lambda i: (0, i))),
        compiler_params=_cparams(("parallel",)),
        name="in_proj0",
    )(h2d, h2d, ln, wqk, wvo, wmq, wg, wgt, cw, qscale, bias, biast)


def _mlstm_kernel(qk_ref, v_ref, so_ref, gates_ref, gatest_ref, hg_ref, out_ref, c_ref, m_ref):
    L = qk_ref.shape[0]

    @pl.when(pl.program_id(1) == 0)
    def _():
        c_ref[...] = jnp.zeros_like(c_ref)
        m_ref[...] = jnp.zeros_like(m_ref)

    gts = gates_ref[...]
    gtt = gatest_ref[...]
    r = lax.broadcasted_iota(jnp.int32, (L, L), 0)
    c = lax.broadcasted_iota(jnp.int32, (L, L), 1)
    causal = r >= c
    tri_lo = jnp.where(causal, 1.0, 0.0).astype(BF16)
    tri_up = jnp.where(r <= c, 1.0, 0.0).astype(BF16)
    hi = gts.astype(BF16)
    lo = (gts - hi.astype(F32)).astype(BF16)
    b_cols = (jnp.dot(tri_lo, hi, preferred_element_type=F32)
              + jnp.dot(tri_lo, lo, preferred_element_type=F32))
    b_rows = _split_dot(gtt, tri_up)
    lane = lax.broadcasted_iota(jnp.int32, (L, HEAD_PAD), 1)
    hg = hg_ref[...]
    for h in range(ML_HEADS):
        m_h = m_ref[h][0:1, 0:1]
        b_c = b_cols[:, ML_HEADS + h:ML_HEADS + h + 1]
        ig_c = gts[:, h:h + 1]
        b_r = b_rows[ML_HEADS + h:ML_HEADS + h + 1, :]
        ig_r = gtt[h:h + 1, :]
        c_r = b_r - ig_r
        log_d = jnp.where(causal, b_c - c_r, NEG)
        inter = b_c + m_h
        m_row = jnp.maximum(inter, jnp.max(log_d, axis=-1, keepdims=True))
        d = jnp.exp(log_d - m_row)
        s_inter = jnp.exp(inter - m_row)
        q_h = qk_ref[:, h * HEAD_PAD:(h + 1) * HEAD_PAD]
        k_h = qk_ref[:, ML_PAD + h * HEAD_PAD:ML_PAD + (h + 1) * HEAD_PAD]
        v_aug = jnp.where(lane == ML_HEAD_DIM, 1.0, v_ref[:, h * HEAD_PAD:(h + 1) * HEAD_PAD].astype(F32)).astype(BF16)
        s = lax.dot_general(q_h, k_h, (((1,), (1,)), ((), ())), preferred_element_type=F32)
        qkd = (s * d).astype(BF16)
        c_old = c_ref[h]
        num = (jnp.dot(qkd, v_aug, preferred_element_type=F32)
               + s_inter * jnp.dot(q_h, c_old.astype(BF16), preferred_element_type=F32))
        den = jnp.sum(jnp.where(lane == ML_HEAD_DIM, num, 0.0), axis=-1, keepdims=True)
        ht = num * (1.0 / jnp.maximum(jnp.abs(den), jnp.exp(-m_row)))
        ht = jnp.where(lane < ML_HEAD_DIM, ht, 0.0)
        ms = jnp.sum(ht * ht, axis=-1, keepdims=True) * (1.0 / ML_HEAD_DIM)
        hn = ht * lax.rsqrt(ms + EPS) * hg * so_ref[:, h * HEAD_PAD:(h + 1) * HEAD_PAD].astype(F32)
        out_ref[:, h * HEAD_PAD:(h + 1) * HEAD_PAD] = hn.astype(BF16)
        b_last = b_c[L - 1:L, :]
        g_r = b_last - c_r
        m_new = jnp.maximum(b_last + m_h, jnp.max(g_r, axis=-1, keepdims=True))
        w_c = jnp.exp(b_last - b_c + ig_c - m_new)
        decay = jnp.exp(b_last + m_h - m_new)
        kw = (k_h.astype(F32) * w_c).astype(BF16)
        upd = lax.dot_general(kw, v_aug, (((0,), (0,)), ((), ())), preferred_element_type=F32)
        c_ref[h] = decay * c_old + upd
        m_ref[h] = jnp.broadcast_to(m_new, m_ref.shape[1:])


def _mlstm(qk, v, so, gates, gatest, hgain, batch, seq):
    t = qk.shape[0]
    L = ML_CHUNK
    nc = seq // L
    row = lambda n: pl.BlockSpec((L, n), lambda b, c: (b * nc + c, 0))
    return pl.pallas_call(
        _mlstm_kernel,
        out_shape=jax.ShapeDtypeStruct((t, ML_PAD), BF16),
        grid=(batch, nc),
        in_specs=[row(2 * ML_PAD), row(ML_PAD), row(ML_PAD), row(GATE_PAD),
                  pl.BlockSpec((8, L), lambda b, c: (0, b * nc + c)),
                  _const_spec((1, HEAD_PAD))],
        out_specs=row(ML_PAD),
        scratch_shapes=[pltpu.VMEM((ML_HEADS, HEAD_PAD, HEAD_PAD), F32),
                        pltpu.VMEM((ML_HEADS, 8, LANE), F32)],
        compiler_params=_cparams(("parallel", "arbitrary")),
        name="mlstm",
    )(qk, v, so, gates, gatest, hgain)


def _out_kernel(main_ref, mq_ref, h_ref, k4_ref, v4_ref, gq_ref, wm_ref, wmem_ref, out_ref):
    q = mq_ref[...].astype(F32)
    bd = _block_diag_ones(MEM_WIDTH, MEM_HEAD_DIM)
    ms = _split_dot(q * q, bd) * (1.0 / MEM_HEAD_DIM)
    qn = (q * lax.rsqrt(ms + EPS) * gq_ref[...] * (MEM_HEAD_DIM ** -0.5)).astype(BF16)
    ymem = jnp.zeros(q.shape, F32)
    for h in range(MEM_HEADS):
        s = lax.dot_general(qn, k4_ref[h], (((1,), (1,)), ((), ())), preferred_element_type=F32)
        e = jnp.exp(s - jnp.max(s, axis=-1, keepdims=True))
        inv = 1.0 / jnp.sum(e, axis=-1, keepdims=True)
        ymem = ymem + jnp.dot(e.astype(BF16), v4_ref[h], preferred_element_type=F32) * inv
    y = (jnp.dot(main_ref[...], wm_ref[...], preferred_element_type=F32)
         + jnp.dot(ymem.astype(BF16), wmem_ref[...], preferred_element_type=F32))
    out_ref[...] = h_ref[...] + y


def _out_proj(main, mq, h2d, k4, v4, gq, wm, wmem, seq):
    t = h2d.shape[0]
    tm = TM_PROJ
    tiles_per_seq = seq // tm
    row = lambda n: pl.BlockSpec((tm, n), lambda i: (i, 0))
    memspec = pl.BlockSpec((MEM_HEADS, MEM_TOKENS, MEM_WIDTH), lambda i: (i // tiles_per_seq, 0, 0))
    return pl.pallas_call(
        _out_kernel,
        out_shape=jax.ShapeDtypeStruct((t, D_MODEL), F32),
        grid=(t // tm,),
        in_specs=[row(main.shape[1]), row(MEM_WIDTH), row(D_MODEL), memspec, memspec,
                  _const_spec((1, MEM_WIDTH)), _const_spec(wm.shape), _const_spec(wmem.shape)],
        out_specs=row(D_MODEL),
        compiler_params=_cparams(("parallel",)),
        name="out_proj",
    )(main, mq, h2d, k4, v4, gq, wm, wmem)


def _ffn_kernel(h_ref, g_ref, wg_ref, wu_ref, wd_ref, out_ref):
    x = h_ref[...]
    xn = _rms(x, g_ref[...]).astype(BF16)
    acc = x
    for c in range(D_FF // FF_CHUNK):
        cs = slice(c * FF_CHUNK, (c + 1) * FF_CHUNK)
        gt = jnp.dot(xn, wg_ref[:, cs], preferred_element_type=F32)
        up = jnp.dot(xn, wu_ref[:, cs], preferred_element_type=F32)
        a = (gt * jax.nn.sigmoid(gt) * up).astype(BF16)
        acc = acc + jnp.dot(a, wd_ref[cs, :], preferred_element_type=F32)
    out_ref[...] = acc


def _ffn(h2d, ln, wg, wu, wd):
    t = h2d.shape[0]
    tm = TM_FFN
    row = pl.BlockSpec((tm, D_MODEL), lambda i: (i, 0))
    res = lambda shape: pl.BlockSpec(shape, lambda i: (0, 0), pipeline_mode=pl.Buffered(1))
    return pl.pallas_call(
        _ffn_kernel,
        out_shape=jax.ShapeDtypeStruct((t, D_MODEL), F32),
        grid=(t // tm,),
        in_specs=[row, _const_spec((1, D_MODEL)), res(wg.shape), res(wu.shape), res(wd.shape)],
        out_specs=row,
        compiler_params=_cparams(("parallel",)),
        name="ffn_dense",
    )(h2d, ln, wg, wu, wd)


def _in1_kernel(h_ref, pos_ref, g_ref, w1_ref, gcq_ref, gckv_ref, wuq_ref, wuk_ref, wuvt_ref,
                gq_ref, gkn_ref, gkr_ref, invf_ref,
                q_ref, k_ref, vt_ref, mq_ref):
    tm = h_ref.shape[0]
    xn = _rms(h_ref[...], g_ref[...]).astype(BF16)
    z = jnp.dot(xn, w1_ref[...], preferred_element_type=F32)
    cq = z[:, :Q_LORA]
    ckv = z[:, Q_LORA:Q_LORA + KV_LORA]
    kr = z[:, Q_LORA + KV_LORA:Q_LORA + KV_LORA + LANE]
    mq_ref[...] = z[:, Q_LORA + KV_LORA + LANE:].astype(BF16)
    cqn = _rms(cq, gcq_ref[...]).astype(BF16)
    ckvn = _rms(ckv, gckv_ref[...]).astype(BF16)
    q = jnp.dot(cqn, wuq_ref[...], preferred_element_type=F32)
    kn = jnp.dot(ckvn, wuk_ref[...], preferred_element_type=F32)
    vt_ref[...] = lax.dot_general(wuvt_ref[...], ckvn, (((1,), (1,)), ((), ())),
                                  preferred_element_type=F32).astype(BF16)

    half = MLA_ROPE // 2
    ang = pos_ref[...].astype(F32) * invf_ref[...]
    lane = lax.broadcasted_iota(jnp.int32, (tm, LANE), 1)
    cos = jnp.where(lane < MLA_ROPE, jnp.cos(ang), 0.0)
    sin = jnp.sin(ang)
    sin_hi = jnp.where((lane >= half) & (lane < MLA_ROPE), sin, 0.0)
    sin_lo = jnp.where(lane < half, -sin, 0.0)

    def rope(t):
        return t * cos + pltpu.roll(t, half, 1) * sin_hi + pltpu.roll(t, LANE - half, 1) * sin_lo

    scale = MLA_QK ** -0.5 * LOG2E
    gq = gq_ref[...]
    for h in range(MLA_HEADS):
        qh = q[:, h * HEAD_PAD:(h + 1) * HEAD_PAD]
        rs = lax.rsqrt(jnp.sum(qh * qh, axis=-1, keepdims=True) * (1.0 / MLA_QK) + EPS) * scale
        qn = qh * rs * gq
        q_ref[:, h * HEAD_PAD:h * HEAD_PAD + LANE] = qn[:, :LANE].astype(BF16)
        q_ref[:, h * HEAD_PAD + LANE:(h + 1) * HEAD_PAD] = rope(qn[:, LANE:]).astype(BF16)
    ss_r = jnp.sum(kr * kr, axis=-1, keepdims=True)
    krr = rope(kr * gkr_ref[...])
    gkn = gkn_ref[...]
    for h in range(MLA_HEADS):
        kh = kn[:, h * MLA_NOPE:(h + 1) * MLA_NOPE]
        rs = lax.rsqrt((jnp.sum(kh * kh, axis=-1, keepdims=True) + ss_r) * (1.0 / MLA_QK) + EPS)
        k_ref[:, h * HEAD_PAD:h * HEAD_PAD + LANE] = (kh * rs * gkn).astype(BF16)
        k_ref[:, h * HEAD_PAD + LANE:(h + 1) * HEAD_PAD] = (krr * rs).astype(BF16)


def _in1(h2d, pos2d, ln, w1, gcq, gckv, wuq, wuk, wuvt, gq, gkn, gkr, invf):
    t = h2d.shape[0]
    tm = TM_PROJ
    tok = lambda n: jax.ShapeDtypeStruct((t, n), BF16)
    row = lambda n: pl.BlockSpec((tm, n), lambda i: (i, 0))
    consts = [ln, w1, gcq, gckv, wuq, wuk, wuvt, gq, gkn, gkr, invf]
    return pl.pallas_call(
        _in1_kernel,
        out_shape=(tok(MLA_PAD), tok(MLA_PAD), jax.ShapeDtypeStruct((MAIN_WIDTH, t), BF16), tok(MEM_WIDTH)),
        grid=(t // tm,),
        in_specs=[row(D_MODEL), row(1)] + [_const_spec(a.shape) for a in consts],
        out_specs=(row(MLA_PAD), row(MLA_PAD), pl.BlockSpec((MAIN_WIDTH, tm), lambda i: (0, i)), row(MEM_WIDTH)),
        compiler_params=_cparams(("parallel",)),
        name="in_proj1",
    )(h2d, pos2d, *consts)


def _attn_kernel(q_ref, k_ref, vt_ref, o_ref, m_ref, l_ref, acc_ref, st_ref, mt_ref):
    qi = pl.program_id(2)
    m_ref[...] = jnp.full_like(m_ref, NEG)
    l_ref[...] = jnp.zeros_like(l_ref)
    acc_ref[...] = jnp.zeros_like(acc_ref)
    n_diag = TQ // TK
    n_full = qi * n_diag
    nt_dims = (((1,), (1,)), ((), ()))

    def scores(j, slot):
        koff = pl.multiple_of(j * TK, TK)
        st = lax.dot_general(k_ref[pl.ds(koff, TK), :], q_ref[...], nt_dims, preferred_element_type=F32)
        st_ref[slot] = st
        mt_ref[slot] = jnp.max(st, axis=0, keepdims=True)

    def absorb(st, m_tile, vt, cols):
        m_old = m_ref[:, cols]
        m_new = jnp.maximum(m_old, m_tile)
        alpha = jnp.exp2(m_old - m_new)
        e = jnp.exp2(st - m_new)
        l_ref[:, cols] = alpha * l_ref[:, cols] + jnp.sum(e, axis=0, keepdims=True)
        acc_ref[:, cols] = alpha * acc_ref[:, cols] + jnp.dot(vt, e.astype(BF16), preferred_element_type=F32)
        m_ref[:, cols] = m_new

    def finish(j, slot):
        koff = pl.multiple_of(j * TK, TK)
        absorb(st_ref[slot], mt_ref[slot], vt_ref[:, pl.ds(koff, TK)], slice(None))

    @pl.when(n_full > 0)
    def _():
        scores(0, 0)

        def body(j, carry):
            finish(j, j % 2)
            scores(j + 1, (j + 1) % 2)
            return carry

        lax.fori_loop(0, n_full - 1, body, 0)
        finish(n_full - 1, (n_full - 1) % 2)

    for d in range(n_diag):
        koff = pl.multiple_of((n_full + d) * TK, TK)
        k = k_ref[pl.ds(koff, TK), :]
        vt = vt_ref[:, pl.ds(koff, TK)]
        for u in range(d, n_diag):
            cols = slice(u * TK, (u + 1) * TK)
            st = lax.dot_general(k, q_ref[cols, :], nt_dims, preferred_element_type=F32)
            if u == d:
                kpos = lax.broadcasted_iota(jnp.int32, st.shape, 0)
                qpos = lax.broadcasted_iota(jnp.int32, st.shape, 1)
                st = jnp.where(kpos <= qpos, st, NEG)
            absorb(st, jnp.max(st, axis=0, keepdims=True), vt, cols)
    o_ref[...] = (acc_ref[...] * (1.0 / l_ref[...])).T.astype(o_ref.dtype)


def _attention(q, k, v, batch, seq):
    t = q.shape[0]
    nq = seq // TQ
    return pl.pallas_call(
        _attn_kernel,
        out_shape=jax.ShapeDtypeStruct((t, MAIN_WIDTH), BF16),
        grid=(batch, MLA_HEADS, nq),
        in_specs=[pl.BlockSpec((TQ, HEAD_PAD), lambda b, h, i: (b * nq + i, h)),
                  pl.BlockSpec((seq, HEAD_PAD), lambda b, h, i: (b, h)),
                  pl.BlockSpec((MLA_V, seq), lambda b, h, i: (h, b))],
        out_specs=pl.BlockSpec((TQ, MLA_V), lambda b, h, i: (b * nq + i, h)),
        scratch_shapes=[pltpu.VMEM((1, TQ), F32), pltpu.VMEM((1, TQ), F32), pltpu.VMEM((MLA_V, TQ), F32),
                        pltpu.VMEM((2, TK, TQ), F32), pltpu.VMEM((2, 1, TQ), F32)],
        compiler_params=_cparams(("parallel", "parallel", "arbitrary")),
        name="mla_attention",
    )(q, k, v)


def _router_kernel(h_ref, g_ref, whi_ref, wlo_ref,
                   xn_ref, gates_ref, rank_ref, rankt_ref, starts_ref, counts_ref, run_ref):
    @pl.when(pl.program_id(0) == 0)
    def _():
        run_ref[...] = jnp.zeros_like(run_ref)

    tm = h_ref.shape[0]
    xn = _rms(h_ref[...], g_ref[...])
    hi = xn.astype(BF16)
    lo = (xn - hi.astype(F32)).astype(BF16)
    xn_ref[...] = hi
    whi = whi_ref[...]
    logits = (jnp.dot(hi, whi, preferred_element_type=F32) + jnp.dot(lo, whi, preferred_element_type=F32)
              + jnp.dot(hi, wlo_ref[...], preferred_element_type=F32))
    lane = lax.broadcasted_iota(jnp.int32, logits.shape, 1).astype(F32)
    logits = jnp.where(lane < N_EXPERTS, logits, NEG)
    m1 = jnp.max(logits, axis=-1, keepdims=True)
    i1 = jnp.min(jnp.where(logits == m1, lane, float(LANE)), axis=-1, keepdims=True)
    rest = jnp.where(lane == i1, NEG, logits)
    m2 = jnp.max(rest, axis=-1, keepdims=True)
    i2 = jnp.min(jnp.where(rest == m2, lane, float(LANE)), axis=-1, keepdims=True)
    e2 = jnp.exp(m2 - m1)
    w1 = 1.0 / (1.0 + e2)
    w2 = e2 * w1
    gates_ref[...] = jnp.where(lane == i1, w1, jnp.where(lane == i2, w2, 0.0))
    routed = (lane == i1) | (lane == i2)
    oh = jnp.where(routed, 1.0, 0.0)
    r = lax.broadcasted_iota(jnp.int32, (tm, tm), 0)
    c = lax.broadcasted_iota(jnp.int32, (tm, tm), 1)
    before = jnp.where(r > c, 1.0, 0.0).astype(BF16)
    prefix = jnp.dot(before, oh.astype(BF16), preferred_element_type=F32)
    run = run_ref[0:1, :]
    rank = jnp.where(routed, run + prefix, -1.0)
    rank_ref[...] = rank
    rankt_ref[...] = rank.T[:N_EXPERTS, :]
    starts_ref[...] = run_ref[...]
    total = jnp.broadcast_to(run + jnp.sum(oh, axis=0, keepdims=True), run_ref.shape)
    run_ref[...] = total
    counts_ref[...] = total


def _router(h2d, ln, whi, wlo):
    t = h2d.shape[0]
    tm = TM_PROJ
    nt = t // tm
    row = lambda n: pl.BlockSpec((tm, n), lambda i: (i, 0))
    return pl.pallas_call(
        _router_kernel,
        out_shape=(jax.ShapeDtypeStruct((t, D_MODEL), BF16), jax.ShapeDtypeStruct((t, LANE), F32),
                   jax.ShapeDtypeStruct((t, LANE), F32), jax.ShapeDtypeStruct((N_EXPERTS, t), F32),
                   jax.ShapeDtypeStruct((nt * 8, LANE), F32), jax.ShapeDtypeStruct((8, LANE), F32)),
        grid=(nt,),
        in_specs=[row(D_MODEL), _const_spec((1, D_MODEL)), _const_spec(whi.shape), _const_spec(wlo.shape)],
        out_specs=(row(D_MODEL), row(LANE), row(LANE), pl.BlockSpec((N_EXPERTS, tm), lambda i: (0, i)),
                   pl.BlockSpec((8, LANE), lambda i: (i, 0)), _const_spec((8, LANE))),
        scratch_shapes=[pltpu.VMEM((8, LANE), F32)],
        compiler_params=_cparams(("arbitrary",)),
        name="moe_router",
    )(h2d, ln, whi, wlo)


def _dispatch_kernel(ge_ref, gr0_ref, gclo_ref, gn_ref, xn_ref, rankt_ref, xs_ref, acc_ref):
    g = pl.program_id(0)
    e = ge_ref[g]
    clo = gclo_ref[g]
    sg = xs_ref.shape[0]
    want = (gr0_ref[g] + lax.broadcasted_iota(jnp.int32, (sg, TM_PROJ), 0)).astype(F32)
    acc_ref[...] = jnp.zeros_like(acc_ref)

    def body(k, carry):
        start = pl.multiple_of((clo + k) * TM_PROJ, TM_PROJ)
        ranks = rankt_ref[pl.ds(e, 1), pl.ds(start, TM_PROJ)]
        onehot = jnp.where(ranks == want, 1.0, 0.0).astype(BF16)
        acc_ref[...] += jnp.dot(onehot, xn_ref[pl.ds(start, TM_PROJ), :], preferred_element_type=F32)
        return carry

    lax.fori_loop(0, gn_ref[g], body, 0)
    xs_ref[...] = acc_ref[...].astype(BF16)


def _dispatch(xn, rankt, g_expert, g_rank0, g_chunk_lo, g_nchunks):
    t = xn.shape[0]
    n_groups = g_expert.shape[0]
    res = lambda shape: pl.BlockSpec(shape, lambda g, *_: (0, 0), pipeline_mode=pl.Buffered(1))
    grid_spec = pltpu.PrefetchScalarGridSpec(
        num_scalar_prefetch=4,
        grid=(n_groups,),
        in_specs=[res((t, D_MODEL)), res((N_EXPERTS, t))],
        out_specs=pl.BlockSpec((SLOT_GROUP, D_MODEL), lambda g, *_: (g, 0)),
        scratch_shapes=[pltpu.VMEM((SLOT_GROUP, D_MODEL), F32)],
    )
    return pl.pallas_call(
        _dispatch_kernel,
        out_shape=jax.ShapeDtypeStruct((n_groups * SLOT_GROUP, D_MODEL), BF16),
        grid_spec=grid_spec,
        compiler_params=_cparams(("arbitrary",)),
        name="moe_dispatch",
    )(g_expert, g_rank0, g_chunk_lo, g_nchunks, xn, rankt)


def _expert_kernel(te_ref, ta_ref, x_ref, wg_ref, wu_ref, wd_ref, out_ref):
    active = ta_ref[pl.program_id(0)] > 0

    @pl.when(active)
    def _():
        x = x_ref[...]
        acc = None
        for c in range(D_FF // FF_CHUNK):
            cs = slice(c * FF_CHUNK, (c + 1) * FF_CHUNK)
            gt = jnp.dot(x, wg_ref[0, :, cs], preferred_element_type=F32)
            up = jnp.dot(x, wu_ref[0, :, cs], preferred_element_type=F32)
            a = (gt * jax.nn.sigmoid(gt) * up).astype(BF16)
            y = jnp.dot(a, wd_ref[0, cs, :], preferred_element_type=F32)
            acc = y if acc is None else acc + y
        out_ref[...] = acc.astype(out_ref.dtype)

    @pl.when(jnp.logical_not(active))
    def _():
        out_ref[...] = jnp.zeros_like(out_ref)


def _experts(x, tile_expert, tile_active, wg, wu, wd):
    n_tiles = tile_expert.shape[0]
    tm = TM_MOE
    wmap = lambda j, te, ta: (te[j], 0, 0)
    grid_spec = pltpu.PrefetchScalarGridSpec(
        num_scalar_prefetch=2,
        grid=(n_tiles,),
        in_specs=[pl.BlockSpec((tm, D_MODEL), lambda j, te, ta: (jnp.where(ta[j] > 0, j, 0), 0)),
                  pl.BlockSpec((1, D_MODEL, D_FF), wmap),
                  pl.BlockSpec((1, D_MODEL, D_FF), wmap),
                  pl.BlockSpec((1, D_FF, D_MODEL), wmap)],
        out_specs=pl.BlockSpec((tm, D_MODEL), lambda j, te, ta: (j, 0)),
    )
    return pl.pallas_call(
        _expert_kernel,
        out_shape=jax.ShapeDtypeStruct((n_tiles * tm, D_MODEL), BF16),
        grid_spec=grid_spec,
        compiler_params=_cparams(("arbitrary",), EXPERT_VMEM_LIMIT),
        name="moe_experts",
    )(tile_expert, tile_active, x, wg, wu, wd)


def _combine_window_copies(ys_ref, buf_ref, sem_ref, ws_ref, tile, buf_slot, row_offset):
    copies = []
    for e in range(N_EXPERTS):
        start = pl.multiple_of(ws_ref[tile * N_EXPERTS + e] + row_offset, SLOT_ALIGN)
        copies.append(pltpu.make_async_copy(ys_ref.at[pl.ds(start, COMBINE_WIN), :],
                                            buf_ref.at[buf_slot, pl.ds(e * COMBINE_WIN, COMBINE_WIN), :],
                                            sem_ref.at[buf_slot, e]))
    return copies


def _combine_kernel(ws_ref, second_ref, h_ref, gates_ref, rank_ref, base_ref, ys_ref, out_ref, buf_ref, sem_ref):
    i = pl.program_id(0)
    n = pl.num_programs(0)
    tm = h_ref.shape[0]

    @pl.when(i == 0)
    def _():
        for cp in _combine_window_copies(ys_ref, buf_ref, sem_ref, ws_ref, 0, 0, 0):
            cp.start()

    @pl.when(i + 1 < n)
    def _():
        for cp in _combine_window_copies(ys_ref, buf_ref, sem_ref, ws_ref, i + 1, (i + 1) % 2, 0):
            cp.start()

    rank = rank_ref[...]
    slots = jnp.where(rank >= 0.0, rank + base_ref[...], -1.0)
    gates = gates_ref[...]
    lane_iota = lax.broadcasted_iota(jnp.int32, (tm, COMBINE_WIN), 1)

    def gathered(buf_slot, row_offset):
        hi_parts, lo_parts = [], []
        for e in range(N_EXPERTS):
            first = ws_ref[i * N_EXPERTS + e] + row_offset
            hit = slots[:, e:e + 1] == (first + lane_iota).astype(F32)
            g = jnp.where(hit, gates[:, e:e + 1], 0.0)
            hi = g.astype(BF16)
            hi_parts.append(hi)
            lo_parts.append((g - hi.astype(F32)).astype(BF16))
        rows = buf_ref[buf_slot]
        return (jnp.dot(jnp.concatenate(hi_parts, axis=1), rows, preferred_element_type=F32)
                + jnp.dot(jnp.concatenate(lo_parts, axis=1), rows, preferred_element_type=F32))

    for cp in _combine_window_copies(ys_ref, buf_ref, sem_ref, ws_ref, i, i % 2, 0):
        cp.wait()
    out_ref[...] = h_ref[...] + gathered(i % 2, 0)

    @pl.when(second_ref[i] > 0)
    def _():
        copies = _combine_window_copies(ys_ref, buf_ref, sem_ref, ws_ref, i, 2, COMBINE_WIN)
        for cp in copies:
            cp.start()
        for cp in copies:
            cp.wait()
        out_ref[...] += gathered(2, COMBINE_WIN)


def _combine(h2d, gates, rank, base_row, ys, win_start, tile_second):
    t = h2d.shape[0]
    tm = TM_PROJ
    row = lambda n: pl.BlockSpec((tm, n), lambda i, *_: (i, 0))
    grid_spec = pltpu.PrefetchScalarGridSpec(
        num_scalar_prefetch=2,
        grid=(t // tm,),
        in_specs=[row(D_MODEL), row(LANE), row(LANE), pl.BlockSpec((1, LANE), lambda i, *_: (0, 0)),
                  pl.BlockSpec(memory_space=pl.ANY)],
        out_specs=row(D_MODEL),
        scratch_shapes=[pltpu.VMEM((3, N_EXPERTS * COMBINE_WIN, D_MODEL), BF16),
                        pltpu.SemaphoreType.DMA((3, N_EXPERTS))],
    )
    return pl.pallas_call(
        _combine_kernel,
        out_shape=jax.ShapeDtypeStruct((t, D_MODEL), F32),
        grid_spec=grid_spec,
        compiler_params=_cparams(("arbitrary",)),
        name="moe_combine",
    )(win_start, tile_second, h2d, gates, rank, base_row, ys)


def _moe_plan(counts, starts, t):
    n_tok_tiles = t // TM_PROJ
    n_tiles = 2 * t // TM_MOE + N_EXPERTS + 1
    n_slots = n_tiles * TM_MOE
    counts = counts.astype(jnp.int32)
    cum = jnp.concatenate([starts.astype(jnp.int32), counts[None, :]], axis=0)
    padded = (counts + TM_MOE - 1) // TM_MOE * TM_MOE
    ends = jnp.cumsum(padded)
    base = ends - padded
    total = ends[-1]
    tile_first = jnp.arange(n_tiles, dtype=jnp.int32) * TM_MOE
    tile_expert = jnp.minimum(jnp.sum(ends[None, :] <= tile_first[:, None], axis=1), N_EXPERTS - 1).astype(jnp.int32)
    tile_active = (tile_first < total).astype(jnp.int32)
    n_groups = n_slots // SLOT_GROUP
    g_first = jnp.arange(n_groups, dtype=jnp.int32) * SLOT_GROUP
    g_expert = tile_expert[g_first // TM_MOE]
    g_r0 = g_first - base[g_expert]
    cum_e = cum.T[g_expert]
    lo = jnp.sum(cum_e <= g_r0[:, None], axis=1) - 1
    hi = jnp.sum(cum_e < (g_r0 + SLOT_GROUP)[:, None], axis=1)
    g_valid = (g_first < total) & (g_r0 < counts[g_expert])
    lo = jnp.clip(lo, 0, n_tok_tiles - 1)
    hi = jnp.clip(hi, 0, n_tok_tiles)
    g_n = jnp.where(g_valid, jnp.maximum(hi - lo, 0), 0)
    first = base[None, :] + cum[:-1]
    cnt = cum[1:] - cum[:-1]
    ws = first // SLOT_ALIGN * SLOT_ALIGN
    pieces = jnp.where(cnt > 0, (first + cnt - ws + COMBINE_WIN - 1) // COMBINE_WIN, 0)
    return dict(tile_expert=tile_expert, tile_active=tile_active, g_expert=g_expert, g_r0=g_r0.astype(jnp.int32),
                g_lo=lo.astype(jnp.int32), g_n=g_n.astype(jnp.int32), base=base,
                ws=ws.reshape(-1).astype(jnp.int32), second=jnp.any(pieces > 1, axis=1).astype(jnp.int32))


def _pad_heads_cols(w, heads, dim, pad):
    lead = w.shape[:-1]
    w = w.reshape(lead + (heads, dim))
    w = jnp.pad(w, [(0, 0)] * len(lead) + [(0, 0), (0, pad - dim)])
    return w.reshape(lead + (heads * pad,))


def _row(v):
    return v.reshape(1, -1).astype(F32)


def kernel(x, mem, positions, mem_norm, w_mem_kv, ln_mix0, w_in0, conv_w0, b_i0, b_f0, h_norm0, mq_norm0, mk_norm0, w_out0, ln_ffn0, w_gate0, w_up0, w_down0, ln_mix1, w_in1, cq_norm1, ckv_norm1, w_uq1, w_ukv1, q_norm1, k_norm1, mq_norm1, mk_norm1, w_out1, ln_ffn1, w_router1, we_gate1, we_up1, we_down1):
    batch, seq, _ = x.shape
    t = batch * seq
    h = x.reshape(t, D_MODEL)
    tile4 = lambda g: _row(jnp.tile(g, MEM_HEADS))

    k4_0, k4_1, v4 = _memkv(mem.reshape(batch * MEM_TOKENS, D_MODEL), _row(mem_norm), w_mem_kv.astype(BF16),
                            tile4(mk_norm0), tile4(mk_norm1), batch)

    mw = MAIN_WIDTH
    w_q, w_k, w_v, w_o = (w_in0[:, i * mw:(i + 1) * mw] for i in range(4))
    w_gates = w_in0[:, 4 * mw:4 * mw + 2 * ML_HEADS]
    w_mq = w_in0[:, 4 * mw + 2 * ML_HEADS:]
    padh = lambda w: _pad_heads_cols(w, ML_HEADS, ML_HEAD_DIM, HEAD_PAD)
    wqk = jnp.concatenate([padh(w_q), padh(w_k)], axis=1).astype(BF16)
    wvo = jnp.concatenate([padh(w_v), padh(w_o)], axis=1).astype(BF16)
    wg = jnp.pad(w_gates, ((0, 0), (0, GATE_PAD - 2 * ML_HEADS))).astype(BF16)
    wgt = w_gates.T.astype(BF16)
    cw = jnp.concatenate([padh(conv_w0[:, :mw]), padh(conv_w0[:, mw:])], axis=1).astype(F32)
    qscale = jnp.concatenate([jnp.full((1, ML_PAD), ML_HEAD_DIM ** -0.5, F32), jnp.ones((1, ML_PAD), F32)], axis=1)
    bias = jnp.concatenate([b_i0, b_f0]).astype(F32)
    bias_row = jnp.pad(bias, (0, GATE_PAD - 2 * ML_HEADS)).reshape(1, GATE_PAD)
    bias_col = bias.reshape(2 * ML_HEADS, 1)
    qk, v0, so, mq0, gates, gatest = _in0(h, _row(ln_mix0), wqk, wvo, w_mq.astype(BF16), wg, wgt, cw, qscale,
                                          bias_row, bias_col, seq)
    hgain = jnp.pad(h_norm0.astype(F32), (0, HEAD_PAD - ML_HEAD_DIM)).reshape(1, HEAD_PAD)
    hn = _mlstm(qk, v0, so, gates, gatest, hgain, batch, seq)
    wm0 = jnp.pad(w_out0[:mw].reshape(ML_HEADS, ML_HEAD_DIM, D_MODEL),
                  ((0, 0), (0, HEAD_PAD - ML_HEAD_DIM), (0, 0))).reshape(ML_PAD, D_MODEL).astype(BF16)
    h = _out_proj(hn, mq0, h, k4_0, v4, tile4(mq_norm0), wm0, w_out0[mw:].astype(BF16), seq)
    h = _ffn(h, _row(ln_ffn0), w_gate0.astype(BF16), w_up0.astype(BF16), w_down0.astype(BF16))

    w_cq = w_in1[:, :Q_LORA]
    w_ckv = w_in1[:, Q_LORA:Q_LORA + KV_LORA]
    w_kr = jnp.pad(w_in1[:, Q_LORA + KV_LORA:Q_LORA + KV_LORA + MLA_ROPE], ((0, 0), (0, LANE - MLA_ROPE)))
    w_mq1 = w_in1[:, Q_LORA + KV_LORA + MLA_ROPE:]
    w1 = jnp.concatenate([w_cq, w_ckv, w_kr, w_mq1], axis=1).astype(BF16)
    wuq = _pad_heads_cols(w_uq1, MLA_HEADS, MLA_QK, HEAD_PAD).astype(BF16)
    wukv = w_ukv1.reshape(KV_LORA, MLA_HEADS, MLA_NOPE + MLA_V)
    wuk = wukv[:, :, :MLA_NOPE].reshape(KV_LORA, MLA_HEADS * MLA_NOPE).astype(BF16)
    wuvt = wukv[:, :, MLA_NOPE:].reshape(KV_LORA, MLA_HEADS * MLA_V).T.astype(BF16)
    gq = jnp.pad(q_norm1.astype(F32), (0, HEAD_PAD - MLA_QK)).reshape(1, HEAD_PAD)
    gkn = _row(k_norm1[:MLA_NOPE])
    gkr = jnp.pad(k_norm1[MLA_NOPE:].astype(F32), (0, LANE - MLA_ROPE)).reshape(1, LANE)
    half = MLA_ROPE // 2
    inv_freq = ROPE_THETA ** (-jnp.arange(half, dtype=F32) / half)
    invf = jnp.concatenate([inv_freq, inv_freq, jnp.zeros((LANE - MLA_ROPE,), F32)]).reshape(1, LANE)
    q1, k1, v1, mq1 = _in1(h, positions.reshape(t, 1).astype(jnp.int32), _row(ln_mix1), w1, _row(cq_norm1),
                           _row(ckv_norm1), wuq, wuk, wuvt, gq, gkn, gkr, invf)
    o1 = _attention(q1, k1, v1, batch, seq)
    h = _out_proj(o1, mq1, h, k4_1, v4, tile4(mq_norm1), w_out1[:mw].astype(BF16), w_out1[mw:].astype(BF16), seq)

    wr = jnp.pad(w_router1.astype(F32), ((0, 0), (0, LANE - N_EXPERTS)))
    wr_hi = wr.astype(BF16)
    wr_lo = (wr - wr_hi.astype(F32)).astype(BF16)
    xn, gate_w, rank, rankt, starts, counts = _router(h, _row(ln_ffn1), wr_hi, wr_lo)
    plan = _moe_plan(counts[0, :N_EXPERTS], starts[::8, :N_EXPERTS], t)
    xs = _dispatch(xn, rankt, plan["g_expert"], plan["g_r0"], plan["g_lo"], plan["g_n"])
    ys = _experts(xs, plan["tile_expert"], plan["tile_active"],
                  we_gate1.astype(BF16), we_up1.astype(BF16), we_down1.astype(BF16))
    base_row = jnp.pad(plan["base"].astype(F32), (0, LANE - N_EXPERTS)).reshape(1, LANE)
    h = _combine(h, gate_w, rank, base_row, ys, plan["ws"], plan["second"])
    return h.reshape(batch, seq, D_MODEL)
```

```python
import functools

import jax
import jax.numpy as jnp
import numpy as np
from jax import lax
from jax.experimental import pallas as pl
from jax.experimental.pallas import tpu as pltpu

F32 = jnp.float32
BF16 = jnp.bfloat16

D_MODEL = 1024
MEM_TOKENS = 256
MEM_HEADS = 4
MEM_WIDTH = 256
MEM_HEAD_DIM = 64
MAIN_WIDTH = 768
ML_HEADS = 4
ML_HEAD_DIM = 192
CONV_WIDTH = 4
MLA_HEADS = 6
MLA_NOPE = 128
MLA_ROPE = 64
MLA_QK = 192
MLA_V = 128
Q_LORA = 384
KV_LORA = 128
ROPE_THETA = 10000.0
D_FF = 3584
N_EXPERTS = 8
EPS = 1e-6

LANE = 128
HEAD_PAD = 256
ML_PAD = ML_HEADS * HEAD_PAD
MLA_PAD = MLA_HEADS * HEAD_PAD
GATE_PAD = 128
ML_CHUNK = 256
TM_PROJ = 256
TM_IN0 = 256
TM_IN1 = 256
TM_OUT = 1024
TM_FFN = 512
FF_CHUNK = 512
TM_MOE = 512
SLOT_GROUP = 128
DISPATCH_K = 1024
DISPATCH_GROUPS = 4
SLOT_ALIGN = 16
COMBINE_WIN = 256
TQ = 2048
TK = 512
NEG = -1e30
LOG2E = 1.4426950408889634
VMEM_LIMIT = 56 * 1024 * 1024
EXPERT_VMEM_LIMIT = 62 * 1024 * 1024


def _cparams(sem, vmem=VMEM_LIMIT):
    return pltpu.CompilerParams(dimension_semantics=sem, vmem_limit_bytes=vmem)


def _rms(x, g):
    return x * lax.rsqrt(jnp.mean(x * x, axis=-1, keepdims=True) + EPS) * g


def _split_dot(x, w_bf16):
    hi = x.astype(BF16)
    lo = (x - hi.astype(F32)).astype(BF16)
    return (jnp.dot(hi, w_bf16, preferred_element_type=F32)
            + jnp.dot(lo, w_bf16, preferred_element_type=F32))


def _block_diag_ones(n, blk):
    r = lax.broadcasted_iota(jnp.int32, (n, n), 0) // blk
    c = lax.broadcasted_iota(jnp.int32, (n, n), 1) // blk
    return jnp.where(r == c, 1.0, 0.0).astype(BF16)


def _const_spec(shape):
    nd = len(shape)
    return pl.BlockSpec(shape, lambda *_: (0,) * nd)


def _memkv_kernel(mem_ref, g_ref, w_ref, gk0_ref, gk1_ref, k0_ref, k1_ref, v_ref):
    xn = _rms(mem_ref[...], g_ref[...]).astype(BF16)
    kv = jnp.dot(xn, w_ref[...], preferred_element_type=F32)
    k = kv[:, :MEM_WIDTH]
    v = kv[:, MEM_WIDTH:]
    bd = _block_diag_ones(MEM_WIDTH, MEM_HEAD_DIM)
    ms = _split_dot(k * k, bd) * (1.0 / MEM_HEAD_DIM)
    kn = k * lax.rsqrt(ms + EPS)
    lane_head = lax.broadcasted_iota(jnp.int32, (MEM_TOKENS, MEM_WIDTH), 1) // MEM_HEAD_DIM
    for h in range(MEM_HEADS):
        sel = lane_head == h
        k0_ref[h] = jnp.where(sel, kn * gk0_ref[...], 0.0).astype(BF16)
        k1_ref[h] = jnp.where(sel, kn * gk1_ref[...], 0.0).astype(BF16)
        v_ref[h] = jnp.where(sel, v, 0.0).astype(BF16)


def _memkv(mem2d, mem_norm, w_kv, gk0, gk1, batch):
    out = jax.ShapeDtypeStruct((batch * MEM_HEADS, MEM_TOKENS, MEM_WIDTH), BF16)
    hspec = pl.BlockSpec((MEM_HEADS, MEM_TOKENS, MEM_WIDTH), lambda b: (b, 0, 0))
    return pl.pallas_call(
        _memkv_kernel,
        out_shape=(out, out, out),
        grid=(batch,),
        in_specs=[pl.BlockSpec((MEM_TOKENS, D_MODEL), lambda b: (b, 0)),
                  _const_spec((1, D_MODEL)), _const_spec((D_MODEL, 2 * MEM_WIDTH)),
                  _const_spec((1, MEM_WIDTH)), _const_spec((1, MEM_WIDTH))],
        out_specs=(hspec, hspec, hspec),
        compiler_params=_cparams(("parallel",)),
        name="memkv",
    )(mem2d, mem_norm, w_kv, gk0, gk1)


def _in0_kernel(tiles_per_seq, h_ref, hp_ref, g_ref, wqk_ref, wvo_ref, wmq_ref, wg_ref, wgt_ref,
                cw_ref, bias_ref, biast_ref,
                q_ref, kt_ref, v_ref, so_ref, mq_ref, gates_ref, gatest_ref):
    i = pl.program_id(0)
    tm = h_ref.shape[0]
    g = g_ref[...]
    xn = _rms(h_ref[...], g).astype(BF16)
    first = (i % tiles_per_seq) == 0
    xpn = _rms(hp_ref[...], g).astype(BF16)
    row8 = lax.broadcasted_iota(jnp.int32, (8, FF_CHUNK), 0)
    ncol = wqk_ref.shape[1]
    for c in range(ncol // FF_CHUNK):
        cs = slice(c * FF_CHUNK, (c + 1) * FF_CHUNK)
        w = wqk_ref[:, cs]
        u = jnp.dot(xn, w, preferred_element_type=F32)
        up = jnp.dot(xpn, w, preferred_element_type=F32)
        up = jnp.where(first, 0.0, up)
        acc = u * cw_ref[CONV_WIDTH - 1:CONV_WIDTH, cs]
        for k in range(1, CONV_WIDTH):
            rolled = pltpu.roll(u, k, 0)
            prev = pltpu.roll(up, k, 0)
            head = jnp.where(row8 < k, prev, rolled[:8])
            shifted = jnp.concatenate([head, rolled[8:]], axis=0)
            acc = acc + shifted * cw_ref[CONV_WIDTH - 1 - k:CONV_WIDTH - k, cs]
        y = acc * jax.nn.sigmoid(acc)
        if c < ML_PAD // FF_CHUNK:
            q_ref[:, cs] = (y * (ML_HEAD_DIM ** -0.5)).astype(BF16)
        else:
            ks = slice(c * FF_CHUNK - ML_PAD, (c + 1) * FF_CHUNK - ML_PAD)
            kt_ref[ks, :] = y.T.astype(BF16)
    zvo = jnp.dot(xn, wvo_ref[...], preferred_element_type=F32)
    v_ref[...] = zvo[:, :ML_PAD].astype(BF16)
    so_ref[...] = jax.nn.sigmoid(zvo[:, ML_PAD:]).astype(BF16)
    mq_ref[...] = jnp.dot(xn, wmq_ref[...], preferred_element_type=F32).astype(BF16)

    def gate_act(z, is_input_gate):
        logsig = jnp.minimum(z, 0.0) - jnp.log(1.0 + jnp.exp(-jnp.abs(z)))
        return jnp.where(is_input_gate, z, logsig)

    zg = jnp.dot(xn, wg_ref[...], preferred_element_type=F32) + bias_ref[...]
    lane = lax.broadcasted_iota(jnp.int32, (tm, GATE_PAD), 1)
    gates_ref[...] = gate_act(zg, lane < ML_HEADS)
    zgt = lax.dot_general(wgt_ref[...], xn, (((1,), (1,)), ((), ())),
                          preferred_element_type=F32) + biast_ref[...]
    row = lax.broadcasted_iota(jnp.int32, (8, tm), 0)
    gatest_ref[...] = gate_act(zgt, row < ML_HEADS)


def _in0(h2d, ln, wqk, wvo, wmq, wg, wgt, cw, bias, biast, seq):
    t = h2d.shape[0]
    tm = TM_IN0
    tiles_per_seq = seq // tm
    tok = lambda n, dt: jax.ShapeDtypeStruct((t, n), dt)
    row = lambda n: pl.BlockSpec((tm, n), lambda i: (i, 0))
    col = lambda n: pl.BlockSpec((n, tm), lambda i: (0, i))
    return pl.pallas_call(
        functools.partial(_in0_kernel, tiles_per_seq),
        out_shape=(tok(ML_PAD, BF16), jax.ShapeDtypeStruct((ML_PAD, t), BF16), tok(ML_PAD, BF16),
                   tok(ML_PAD, BF16), tok(MEM_WIDTH, BF16), tok(GATE_PAD, F32), jax.ShapeDtypeStruct((8, t), F32)),
        grid=(t // tm,),
        in_specs=[row(D_MODEL),
                  pl.BlockSpec((8, D_MODEL), lambda i: (jnp.maximum(i * (tm // 8) - 1, 0), 0)),
                  _const_spec((1, D_MODEL)), _const_spec(wqk.shape), _const_spec(wvo.shape),
                  _const_spec(wmq.shape), _const_spec(wg.shape), _const_spec(wgt.shape),
                  _const_spec(cw.shape), _const_spec(bias.shape), _const_spec(biast.shape)],
        out_specs=(row(ML_PAD), col(ML_PAD), row(ML_PAD), row(ML_PAD), row(MEM_WIDTH), row(GATE_PAD), col(8)),
        compiler_params=_cparams(("parallel",)),
        name="in_proj0",
    )(h2d, h2d, ln, wqk, wvo, wmq, wg, wgt, cw, bias, biast)


def _mlstm_kernel(batch, q_ref, v_ref, so_ref, gates_ref, *rest):
    kt_refs, gatest_refs = rest[:batch], rest[batch:2 * batch]
    hg_ref, out_ref, c_ref, m_ref = rest[2 * batch:]
    L = q_ref.shape[1]

    @pl.when(pl.program_id(0) == 0)
    def _():
        c_ref[...] = jnp.zeros_like(c_ref)
        m_ref[...] = jnp.zeros_like(m_ref)

    r = lax.broadcasted_iota(jnp.int32, (L, L), 0)
    c = lax.broadcasted_iota(jnp.int32, (L, L), 1)
    causal = r >= c
    tri_lo = jnp.where(causal, 1.0, 0.0).astype(BF16)
    tri_up = jnp.where(r <= c, 1.0, 0.0).astype(BF16)
    lane = lax.broadcasted_iota(jnp.int32, (L, HEAD_PAD), 1)
    hg = hg_ref[...]
    for b in range(batch):
        gts = gates_ref[b]
        gtt = gatest_refs[b][...]
        hi = gts.astype(BF16)
        lo = (gts - hi.astype(F32)).astype(BF16)
        b_cols = (jnp.dot(tri_lo, hi, preferred_element_type=F32)
                  + jnp.dot(tri_lo, lo, preferred_element_type=F32))
        b_rows = _split_dot(gtt, tri_up)
        for h in range(ML_HEADS):
            hs = slice(h * HEAD_PAD, (h + 1) * HEAD_PAD)
            state = b * ML_HEADS + h
            m_h = m_ref[state][0:1, 0:1]
            b_c = b_cols[:, ML_HEADS + h:ML_HEADS + h + 1]
            c_r = b_rows[ML_HEADS + h:ML_HEADS + h + 1, :] - gtt[h:h + 1, :]
            log_d = jnp.where(causal, b_c - c_r, NEG)
            inter = b_c + m_h
            m_row = jnp.maximum(inter, jnp.max(log_d, axis=-1, keepdims=True))
            d = jnp.exp(log_d - m_row)
            s_inter = jnp.exp(inter - m_row)
            q_h = q_ref[b, :, hs]
            kt_h = kt_refs[b][hs, :]
            v_aug = jnp.where(lane == ML_HEAD_DIM, 1.0, v_ref[b, :, hs].astype(F32)).astype(BF16)
            s = jnp.dot(q_h, kt_h, preferred_element_type=F32)
            qkd = (s * d).astype(BF16)
            c_old = c_ref[state]
            num = (jnp.dot(qkd, v_aug, preferred_element_type=F32)
                   + s_inter * jnp.dot(q_h, c_old.astype(BF16), preferred_element_type=F32))
            den = jnp.sum(jnp.where(lane == ML_HEAD_DIM, num, 0.0), axis=-1, keepdims=True)
            ht = num * (1.0 / jnp.maximum(jnp.abs(den), jnp.exp(-m_row)))
            ht = jnp.where(lane < ML_HEAD_DIM, ht, 0.0)
            ms = jnp.sum(ht * ht, axis=-1, keepdims=True) * (1.0 / ML_HEAD_DIM)
            hn = ht * lax.rsqrt(ms + EPS) * hg * so_ref[b, :, hs].astype(F32)
            out_ref[b, :, hs] = hn.astype(BF16)
            b_last = b_c[L - 1:L, :]
            g_r = b_last - c_r
            m_new = jnp.maximum(b_last + m_h, jnp.max(g_r, axis=-1, keepdims=True))
            w_r = jnp.exp(g_r - m_new)
            decay = jnp.exp(b_last + m_h - m_new)
            ktw = (kt_h.astype(F32) * w_r).astype(BF16)
            c_ref[state] = decay * c_old + jnp.dot(ktw, v_aug, preferred_element_type=F32)
            m_ref[state] = jnp.broadcast_to(m_new, m_ref.shape[1:])


def _mlstm(q, kt, v, so, gates, gatest, hgain, batch, seq):
    L = ML_CHUNK
    nc = seq // L
    seq3 = lambda a: a.reshape(batch, seq, a.shape[-1])
    row = lambda n: pl.BlockSpec((batch, L, n), lambda c: (0, c, 0))
    col = lambda rows: [pl.BlockSpec((rows, L), functools.partial(lambda b, c: (0, b * nc + c), b))
                        for b in range(batch)]
    out = pl.pallas_call(
        functools.partial(_mlstm_kernel, batch),
        out_shape=jax.ShapeDtypeStruct((batch, seq, ML_PAD), BF16),
        grid=(nc,),
        in_specs=[row(ML_PAD), row(ML_PAD), row(ML_PAD), row(GATE_PAD)] + col(ML_PAD) + col(8)
                 + [_const_spec((1, HEAD_PAD))],
        out_specs=row(ML_PAD),
        scratch_shapes=[pltpu.VMEM((batch * ML_HEADS, HEAD_PAD, HEAD_PAD), F32),
                        pltpu.VMEM((batch * ML_HEADS, 8, LANE), F32)],
        compiler_params=_cparams(("arbitrary",)),
        name="mlstm",
    )(seq3(q), seq3(v), seq3(so), seq3(gates), *([kt] * batch), *([gatest] * batch), hgain)
    return out.reshape(batch * seq, ML_PAD)


def _out_kernel(main_ref, mq_ref, h_ref, k4_ref, v4_ref, gq_ref, wm_ref, wmem_ref, out_ref):
    q = mq_ref[...].astype(F32)
    bd = _block_diag_ones(MEM_WIDTH, MEM_HEAD_DIM)
    ms = _split_dot(q * q, bd) * (1.0 / MEM_HEAD_DIM)
    qn = (q * lax.rsqrt(ms + EPS) * gq_ref[...] * (MEM_HEAD_DIM ** -0.5)).astype(BF16)
    ymem = jnp.zeros(q.shape, F32)
    for h in range(MEM_HEADS):
        s = lax.dot_general(qn, k4_ref[h], (((1,), (1,)), ((), ())), preferred_element_type=F32)
        e = jnp.exp(s - jnp.max(s, axis=-1, keepdims=True))
        inv = 1.0 / jnp.sum(e, axis=-1, keepdims=True)
        ymem = ymem + jnp.dot(e.astype(BF16), v4_ref[h], preferred_element_type=F32) * inv
    y = (jnp.dot(main_ref[...], wm_ref[...], preferred_element_type=F32)
         + jnp.dot(ymem.astype(BF16), wmem_ref[...], preferred_element_type=F32))
    out_ref[...] = h_ref[...] + y


def _out_proj(main, mq, h2d, k4, v4, gq, wm, wmem, seq):
    t = h2d.shape[0]
    tm = TM_OUT
    tiles_per_seq = seq // tm
    row = lambda n: pl.BlockSpec((tm, n), lambda i: (i, 0))
    memspec = pl.BlockSpec((MEM_HEADS, MEM_TOKENS, MEM_WIDTH), lambda i: (i // tiles_per_seq, 0, 0))
    return pl.pallas_call(
        _out_kernel,
        out_shape=jax.ShapeDtypeStruct((t, D_MODEL), F32),
        grid=(t // tm,),
        in_specs=[row(main.shape[1]), row(MEM_WIDTH), row(D_MODEL), memspec, memspec,
                  _const_spec((1, MEM_WIDTH)), _const_spec(wm.shape), _const_spec(wmem.shape)],
        out_specs=row(D_MODEL),
        compiler_params=_cparams(("parallel",)),
        name="out_proj",
    )(main, mq, h2d, k4, v4, gq, wm, wmem)


def _ffn_kernel(h_ref, g_ref, wg_ref, wu_ref, wd_ref, out_ref):
    x = h_ref[...]
    xn = _rms(x, g_ref[...]).astype(BF16)
    acc = x
    for c in range(D_FF // FF_CHUNK):
        cs = slice(c * FF_CHUNK, (c + 1) * FF_CHUNK)
        gt = jnp.dot(xn, wg_ref[:, cs], preferred_element_type=F32)
        up = jnp.dot(xn, wu_ref[:, cs], preferred_element_type=F32)
        a = (gt * jax.nn.sigmoid(gt) * up).astype(BF16)
        acc = acc + jnp.dot(a, wd_ref[cs, :], preferred_element_type=F32)
    out_ref[...] = acc


def _ffn(h2d, ln, wg, wu, wd):
    t = h2d.shape[0]
    tm = TM_FFN
    row = pl.BlockSpec((tm, D_MODEL), lambda i: (i, 0))
    res = lambda shape: pl.BlockSpec(shape, lambda i: (0, 0), pipeline_mode=pl.Buffered(1))
    return pl.pallas_call(
        _ffn_kernel,
        out_shape=jax.ShapeDtypeStruct((t, D_MODEL), F32),
        grid=(t // tm,),
        in_specs=[row, _const_spec((1, D_MODEL)), res(wg.shape), res(wu.shape), res(wd.shape)],
        out_specs=row,
        compiler_params=_cparams(("parallel",)),
        name="ffn_dense",
    )(h2d, ln, wg, wu, wd)


def _in1_kernel(h_ref, pos_ref, g_ref, w1_ref, gcq_ref, gckv_ref, wuq_ref, wuk_ref, wuvt_ref,
                gq_ref, gkn_ref, gkr_ref, invf_ref,
                q_ref, k_ref, vt_ref, mq_ref):
    tm = h_ref.shape[0]
    xn = _rms(h_ref[...], g_ref[...]).astype(BF16)
    z = jnp.dot(xn, w1_ref[...], preferred_element_type=F32)
    cq = z[:, :Q_LORA]
    ckv = z[:, Q_LORA:Q_LORA + KV_LORA]
    kr = z[:, Q_LORA + KV_LORA:Q_LORA + KV_LORA + LANE]
    mq_ref[...] = z[:, Q_LORA + KV_LORA + LANE:].astype(BF16)
    cqn = _rms(cq, gcq_ref[...]).astype(BF16)
    ckvn = _rms(ckv, gckv_ref[...]).astype(BF16)
    q = jnp.dot(cqn, wuq_ref[...], preferred_element_type=F32)
    kn = jnp.dot(ckvn, wuk_ref[...], preferred_element_type=F32)
    vt_ref[...] = lax.dot_general(wuvt_ref[...], ckvn, (((1,), (1,)), ((), ())),
                                  preferred_element_type=F32).astype(BF16)

    half = MLA_ROPE // 2
    ang = pos_ref[...].astype(F32) * invf_ref[...]
    lane = lax.broadcasted_iota(jnp.int32, (tm, LANE), 1)
    cos = jnp.where(lane < MLA_ROPE, jnp.cos(ang), 0.0)
    sin = jnp.sin(ang)
    sin_hi = jnp.where((lane >= half) & (lane < MLA_ROPE), sin, 0.0)
    sin_lo = jnp.where(lane < half, -sin, 0.0)

    def rope(t):
        return t * cos + pltpu.roll(t, half, 1) * sin_hi + pltpu.roll(t, LANE - half, 1) * sin_lo

    scale = MLA_QK ** -0.5 * LOG2E
    gq = gq_ref[...]
    for h in range(MLA_HEADS):
        qh = q[:, h * HEAD_PAD:(h + 1) * HEAD_PAD]
        rs = lax.rsqrt(jnp.sum(qh * qh, axis=-1, keepdims=True) * (1.0 / MLA_QK) + EPS) * scale
        qn = qh * rs * gq
        q_ref[:, h * HEAD_PAD:h * HEAD_PAD + LANE] = qn[:, :LANE].astype(BF16)
        q_ref[:, h * HEAD_PAD + LANE:(h + 1) * HEAD_PAD] = rope(qn[:, LANE:]).astype(BF16)
    ss_r = jnp.sum(kr * kr, axis=-1, keepdims=True)
    krr = rope(kr * gkr_ref[...])
    gkn = gkn_ref[...]
    for h in range(MLA_HEADS):
        kh = kn[:, h * MLA_NOPE:(h + 1) * MLA_NOPE]
        rs = lax.rsqrt((jnp.sum(kh * kh, axis=-1, keepdims=True) + ss_r) * (1.0 / MLA_QK) + EPS)
        k_ref[:, h * HEAD_PAD:h * HEAD_PAD + LANE] = (kh * rs * gkn).astype(BF16)
        k_ref[:, h * HEAD_PAD + LANE:(h + 1) * HEAD_PAD] = (krr * rs).astype(BF16)


def _in1(h2d, pos2d, ln, w1, gcq, gckv, wuq, wuk, wuvt, gq, gkn, gkr, invf):
    t = h2d.shape[0]
    tm = TM_IN1
    tok = lambda n: jax.ShapeDtypeStruct((t, n), BF16)
    row = lambda n: pl.BlockSpec((tm, n), lambda i: (i, 0))
    consts = [ln, w1, gcq, gckv, wuq, wuk, wuvt, gq, gkn, gkr, invf]
    return pl.pallas_call(
        _in1_kernel,
        out_shape=(tok(MLA_PAD), tok(MLA_PAD), jax.ShapeDtypeStruct((MAIN_WIDTH, t), BF16), tok(MEM_WIDTH)),
        grid=(t // tm,),
        in_specs=[row(D_MODEL), row(1)] + [_const_spec(a.shape) for a in consts],
        out_specs=(row(MLA_PAD), row(MLA_PAD), pl.BlockSpec((MAIN_WIDTH, tm), lambda i: (0, i)), row(MEM_WIDTH)),
        compiler_params=_cparams(("parallel",)),
        name="in_proj1",
    )(h2d, pos2d, *consts)


def _attn_kernel(q_ref, k_ref, vt_ref, o_ref, m_ref, l_ref, acc_ref, st_ref, mt_ref):
    qi = pl.program_id(2)
    m_ref[...] = jnp.full_like(m_ref, NEG)
    l_ref[...] = jnp.zeros_like(l_ref)
    acc_ref[...] = jnp.zeros_like(acc_ref)
    n_diag = TQ // TK
    n_full = qi * n_diag
    nt_dims = (((1,), (1,)), ((), ()))

    def scores(j, slot):
        koff = pl.multiple_of(j * TK, TK)
        st = lax.dot_general(k_ref[pl.ds(koff, TK), :], q_ref[...], nt_dims, preferred_element_type=F32)
        st_ref[slot] = st
        mt_ref[slot] = jnp.max(st, axis=0, keepdims=True)

    def absorb(st, m_tile, vt, cols):
        m_old = m_ref[:, cols]
        m_new = jnp.maximum(m_old, m_tile)
        alpha = jnp.exp2(m_old - m_new)
        e = jnp.exp2(st - m_new)
        l_ref[:, cols] = alpha * l_ref[:, cols] + jnp.sum(e, axis=0, keepdims=True)
        acc_ref[:, cols] = alpha * acc_ref[:, cols] + jnp.dot(vt, e.astype(BF16), preferred_element_type=F32)
        m_ref[:, cols] = m_new

    def finish(j, slot):
        koff = pl.multiple_of(j * TK, TK)
        absorb(st_ref[slot], mt_ref[slot], vt_ref[:, pl.ds(koff, TK)], slice(None))

    @pl.when(n_full > 0)
    def _():
        scores(0, 0)

        def body(j, carry):
            finish(j, j % 2)
            scores(j + 1, (j + 1) % 2)
            return carry

        lax.fori_loop(0, n_full - 1, body, 0)
        finish(n_full - 1, (n_full - 1) % 2)

    kd0 = pl.multiple_of(n_full * TK, TK)
    for u in range(n_diag):
        nk = (u + 1) * TK
        cols = slice(u * TK, (u + 1) * TK)
        st = lax.dot_general(k_ref[pl.ds(kd0, nk), :], q_ref[cols, :], nt_dims, preferred_element_type=F32)
        kpos = lax.broadcasted_iota(jnp.int32, st.shape, 0)
        qpos = u * TK + lax.broadcasted_iota(jnp.int32, st.shape, 1)
        st = jnp.where(kpos <= qpos, st, NEG)
        absorb(st, jnp.max(st, axis=0, keepdims=True), vt_ref[:, pl.ds(kd0, nk)], cols)
    o_ref[...] = (acc_ref[...] * (1.0 / l_ref[...])).T.astype(o_ref.dtype)


def _attention(q, k, v, batch, seq):
    t = q.shape[0]
    nq = seq // TQ
    return pl.pallas_call(
        _attn_kernel,
        out_shape=jax.ShapeDtypeStruct((t, MAIN_WIDTH), BF16),
        grid=(batch, MLA_HEADS, nq),
        in_specs=[pl.BlockSpec((TQ, HEAD_PAD), lambda b, h, i: (b * nq + i, h)),
                  pl.BlockSpec((seq, HEAD_PAD), lambda b, h, i: (b, h)),
                  pl.BlockSpec((MLA_V, seq), lambda b, h, i: (h, b))],
        out_specs=pl.BlockSpec((TQ, MLA_V), lambda b, h, i: (b * nq + i, h)),
        scratch_shapes=[pltpu.VMEM((1, TQ), F32), pltpu.VMEM((1, TQ), F32), pltpu.VMEM((MLA_V, TQ), F32),
                        pltpu.VMEM((2, TK, TQ), F32), pltpu.VMEM((2, 1, TQ), F32)],
        compiler_params=_cparams(("parallel", "parallel", "arbitrary")),
        name="mla_attention",
    )(q, k, v)


def _router_kernel(h_ref, g_ref, whi_ref, wlo_ref,
                   xn_ref, gates_ref, rank_ref, rankt_ref, starts_ref, counts_ref, run_ref):
    @pl.when(pl.program_id(0) == 0)
    def _():
        run_ref[...] = jnp.zeros_like(run_ref)

    tm = h_ref.shape[0]
    xn = _rms(h_ref[...], g_ref[...])
    hi = xn.astype(BF16)
    lo = (xn - hi.astype(F32)).astype(BF16)
    xn_ref[...] = hi
    whi = whi_ref[...]
    logits = (jnp.dot(hi, whi, preferred_element_type=F32) + jnp.dot(lo, whi, preferred_element_type=F32)
              + jnp.dot(hi, wlo_ref[...], preferred_element_type=F32))
    lane = lax.broadcasted_iota(jnp.int32, logits.shape, 1).astype(F32)
    logits = jnp.where(lane < N_EXPERTS, logits, NEG)
    m1 = jnp.max(logits, axis=-1, keepdims=True)
    i1 = jnp.min(jnp.where(logits == m1, lane, float(LANE)), axis=-1, keepdims=True)
    rest = jnp.where(lane == i1, NEG, logits)
    m2 = jnp.max(rest, axis=-1, keepdims=True)
    i2 = jnp.min(jnp.where(rest == m2, lane, float(LANE)), axis=-1, keepdims=True)
    e2 = jnp.exp(m2 - m1)
    w1 = 1.0 / (1.0 + e2)
    w2 = e2 * w1
    gates_ref[...] = jnp.where(lane == i1, w1, jnp.where(lane == i2, w2, 0.0))
    routed = (lane == i1) | (lane == i2)
    oh = jnp.where(routed, 1.0, 0.0)
    r = lax.broadcasted_iota(jnp.int32, (tm, tm), 0)
    c = lax.broadcasted_iota(jnp.int32, (tm, tm), 1)
    before = jnp.where(r > c, 1.0, 0.0).astype(BF16)
    prefix = jnp.dot(before, oh.astype(BF16), preferred_element_type=F32)
    run = run_ref[0:1, :]
    rank = jnp.where(routed, run + prefix, -1.0)
    rank_ref[...] = rank
    rankt_ref[...] = rank.T[:N_EXPERTS, :]
    starts_ref[...] = run_ref[...]
    total = jnp.broadcast_to(run + jnp.sum(oh, axis=0, keepdims=True), run_ref.shape)
    run_ref[...] = total
    counts_ref[...] = total


def _router(h2d, ln, whi, wlo):
    t = h2d.shape[0]
    tm = TM_PROJ
    nt = t // tm
    row = lambda n: pl.BlockSpec((tm, n), lambda i: (i, 0))
    return pl.pallas_call(
        _router_kernel,
        out_shape=(jax.ShapeDtypeStruct((t, D_MODEL), BF16), jax.ShapeDtypeStruct((t, LANE), F32),
                   jax.ShapeDtypeStruct((t, LANE), F32), jax.ShapeDtypeStruct((N_EXPERTS, t), F32),
                   jax.ShapeDtypeStruct((nt * 8, LANE), F32), jax.ShapeDtypeStruct((8, LANE), F32)),
        grid=(nt,),
        in_specs=[row(D_MODEL), _const_spec((1, D_MODEL)), _const_spec(whi.shape), _const_spec(wlo.shape)],
        out_specs=(row(D_MODEL), row(LANE), row(LANE), pl.BlockSpec((N_EXPERTS, tm), lambda i: (0, i)),
                   pl.BlockSpec((8, LANE), lambda i: (i, 0)), _const_spec((8, LANE))),
        scratch_shapes=[pltpu.VMEM((8, LANE), F32)],
        compiler_params=_cparams(("arbitrary",)),
        name="moe_router",
    )(h2d, ln, whi, wlo)


def _dispatch_kernel(ge_ref, gr0_ref, gstart_ref, gn_ref, xn_ref, rankt_ref, xs_ref, acc_ref):
    groups = [pl.program_id(0) * DISPATCH_GROUPS + u for u in range(DISPATCH_GROUPS)]

    def gathered(g, start):
        start = pl.multiple_of(start, TM_PROJ)
        want = (gr0_ref[g] + lax.broadcasted_iota(jnp.int32, (SLOT_GROUP, DISPATCH_K), 0)).astype(F32)
        ranks = rankt_ref[pl.ds(ge_ref[g], 1), pl.ds(start, DISPATCH_K)]
        onehot = jnp.where(ranks == want, 1.0, 0.0).astype(BF16)
        return jnp.dot(onehot, xn_ref[pl.ds(start, DISPATCH_K), :], preferred_element_type=F32)

    single = gn_ref[groups[0]] <= 1
    for g in groups[1:]:
        single = single & (gn_ref[g] <= 1)

    @pl.when(single)
    def _():
        for u, g in enumerate(groups):
            xs_ref[u * SLOT_GROUP:(u + 1) * SLOT_GROUP, :] = gathered(g, gstart_ref[g]).astype(BF16)

    @pl.when(jnp.logical_not(single))
    def _():
        for u, g in enumerate(groups):
            acc_ref[...] = jnp.zeros_like(acc_ref)

            def body(k, carry, g=g):
                acc_ref[...] += gathered(g, gstart_ref[g] + k * DISPATCH_K)
                return carry

            lax.fori_loop(0, gn_ref[g], body, 0)
            xs_ref[u * SLOT_GROUP:(u + 1) * SLOT_GROUP, :] = acc_ref[...].astype(BF16)


def _dispatch(xn, rankt, g_expert, g_rank0, g_start, g_nspans):
    t = xn.shape[0]
    n_steps = g_expert.shape[0] // DISPATCH_GROUPS
    rows = DISPATCH_GROUPS * SLOT_GROUP
    res = lambda shape: pl.BlockSpec(shape, lambda g, *_: (0, 0), pipeline_mode=pl.Buffered(1))
    grid_spec = pltpu.PrefetchScalarGridSpec(
        num_scalar_prefetch=4,
        grid=(n_steps,),
        in_specs=[res((t, D_MODEL)), res((N_EXPERTS, t))],
        out_specs=pl.BlockSpec((rows, D_MODEL), lambda g, *_: (g, 0)),
        scratch_shapes=[pltpu.VMEM((SLOT_GROUP, D_MODEL), F32)],
    )
    return pl.pallas_call(
        _dispatch_kernel,
        out_shape=jax.ShapeDtypeStruct((n_steps * rows, D_MODEL), BF16),
        grid_spec=grid_spec,
        compiler_params=_cparams(("arbitrary",)),
        name="moe_dispatch",
    )(g_expert, g_rank0, g_start, g_nspans, xn, rankt)


def _expert_kernel(te_ref, ta_ref, x_ref, wg_ref, wu_ref, wd_ref, out_ref):
    active = ta_ref[pl.program_id(0)] > 0

    @pl.when(active)
    def _():
        x = x_ref[...]
        acc = None
        for c in range(D_FF // FF_CHUNK):
            cs = slice(c * FF_CHUNK, (c + 1) * FF_CHUNK)
            gt = jnp.dot(x, wg_ref[0, :, cs], preferred_element_type=F32)
            up = jnp.dot(x, wu_ref[0, :, cs], preferred_element_type=F32)
            a = (gt * jax.nn.sigmoid(gt) * up).astype(BF16)
            y = jnp.dot(a, wd_ref[0, cs, :], preferred_element_type=F32)
            acc = y if acc is None else acc + y
        out_ref[...] = acc.astype(out_ref.dtype)

    @pl.when(jnp.logical_not(active))
    def _():
        out_ref[...] = jnp.zeros_like(out_ref)


def _experts(x, tile_expert, tile_active, wg, wu, wd):
    n_tiles = tile_expert.shape[0]
    tm = TM_MOE
    wmap = lambda j, te, ta: (te[j], 0, 0)
    grid_spec = pltpu.PrefetchScalarGridSpec(
        num_scalar_prefetch=2,
        grid=(n_tiles,),
        in_specs=[pl.BlockSpec((tm, D_MODEL), lambda j, te, ta: (jnp.where(ta[j] > 0, j, 0), 0)),
                  pl.BlockSpec((1, D_MODEL, D_FF), wmap),
                  pl.BlockSpec((1, D_MODEL, D_FF), wmap),
                  pl.BlockSpec((1, D_FF, D_MODEL), wmap)],
        out_specs=pl.BlockSpec((tm, D_MODEL), lambda j, te, ta: (j, 0)),
    )
    return pl.pallas_call(
        _expert_kernel,
        out_shape=jax.ShapeDtypeStruct((n_tiles * tm, D_MODEL), BF16),
        grid_spec=grid_spec,
        compiler_params=_cparams(("arbitrary",), EXPERT_VMEM_LIMIT),
        name="moe_experts",
    )(tile_expert, tile_active, x, wg, wu, wd)


def _combine_window_copies(ys_ref, buf_ref, sem_ref, ws_ref, tile, buf_slot, row_offset):
    copies = []
    for e in range(N_EXPERTS):
        start = pl.multiple_of(ws_ref[tile * N_EXPERTS + e] + row_offset, SLOT_ALIGN)
        copies.append(pltpu.make_async_copy(ys_ref.at[pl.ds(start, COMBINE_WIN), :],
                                            buf_ref.at[buf_slot, pl.ds(e * COMBINE_WIN, COMBINE_WIN), :],
                                            sem_ref.at[buf_slot, e]))
    return copies


def _combine_kernel(ws_ref, second_ref, h_ref, gates_ref, rank_ref, base_ref, ys_ref, out_ref, buf_ref, sem_ref):
    i = pl.program_id(0)
    n = pl.num_programs(0)
    tm = h_ref.shape[0]

    @pl.when(i == 0)
    def _():
        for cp in _combine_window_copies(ys_ref, buf_ref, sem_ref, ws_ref, 0, 0, 0):
            cp.start()

    @pl.when(i + 1 < n)
    def _():
        for cp in _combine_window_copies(ys_ref, buf_ref, sem_ref, ws_ref, i + 1, (i + 1) % 2, 0):
            cp.start()

    rank = rank_ref[...]
    slots = jnp.where(rank >= 0.0, rank + base_ref[...], -1.0)
    gates = gates_ref[...]
    lane_iota = lax.broadcasted_iota(jnp.int32, (tm, COMBINE_WIN), 1)

    def gathered(buf_slot, row_offset):
        hi_parts, lo_parts = [], []
        for e in range(N_EXPERTS):
            first = ws_ref[i * N_EXPERTS + e] + row_offset
            hit = slots[:, e:e + 1] == (first + lane_iota).astype(F32)
            g = jnp.where(hit, gates[:, e:e + 1], 0.0)
            hi = g.astype(BF16)
            hi_parts.append(hi)
            lo_parts.append((g - hi.astype(F32)).astype(BF16))
        rows = buf_ref[buf_slot]
        return (jnp.dot(jnp.concatenate(hi_parts, axis=1), rows, preferred_element_type=F32)
                + jnp.dot(jnp.concatenate(lo_parts, axis=1), rows, preferred_element_type=F32))

    for cp in _combine_window_copies(ys_ref, buf_ref, sem_ref, ws_ref, i, i % 2, 0):
        cp.wait()
    out_ref[...] = h_ref[...] + gathered(i % 2, 0)

    @pl.when(second_ref[i] > 0)
    def _():
        copies = _combine_window_copies(ys_ref, buf_ref, sem_ref, ws_ref, i, 2, COMBINE_WIN)
        for cp in copies:
            cp.start()
        for cp in copies:
            cp.wait()
        out_ref[...] += gathered(2, COMBINE_WIN)


def _combine(h2d, gates, rank, base_row, ys, win_start, tile_second):
    t = h2d.shape[0]
    tm = TM_PROJ
    row = lambda n: pl.BlockSpec((tm, n), lambda i, *_: (i, 0))
    grid_spec = pltpu.PrefetchScalarGridSpec(
        num_scalar_prefetch=2,
        grid=(t // tm,),
        in_specs=[row(D_MODEL), row(LANE), row(LANE), pl.BlockSpec((1, LANE), lambda i, *_: (0, 0)),
                  pl.BlockSpec(memory_space=pl.ANY)],
        out_specs=row(D_MODEL),
        scratch_shapes=[pltpu.VMEM((3, N_EXPERTS * COMBINE_WIN, D_MODEL), BF16),
                        pltpu.SemaphoreType.DMA((3, N_EXPERTS))],
    )
    return pl.pallas_call(
        _combine_kernel,
        out_shape=jax.ShapeDtypeStruct((t, D_MODEL), F32),
        grid_spec=grid_spec,
        compiler_params=_cparams(("arbitrary",)),
        name="moe_combine",
    )(win_start, tile_second, h2d, gates, rank, base_row, ys)


def _moe_plan(counts, starts, t):
    n_tok_tiles = t // TM_PROJ
    n_tiles = 2 * t // TM_MOE + N_EXPERTS + 1
    n_slots = n_tiles * TM_MOE
    counts = counts.astype(jnp.int32)
    cum = jnp.concatenate([starts.astype(jnp.int32), counts[None, :]], axis=0)
    padded = (counts + TM_MOE - 1) // TM_MOE * TM_MOE
    ends = jnp.cumsum(padded)
    base = ends - padded
    total = ends[-1]
    tile_first = jnp.arange(n_tiles, dtype=jnp.int32) * TM_MOE
    tile_expert = jnp.minimum(jnp.sum(ends[None, :] <= tile_first[:, None], axis=1), N_EXPERTS - 1).astype(jnp.int32)
    tile_active = (tile_first < total).astype(jnp.int32)
    n_groups = n_slots // SLOT_GROUP
    g_first = jnp.arange(n_groups, dtype=jnp.int32) * SLOT_GROUP
    g_expert = tile_expert[g_first // TM_MOE]
    g_r0 = g_first - base[g_expert]
    cum_e = cum.T[g_expert]
    lo = jnp.sum(cum_e <= g_r0[:, None], axis=1) - 1
    hi = jnp.sum(cum_e < (g_r0 + SLOT_GROUP)[:, None], axis=1)
    g_valid = (g_first < total) & (g_r0 < counts[g_expert])
    lo = jnp.clip(lo, 0, n_tok_tiles - 1)
    hi = jnp.clip(hi, 0, n_tok_tiles)
    g_n = jnp.where(g_valid, (jnp.maximum(hi - lo, 0) * TM_PROJ + DISPATCH_K - 1) // DISPATCH_K, 0)
    lo = jnp.minimum(lo * TM_PROJ, t - jnp.maximum(g_n, 1) * DISPATCH_K)
    first = base[None, :] + cum[:-1]
    cnt = cum[1:] - cum[:-1]
    ws = first // SLOT_ALIGN * SLOT_ALIGN
    pieces = jnp.where(cnt > 0, (first + cnt - ws + COMBINE_WIN - 1) // COMBINE_WIN, 0)
    return dict(tile_expert=tile_expert, tile_active=tile_active, g_expert=g_expert, g_r0=g_r0.astype(jnp.int32),
                g_lo=lo.astype(jnp.int32), g_n=g_n.astype(jnp.int32), base=base,
                ws=ws.reshape(-1).astype(jnp.int32), second=jnp.any(pieces > 1, axis=1).astype(jnp.int32))


def _pad_heads_cols(w, heads, dim, pad):
    lead = w.shape[:-1]
    w = w.reshape(lead + (heads, dim))
    w = jnp.pad(w, [(0, 0)] * len(lead) + [(0, 0), (0, pad - dim)])
    return w.reshape(lead + (heads * pad,))


def _row(v):
    return v.reshape(1, -1).astype(F32)


def kernel(x, mem, positions, mem_norm, w_mem_kv, ln_mix0, w_in0, conv_w0, b_i0, b_f0, h_norm0, mq_norm0, mk_norm0, w_out0, ln_ffn0, w_gate0, w_up0, w_down0, ln_mix1, w_in1, cq_norm1, ckv_norm1, w_uq1, w_ukv1, q_norm1, k_norm1, mq_norm1, mk_norm1, w_out1, ln_ffn1, w_router1, we_gate1, we_up1, we_down1):
    batch, seq, _ = x.shape
    t = batch * seq
    h = x.reshape(t, D_MODEL)
    tile4 = lambda g: _row(jnp.tile(g, MEM_HEADS))

    k4_0, k4_1, v4 = _memkv(mem.reshape(batch * MEM_TOKENS, D_MODEL), _row(mem_norm), w_mem_kv.astype(BF16),
                            tile4(mk_norm0), tile4(mk_norm1), batch)

    mw = MAIN_WIDTH
    w_q, w_k, w_v, w_o = (w_in0[:, i * mw:(i + 1) * mw] for i in range(4))
    w_gates = w_in0[:, 4 * mw:4 * mw + 2 * ML_HEADS]
    w_mq = w_in0[:, 4 * mw + 2 * ML_HEADS:]
    padh = lambda w: _pad_heads_cols(w, ML_HEADS, ML_HEAD_DIM, HEAD_PAD)
    wqk = jnp.concatenate([padh(w_q), padh(w_k)], axis=1).astype(BF16)
    wvo = jnp.concatenate([padh(w_v), padh(w_o)], axis=1).astype(BF16)
    wg = jnp.pad(w_gates, ((0, 0), (0, GATE_PAD - 2 * ML_HEADS))).astype(BF16)
    wgt = w_gates.T.astype(BF16)
    cw = jnp.concatenate([padh(conv_w0[:, :mw]), padh(conv_w0[:, mw:])], axis=1).astype(F32)
    bias = jnp.concatenate([b_i0, b_f0]).astype(F32)
    bias_row = jnp.pad(bias, (0, GATE_PAD - 2 * ML_HEADS)).reshape(1, GATE_PAD)
    bias_col = bias.reshape(2 * ML_HEADS, 1)
    q0, kt0, v0, so, mq0, gates, gatest = _in0(h, _row(ln_mix0), wqk, wvo, w_mq.astype(BF16), wg, wgt, cw,
                                               bias_row, bias_col, seq)
    hgain = jnp.pad(h_norm0.astype(F32), (0, HEAD_PAD - ML_HEAD_DIM)).reshape(1, HEAD_PAD)
    hn = _mlstm(q0, kt0, v0, so, gates, gatest, hgain, batch, seq)
    wm0 = jnp.pad(w_out0[:mw].reshape(ML_HEADS, ML_HEAD_DIM, D_MODEL),
                  ((0, 0), (0, HEAD_PAD - ML_HEAD_DIM), (0, 0))).reshape(ML_PAD, D_MODEL).astype(BF16)
    h = _out_proj(hn, mq0, h, k4_0, v4, tile4(mq_norm0), wm0, w_out0[mw:].astype(BF16), seq)
    h = _ffn(h, _row(ln_ffn0), w_gate0.astype(BF16), w_up0.astype(BF16), w_down0.astype(BF16))

    w_cq = w_in1[:, :Q_LORA]
    w_ckv = w_in1[:, Q_LORA:Q_LORA + KV_LORA]
    w_kr = jnp.pad(w_in1[:, Q_LORA + KV_LORA:Q_LORA + KV_LORA + MLA_ROPE], ((0, 0), (0, LANE - MLA_ROPE)))
    w_mq1 = w_in1[:, Q_LORA + KV_LORA + MLA_ROPE:]
    w1 = jnp.concatenate([w_cq, w_ckv, w_kr, w_mq1], axis=1).astype(BF16)
    wuq = _pad_heads_cols(w_uq1, MLA_HEADS, MLA_QK, HEAD_PAD).astype(BF16)
    wukv = w_ukv1.reshape(KV_LORA, MLA_HEADS, MLA_NOPE + MLA_V)
    wuk = wukv[:, :, :MLA_NOPE].reshape(KV_LORA, MLA_HEADS * MLA_NOPE).astype(BF16)
    wuvt = wukv[:, :, MLA_NOPE:].reshape(KV_LORA, MLA_HEADS * MLA_V).T.astype(BF16)
    gq = jnp.pad(q_norm1.astype(F32), (0, HEAD_PAD - MLA_QK)).reshape(1, HEAD_PAD)
    gkn = _row(k_norm1[:MLA_NOPE])
    gkr = jnp.pad(k_norm1[MLA_NOPE:].astype(F32), (0, LANE - MLA_ROPE)).reshape(1, LANE)
    half = MLA_ROPE // 2
    inv_freq = ROPE_THETA ** (-jnp.arange(half, dtype=F32) / half)
    invf = jnp.concatenate([inv_freq, inv_freq, jnp.zeros((LANE - MLA_ROPE,), F32)]).reshape(1, LANE)
    q1, k1, v1, mq1 = _in1(h, positions.reshape(t, 1).astype(jnp.int32), _row(ln_mix1), w1, _row(cq_norm1),
                           _row(ckv_norm1), wuq, wuk, wuvt, gq, gkn, gkr, invf)
    o1 = _attention(q1, k1, v1, batch, seq)
    h = _out_proj(o1, mq1, h, k4_1, v4, tile4(mq_norm1), w_out1[:mw].astype(BF16), w_out1[mw:].astype(BF16), seq)

    wr = jnp.pad(w_router1.astype(F32), ((0, 0), (0, LANE - N_EXPERTS)))
    wr_hi = wr.astype(BF16)
    wr_lo = (wr - wr_hi.astype(F32)).astype(BF16)
    xn, gate_w, rank, rankt, starts, counts = _router(h, _row(ln_ffn1), wr_hi, wr_lo)
    plan = _moe_plan(counts[0, :N_EXPERTS], starts[::8, :N_EXPERTS], t)
    xs = _dispatch(xn, rankt, plan["g_expert"], plan["g_r0"], plan["g_lo"], plan["g_n"])
    ys = _experts(xs, plan["tile_expert"], plan["tile_active"],
                  we_gate1.astype(BF16), we_up1.astype(BF16), we_down1.astype(BF16))
    base_row = jnp.pad(plan["base"].astype(F32), (0, LANE - N_EXPERTS)).reshape(1, LANE)
    h = _combine(h, gate_w, rank, base_row, ys, plan["ws"], plan["second"])
    return h.reshape(batch, seq, D_MODEL)
```

```python
import functools

import jax
import jax.numpy as jnp
import numpy as np
from jax import lax
from jax.experimental import pallas as pl
from jax.experimental.pallas import tpu as pltpu

F32 = jnp.float32
BF16 = jnp.bfloat16

D_MODEL = 1024
MEM_TOKENS = 256
MEM_HEADS = 4
MEM_WIDTH = 256
MEM_HEAD_DIM = 64
MAIN_WIDTH = 768
ML_HEADS = 4
ML_HEAD_DIM = 192
CONV_WIDTH = 4
MLA_HEADS = 6
MLA_NOPE = 128
MLA_ROPE = 64
MLA_QK = 192
MLA_V = 128
Q_LORA = 384
KV_LORA = 128
ROPE_THETA = 10000.0
D_FF = 3584
N_EXPERTS = 8
EPS = 1e-6

LANE = 128
HEAD_PAD = 256
ML_PAD = ML_HEADS * HEAD_PAD
MLA_PAD = MLA_HEADS * HEAD_PAD
GATE_PAD = 128
ML_CHUNK = 256
TM_PROJ = 256
TM_IN0 = 256
TM_IN1 = 256
TM_OUT = 1024
TM_FFN = 512
FF_CHUNK = 512
TM_MOE = 512
SLOT_GROUP = 128
DISPATCH_K = 1024
DISPATCH_GROUPS = 4
CAST_PARTS = 4
SLOT_ALIGN = 16
COMBINE_WIN = 256
TQ = 2048
TK = 512
NEG = -1e30
LOG2E = 1.4426950408889634
VMEM_LIMIT = 56 * 1024 * 1024
EXPERT_VMEM_LIMIT = 62 * 1024 * 1024


def _cparams(sem, vmem=VMEM_LIMIT):
    return pltpu.CompilerParams(dimension_semantics=sem, vmem_limit_bytes=vmem)


def _rms(x, g):
    return x * lax.rsqrt(jnp.mean(x * x, axis=-1, keepdims=True) + EPS) * g


def _split_dot(x, w_bf16):
    hi = x.astype(BF16)
    lo = (x - hi.astype(F32)).astype(BF16)
    return (jnp.dot(hi, w_bf16, preferred_element_type=F32)
            + jnp.dot(lo, w_bf16, preferred_element_type=F32))


def _block_diag_ones(n, blk):
    r = lax.broadcasted_iota(jnp.int32, (n, n), 0) // blk
    c = lax.broadcasted_iota(jnp.int32, (n, n), 1) // blk
    return jnp.where(r == c, 1.0, 0.0).astype(BF16)


def _const_spec(shape):
    nd = len(shape)
    return pl.BlockSpec(shape, lambda *_: (0,) * nd)


def _memkv_kernel(mem_ref, g_ref, w_ref, gk0_ref, gk1_ref, k0_ref, k1_ref, v_ref):
    xn = _rms(mem_ref[...], g_ref[...]).astype(BF16)
    kv = jnp.dot(xn, w_ref[...], preferred_element_type=F32)
    k = kv[:, :MEM_WIDTH]
    v = kv[:, MEM_WIDTH:]
    bd = _block_diag_ones(MEM_WIDTH, MEM_HEAD_DIM)
    ms = _split_dot(k * k, bd) * (1.0 / MEM_HEAD_DIM)
    kn = k * lax.rsqrt(ms + EPS)
    lane_head = lax.broadcasted_iota(jnp.int32, (MEM_TOKENS, MEM_WIDTH), 1) // MEM_HEAD_DIM
    for h in range(MEM_HEADS):
        sel = lane_head == h
        k0_ref[h] = jnp.where(sel, kn * gk0_ref[...], 0.0).astype(BF16)
        k1_ref[h] = jnp.where(sel, kn * gk1_ref[...], 0.0).astype(BF16)
        v_ref[h] = jnp.where(sel, v, 0.0).astype(BF16)


def _memkv(mem2d, mem_norm, w_kv, gk0, gk1, batch):
    out = jax.ShapeDtypeStruct((batch * MEM_HEADS, MEM_TOKENS, MEM_WIDTH), BF16)
    hspec = pl.BlockSpec((MEM_HEADS, MEM_TOKENS, MEM_WIDTH), lambda b: (b, 0, 0))
    return pl.pallas_call(
        _memkv_kernel,
        out_shape=(out, out, out),
        grid=(batch,),
        in_specs=[pl.BlockSpec((MEM_TOKENS, D_MODEL), lambda b: (b, 0)),
                  _const_spec((1, D_MODEL)), _const_spec((D_MODEL, 2 * MEM_WIDTH)),
                  _const_spec((1, MEM_WIDTH)), _const_spec((1, MEM_WIDTH))],
        out_specs=(hspec, hspec, hspec),
        compiler_params=_cparams(("parallel",)),
        name="memkv",
    )(mem2d, mem_norm, w_kv, gk0, gk1)


def _in0_kernel(tiles_per_seq, h_ref, hp_ref, g_ref, wqk_ref, wvo_ref, wmq_ref, wg_ref, wgt_ref,
                cw_ref, bias_ref, biast_ref,
                q_ref, kt_ref, v_ref, so_ref, mq_ref, gates_ref, gatest_ref):
    i = pl.program_id(0)
    tm = h_ref.shape[0]
    g = g_ref[...]
    xn = _rms(h_ref[...], g).astype(BF16)
    first = (i % tiles_per_seq) == 0
    xpn = _rms(hp_ref[...], g).astype(BF16)
    row8 = lax.broadcasted_iota(jnp.int32, (8, FF_CHUNK), 0)
    ncol = wqk_ref.shape[1]
    for c in range(ncol // FF_CHUNK):
        cs = slice(c * FF_CHUNK, (c + 1) * FF_CHUNK)
        w = wqk_ref[:, cs]
        u = jnp.dot(xn, w, preferred_element_type=F32)
        up = jnp.dot(xpn, w, preferred_element_type=F32)
        up = jnp.where(first, 0.0, up)
        acc = u * cw_ref[CONV_WIDTH - 1:CONV_WIDTH, cs]
        for k in range(1, CONV_WIDTH):
            rolled = pltpu.roll(u, k, 0)
            prev = pltpu.roll(up, k, 0)
            head = jnp.where(row8 < k, prev, rolled[:8])
            shifted = jnp.concatenate([head, rolled[8:]], axis=0)
            acc = acc + shifted * cw_ref[CONV_WIDTH - 1 - k:CONV_WIDTH - k, cs]
        y = acc * jax.nn.sigmoid(acc)
        if c < ML_PAD // FF_CHUNK:
            q_ref[:, cs] = (y * (ML_HEAD_DIM ** -0.5)).astype(BF16)
        else:
            ks = slice(c * FF_CHUNK - ML_PAD, (c + 1) * FF_CHUNK - ML_PAD)
            kt_ref[ks, :] = y.T.astype(BF16)
    zvo = jnp.dot(xn, wvo_ref[...], preferred_element_type=F32)
    v_ref[...] = zvo[:, :ML_PAD].astype(BF16)
    so_ref[...] = jax.nn.sigmoid(zvo[:, ML_PAD:]).astype(BF16)
    mq_ref[...] = jnp.dot(xn, wmq_ref[...], preferred_element_type=F32).astype(BF16)

    def gate_act(z, is_input_gate):
        logsig = jnp.minimum(z, 0.0) - jnp.log(1.0 + jnp.exp(-jnp.abs(z)))
        return jnp.where(is_input_gate, z, logsig)

    zg = jnp.dot(xn, wg_ref[...], preferred_element_type=F32) + bias_ref[...]
    lane = lax.broadcasted_iota(jnp.int32, (tm, GATE_PAD), 1)
    gates_ref[...] = gate_act(zg, lane < ML_HEADS)
    zgt = lax.dot_general(wgt_ref[...], xn, (((1,), (1,)), ((), ())),
                          preferred_element_type=F32) + biast_ref[...]
    row = lax.broadcasted_iota(jnp.int32, (8, tm), 0)
    gatest_ref[...] = gate_act(zgt, row < ML_HEADS)


def _in0(h2d, ln, wqk, wvo, wmq, wg, wgt, cw, bias, biast, seq):
    t = h2d.shape[0]
    tm = TM_IN0
    tiles_per_seq = seq // tm
    tok = lambda n, dt: jax.ShapeDtypeStruct((t, n), dt)
    row = lambda n: pl.BlockSpec((tm, n), lambda i: (i, 0))
    col = lambda n: pl.BlockSpec((n, tm), lambda i: (0, i))
    return pl.pallas_call(
        functools.partial(_in0_kernel, tiles_per_seq),
        out_shape=(tok(ML_PAD, BF16), jax.ShapeDtypeStruct((ML_PAD, t), BF16), tok(ML_PAD, BF16),
                   tok(ML_PAD, BF16), tok(MEM_WIDTH, BF16), tok(GATE_PAD, F32), jax.ShapeDtypeStruct((8, t), F32)),
        grid=(t // tm,),
        in_specs=[row(D_MODEL),
                  pl.BlockSpec((8, D_MODEL), lambda i: (jnp.maximum(i * (tm // 8) - 1, 0), 0)),
                  _const_spec((1, D_MODEL)), _const_spec(wqk.shape), _const_spec(wvo.shape),
                  _const_spec(wmq.shape), _const_spec(wg.shape), _const_spec(wgt.shape),
                  _const_spec(cw.shape), _const_spec(bias.shape), _const_spec(biast.shape)],
        out_specs=(row(ML_PAD), col(ML_PAD), row(ML_PAD), row(ML_PAD), row(MEM_WIDTH), row(GATE_PAD), col(8)),
        compiler_params=_cparams(("parallel",)),
        name="in_proj0",
    )(h2d, h2d, ln, wqk, wvo, wmq, wg, wgt, cw, bias, biast)


def _mlstm_kernel(batch, q_ref, v_ref, so_ref, gates_ref, *rest):
    kt_refs, gatest_refs = rest[:batch], rest[batch:2 * batch]
    hg_ref, out_ref, c_ref, m_ref = rest[2 * batch:]
    L = q_ref.shape[1]

    @pl.when(pl.program_id(0) == 0)
    def _():
        c_ref[...] = jnp.zeros_like(c_ref)
        m_ref[...] = jnp.zeros_like(m_ref)

    r = lax.broadcasted_iota(jnp.int32, (L, L), 0)
    c = lax.broadcasted_iota(jnp.int32, (L, L), 1)
    causal = r >= c
    tri_lo = jnp.where(causal, 1.0, 0.0).astype(BF16)
    tri_up = jnp.where(r <= c, 1.0, 0.0).astype(BF16)
    lane = lax.broadcasted_iota(jnp.int32, (L, HEAD_PAD), 1)
    hg = hg_ref[...]
    for b in range(batch):
        gts = gates_ref[b]
        gtt = gatest_refs[b][...]
        hi = gts.astype(BF16)
        lo = (gts - hi.astype(F32)).astype(BF16)
        b_cols = (jnp.dot(tri_lo, hi, preferred_element_type=F32)
                  + jnp.dot(tri_lo, lo, preferred_element_type=F32))
        b_rows = _split_dot(gtt, tri_up)
        for h in range(ML_HEADS):
            hs = slice(h * HEAD_PAD, (h + 1) * HEAD_PAD)
            state = b * ML_HEADS + h
            m_h = m_ref[state][0:1, 0:1]
            b_c = b_cols[:, ML_HEADS + h:ML_HEADS + h + 1]
            c_r = b_rows[ML_HEADS + h:ML_HEADS + h + 1, :] - gtt[h:h + 1, :]
            log_d = jnp.where(causal, b_c - c_r, NEG)
            inter = b_c + m_h
            m_row = jnp.maximum(inter, jnp.max(log_d, axis=-1, keepdims=True))
            d = jnp.exp(log_d - m_row)
            s_inter = jnp.exp(inter - m_row)
            q_h = q_ref[b, :, hs]
            kt_h = kt_refs[b][hs, :]
            v_aug = jnp.where(lane == ML_HEAD_DIM, 1.0, v_ref[b, :, hs].astype(F32)).astype(BF16)
            s = jnp.dot(q_h, kt_h, preferred_element_type=F32)
            qkd = (s * d).astype(BF16)
            c_old = c_ref[state]
            num = (jnp.dot(qkd, v_aug, preferred_element_type=F32)
                   + s_inter * jnp.dot(q_h, c_old.astype(BF16), preferred_element_type=F32))
            den = jnp.sum(jnp.where(lane == ML_HEAD_DIM, num, 0.0), axis=-1, keepdims=True)
            ht = num * (1.0 / jnp.maximum(jnp.abs(den), jnp.exp(-m_row)))
            ht = jnp.where(lane < ML_HEAD_DIM, ht, 0.0)
            ms = jnp.sum(ht * ht, axis=-1, keepdims=True) * (1.0 / ML_HEAD_DIM)
            hn = ht * lax.rsqrt(ms + EPS) * hg * so_ref[b, :, hs].astype(F32)
            out_ref[b, :, hs] = hn.astype(BF16)
            b_last = b_c[L - 1:L, :]
            g_r = b_last - c_r
            m_new = jnp.maximum(b_last + m_h, jnp.max(g_r, axis=-1, keepdims=True))
            w_r = jnp.exp(g_r - m_new)
            decay = jnp.exp(b_last + m_h - m_new)
            ktw = (kt_h.astype(F32) * w_r).astype(BF16)
            c_ref[state] = decay * c_old + jnp.dot(ktw, v_aug, preferred_element_type=F32)
            m_ref[state] = jnp.broadcast_to(m_new, m_ref.shape[1:])


def _mlstm(q, kt, v, so, gates, gatest, hgain, batch, seq):
    L = ML_CHUNK
    nc = seq // L
    seq3 = lambda a: a.reshape(batch, seq, a.shape[-1])
    row = lambda n: pl.BlockSpec((batch, L, n), lambda c: (0, c, 0))
    col = lambda rows: [pl.BlockSpec((rows, L), functools.partial(lambda b, c: (0, b * nc + c), b))
                        for b in range(batch)]
    out = pl.pallas_call(
        functools.partial(_mlstm_kernel, batch),
        out_shape=jax.ShapeDtypeStruct((batch, seq, ML_PAD), BF16),
        grid=(nc,),
        in_specs=[row(ML_PAD), row(ML_PAD), row(ML_PAD), row(GATE_PAD)] + col(ML_PAD) + col(8)
                 + [_const_spec((1, HEAD_PAD))],
        out_specs=row(ML_PAD),
        scratch_shapes=[pltpu.VMEM((batch * ML_HEADS, HEAD_PAD, HEAD_PAD), F32),
                        pltpu.VMEM((batch * ML_HEADS, 8, LANE), F32)],
        compiler_params=_cparams(("arbitrary",)),
        name="mlstm",
    )(seq3(q), seq3(v), seq3(so), seq3(gates), *([kt] * batch), *([gatest] * batch), hgain)
    return out.reshape(batch * seq, ML_PAD)


def _out_kernel(main_ref, mq_ref, h_ref, k4_ref, v4_ref, gq_ref, wm_ref, wmem_ref, out_ref):
    q = mq_ref[...].astype(F32)
    bd = _block_diag_ones(MEM_WIDTH, MEM_HEAD_DIM)
    ms = _split_dot(q * q, bd) * (1.0 / MEM_HEAD_DIM)
    qn = (q * lax.rsqrt(ms + EPS) * gq_ref[...] * (MEM_HEAD_DIM ** -0.5)).astype(BF16)
    ymem = jnp.zeros(q.shape, F32)
    for h in range(MEM_HEADS):
        s = lax.dot_general(qn, k4_ref[h], (((1,), (1,)), ((), ())), preferred_element_type=F32)
        e = jnp.exp(s - jnp.max(s, axis=-1, keepdims=True))
        inv = 1.0 / jnp.sum(e, axis=-1, keepdims=True)
        ymem = ymem + jnp.dot(e.astype(BF16), v4_ref[h], preferred_element_type=F32) * inv
    y = (jnp.dot(main_ref[...], wm_ref[...], preferred_element_type=F32)
         + jnp.dot(ymem.astype(BF16), wmem_ref[...], preferred_element_type=F32))
    out_ref[...] = h_ref[...] + y


def _out_proj(main, mq, h2d, k4, v4, gq, wm, wmem, seq):
    t = h2d.shape[0]
    tm = TM_OUT
    tiles_per_seq = seq // tm
    row = lambda n: pl.BlockSpec((tm, n), lambda i: (i, 0))
    memspec = pl.BlockSpec((MEM_HEADS, MEM_TOKENS, MEM_WIDTH), lambda i: (i // tiles_per_seq, 0, 0))
    return pl.pallas_call(
        _out_kernel,
        out_shape=jax.ShapeDtypeStruct((t, D_MODEL), F32),
        grid=(t // tm,),
        in_specs=[row(main.shape[1]), row(MEM_WIDTH), row(D_MODEL), memspec, memspec,
                  _const_spec((1, MEM_WIDTH)), _const_spec(wm.shape), _const_spec(wmem.shape)],
        out_specs=row(D_MODEL),
        compiler_params=_cparams(("parallel",)),
        name="out_proj",
    )(main, mq, h2d, k4, v4, gq, wm, wmem)


def _cast_spec(shape, step_of):
    e, r, c = shape
    n_blocks = e * CAST_PARTS

    def index_map(*grid_idx):
        s = jnp.minimum(step_of(*grid_idx), n_blocks - 1)
        return (s // CAST_PARTS, s % CAST_PARTS, 0)

    return pl.BlockSpec((1, r // CAST_PARTS, c), index_map)


def _cast_block(step, src_ref, dst_ref, n_experts):
    @pl.when(step < n_experts * CAST_PARTS)
    def _():
        dst_ref[...] = src_ref[...].astype(dst_ref.dtype)


def _ffn_kernel(h_ref, g_ref, wg_ref, wu_ref, wd_ref, cast_src_ref, out_ref, cast_dst_ref):
    _cast_block(pl.program_id(0), cast_src_ref, cast_dst_ref, N_EXPERTS)
    x = h_ref[...]
    xn = _rms(x, g_ref[...]).astype(BF16)
    acc = x
    for c in range(D_FF // FF_CHUNK):
        cs = slice(c * FF_CHUNK, (c + 1) * FF_CHUNK)
        gt = jnp.dot(xn, wg_ref[:, cs], preferred_element_type=F32)
        up = jnp.dot(xn, wu_ref[:, cs], preferred_element_type=F32)
        a = (gt * jax.nn.sigmoid(gt) * up).astype(BF16)
        acc = acc + jnp.dot(a, wd_ref[cs, :], preferred_element_type=F32)
    out_ref[...] = acc


def _ffn(h2d, ln, wg, wu, wd, cast_src):
    t = h2d.shape[0]
    tm = TM_FFN
    assert t // tm >= cast_src.shape[0] * CAST_PARTS
    row = pl.BlockSpec((tm, D_MODEL), lambda i: (i, 0))
    res = lambda shape: pl.BlockSpec(shape, lambda i: (0, 0), pipeline_mode=pl.Buffered(1))
    cast = _cast_spec(cast_src.shape, lambda i: i)
    return pl.pallas_call(
        _ffn_kernel,
        out_shape=(jax.ShapeDtypeStruct((t, D_MODEL), F32), jax.ShapeDtypeStruct(cast_src.shape, BF16)),
        grid=(t // tm,),
        in_specs=[row, _const_spec((1, D_MODEL)), res(wg.shape), res(wu.shape), res(wd.shape), cast],
        out_specs=(row, cast),
        compiler_params=_cparams(("arbitrary",)),
        name="ffn_dense",
    )(h2d, ln, wg, wu, wd, cast_src)


def _in1_kernel(h_ref, pos_ref, g_ref, w1_ref, gcq_ref, gckv_ref, wuq_ref, wuk_ref, wuvt_ref,
                gq_ref, gkn_ref, gkr_ref, invf_ref,
                q_ref, k_ref, vt_ref, mq_ref):
    tm = h_ref.shape[0]
    xn = _rms(h_ref[...], g_ref[...]).astype(BF16)
    z = jnp.dot(xn, w1_ref[...], preferred_element_type=F32)
    cq = z[:, :Q_LORA]
    ckv = z[:, Q_LORA:Q_LORA + KV_LORA]
    kr = z[:, Q_LORA + KV_LORA:Q_LORA + KV_LORA + LANE]
    mq_ref[...] = z[:, Q_LORA + KV_LORA + LANE:].astype(BF16)
    cqn = _rms(cq, gcq_ref[...]).astype(BF16)
    ckvn = _rms(ckv, gckv_ref[...]).astype(BF16)
    q = jnp.dot(cqn, wuq_ref[...], preferred_element_type=F32)
    kn = jnp.dot(ckvn, wuk_ref[...], preferred_element_type=F32)
    vt_ref[...] = lax.dot_general(wuvt_ref[...], ckvn, (((1,), (1,)), ((), ())),
                                  preferred_element_type=F32).astype(BF16)

    half = MLA_ROPE // 2
    ang = pos_ref[...].astype(F32) * invf_ref[...]
    lane = lax.broadcasted_iota(jnp.int32, (tm, LANE), 1)
    cos = jnp.where(lane < MLA_ROPE, jnp.cos(ang), 0.0)
    sin = jnp.sin(ang)
    sin_hi = jnp.where((lane >= half) & (lane < MLA_ROPE), sin, 0.0)
    sin_lo = jnp.where(lane < half, -sin, 0.0)

    def rope(t):
        return t * cos + pltpu.roll(t, half, 1) * sin_hi + pltpu.roll(t, LANE - half, 1) * sin_lo

    scale = MLA_QK ** -0.5 * LOG2E
    gq = gq_ref[...]
    for h in range(MLA_HEADS):
        qh = q[:, h * HEAD_PAD:(h + 1) * HEAD_PAD]
        rs = lax.rsqrt(jnp.sum(qh * qh, axis=-1, keepdims=True) * (1.0 / MLA_QK) + EPS) * scale
        qn = qh * rs * gq
        q_ref[:, h * HEAD_PAD:h * HEAD_PAD + LANE] = qn[:, :LANE].astype(BF16)
        q_ref[:, h * HEAD_PAD + LANE:(h + 1) * HEAD_PAD] = rope(qn[:, LANE:]).astype(BF16)
    ss_r = jnp.sum(kr * kr, axis=-1, keepdims=True)
    krr = rope(kr * gkr_ref[...])
    gkn = gkn_ref[...]
    for h in range(MLA_HEADS):
        kh = kn[:, h * MLA_NOPE:(h + 1) * MLA_NOPE]
        rs = lax.rsqrt((jnp.sum(kh * kh, axis=-1, keepdims=True) + ss_r) * (1.0 / MLA_QK) + EPS)
        k_ref[:, h * HEAD_PAD:h * HEAD_PAD + LANE] = (kh * rs * gkn).astype(BF16)
        k_ref[:, h * HEAD_PAD + LANE:(h + 1) * HEAD_PAD] = (krr * rs).astype(BF16)


def _in1(h2d, pos2d, ln, w1, gcq, gckv, wuq, wuk, wuvt, gq, gkn, gkr, invf):
    t = h2d.shape[0]
    tm = TM_IN1
    tok = lambda n: jax.ShapeDtypeStruct((t, n), BF16)
    row = lambda n: pl.BlockSpec((tm, n), lambda i: (i, 0))
    consts = [ln, w1, gcq, gckv, wuq, wuk, wuvt, gq, gkn, gkr, invf]
    return pl.pallas_call(
        _in1_kernel,
        out_shape=(tok(MLA_PAD), tok(MLA_PAD), jax.ShapeDtypeStruct((MAIN_WIDTH, t), BF16), tok(MEM_WIDTH)),
        grid=(t // tm,),
        in_specs=[row(D_MODEL), row(1)] + [_const_spec(a.shape) for a in consts],
        out_specs=(row(MLA_PAD), row(MLA_PAD), pl.BlockSpec((MAIN_WIDTH, tm), lambda i: (0, i)), row(MEM_WIDTH)),
        compiler_params=_cparams(("parallel",)),
        name="in_proj1",
    )(h2d, pos2d, *consts)


def _attn_kernel(q_ref, k_ref, vt_ref, cast_a_ref, cast_b_ref, o_ref, cast_a_out_ref, cast_b_out_ref,
                 m_ref, l_ref, acc_ref, st_ref, mt_ref):
    qi = pl.program_id(2)
    step = (pl.program_id(0) * pl.num_programs(1) + pl.program_id(1)) * pl.num_programs(2) + qi
    _cast_block(step, cast_a_ref, cast_a_out_ref, N_EXPERTS)
    _cast_block(step, cast_b_ref, cast_b_out_ref, N_EXPERTS)
    m_ref[...] = jnp.full_like(m_ref, NEG)
    l_ref[...] = jnp.zeros_like(l_ref)
    acc_ref[...] = jnp.zeros_like(acc_ref)
    n_diag = TQ // TK
    n_full = qi * n_diag
    nt_dims = (((1,), (1,)), ((), ()))

    def scores(j, slot):
        koff = pl.multiple_of(j * TK, TK)
        st = lax.dot_general(k_ref[pl.ds(koff, TK), :], q_ref[...], nt_dims, preferred_element_type=F32)
        st_ref[slot] = st
        mt_ref[slot] = jnp.max(st, axis=0, keepdims=True)

    def absorb(st, m_tile, vt, cols):
        m_old = m_ref[:, cols]
        m_new = jnp.maximum(m_old, m_tile)
        alpha = jnp.exp2(m_old - m_new)
        e = jnp.exp2(st - m_new)
        l_ref[:, cols] = alpha * l_ref[:, cols] + jnp.sum(e, axis=0, keepdims=True)
        acc_ref[:, cols] = alpha * acc_ref[:, cols] + jnp.dot(vt, e.astype(BF16), preferred_element_type=F32)
        m_ref[:, cols] = m_new

    def finish(j, slot):
        koff = pl.multiple_of(j * TK, TK)
        absorb(st_ref[slot], mt_ref[slot], vt_ref[:, pl.ds(koff, TK)], slice(None))

    @pl.when(n_full > 0)
    def _():
        scores(0, 0)

        def body(j, carry):
            finish(j, j % 2)
            scores(j + 1, (j + 1) % 2)
            return carry

        lax.fori_loop(0, n_full - 1, body, 0)
        finish(n_full - 1, (n_full - 1) % 2)

    kd0 = pl.multiple_of(n_full * TK, TK)
    for u in range(n_diag):
        nk = (u + 1) * TK
        cols = slice(u * TK, (u + 1) * TK)
        st = lax.dot_general(k_ref[pl.ds(kd0, nk), :], q_ref[cols, :], nt_dims, preferred_element_type=F32)
        kpos = lax.broadcasted_iota(jnp.int32, st.shape, 0)
        qpos = u * TK + lax.broadcasted_iota(jnp.int32, st.shape, 1)
        st = jnp.where(kpos <= qpos, st, NEG)
        absorb(st, jnp.max(st, axis=0, keepdims=True), vt_ref[:, pl.ds(kd0, nk)], cols)
    o_ref[...] = (acc_ref[...] * (1.0 / l_ref[...])).T.astype(o_ref.dtype)


def _attention(q, k, v, cast_a, cast_b, batch, seq):
    t = q.shape[0]
    nq = seq // TQ
    assert batch * MLA_HEADS * nq >= cast_a.shape[0] * CAST_PARTS
    step_of = lambda b, h, i: (b * MLA_HEADS + h) * nq + i
    cast_specs = [_cast_spec(cast_a.shape, step_of), _cast_spec(cast_b.shape, step_of)]
    return pl.pallas_call(
        _attn_kernel,
        out_shape=(jax.ShapeDtypeStruct((t, MAIN_WIDTH), BF16), jax.ShapeDtypeStruct(cast_a.shape, BF16),
                   jax.ShapeDtypeStruct(cast_b.shape, BF16)),
        grid=(batch, MLA_HEADS, nq),
        in_specs=[pl.BlockSpec((TQ, HEAD_PAD), lambda b, h, i: (b * nq + i, h)),
                  pl.BlockSpec((seq, HEAD_PAD), lambda b, h, i: (b, h)),
                  pl.BlockSpec((MLA_V, seq), lambda b, h, i: (h, b))] + cast_specs,
        out_specs=[pl.BlockSpec((TQ, MLA_V), lambda b, h, i: (b * nq + i, h))] + cast_specs,
        scratch_shapes=[pltpu.VMEM((1, TQ), F32), pltpu.VMEM((1, TQ), F32), pltpu.VMEM((MLA_V, TQ), F32),
                        pltpu.VMEM((2, TK, TQ), F32), pltpu.VMEM((2, 1, TQ), F32)],
        compiler_params=_cparams(("arbitrary", "arbitrary", "arbitrary")),
        name="mla_attention",
    )(q, k, v, cast_a, cast_b)


def _router_kernel(h_ref, g_ref, whi_ref, wlo_ref,
                   xn_ref, gates_ref, rank_ref, rankt_ref, starts_ref, counts_ref, run_ref):
    @pl.when(pl.program_id(0) == 0)
    def _():
        run_ref[...] = jnp.zeros_like(run_ref)

    tm = h_ref.shape[0]
    xn = _rms(h_ref[...], g_ref[...])
    hi = xn.astype(BF16)
    lo = (xn - hi.astype(F32)).astype(BF16)
    xn_ref[...] = hi
    whi = whi_ref[...]
    logits = (jnp.dot(hi, whi, preferred_element_type=F32) + jnp.dot(lo, whi, preferred_element_type=F32)
              + jnp.dot(hi, wlo_ref[...], preferred_element_type=F32))
    lane = lax.broadcasted_iota(jnp.int32, logits.shape, 1).astype(F32)
    logits = jnp.where(lane < N_EXPERTS, logits, NEG)
    m1 = jnp.max(logits, axis=-1, keepdims=True)
    i1 = jnp.min(jnp.where(logits == m1, lane, float(LANE)), axis=-1, keepdims=True)
    rest = jnp.where(lane == i1, NEG, logits)
    m2 = jnp.max(rest, axis=-1, keepdims=True)
    i2 = jnp.min(jnp.where(rest == m2, lane, float(LANE)), axis=-1, keepdims=True)
    e2 = jnp.exp(m2 - m1)
    w1 = 1.0 / (1.0 + e2)
    w2 = e2 * w1
    gates_ref[...] = jnp.where(lane == i1, w1, jnp.where(lane == i2, w2, 0.0))
    routed = (lane == i1) | (lane == i2)
    oh = jnp.where(routed, 1.0, 0.0)
    r = lax.broadcasted_iota(jnp.int32, (tm, tm), 0)
    c = lax.broadcasted_iota(jnp.int32, (tm, tm), 1)
    before = jnp.where(r > c, 1.0, 0.0).astype(BF16)
    prefix = jnp.dot(before, oh.astype(BF16), preferred_element_type=F32)
    run = run_ref[0:1, :]
    rank = jnp.where(routed, run + prefix, -1.0)
    rank_ref[...] = rank
    rankt_ref[...] = rank.T[:N_EXPERTS, :]
    starts_ref[...] = run_ref[...]
    total = jnp.broadcast_to(run + jnp.sum(oh, axis=0, keepdims=True), run_ref.shape)
    run_ref[...] = total
    counts_ref[...] = total


def _router(h2d, ln, whi, wlo):
    t = h2d.shape[0]
    tm = TM_PROJ
    nt = t // tm
    row = lambda n: pl.BlockSpec((tm, n), lambda i: (i, 0))
    return pl.pallas_call(
        _router_kernel,
        out_shape=(jax.ShapeDtypeStruct((t, D_MODEL), BF16), jax.ShapeDtypeStruct((t, LANE), F32),
                   jax.ShapeDtypeStruct((t, LANE), F32), jax.ShapeDtypeStruct((N_EXPERTS, t), F32),
                   jax.ShapeDtypeStruct((nt * 8, LANE), F32), jax.ShapeDtypeStruct((8, LANE), F32)),
        grid=(nt,),
        in_specs=[row(D_MODEL), _const_spec((1, D_MODEL)), _const_spec(whi.shape), _const_spec(wlo.shape)],
        out_specs=(row(D_MODEL), row(LANE), row(LANE), pl.BlockSpec((N_EXPERTS, tm), lambda i: (0, i)),
                   pl.BlockSpec((8, LANE), lambda i: (i, 0)), _const_spec((8, LANE))),
        scratch_shapes=[pltpu.VMEM((8, LANE), F32)],
        compiler_params=_cparams(("arbitrary",)),
        name="moe_router",
    )(h2d, ln, whi, wlo)


def _dispatch_kernel(ge_ref, gr0_ref, gstart_ref, gn_ref, xn_ref, rankt_ref, xs_ref, acc_ref):
    groups = [pl.program_id(0) * DISPATCH_GROUPS + u for u in range(DISPATCH_GROUPS)]

    def gathered(g, start):
        start = pl.multiple_of(start, TM_PROJ)
        want = (gr0_ref[g] + lax.broadcasted_iota(jnp.int32, (SLOT_GROUP, DISPATCH_K), 0)).astype(F32)
        ranks = rankt_ref[pl.ds(ge_ref[g], 1), pl.ds(start, DISPATCH_K)]
        onehot = jnp.where(ranks == want, 1.0, 0.0).astype(BF16)
        return jnp.dot(onehot, xn_ref[pl.ds(start, DISPATCH_K), :], preferred_element_type=F32)

    single = gn_ref[groups[0]] <= 1
    for g in groups[1:]:
        single = single & (gn_ref[g] <= 1)

    @pl.when(single)
    def _():
        for u, g in enumerate(groups):
            xs_ref[u * SLOT_GROUP:(u + 1) * SLOT_GROUP, :] = gathered(g, gstart_ref[g]).astype(BF16)

    @pl.when(jnp.logical_not(single))
    def _():
        for u, g in enumerate(groups):
            acc_ref[...] = jnp.zeros_like(acc_ref)

            def body(k, carry, g=g):
                acc_ref[...] += gathered(g, gstart_ref[g] + k * DISPATCH_K)
                return carry

            lax.fori_loop(0, gn_ref[g], body, 0)
            xs_ref[u * SLOT_GROUP:(u + 1) * SLOT_GROUP, :] = acc_ref[...].astype(BF16)


def _dispatch(xn, rankt, g_expert, g_rank0, g_start, g_nspans):
    t = xn.shape[0]
    n_steps = g_expert.shape[0] // DISPATCH_GROUPS
    rows = DISPATCH_GROUPS * SLOT_GROUP
    res = lambda shape: pl.BlockSpec(shape, lambda g, *_: (0, 0), pipeline_mode=pl.Buffered(1))
    grid_spec = pltpu.PrefetchScalarGridSpec(
        num_scalar_prefetch=4,
        grid=(n_steps,),
        in_specs=[res((t, D_MODEL)), res((N_EXPERTS, t))],
        out_specs=pl.BlockSpec((rows, D_MODEL), lambda g, *_: (g, 0)),
        scratch_shapes=[pltpu.VMEM((SLOT_GROUP, D_MODEL), F32)],
    )
    return pl.pallas_call(
        _dispatch_kernel,
        out_shape=jax.ShapeDtypeStruct((n_steps * rows, D_MODEL), BF16),
        grid_spec=grid_spec,
        compiler_params=_cparams(("arbitrary",)),
        name="moe_dispatch",
    )(g_expert, g_rank0, g_start, g_nspans, xn, rankt)


def _expert_kernel(te_ref, ta_ref, x_ref, wg_ref, wu_ref, wd_ref, out_ref):
    active = ta_ref[pl.program_id(0)] > 0

    @pl.when(active)
    def _():
        x = x_ref[...]
        acc = None
        for c in range(D_FF // FF_CHUNK):
            cs = slice(c * FF_CHUNK, (c + 1) * FF_CHUNK)
            gt = jnp.dot(x, wg_ref[0, :, cs], preferred_element_type=F32)
            up = jnp.dot(x, wu_ref[0, :, cs], preferred_element_type=F32)
            a = (gt * jax.nn.sigmoid(gt) * up).astype(BF16)
            y = jnp.dot(a, wd_ref[0, cs, :], preferred_element_type=F32)
            acc = y if acc is None else acc + y
        out_ref[...] = acc.astype(out_ref.dtype)

    @pl.when(jnp.logical_not(active))
    def _():
        out_ref[...] = jnp.zeros_like(out_ref)


def _experts(x, tile_expert, tile_active, wg, wu, wd):
    n_tiles = tile_expert.shape[0]
    tm = TM_MOE
    wmap = lambda j, te, ta: (te[j], 0, 0)
    grid_spec = pltpu.PrefetchScalarGridSpec(
        num_scalar_prefetch=2,
        grid=(n_tiles,),
        in_specs=[pl.BlockSpec((tm, D_MODEL), lambda j, te, ta: (jnp.where(ta[j] > 0, j, 0), 0)),
                  pl.BlockSpec((1, D_MODEL, D_FF), wmap),
                  pl.BlockSpec((1, D_MODEL, D_FF), wmap),
                  pl.BlockSpec((1, D_FF, D_MODEL), wmap)],
        out_specs=pl.BlockSpec((tm, D_MODEL), lambda j, te, ta: (j, 0)),
    )
    return pl.pallas_call(
        _expert_kernel,
        out_shape=jax.ShapeDtypeStruct((n_tiles * tm, D_MODEL), BF16),
        grid_spec=grid_spec,
        compiler_params=_cparams(("arbitrary",), EXPERT_VMEM_LIMIT),
        name="moe_experts",
    )(tile_expert, tile_active, x, wg, wu, wd)


def _combine_window_copies(ys_ref, buf_ref, sem_ref, ws_ref, tile, buf_slot, row_offset):
    copies = []
    for e in range(N_EXPERTS):
        start = pl.multiple_of(ws_ref[tile * N_EXPERTS + e] + row_offset, SLOT_ALIGN)
        copies.append(pltpu.make_async_copy(ys_ref.at[pl.ds(start, COMBINE_WIN), :],
                                            buf_ref.at[buf_slot, pl.ds(e * COMBINE_WIN, COMBINE_WIN), :],
                                            sem_ref.at[buf_slot, e]))
    return copies


def _combine_kernel(ws_ref, second_ref, h_ref, gates_ref, rank_ref, base_ref, ys_ref, out_ref, buf_ref, sem_ref):
    i = pl.program_id(0)
    n = pl.num_programs(0)
    tm = h_ref.shape[0]

    @pl.when(i == 0)
    def _():
        for cp in _combine_window_copies(ys_ref, buf_ref, sem_ref, ws_ref, 0, 0, 0):
            cp.start()

    @pl.when(i + 1 < n)
    def _():
        for cp in _combine_window_copies(ys_ref, buf_ref, sem_ref, ws_ref, i + 1, (i + 1) % 2, 0):
            cp.start()

    rank = rank_ref[...]
    slots = jnp.where(rank >= 0.0, rank + base_ref[...], -1.0)
    gates = gates_ref[...]
    lane_iota = lax.broadcasted_iota(jnp.int32, (tm, COMBINE_WIN), 1)

    def gathered(buf_slot, row_offset):
        hi_parts, lo_parts = [], []
        for e in range(N_EXPERTS):
            first = ws_ref[i * N_EXPERTS + e] + row_offset
            hit = slots[:, e:e + 1] == (first + lane_iota).astype(F32)
            g = jnp.where(hit, gates[:, e:e + 1], 0.0)
            hi = g.astype(BF16)
            hi_parts.append(hi)
            lo_parts.append((g - hi.astype(F32)).astype(BF16))
        rows = buf_ref[buf_slot]
        return (jnp.dot(jnp.concatenate(hi_parts, axis=1), rows, preferred_element_type=F32)
                + jnp.dot(jnp.concatenate(lo_parts, axis=1), rows, preferred_element_type=F32))

    for cp in _combine_window_copies(ys_ref, buf_ref, sem_ref, ws_ref, i, i % 2, 0):
        cp.wait()
    out_ref[...] = h_ref[...] + gathered(i % 2, 0)

    @pl.when(second_ref[i] > 0)
    def _():
        copies = _combine_window_copies(ys_ref, buf_ref, sem_ref, ws_ref, i, 2, COMBINE_WIN)
        for cp in copies:
            cp.start()
        for cp in copies:
            cp.wait()
        out_ref[...] += gathered(2, COMBINE_WIN)


def _combine(h2d, gates, rank, base_row, ys, win_start, tile_second):
    t = h2d.shape[0]
    tm = TM_PROJ
    row = lambda n: pl.BlockSpec((tm, n), lambda i, *_: (i, 0))
    grid_spec = pltpu.PrefetchScalarGridSpec(
        num_scalar_prefetch=2,
        grid=(t // tm,),
        in_specs=[row(D_MODEL), row(LANE), row(LANE), pl.BlockSpec((1, LANE), lambda i, *_: (0, 0)),
                  pl.BlockSpec(memory_space=pl.ANY)],
        out_specs=row(D_MODEL),
        scratch_shapes=[pltpu.VMEM((3, N_EXPERTS * COMBINE_WIN, D_MODEL), BF16),
                        pltpu.SemaphoreType.DMA((3, N_EXPERTS))],
    )
    return pl.pallas_call(
        _combine_kernel,
        out_shape=jax.ShapeDtypeStruct((t, D_MODEL), F32),
        grid_spec=grid_spec,
        compiler_params=_cparams(("arbitrary",)),
        name="moe_combine",
    )(win_start, tile_second, h2d, gates, rank, base_row, ys)


def _moe_plan(counts, starts, t):
    n_tok_tiles = t // TM_PROJ
    n_tiles = 2 * t // TM_MOE + N_EXPERTS + 1
    n_slots = n_tiles * TM_MOE
    counts = counts.astype(jnp.int32)
    cum = jnp.concatenate([starts.astype(jnp.int32), counts[None, :]], axis=0)
    padded = (counts + TM_MOE - 1) // TM_MOE * TM_MOE
    ends = jnp.cumsum(padded)
    base = ends - padded
    total = ends[-1]
    tile_first = jnp.arange(n_tiles, dtype=jnp.int32) * TM_MOE
    tile_expert = jnp.minimum(jnp.sum(ends[None, :] <= tile_first[:, None], axis=1), N_EXPERTS - 1).astype(jnp.int32)
    tile_active = (tile_first < total).astype(jnp.int32)
    n_groups = n_slots // SLOT_GROUP
    g_first = jnp.arange(n_groups, dtype=jnp.int32) * SLOT_GROUP
    g_expert = tile_expert[g_first // TM_MOE]
    g_r0 = g_first - base[g_expert]
    cum_e = cum.T[g_expert]
    lo = jnp.sum(cum_e <= g_r0[:, None], axis=1) - 1
    hi = jnp.sum(cum_e < (g_r0 + SLOT_GROUP)[:, None], axis=1)
    g_valid = (g_first < total) & (g_r0 < counts[g_expert])
    lo = jnp.clip(lo, 0, n_tok_tiles - 1)
    hi = jnp.clip(hi, 0, n_tok_tiles)
    g_n = jnp.where(g_valid, (jnp.maximum(hi - lo, 0) * TM_PROJ + DISPATCH_K - 1) // DISPATCH_K, 0)
    lo = jnp.minimum(lo * TM_PROJ, t - jnp.maximum(g_n, 1) * DISPATCH_K)
    first = base[None, :] + cum[:-1]
    cnt = cum[1:] - cum[:-1]
    ws = first // SLOT_ALIGN * SLOT_ALIGN
    pieces = jnp.where(cnt > 0, (first + cnt - ws + COMBINE_WIN - 1) // COMBINE_WIN, 0)
    return dict(tile_expert=tile_expert, tile_active=tile_active, g_expert=g_expert, g_r0=g_r0.astype(jnp.int32),
                g_lo=lo.astype(jnp.int32), g_n=g_n.astype(jnp.int32), base=base,
                ws=ws.reshape(-1).astype(jnp.int32), second=jnp.any(pieces > 1, axis=1).astype(jnp.int32))


def _pad_heads_cols(w, heads, dim, pad):
    lead = w.shape[:-1]
    w = w.reshape(lead + (heads, dim))
    w = jnp.pad(w, [(0, 0)] * len(lead) + [(0, 0), (0, pad - dim)])
    return w.reshape(lead + (heads * pad,))


def _row(v):
    return v.reshape(1, -1).astype(F32)


def kernel(x, mem, positions, mem_norm, w_mem_kv, ln_mix0, w_in0, conv_w0, b_i0, b_f0, h_norm0, mq_norm0, mk_norm0, w_out0, ln_ffn0, w_gate0, w_up0, w_down0, ln_mix1, w_in1, cq_norm1, ckv_norm1, w_uq1, w_ukv1, q_norm1, k_norm1, mq_norm1, mk_norm1, w_out1, ln_ffn1, w_router1, we_gate1, we_up1, we_down1):
    batch, seq, _ = x.shape
    t = batch * seq
    h = x.reshape(t, D_MODEL)
    tile4 = lambda g: _row(jnp.tile(g, MEM_HEADS))

    k4_0, k4_1, v4 = _memkv(mem.reshape(batch * MEM_TOKENS, D_MODEL), _row(mem_norm), w_mem_kv.astype(BF16),
                            tile4(mk_norm0), tile4(mk_norm1), batch)

    mw = MAIN_WIDTH
    w_q, w_k, w_v, w_o = (w_in0[:, i * mw:(i + 1) * mw] for i in range(4))
    w_gates = w_in0[:, 4 * mw:4 * mw + 2 * ML_HEADS]
    w_mq = w_in0[:, 4 * mw + 2 * ML_HEADS:]
    padh = lambda w: _pad_heads_cols(w, ML_HEADS, ML_HEAD_DIM, HEAD_PAD)
    wqk = jnp.concatenate([padh(w_q), padh(w_k)], axis=1).astype(BF16)
    wvo = jnp.concatenate([padh(w_v), padh(w_o)], axis=1).astype(BF16)
    wg = jnp.pad(w_gates, ((0, 0), (0, GATE_PAD - 2 * ML_HEADS))).astype(BF16)
    wgt = w_gates.T.astype(BF16)
    cw = jnp.concatenate([padh(conv_w0[:, :mw]), padh(conv_w0[:, mw:])], axis=1).astype(F32)
    bias = jnp.concatenate([b_i0, b_f0]).astype(F32)
    bias_row = jnp.pad(bias, (0, GATE_PAD - 2 * ML_HEADS)).reshape(1, GATE_PAD)
    bias_col = bias.reshape(2 * ML_HEADS, 1)
    q0, kt0, v0, so, mq0, gates, gatest = _in0(h, _row(ln_mix0), wqk, wvo, w_mq.astype(BF16), wg, wgt, cw,
                                               bias_row, bias_col, seq)
    hgain = jnp.pad(h_norm0.astype(F32), (0, HEAD_PAD - ML_HEAD_DIM)).reshape(1, HEAD_PAD)
    hn = _mlstm(q0, kt0, v0, so, gates, gatest, hgain, batch, seq)
    wm0 = jnp.pad(w_out0[:mw].reshape(ML_HEADS, ML_HEAD_DIM, D_MODEL),
                  ((0, 0), (0, HEAD_PAD - ML_HEAD_DIM), (0, 0))).reshape(ML_PAD, D_MODEL).astype(BF16)
    h = _out_proj(hn, mq0, h, k4_0, v4, tile4(mq_norm0), wm0, w_out0[mw:].astype(BF16), seq)
    h, we_gate_bf = _ffn(h, _row(ln_ffn0), w_gate0.astype(BF16), w_up0.astype(BF16), w_down0.astype(BF16), we_gate1)

    w_cq = w_in1[:, :Q_LORA]
    w_ckv = w_in1[:, Q_LORA:Q_LORA + KV_LORA]
    w_kr = jnp.pad(w_in1[:, Q_LORA + KV_LORA:Q_LORA + KV_LORA + MLA_ROPE], ((0, 0), (0, LANE - MLA_ROPE)))
    w_mq1 = w_in1[:, Q_LORA + KV_LORA + MLA_ROPE:]
    w1 = jnp.concatenate([w_cq, w_ckv, w_kr, w_mq1], axis=1).astype(BF16)
    wuq = _pad_heads_cols(w_uq1, MLA_HEADS, MLA_QK, HEAD_PAD).astype(BF16)
    wukv = w_ukv1.reshape(KV_LORA, MLA_HEADS, MLA_NOPE + MLA_V)
    wuk = wukv[:, :, :MLA_NOPE].reshape(KV_LORA, MLA_HEADS * MLA_NOPE).astype(BF16)
    wuvt = wukv[:, :, MLA_NOPE:].reshape(KV_LORA, MLA_HEADS * MLA_V).T.astype(BF16)
    gq = jnp.pad(q_norm1.astype(F32), (0, HEAD_PAD - MLA_QK)).reshape(1, HEAD_PAD)
    gkn = _row(k_norm1[:MLA_NOPE])
    gkr = jnp.pad(k_norm1[MLA_NOPE:].astype(F32), (0, LANE - MLA_ROPE)).reshape(1, LANE)
    half = MLA_ROPE // 2
    inv_freq = ROPE_THETA ** (-jnp.arange(half, dtype=F32) / half)
    invf = jnp.concatenate([inv_freq, inv_freq, jnp.zeros((LANE - MLA_ROPE,), F32)]).reshape(1, LANE)
    q1, k1, v1, mq1 = _in1(h, positions.reshape(t, 1).astype(jnp.int32), _row(ln_mix1), w1, _row(cq_norm1),
                           _row(ckv_norm1), wuq, wuk, wuvt, gq, gkn, gkr, invf)
    o1, we_up_bf, we_down_bf = _attention(q1, k1, v1, we_up1, we_down1, batch, seq)
    h = _out_proj(o1, mq1, h, k4_1, v4, tile4(mq_norm1), w_out1[:mw].astype(BF16), w_out1[mw:].astype(BF16), seq)

    wr = jnp.pad(w_router1.astype(F32), ((0, 0), (0, LANE - N_EXPERTS)))
    wr_hi = wr.astype(BF16)
    wr_lo = (wr - wr_hi.astype(F32)).astype(BF16)
    xn, gate_w, rank, rankt, starts, counts = _router(h, _row(ln_ffn1), wr_hi, wr_lo)
    plan = _moe_plan(counts[0, :N_EXPERTS], starts[::8, :N_EXPERTS], t)
    xs = _dispatch(xn, rankt, plan["g_expert"], plan["g_r0"], plan["g_lo"], plan["g_n"])
    ys = _experts(xs, plan["tile_expert"], plan["tile_active"], we_gate_bf, we_up_bf, we_down_bf)
    base_row = jnp.pad(plan["base"].astype(F32), (0, LANE - N_EXPERTS)).reshape(1, LANE)
    h = _combine(h, gate_w, rank, base_row, ys, plan["ws"], plan["second"])
    return h.reshape(batch, seq, D_MODEL)
```

```python
import functools

import jax
import jax.numpy as jnp
import numpy as np
from jax import lax
from jax.experimental import pallas as pl
from jax.experimental.pallas import tpu as pltpu

F32 = jnp.float32
BF16 = jnp.bfloat16

D_MODEL = 1024
MEM_TOKENS = 256
MEM_HEADS = 4
MEM_WIDTH = 256
MEM_HEAD_DIM = 64
MAIN_WIDTH = 768
ML_HEADS = 4
ML_HEAD_DIM = 192
CONV_WIDTH = 4
MLA_HEADS = 6
MLA_NOPE = 128
MLA_ROPE = 64
MLA_QK = 192
MLA_V = 128
Q_LORA = 384
KV_LORA = 128
ROPE_THETA = 10000.0
D_FF = 3584
N_EXPERTS = 8
EPS = 1e-6

LANE = 128
HEAD_PAD = 256
ML_PAD = ML_HEADS * HEAD_PAD
MLA_PAD = MLA_HEADS * HEAD_PAD
GATE_PAD = 128
ML_CHUNK = 256
TM_PROJ = 256
TM_ROUTER = 1024
TM_IN0 = 256
TM_IN1 = 256
TM_OUT = 1024
TM_FFN = 512
FF_CHUNK = 512
TM_MOE = 512
SLOT_GROUP = 128
DISPATCH_K = 1024
DISPATCH_GROUPS = 4
CAST_PARTS = 4
SLOT_ALIGN = 16
COMBINE_WIN = 256
TQ = 2048
TK = 512
NEG = -1e30
LOG2E = 1.4426950408889634
VMEM_LIMIT = 56 * 1024 * 1024
EXPERT_VMEM_LIMIT = 62 * 1024 * 1024


def _cparams(sem, vmem=VMEM_LIMIT):
    return pltpu.CompilerParams(dimension_semantics=sem, vmem_limit_bytes=vmem)


def _rms(x, g):
    return x * lax.rsqrt(jnp.mean(x * x, axis=-1, keepdims=True) + EPS) * g


def _split_dot(x, w_bf16):
    hi = x.astype(BF16)
    lo = (x - hi.astype(F32)).astype(BF16)
    return (jnp.dot(hi, w_bf16, preferred_element_type=F32)
            + jnp.dot(lo, w_bf16, preferred_element_type=F32))


def _block_diag_ones(n, blk):
    r = lax.broadcasted_iota(jnp.int32, (n, n), 0) // blk
    c = lax.broadcasted_iota(jnp.int32, (n, n), 1) // blk
    return jnp.where(r == c, 1.0, 0.0).astype(BF16)


def _const_spec(shape):
    nd = len(shape)
    return pl.BlockSpec(shape, lambda *_: (0,) * nd)


def _memkv_kernel(mem_ref, g_ref, w_ref, gk0_ref, gk1_ref, k0_ref, k1_ref, v_ref):
    xn = _rms(mem_ref[...], g_ref[...]).astype(BF16)
    kv = jnp.dot(xn, w_ref[...], preferred_element_type=F32)
    k = kv[:, :MEM_WIDTH]
    v = kv[:, MEM_WIDTH:]
    bd = _block_diag_ones(MEM_WIDTH, MEM_HEAD_DIM)
    ms = _split_dot(k * k, bd) * (1.0 / MEM_HEAD_DIM)
    kn = k * lax.rsqrt(ms + EPS)
    lane_head = lax.broadcasted_iota(jnp.int32, (MEM_TOKENS, MEM_WIDTH), 1) // MEM_HEAD_DIM
    for h in range(MEM_HEADS):
        sel = lane_head == h
        k0_ref[h] = jnp.where(sel, kn * gk0_ref[...], 0.0).astype(BF16)
        k1_ref[h] = jnp.where(sel, kn * gk1_ref[...], 0.0).astype(BF16)
        v_ref[h] = jnp.where(sel, v, 0.0).astype(BF16)


def _memkv(mem2d, mem_norm, w_kv, gk0, gk1, batch):
    out = jax.ShapeDtypeStruct((batch * MEM_HEADS, MEM_TOKENS, MEM_WIDTH), BF16)
    hspec = pl.BlockSpec((MEM_HEADS, MEM_TOKENS, MEM_WIDTH), lambda b: (b, 0, 0))
    return pl.pallas_call(
        _memkv_kernel,
        out_shape=(out, out, out),
        grid=(batch,),
        in_specs=[pl.BlockSpec((MEM_TOKENS, D_MODEL), lambda b: (b, 0)),
                  _const_spec((1, D_MODEL)), _const_spec((D_MODEL, 2 * MEM_WIDTH)),
                  _const_spec((1, MEM_WIDTH)), _const_spec((1, MEM_WIDTH))],
        out_specs=(hspec, hspec, hspec),
        compiler_params=_cparams(("parallel",)),
        name="memkv",
    )(mem2d, mem_norm, w_kv, gk0, gk1)


def _in0_kernel(tiles_per_seq, h_ref, hp_ref, g_ref, wqk_ref, wvo_ref, wmq_ref, wg_ref, wgt_ref,
                cw_ref, bias_ref, biast_ref,
                q_ref, kt_ref, v_ref, so_ref, mq_ref, gates_ref, gatest_ref):
    i = pl.program_id(0)
    tm = h_ref.shape[0]
    g = g_ref[...]
    xn = _rms(h_ref[...], g).astype(BF16)
    first = (i % tiles_per_seq) == 0
    xpn = _rms(hp_ref[...], g).astype(BF16)
    row8 = lax.broadcasted_iota(jnp.int32, (8, FF_CHUNK), 0)
    ncol = wqk_ref.shape[1]
    for c in range(ncol // FF_CHUNK):
        cs = slice(c * FF_CHUNK, (c + 1) * FF_CHUNK)
        w = wqk_ref[:, cs]
        u = jnp.dot(xn, w, preferred_element_type=F32)
        up = jnp.dot(xpn, w, preferred_element_type=F32)
        up = jnp.where(first, 0.0, up)
        acc = u * cw_ref[CONV_WIDTH - 1:CONV_WIDTH, cs]
        for k in range(1, CONV_WIDTH):
            rolled = pltpu.roll(u, k, 0)
            prev = pltpu.roll(up, k, 0)
            head = jnp.where(row8 < k, prev, rolled[:8])
            shifted = jnp.concatenate([head, rolled[8:]], axis=0)
            acc = acc + shifted * cw_ref[CONV_WIDTH - 1 - k:CONV_WIDTH - k, cs]
        y = acc * jax.nn.sigmoid(acc)
        if c < ML_PAD // FF_CHUNK:
            q_ref[:, cs] = (y * (ML_HEAD_DIM ** -0.5)).astype(BF16)
        else:
            ks = slice(c * FF_CHUNK - ML_PAD, (c + 1) * FF_CHUNK - ML_PAD)
            kt_ref[ks, :] = y.T.astype(BF16)
    zvo = jnp.dot(xn, wvo_ref[...], preferred_element_type=F32)
    v_ref[...] = zvo[:, :ML_PAD].astype(BF16)
    so_ref[...] = jax.nn.sigmoid(zvo[:, ML_PAD:]).astype(BF16)
    mq_ref[...] = jnp.dot(xn, wmq_ref[...], preferred_element_type=F32).astype(BF16)

    def gate_act(z, is_input_gate):
        logsig = jnp.minimum(z, 0.0) - jnp.log(1.0 + jnp.exp(-jnp.abs(z)))
        return jnp.where(is_input_gate, z, logsig)

    zg = jnp.dot(xn, wg_ref[...], preferred_element_type=F32) + bias_ref[...]
    lane = lax.broadcasted_iota(jnp.int32, (tm, GATE_PAD), 1)
    gates_ref[...] = gate_act(zg, lane < ML_HEADS)
    zgt = lax.dot_general(wgt_ref[...], xn, (((1,), (1,)), ((), ())),
                          preferred_element_type=F32) + biast_ref[...]
    row = lax.broadcasted_iota(jnp.int32, (8, tm), 0)
    gatest_ref[...] = gate_act(zgt, row < ML_HEADS)


def _in0(h2d, ln, wqk, wvo, wmq, wg, wgt, cw, bias, biast, seq):
    t = h2d.shape[0]
    tm = TM_IN0
    tiles_per_seq = seq // tm
    tok = lambda n, dt: jax.ShapeDtypeStruct((t, n), dt)
    row = lambda n: pl.BlockSpec((tm, n), lambda i: (i, 0))
    col = lambda n: pl.BlockSpec((n, tm), lambda i: (0, i))
    return pl.pallas_call(
        functools.partial(_in0_kernel, tiles_per_seq),
        out_shape=(tok(ML_PAD, BF16), jax.ShapeDtypeStruct((ML_PAD, t), BF16), tok(ML_PAD, BF16),
                   tok(ML_PAD, BF16), tok(MEM_WIDTH, BF16), tok(GATE_PAD, F32), jax.ShapeDtypeStruct((8, t), F32)),
        grid=(t // tm,),
        in_specs=[row(D_MODEL),
                  pl.BlockSpec((8, D_MODEL), lambda i: (jnp.maximum(i * (tm // 8) - 1, 0), 0)),
                  _const_spec((1, D_MODEL)), _const_spec(wqk.shape), _const_spec(wvo.shape),
                  _const_spec(wmq.shape), _const_spec(wg.shape), _const_spec(wgt.shape),
                  _const_spec(cw.shape), _const_spec(bias.shape), _const_spec(biast.shape)],
        out_specs=(row(ML_PAD), col(ML_PAD), row(ML_PAD), row(ML_PAD), row(MEM_WIDTH), row(GATE_PAD), col(8)),
        compiler_params=_cparams(("parallel",)),
        name="in_proj0",
    )(h2d, h2d, ln, wqk, wvo, wmq, wg, wgt, cw, bias, biast)


def _mlstm_kernel(batch, cast_blocks, q_ref, v_ref, so_ref, gates_ref, *rest):
    n_cast = len(cast_blocks)
    kt_refs, gatest_refs = rest[:batch], rest[batch:2 * batch]
    hg_ref = rest[2 * batch]
    cast_src = rest[2 * batch + 1:2 * batch + 1 + n_cast]
    out_ref = rest[2 * batch + 1 + n_cast]
    cast_dst = rest[2 * batch + 2 + n_cast:2 * batch + 2 + 2 * n_cast]
    c_ref, m_ref = rest[2 * batch + 2 + 2 * n_cast:]
    L = q_ref.shape[1]
    for src, dst, n_blocks in zip(cast_src, cast_dst, cast_blocks):
        _cast_block(pl.program_id(0), src, dst, n_blocks)

    @pl.when(pl.program_id(0) == 0)
    def _():
        c_ref[...] = jnp.zeros_like(c_ref)
        m_ref[...] = jnp.zeros_like(m_ref)

    r = lax.broadcasted_iota(jnp.int32, (L, L), 0)
    c = lax.broadcasted_iota(jnp.int32, (L, L), 1)
    causal = r >= c
    tri_lo = jnp.where(causal, 1.0, 0.0).astype(BF16)
    tri_up = jnp.where(r <= c, 1.0, 0.0).astype(BF16)
    lane = lax.broadcasted_iota(jnp.int32, (L, HEAD_PAD), 1)
    hg = hg_ref[...]
    for b in range(batch):
        gts = gates_ref[b]
        gtt = gatest_refs[b][...]
        hi = gts.astype(BF16)
        lo = (gts - hi.astype(F32)).astype(BF16)
        b_cols = (jnp.dot(tri_lo, hi, preferred_element_type=F32)
                  + jnp.dot(tri_lo, lo, preferred_element_type=F32))
        b_rows = _split_dot(gtt, tri_up)
        for h in range(ML_HEADS):
            hs = slice(h * HEAD_PAD, (h + 1) * HEAD_PAD)
            state = b * ML_HEADS + h
            m_h = m_ref[state][0:1, 0:1]
            b_c = b_cols[:, ML_HEADS + h:ML_HEADS + h + 1]
            c_r = b_rows[ML_HEADS + h:ML_HEADS + h + 1, :] - gtt[h:h + 1, :]
            log_d = jnp.where(causal, b_c - c_r, NEG)
            inter = b_c + m_h
            m_row = jnp.maximum(inter, jnp.max(log_d, axis=-1, keepdims=True))
            d = jnp.exp(log_d - m_row)
            s_inter = jnp.exp(inter - m_row)
            q_h = q_ref[b, :, hs]
            kt_h = kt_refs[b][hs, :]
            v_aug = jnp.where(lane == ML_HEAD_DIM, 1.0, v_ref[b, :, hs].astype(F32)).astype(BF16)
            s = jnp.dot(q_h, kt_h, preferred_element_type=F32)
            qkd = (s * d).astype(BF16)
            c_old = c_ref[state]
            num = (jnp.dot(qkd, v_aug, preferred_element_type=F32)
                   + s_inter * jnp.dot(q_h, c_old.astype(BF16), preferred_element_type=F32))
            den = jnp.sum(jnp.where(lane == ML_HEAD_DIM, num, 0.0), axis=-1, keepdims=True)
            ht = num * (1.0 / jnp.maximum(jnp.abs(den), jnp.exp(-m_row)))
            ht = jnp.where(lane < ML_HEAD_DIM, ht, 0.0)
            ms = jnp.sum(ht * ht, axis=-1, keepdims=True) * (1.0 / ML_HEAD_DIM)
            hn = ht * lax.rsqrt(ms + EPS) * hg * so_ref[b, :, hs].astype(F32)
            out_ref[b, :, hs] = hn.astype(BF16)
            b_last = b_c[L - 1:L, :]
            g_r = b_last - c_r
            m_new = jnp.maximum(b_last + m_h, jnp.max(g_r, axis=-1, keepdims=True))
            w_r = jnp.exp(g_r - m_new)
            decay = jnp.exp(b_last + m_h - m_new)
            ktw = (kt_h.astype(F32) * w_r).astype(BF16)
            c_ref[state] = decay * c_old + jnp.dot(ktw, v_aug, preferred_element_type=F32)
            m_ref[state] = jnp.broadcast_to(m_new, m_ref.shape[1:])


def _mlstm(q, kt, v, so, gates, gatest, hgain, casts, batch, seq):
    L = ML_CHUNK
    nc = seq // L
    seq3 = lambda a: a.reshape(batch, seq, a.shape[-1])
    row = lambda n: pl.BlockSpec((batch, L, n), lambda c: (0, c, 0))
    col = lambda rows: [pl.BlockSpec((rows, L), functools.partial(lambda b, c: (0, b * nc + c), b))
                        for b in range(batch)]
    cast_blocks = tuple(a.shape[0] * CAST_PARTS for a in casts)
    assert all(n <= nc for n in cast_blocks)
    cast_specs = [_cast_spec(a.shape, lambda c: c) for a in casts]
    outs = pl.pallas_call(
        functools.partial(_mlstm_kernel, batch, cast_blocks),
        out_shape=[jax.ShapeDtypeStruct((batch, seq, ML_PAD), BF16)]
                  + [jax.ShapeDtypeStruct(a.shape, BF16) for a in casts],
        grid=(nc,),
        in_specs=[row(ML_PAD), row(ML_PAD), row(ML_PAD), row(GATE_PAD)] + col(ML_PAD) + col(8)
                 + [_const_spec((1, HEAD_PAD))] + cast_specs,
        out_specs=[row(ML_PAD)] + cast_specs,
        scratch_shapes=[pltpu.VMEM((batch * ML_HEADS, HEAD_PAD, HEAD_PAD), F32),
                        pltpu.VMEM((batch * ML_HEADS, 8, LANE), F32)],
        compiler_params=_cparams(("arbitrary",)),
        name="mlstm",
    )(seq3(q), seq3(v), seq3(so), seq3(gates), *([kt] * batch), *([gatest] * batch), hgain, *casts)
    return outs[0].reshape(batch * seq, ML_PAD), outs[1:]


def _out_kernel(main_ref, mq_ref, h_ref, k4_ref, v4_ref, gq_ref, wm_ref, wmem_ref, out_ref):
    q = mq_ref[...].astype(F32)
    bd = _block_diag_ones(MEM_WIDTH, MEM_HEAD_DIM)
    ms = _split_dot(q * q, bd) * (1.0 / MEM_HEAD_DIM)
    qn = (q * lax.rsqrt(ms + EPS) * gq_ref[...] * (MEM_HEAD_DIM ** -0.5)).astype(BF16)
    ymem = jnp.zeros(q.shape, F32)
    for h in range(MEM_HEADS):
        s = lax.dot_general(qn, k4_ref[h], (((1,), (1,)), ((), ())), preferred_element_type=F32)
        e = jnp.exp(s - jnp.max(s, axis=-1, keepdims=True))
        inv = 1.0 / jnp.sum(e, axis=-1, keepdims=True)
        ymem = ymem + jnp.dot(e.astype(BF16), v4_ref[h], preferred_element_type=F32) * inv
    y = (jnp.dot(main_ref[...], wm_ref[...], preferred_element_type=F32)
         + jnp.dot(ymem.astype(BF16), wmem_ref[...], preferred_element_type=F32))
    out_ref[...] = h_ref[...] + y


def _out_proj(main, mq, h2d, k4, v4, gq, wm, wmem, seq):
    t = h2d.shape[0]
    tm = TM_OUT
    tiles_per_seq = seq // tm
    row = lambda n: pl.BlockSpec((tm, n), lambda i: (i, 0))
    memspec = pl.BlockSpec((MEM_HEADS, MEM_TOKENS, MEM_WIDTH), lambda i: (i // tiles_per_seq, 0, 0))
    return pl.pallas_call(
        _out_kernel,
        out_shape=jax.ShapeDtypeStruct((t, D_MODEL), F32),
        grid=(t // tm,),
        in_specs=[row(main.shape[1]), row(MEM_WIDTH), row(D_MODEL), memspec, memspec,
                  _const_spec((1, MEM_WIDTH)), _const_spec(wm.shape), _const_spec(wmem.shape)],
        out_specs=row(D_MODEL),
        compiler_params=_cparams(("parallel",)),
        name="out_proj",
    )(main, mq, h2d, k4, v4, gq, wm, wmem)


def _cast_spec(shape, step_of):
    e, r, c = shape
    n_blocks = e * CAST_PARTS

    def index_map(*grid_idx):
        s = jnp.minimum(step_of(*grid_idx), n_blocks - 1)
        return (s // CAST_PARTS, s % CAST_PARTS, 0)

    return pl.BlockSpec((1, r // CAST_PARTS, c), index_map)


def _cast_block(step, src_ref, dst_ref, n_blocks):
    @pl.when(step < n_blocks)
    def _():
        dst_ref[...] = src_ref[...].astype(dst_ref.dtype)


def _ffn_kernel(h_ref, g_ref, wg_ref, wu_ref, wd_ref, cast_src_ref, out_ref, cast_dst_ref):
    _cast_block(pl.program_id(0), cast_src_ref, cast_dst_ref, N_EXPERTS * CAST_PARTS)
    x = h_ref[...]
    xn = _rms(x, g_ref[...]).astype(BF16)
    acc = x
    for c in range(D_FF // FF_CHUNK):
        cs = slice(c * FF_CHUNK, (c + 1) * FF_CHUNK)
        gt = jnp.dot(xn, wg_ref[:, cs], preferred_element_type=F32)
        up = jnp.dot(xn, wu_ref[:, cs], preferred_element_type=F32)
        a = (gt * jax.nn.sigmoid(gt) * up).astype(BF16)
        acc = acc + jnp.dot(a, wd_ref[cs, :], preferred_element_type=F32)
    out_ref[...] = acc


def _ffn(h2d, ln, wg, wu, wd, cast_src):
    t = h2d.shape[0]
    tm = TM_FFN
    assert t // tm >= cast_src.shape[0] * CAST_PARTS
    row = pl.BlockSpec((tm, D_MODEL), lambda i: (i, 0))
    res = lambda shape: pl.BlockSpec(shape, lambda i: (0, 0), pipeline_mode=pl.Buffered(1))
    cast = _cast_spec(cast_src.shape, lambda i: i)
    return pl.pallas_call(
        _ffn_kernel,
        out_shape=(jax.ShapeDtypeStruct((t, D_MODEL), F32), jax.ShapeDtypeStruct(cast_src.shape, BF16)),
        grid=(t // tm,),
        in_specs=[row, _const_spec((1, D_MODEL)), res(wg.shape), res(wu.shape), res(wd.shape), cast],
        out_specs=(row, cast),
        compiler_params=_cparams(("arbitrary",)),
        name="ffn_dense",
    )(h2d, ln, wg, wu, wd, cast_src)


def _in1_kernel(h_ref, pos_ref, g_ref, w1_ref, gcq_ref, gckv_ref, wuq_ref, wuk_ref, wuvt_ref,
                gq_ref, gkn_ref, gkr_ref, invf_ref,
                q_ref, k_ref, vt_ref, mq_ref):
    tm = h_ref.shape[0]
    xn = _rms(h_ref[...], g_ref[...]).astype(BF16)
    z = jnp.dot(xn, w1_ref[...], preferred_element_type=F32)
    cq = z[:, :Q_LORA]
    ckv = z[:, Q_LORA:Q_LORA + KV_LORA]
    kr = z[:, Q_LORA + KV_LORA:Q_LORA + KV_LORA + LANE]
    mq_ref[...] = z[:, Q_LORA + KV_LORA + LANE:].astype(BF16)
    cqn = _rms(cq, gcq_ref[...]).astype(BF16)
    ckvn = _rms(ckv, gckv_ref[...]).astype(BF16)
    q = jnp.dot(cqn, wuq_ref[...], preferred_element_type=F32)
    kn = jnp.dot(ckvn, wuk_ref[...], preferred_element_type=F32)
    vt_ref[...] = lax.dot_general(wuvt_ref[...], ckvn, (((1,), (1,)), ((), ())),
                                  preferred_element_type=F32).astype(BF16)

    half = MLA_ROPE // 2
    ang = pos_ref[...].astype(F32) * invf_ref[...]
    lane = lax.broadcasted_iota(jnp.int32, (tm, LANE), 1)
    cos = jnp.where(lane < MLA_ROPE, jnp.cos(ang), 0.0)
    sin = jnp.sin(ang)
    sin_hi = jnp.where((lane >= half) & (lane < MLA_ROPE), sin, 0.0)
    sin_lo = jnp.where(lane < half, -sin, 0.0)

    def rope(t):
        return t * cos + pltpu.roll(t, half, 1) * sin_hi + pltpu.roll(t, LANE - half, 1) * sin_lo

    scale = MLA_QK ** -0.5 * LOG2E
    gq = gq_ref[...]
    for h in range(MLA_HEADS):
        qh = q[:, h * HEAD_PAD:(h + 1) * HEAD_PAD]
        rs = lax.rsqrt(jnp.sum(qh * qh, axis=-1, keepdims=True) * (1.0 / MLA_QK) + EPS) * scale
        qn = qh * rs * gq
        q_ref[:, h * HEAD_PAD:h * HEAD_PAD + LANE] = qn[:, :LANE].astype(BF16)
        q_ref[:, h * HEAD_PAD + LANE:(h + 1) * HEAD_PAD] = rope(qn[:, LANE:]).astype(BF16)
    ss_r = jnp.sum(kr * kr, axis=-1, keepdims=True)
    krr = rope(kr * gkr_ref[...])
    gkn = gkn_ref[...]
    for h in range(MLA_HEADS):
        kh = kn[:, h * MLA_NOPE:(h + 1) * MLA_NOPE]
        rs = lax.rsqrt((jnp.sum(kh * kh, axis=-1, keepdims=True) + ss_r) * (1.0 / MLA_QK) + EPS)
        k_ref[:, h * HEAD_PAD:h * HEAD_PAD + LANE] = (kh * rs * gkn).astype(BF16)
        k_ref[:, h * HEAD_PAD + LANE:(h + 1) * HEAD_PAD] = (krr * rs).astype(BF16)


def _in1(h2d, pos2d, ln, w1, gcq, gckv, wuq, wuk, wuvt, gq, gkn, gkr, invf):
    t = h2d.shape[0]
    tm = TM_IN1
    tok = lambda n: jax.ShapeDtypeStruct((t, n), BF16)
    row = lambda n: pl.BlockSpec((tm, n), lambda i: (i, 0))
    consts = [ln, w1, gcq, gckv, wuq, wuk, wuvt, gq, gkn, gkr, invf]
    return pl.pallas_call(
        _in1_kernel,
        out_shape=(tok(MLA_PAD), tok(MLA_PAD), jax.ShapeDtypeStruct((MAIN_WIDTH, t), BF16), tok(MEM_WIDTH)),
        grid=(t // tm,),
        in_specs=[row(D_MODEL), row(1)] + [_const_spec(a.shape) for a in consts],
        out_specs=(row(MLA_PAD), row(MLA_PAD), pl.BlockSpec((MAIN_WIDTH, tm), lambda i: (0, i)), row(MEM_WIDTH)),
        compiler_params=_cparams(("parallel",)),
        name="in_proj1",
    )(h2d, pos2d, *consts)


def _attn_kernel(q_ref, k_ref, vt_ref, cast_a_ref, cast_b_ref, o_ref, cast_a_out_ref, cast_b_out_ref,
                 m_ref, l_ref, acc_ref, st_ref, mt_ref):
    qi = pl.program_id(2)
    step = (pl.program_id(0) * pl.num_programs(1) + pl.program_id(1)) * pl.num_programs(2) + qi
    _cast_block(step, cast_a_ref, cast_a_out_ref, N_EXPERTS * CAST_PARTS)
    _cast_block(step, cast_b_ref, cast_b_out_ref, N_EXPERTS * CAST_PARTS)
    m_ref[...] = jnp.full_like(m_ref, NEG)
    l_ref[...] = jnp.zeros_like(l_ref)
    acc_ref[...] = jnp.zeros_like(acc_ref)
    n_diag = TQ // TK
    n_full = qi * n_diag
    nt_dims = (((1,), (1,)), ((), ()))

    def scores(j, slot):
        koff = pl.multiple_of(j * TK, TK)
        st = lax.dot_general(k_ref[pl.ds(koff, TK), :], q_ref[...], nt_dims, preferred_element_type=F32)
        st_ref[slot] = st
        mt_ref[slot] = jnp.max(st, axis=0, keepdims=True)

    def absorb(st, m_tile, vt, cols):
        m_old = m_ref[:, cols]
        m_new = jnp.maximum(m_old, m_tile)
        alpha = jnp.exp2(m_old - m_new)
        e = jnp.exp2(st - m_new)
        l_ref[:, cols] = alpha * l_ref[:, cols] + jnp.sum(e, axis=0, keepdims=True)
        acc_ref[:, cols] = alpha * acc_ref[:, cols] + jnp.dot(vt, e.astype(BF16), preferred_element_type=F32)
        m_ref[:, cols] = m_new

    def finish(j, slot):
        koff = pl.multiple_of(j * TK, TK)
        absorb(st_ref[slot], mt_ref[slot], vt_ref[:, pl.ds(koff, TK)], slice(None))

    @pl.when(n_full > 0)
    def _():
        scores(0, 0)

        def body(j, carry):
            finish(j, j % 2)
            scores(j + 1, (j + 1) % 2)
            return carry

        lax.fori_loop(0, n_full - 1, body, 0)
        finish(n_full - 1, (n_full - 1) % 2)

    kd0 = pl.multiple_of(n_full * TK, TK)
    for u in range(n_diag):
        nk = (u + 1) * TK
        cols = slice(u * TK, (u + 1) * TK)
        st = lax.dot_general(k_ref[pl.ds(kd0, nk), :], q_ref[cols, :], nt_dims, preferred_element_type=F32)
        kpos = lax.broadcasted_iota(jnp.int32, st.shape, 0)
        qpos = u * TK + lax.broadcasted_iota(jnp.int32, st.shape, 1)
        st = jnp.where(kpos <= qpos, st, NEG)
        absorb(st, jnp.max(st, axis=0, keepdims=True), vt_ref[:, pl.ds(kd0, nk)], cols)
    o_ref[...] = (acc_ref[...] * (1.0 / l_ref[...])).T.astype(o_ref.dtype)


def _attention(q, k, v, cast_a, cast_b, batch, seq):
    t = q.shape[0]
    nq = seq // TQ
    assert batch * MLA_HEADS * nq >= cast_a.shape[0] * CAST_PARTS
    step_of = lambda b, h, i: (b * MLA_HEADS + h) * nq + i
    cast_specs = [_cast_spec(cast_a.shape, step_of), _cast_spec(cast_b.shape, step_of)]
    return pl.pallas_call(
        _attn_kernel,
        out_shape=(jax.ShapeDtypeStruct((t, MAIN_WIDTH), BF16), jax.ShapeDtypeStruct(cast_a.shape, BF16),
                   jax.ShapeDtypeStruct(cast_b.shape, BF16)),
        grid=(batch, MLA_HEADS, nq),
        in_specs=[pl.BlockSpec((TQ, HEAD_PAD), lambda b, h, i: (b * nq + i, h)),
                  pl.BlockSpec((seq, HEAD_PAD), lambda b, h, i: (b, h)),
                  pl.BlockSpec((MLA_V, seq), lambda b, h, i: (h, b))] + cast_specs,
        out_specs=[pl.BlockSpec((TQ, MLA_V), lambda b, h, i: (b * nq + i, h))] + cast_specs,
        scratch_shapes=[pltpu.VMEM((1, TQ), F32), pltpu.VMEM((1, TQ), F32), pltpu.VMEM((MLA_V, TQ), F32),
                        pltpu.VMEM((2, TK, TQ), F32), pltpu.VMEM((2, 1, TQ), F32)],
        compiler_params=_cparams(("arbitrary", "arbitrary", "arbitrary")),
        name="mla_attention",
    )(q, k, v, cast_a, cast_b)


def _router_kernel(h_ref, g_ref, whi_ref, wlo_ref,
                   xn_ref, gatest_ref, rank_ref, rankt_ref, starts_ref, counts_ref, run_ref):
    @pl.when(pl.program_id(0) == 0)
    def _():
        run_ref[...] = jnp.zeros_like(run_ref)

    tm = h_ref.shape[0]
    xn = _rms(h_ref[...], g_ref[...])
    hi = xn.astype(BF16)
    lo = (xn - hi.astype(F32)).astype(BF16)
    xn_ref[...] = hi
    whi = whi_ref[...]
    logits = (jnp.dot(hi, whi, preferred_element_type=F32) + jnp.dot(lo, whi, preferred_element_type=F32)
              + jnp.dot(hi, wlo_ref[...], preferred_element_type=F32))
    lane = lax.broadcasted_iota(jnp.int32, logits.shape, 1).astype(F32)
    logits = jnp.where(lane < N_EXPERTS, logits, NEG)
    m1 = jnp.max(logits, axis=-1, keepdims=True)
    i1 = jnp.min(jnp.where(logits == m1, lane, float(LANE)), axis=-1, keepdims=True)
    rest = jnp.where(lane == i1, NEG, logits)
    m2 = jnp.max(rest, axis=-1, keepdims=True)
    i2 = jnp.min(jnp.where(rest == m2, lane, float(LANE)), axis=-1, keepdims=True)
    e2 = jnp.exp(m2 - m1)
    w1 = 1.0 / (1.0 + e2)
    w2 = e2 * w1
    gates = jnp.where(lane == i1, w1, jnp.where(lane == i2, w2, 0.0))
    gatest_ref[...] = gates.T[:N_EXPERTS, :]
    routed = (lane == i1) | (lane == i2)
    oh = jnp.where(routed, 1.0, 0.0)
    r = lax.broadcasted_iota(jnp.int32, (TM_PROJ, TM_PROJ), 0)
    c = lax.broadcasted_iota(jnp.int32, (TM_PROJ, TM_PROJ), 1)
    before = jnp.where(r > c, 1.0, 0.0).astype(BF16)
    run = run_ref[0:1, :]
    ranks = []
    for ch in range(tm // TM_PROJ):
        rows = slice(ch * TM_PROJ, (ch + 1) * TM_PROJ)
        starts_ref[8 * ch:8 * ch + 8, :] = jnp.broadcast_to(run, (8, LANE))
        prefix = jnp.dot(before, oh[rows].astype(BF16), preferred_element_type=F32)
        ranks.append(jnp.where(routed[rows], run + prefix, -1.0))
        run = run + jnp.sum(oh[rows], axis=0, keepdims=True)
    rank = jnp.concatenate(ranks, axis=0)
    rank_ref[...] = rank
    rankt_ref[...] = rank.T[:N_EXPERTS, :]
    total = jnp.broadcast_to(run, run_ref.shape)
    run_ref[...] = total
    counts_ref[...] = total


def _router(h2d, ln, whi, wlo):
    t = h2d.shape[0]
    tm = min(TM_ROUTER, t)
    nt = t // tm
    chunk_rows = 8 * (tm // TM_PROJ)
    row = lambda n: pl.BlockSpec((tm, n), lambda i: (i, 0))
    return pl.pallas_call(
        _router_kernel,
        out_shape=(jax.ShapeDtypeStruct((t, D_MODEL), BF16), jax.ShapeDtypeStruct((N_EXPERTS, t), F32),
                   jax.ShapeDtypeStruct((t, LANE), F32), jax.ShapeDtypeStruct((N_EXPERTS, t), F32),
                   jax.ShapeDtypeStruct((nt * chunk_rows, LANE), F32), jax.ShapeDtypeStruct((8, LANE), F32)),
        grid=(nt,),
        in_specs=[row(D_MODEL), _const_spec((1, D_MODEL)), _const_spec(whi.shape), _const_spec(wlo.shape)],
        out_specs=(row(D_MODEL), pl.BlockSpec((N_EXPERTS, tm), lambda i: (0, i)), row(LANE),
                   pl.BlockSpec((N_EXPERTS, tm), lambda i: (0, i)),
                   pl.BlockSpec((chunk_rows, LANE), lambda i: (i, 0)), _const_spec((8, LANE))),
        scratch_shapes=[pltpu.VMEM((8, LANE), F32)],
        compiler_params=_cparams(("arbitrary",)),
        name="moe_router",
    )(h2d, ln, whi, wlo)


def _dispatch_kernel(ge_ref, gr0_ref, gstart_ref, gn_ref, xn_ref, rankt_ref, gatest_ref, xs_ref, gs_ref,
                     acc_ref, gacc_ref):
    groups = [pl.program_id(0) * DISPATCH_GROUPS + u for u in range(DISPATCH_GROUPS)]
    eye = (lax.broadcasted_iota(jnp.int32, (SLOT_GROUP, SLOT_GROUP), 0)
           == lax.broadcasted_iota(jnp.int32, (SLOT_GROUP, SLOT_GROUP), 1))

    def gathered(g, start):
        start = pl.multiple_of(start, TM_PROJ)
        want = (gr0_ref[g] + lax.broadcasted_iota(jnp.int32, (SLOT_GROUP, DISPATCH_K), 0)).astype(F32)
        span = (pl.ds(ge_ref[g], 1), pl.ds(start, DISPATCH_K))
        hit = rankt_ref[span] == want
        rows = jnp.dot(jnp.where(hit, 1.0, 0.0).astype(BF16), xn_ref[pl.ds(start, DISPATCH_K), :],
                       preferred_element_type=F32)
        gate = jnp.sum(jnp.where(hit, gatest_ref[span], 0.0), axis=-1, keepdims=True)
        return rows, gate

    def as_row(col):
        return jnp.sum(jnp.where(eye, col, 0.0), axis=0, keepdims=True)

    single = gn_ref[groups[0]] <= 1
    for g in groups[1:]:
        single = single & (gn_ref[g] <= 1)

    @pl.when(single)
    def _():
        gate_rows = []
        for u, g in enumerate(groups):
            rows, gate = gathered(g, gstart_ref[g])
            xs_ref[u * SLOT_GROUP:(u + 1) * SLOT_GROUP, :] = rows.astype(BF16)
            gate_rows.append(as_row(gate))
        gs_ref[...] = jnp.broadcast_to(jnp.concatenate(gate_rows, axis=1), gs_ref.shape)

    @pl.when(jnp.logical_not(single))
    def _():
        gate_rows = []
        for u, g in enumerate(groups):
            acc_ref[...] = jnp.zeros_like(acc_ref)
            gacc_ref[...] = jnp.zeros_like(gacc_ref)

            def body(k, carry, g=g):
                rows, gate = gathered(g, gstart_ref[g] + k * DISPATCH_K)
                acc_ref[...] += rows
                gacc_ref[...] += gate
                return carry

            lax.fori_loop(0, gn_ref[g], body, 0)
            xs_ref[u * SLOT_GROUP:(u + 1) * SLOT_GROUP, :] = acc_ref[...].astype(BF16)
            gate_rows.append(as_row(gacc_ref[...]))
        gs_ref[...] = jnp.broadcast_to(jnp.concatenate(gate_rows, axis=1), gs_ref.shape)


def _dispatch(xn, rankt, gatest, g_expert, g_rank0, g_start, g_nspans):
    t = xn.shape[0]
    n_steps = g_expert.shape[0] // DISPATCH_GROUPS
    rows = DISPATCH_GROUPS * SLOT_GROUP
    assert rows == TM_MOE
    res = lambda shape: pl.BlockSpec(shape, lambda g, *_: (0, 0), pipeline_mode=pl.Buffered(1))
    grid_spec = pltpu.PrefetchScalarGridSpec(
        num_scalar_prefetch=4,
        grid=(n_steps,),
        in_specs=[res((t, D_MODEL)), res((N_EXPERTS, t)), res((N_EXPERTS, t))],
        out_specs=(pl.BlockSpec((rows, D_MODEL), lambda g, *_: (g, 0)),
                   pl.BlockSpec((8, rows), lambda g, *_: (g, 0))),
        scratch_shapes=[pltpu.VMEM((SLOT_GROUP, D_MODEL), F32), pltpu.VMEM((SLOT_GROUP, 1), F32)],
    )
    return pl.pallas_call(
        _dispatch_kernel,
        out_shape=(jax.ShapeDtypeStruct((n_steps * rows, D_MODEL), BF16),
                   jax.ShapeDtypeStruct((n_steps * 8, rows), F32)),
        grid_spec=grid_spec,
        compiler_params=_cparams(("arbitrary",)),
        name="moe_dispatch",
    )(g_expert, g_rank0, g_start, g_nspans, xn, rankt, gatest)


def _expert_kernel(te_ref, ta_ref, x_ref, gs_ref, wg_ref, wu_ref, wd_ref, out_ref):
    active = ta_ref[pl.program_id(0)] > 0

    @pl.when(active)
    def _():
        x = x_ref[...]
        acc = None
        for c in range(D_FF // FF_CHUNK):
            cs = slice(c * FF_CHUNK, (c + 1) * FF_CHUNK)
            gt = jnp.dot(x, wg_ref[0, :, cs], preferred_element_type=F32)
            up = jnp.dot(x, wu_ref[0, :, cs], preferred_element_type=F32)
            a = (gt * jax.nn.sigmoid(gt) * up).astype(BF16)
            y = jnp.dot(a, wd_ref[0, cs, :], preferred_element_type=F32)
            acc = y if acc is None else acc + y
        tm = x.shape[0]
        eye = lax.broadcasted_iota(jnp.int32, (tm, tm), 0) == lax.broadcasted_iota(jnp.int32, (tm, tm), 1)
        gate = jnp.sum(jnp.where(eye, gs_ref[0:1, :], 0.0), axis=-1, keepdims=True)
        out_ref[...] = (acc * gate).astype(out_ref.dtype)

    @pl.when(jnp.logical_not(active))
    def _():
        out_ref[...] = jnp.zeros_like(out_ref)


def _experts(x, gate_rows, tile_expert, tile_active, wg, wu, wd):
    n_tiles = tile_expert.shape[0]
    tm = TM_MOE
    wmap = lambda j, te, ta: (te[j], 0, 0)
    tile = lambda j, te, ta: (jnp.where(ta[j] > 0, j, 0), 0)
    grid_spec = pltpu.PrefetchScalarGridSpec(
        num_scalar_prefetch=2,
        grid=(n_tiles,),
        in_specs=[pl.BlockSpec((tm, D_MODEL), tile),
                  pl.BlockSpec((8, tm), tile),
                  pl.BlockSpec((1, D_MODEL, D_FF), wmap),
                  pl.BlockSpec((1, D_MODEL, D_FF), wmap),
                  pl.BlockSpec((1, D_FF, D_MODEL), wmap)],
        out_specs=pl.BlockSpec((tm, D_MODEL), lambda j, te, ta: (j, 0)),
    )
    return pl.pallas_call(
        _expert_kernel,
        out_shape=jax.ShapeDtypeStruct((n_tiles * tm, D_MODEL), BF16),
        grid_spec=grid_spec,
        compiler_params=_cparams(("arbitrary",), EXPERT_VMEM_LIMIT),
        name="moe_experts",
    )(tile_expert, tile_active, x, gate_rows, wg, wu, wd)


def _combine_window_copies(ys_ref, buf_ref, sem_ref, ws_ref, tile, buf_slot, row_offset):
    copies = []
    for e in range(N_EXPERTS):
        start = pl.multiple_of(ws_ref[tile * N_EXPERTS + e] + row_offset, SLOT_ALIGN)
        copies.append(pltpu.make_async_copy(ys_ref.at[pl.ds(start, COMBINE_WIN), :],
                                            buf_ref.at[buf_slot, pl.ds(e * COMBINE_WIN, COMBINE_WIN), :],
                                            sem_ref.at[buf_slot, e]))
    return copies


def _combine_kernel(ws_ref, second_ref, h_ref, rank_ref, base_ref, ys_ref, out_ref, buf_ref, sem_ref):
    i = pl.program_id(0)
    n = pl.num_programs(0)
    tm = h_ref.shape[0]

    @pl.when(i == 0)
    def _():
        for cp in _combine_window_copies(ys_ref, buf_ref, sem_ref, ws_ref, 0, 0, 0):
            cp.start()

    @pl.when(i + 1 < n)
    def _():
        for cp in _combine_window_copies(ys_ref, buf_ref, sem_ref, ws_ref, i + 1, (i + 1) % 2, 0):
            cp.start()

    rank = rank_ref[...]
    slots = jnp.where(rank >= 0.0, rank + base_ref[...], -1.0)
    lane_iota = lax.broadcasted_iota(jnp.int32, (tm, COMBINE_WIN), 1)

    def gathered(buf_slot, row_offset):
        parts = []
        for e in range(N_EXPERTS):
            first = ws_ref[i * N_EXPERTS + e] + row_offset
            hit = slots[:, e:e + 1] == (first + lane_iota).astype(F32)
            parts.append(jnp.where(hit, 1.0, 0.0).astype(BF16))
        return jnp.dot(jnp.concatenate(parts, axis=1), buf_ref[buf_slot], preferred_element_type=F32)

    for cp in _combine_window_copies(ys_ref, buf_ref, sem_ref, ws_ref, i, i % 2, 0):
        cp.wait()
    out_ref[...] = h_ref[...] + gathered(i % 2, 0)

    @pl.when(second_ref[i] > 0)
    def _():
        copies = _combine_window_copies(ys_ref, buf_ref, sem_ref, ws_ref, i, 2, COMBINE_WIN)
        for cp in copies:
            cp.start()
        for cp in copies:
            cp.wait()
        out_ref[...] += gathered(2, COMBINE_WIN)


def _combine(h2d, rank, base_row, ys, win_start, tile_second):
    t = h2d.shape[0]
    tm = TM_PROJ
    row = lambda n: pl.BlockSpec((tm, n), lambda i, *_: (i, 0))
    grid_spec = pltpu.PrefetchScalarGridSpec(
        num_scalar_prefetch=2,
        grid=(t // tm,),
        in_specs=[row(D_MODEL), row(LANE), pl.BlockSpec((1, LANE), lambda i, *_: (0, 0)),
                  pl.BlockSpec(memory_space=pl.ANY)],
        out_specs=row(D_MODEL),
        scratch_shapes=[pltpu.VMEM((3, N_EXPERTS * COMBINE_WIN, D_MODEL), BF16),
                        pltpu.SemaphoreType.DMA((3, N_EXPERTS))],
    )
    return pl.pallas_call(
        _combine_kernel,
        out_shape=jax.ShapeDtypeStruct((t, D_MODEL), F32),
        grid_spec=grid_spec,
        compiler_params=_cparams(("arbitrary",)),
        name="moe_combine",
    )(win_start, tile_second, h2d, rank, base_row, ys)


def _moe_plan(counts, starts, t):
    n_tok_tiles = t // TM_PROJ
    n_tiles = 2 * t // TM_MOE + N_EXPERTS + 1
    n_slots = n_tiles * TM_MOE
    counts = counts.astype(jnp.int32)
    cum = jnp.concatenate([starts.astype(jnp.int32), counts[None, :]], axis=0)
    padded = (counts + TM_MOE - 1) // TM_MOE * TM_MOE
    ends = jnp.cumsum(padded)
    base = ends - padded
    total = ends[-1]
    tile_first = jnp.arange(n_tiles, dtype=jnp.int32) * TM_MOE
    tile_expert = jnp.minimum(jnp.sum(ends[None, :] <= tile_first[:, None], axis=1), N_EXPERTS - 1).astype(jnp.int32)
    tile_active = (tile_first < total).astype(jnp.int32)
    n_groups = n_slots // SLOT_GROUP
    g_first = jnp.arange(n_groups, dtype=jnp.int32) * SLOT_GROUP
    g_expert = tile_expert[g_first // TM_MOE]
    g_r0 = g_first - base[g_expert]
    cum_e = cum.T[g_expert]
    lo = jnp.sum(cum_e <= g_r0[:, None], axis=1) - 1
    hi = jnp.sum(cum_e < (g_r0 + SLOT_GROUP)[:, None], axis=1)
    g_valid = (g_first < total) & (g_r0 < counts[g_expert])
    lo = jnp.clip(lo, 0, n_tok_tiles - 1)
    hi = jnp.clip(hi, 0, n_tok_tiles)
    g_n = jnp.where(g_valid, (jnp.maximum(hi - lo, 0) * TM_PROJ + DISPATCH_K - 1) // DISPATCH_K, 0)
    lo = jnp.minimum(lo * TM_PROJ, t - jnp.maximum(g_n, 1) * DISPATCH_K)
    first = base[None, :] + cum[:-1]
    cnt = cum[1:] - cum[:-1]
    ws = first // SLOT_ALIGN * SLOT_ALIGN
    pieces = jnp.where(cnt > 0, (first + cnt - ws + COMBINE_WIN - 1) // COMBINE_WIN, 0)
    return dict(tile_expert=tile_expert, tile_active=tile_active, g_expert=g_expert, g_r0=g_r0.astype(jnp.int32),
                g_lo=lo.astype(jnp.int32), g_n=g_n.astype(jnp.int32), base=base,
                ws=ws.reshape(-1).astype(jnp.int32), second=jnp.any(pieces > 1, axis=1).astype(jnp.int32))


def _pad_heads_cols(w, heads, dim, pad):
    lead = w.shape[:-1]
    w = w.reshape(lead + (heads, dim))
    w = jnp.pad(w, [(0, 0)] * len(lead) + [(0, 0), (0, pad - dim)])
    return w.reshape(lead + (heads * pad,))


def _row(v):
    return v.reshape(1, -1).astype(F32)


def kernel(x, mem, positions, mem_norm, w_mem_kv, ln_mix0, w_in0, conv_w0, b_i0, b_f0, h_norm0, mq_norm0, mk_norm0, w_out0, ln_ffn0, w_gate0, w_up0, w_down0, ln_mix1, w_in1, cq_norm1, ckv_norm1, w_uq1, w_ukv1, q_norm1, k_norm1, mq_norm1, mk_norm1, w_out1, ln_ffn1, w_router1, we_gate1, we_up1, we_down1):
    batch, seq, _ = x.shape
    t = batch * seq
    h = x.reshape(t, D_MODEL)
    tile4 = lambda g: _row(jnp.tile(g, MEM_HEADS))

    k4_0, k4_1, v4 = _memkv(mem.reshape(batch * MEM_TOKENS, D_MODEL), _row(mem_norm), w_mem_kv.astype(BF16),
                            tile4(mk_norm0), tile4(mk_norm1), batch)

    mw = MAIN_WIDTH
    w_q, w_k, w_v, w_o = (w_in0[:, i * mw:(i + 1) * mw] for i in range(4))
    w_gates = w_in0[:, 4 * mw:4 * mw + 2 * ML_HEADS]
    w_mq = w_in0[:, 4 * mw + 2 * ML_HEADS:]
    padh = lambda w: _pad_heads_cols(w, ML_HEADS, ML_HEAD_DIM, HEAD_PAD)
    wqk = jnp.concatenate([padh(w_q), padh(w_k)], axis=1).astype(BF16)
    wvo = jnp.concatenate([padh(w_v), padh(w_o)], axis=1).astype(BF16)
    wg = jnp.pad(w_gates, ((0, 0), (0, GATE_PAD - 2 * ML_HEADS))).astype(BF16)
    wgt = w_gates.T.astype(BF16)
    cw = jnp.concatenate([padh(conv_w0[:, :mw]), padh(conv_w0[:, mw:])], axis=1).astype(F32)
    bias = jnp.concatenate([b_i0, b_f0]).astype(F32)
    bias_row = jnp.pad(bias, (0, GATE_PAD - 2 * ML_HEADS)).reshape(1, GATE_PAD)
    bias_col = bias.reshape(2 * ML_HEADS, 1)
    q0, kt0, v0, so, mq0, gates, gatest = _in0(h, _row(ln_mix0), wqk, wvo, w_mq.astype(BF16), wg, wgt, cw,
                                               bias_row, bias_col, seq)
    hgain = jnp.pad(h_norm0.astype(F32), (0, HEAD_PAD - ML_HEAD_DIM)).reshape(1, HEAD_PAD)
    ffn_groups = lambda w: w.reshape(N_EXPERTS, w.shape[0] // N_EXPERTS, w.shape[1])
    hn, (wg0, wu0, wd0) = _mlstm(q0, kt0, v0, so, gates, gatest, hgain,
                                 [ffn_groups(w_gate0), ffn_groups(w_up0), ffn_groups(w_down0)], batch, seq)
    wm0 = jnp.pad(w_out0[:mw].reshape(ML_HEADS, ML_HEAD_DIM, D_MODEL),
                  ((0, 0), (0, HEAD_PAD - ML_HEAD_DIM), (0, 0))).reshape(ML_PAD, D_MODEL).astype(BF16)
    h = _out_proj(hn, mq0, h, k4_0, v4, tile4(mq_norm0), wm0, w_out0[mw:].astype(BF16), seq)
    h, we_gate_bf = _ffn(h, _row(ln_ffn0), wg0.reshape(w_gate0.shape), wu0.reshape(w_up0.shape),
                         wd0.reshape(w_down0.shape), we_gate1)

    w_cq = w_in1[:, :Q_LORA]
    w_ckv = w_in1[:, Q_LORA:Q_LORA + KV_LORA]
    w_kr = jnp.pad(w_in1[:, Q_LORA + KV_LORA:Q_LORA + KV_LORA + MLA_ROPE], ((0, 0), (0, LANE - MLA_ROPE)))
    w_mq1 = w_in1[:, Q_LORA + KV_LORA + MLA_ROPE:]
    w1 = jnp.concatenate([w_cq, w_ckv, w_kr, w_mq1], axis=1).astype(BF16)
    wuq = _pad_heads_cols(w_uq1, MLA_HEADS, MLA_QK, HEAD_PAD).astype(BF16)
    wukv = w_ukv1.reshape(KV_LORA, MLA_HEADS, MLA_NOPE + MLA_V)
    wuk = wukv[:, :, :MLA_NOPE].reshape(KV_LORA, MLA_HEADS * MLA_NOPE).astype(BF16)
    wuvt = wukv[:, :, MLA_NOPE:].reshape(KV_LORA, MLA_HEADS * MLA_V).T.astype(BF16)
    gq = jnp.pad(q_norm1.astype(F32), (0, HEAD_PAD - MLA_QK)).reshape(1, HEAD_PAD)
    gkn = _row(k_norm1[:MLA_NOPE])
    gkr = jnp.pad(k_norm1[MLA_NOPE:].astype(F32), (0, LANE - MLA_ROPE)).reshape(1, LANE)
    half = MLA_ROPE // 2
    inv_freq = ROPE_THETA ** (-jnp.arange(half, dtype=F32) / half)
    invf = jnp.concatenate([inv_freq, inv_freq, jnp.zeros((LANE - MLA_ROPE,), F32)]).reshape(1, LANE)
    q1, k1, v1, mq1 = _in1(h, positions.reshape(t, 1).astype(jnp.int32), _row(ln_mix1), w1, _row(cq_norm1),
                           _row(ckv_norm1), wuq, wuk, wuvt, gq, gkn, gkr, invf)
    o1, we_up_bf, we_down_bf = _attention(q1, k1, v1, we_up1, we_down1, batch, seq)
    h = _out_proj(o1, mq1, h, k4_1, v4, tile4(mq_norm1), w_out1[:mw].astype(BF16), w_out1[mw:].astype(BF16), seq)

    wr = jnp.pad(w_router1.astype(F32), ((0, 0), (0, LANE - N_EXPERTS)))
    wr_hi = wr.astype(BF16)
    wr_lo = (wr - wr_hi.astype(F32)).astype(BF16)
    xn, gatest, rank, rankt, starts, counts = _router(h, _row(ln_ffn1), wr_hi, wr_lo)
    plan = _moe_plan(counts[0, :N_EXPERTS], starts[::8, :N_EXPERTS], t)
    xs, gate_rows = _dispatch(xn, rankt, gatest, plan["g_expert"], plan["g_r0"], plan["g_lo"], plan["g_n"])
    ys = _experts(xs, gate_rows, plan["tile_expert"], plan["tile_active"], we_gate_bf, we_up_bf, we_down_bf)
    base_row = jnp.pad(plan["base"].astype(F32), (0, LANE - N_EXPERTS)).reshape(1, LANE)
    h = _combine(h, rank, base_row, ys, plan["ws"], plan["second"])
    return h.reshape(batch, seq, D_MODEL)
```

```python
import functools

import jax
import jax.numpy as jnp
import numpy as np
from jax import lax
from jax.experimental import pallas as pl
from jax.experimental.pallas import tpu as pltpu

F32 = jnp.float32
BF16 = jnp.bfloat16

D_MODEL = 1024
MEM_TOKENS = 256
MEM_HEADS = 4
MEM_WIDTH = 256
MEM_HEAD_DIM = 64
MAIN_WIDTH = 768
ML_HEADS = 4
ML_HEAD_DIM = 192
CONV_WIDTH = 4
MLA_HEADS = 6
MLA_NOPE = 128
MLA_ROPE = 64
MLA_QK = 192
MLA_V = 128
Q_LORA = 384
KV_LORA = 128
ROPE_THETA = 10000.0
D_FF = 3584
N_EXPERTS = 8
EPS = 1e-6

LANE = 128
HEAD_PAD = 256
ML_PAD = ML_HEADS * HEAD_PAD
MLA_PAD = MLA_HEADS * HEAD_PAD
GATE_PAD = 128
ML_CHUNK = 256
TM_PROJ = 256
TM_ROUTER = 1024
TM_IN0 = 256
TM_IN1 = 256
TM_OUT = 1024
TM_FFN = 512
FF_CHUNK = 512
TM_MOE = 512
SLOT_GROUP = 128
DISPATCH_K = 1024
DISPATCH_GROUPS = 4
CAST_PARTS = 4
SLOT_ALIGN = 16
COMBINE_WIN = 128
COMBINE_MAX_WINDOWS = -(-(TM_PROJ + SLOT_ALIGN - 1) // COMBINE_WIN)
TQ = 2048
TK = 512
NEG = -1e30
LOG2E = 1.4426950408889634
VMEM_LIMIT = 56 * 1024 * 1024
EXPERT_VMEM_LIMIT = 62 * 1024 * 1024


def _cparams(sem, vmem=VMEM_LIMIT):
    return pltpu.CompilerParams(dimension_semantics=sem, vmem_limit_bytes=vmem)


def _rms(x, g):
    return x * lax.rsqrt(jnp.mean(x * x, axis=-1, keepdims=True) + EPS) * g


def _split_dot(x, w_bf16):
    hi = x.astype(BF16)
    lo = (x - hi.astype(F32)).astype(BF16)
    return (jnp.dot(hi, w_bf16, preferred_element_type=F32)
            + jnp.dot(lo, w_bf16, preferred_element_type=F32))


def _block_diag_ones(n, blk):
    r = lax.broadcasted_iota(jnp.int32, (n, n), 0) // blk
    c = lax.broadcasted_iota(jnp.int32, (n, n), 1) // blk
    return jnp.where(r == c, 1.0, 0.0).astype(BF16)


def _const_spec(shape):
    nd = len(shape)
    return pl.BlockSpec(shape, lambda *_: (0,) * nd)


def _memkv_kernel(mem_ref, g_ref, w_ref, gk0_ref, gk1_ref, k0_ref, k1_ref, v_ref):
    xn = _rms(mem_ref[...], g_ref[...]).astype(BF16)
    kv = jnp.dot(xn, w_ref[...], preferred_element_type=F32)
    k = kv[:, :MEM_WIDTH]
    v = kv[:, MEM_WIDTH:]
    bd = _block_diag_ones(MEM_WIDTH, MEM_HEAD_DIM)
    ms = _split_dot(k * k, bd) * (1.0 / MEM_HEAD_DIM)
    kn = k * lax.rsqrt(ms + EPS)
    lane_head = lax.broadcasted_iota(jnp.int32, (MEM_TOKENS, MEM_WIDTH), 1) // MEM_HEAD_DIM
    for h in range(MEM_HEADS):
        sel = lane_head == h
        k0_ref[h] = jnp.where(sel, kn * gk0_ref[...], 0.0).astype(BF16)
        k1_ref[h] = jnp.where(sel, kn * gk1_ref[...], 0.0).astype(BF16)
        v_ref[h] = jnp.where(sel, v, 0.0).astype(BF16)


def _memkv(mem2d, mem_norm, w_kv, gk0, gk1, batch):
    out = jax.ShapeDtypeStruct((batch * MEM_HEADS, MEM_TOKENS, MEM_WIDTH), BF16)
    hspec = pl.BlockSpec((MEM_HEADS, MEM_TOKENS, MEM_WIDTH), lambda b: (b, 0, 0))
    return pl.pallas_call(
        _memkv_kernel,
        out_shape=(out, out, out),
        grid=(batch,),
        in_specs=[pl.BlockSpec((MEM_TOKENS, D_MODEL), lambda b: (b, 0)),
                  _const_spec((1, D_MODEL)), _const_spec((D_MODEL, 2 * MEM_WIDTH)),
                  _const_spec((1, MEM_WIDTH)), _const_spec((1, MEM_WIDTH))],
        out_specs=(hspec, hspec, hspec),
        compiler_params=_cparams(("parallel",)),
        name="memkv",
    )(mem2d, mem_norm, w_kv, gk0, gk1)


def _in0_kernel(tiles_per_seq, h_ref, hp_ref, g_ref, wqk_ref, wvo_ref, wmq_ref, wg_ref, wgt_ref,
                cw_ref, bias_ref, biast_ref,
                q_ref, kt_ref, v_ref, so_ref, mq_ref, gates_ref, gatest_ref):
    i = pl.program_id(0)
    tm = h_ref.shape[0]
    g = g_ref[...]
    xn = _rms(h_ref[...], g).astype(BF16)
    first = (i % tiles_per_seq) == 0
    xpn = _rms(hp_ref[...], g).astype(BF16)
    row8 = lax.broadcasted_iota(jnp.int32, (8, FF_CHUNK), 0)
    ncol = wqk_ref.shape[1]
    for c in range(ncol // FF_CHUNK):
        cs = slice(c * FF_CHUNK, (c + 1) * FF_CHUNK)
        w = wqk_ref[:, cs]
        u = jnp.dot(xn, w, preferred_element_type=F32)
        up = jnp.dot(xpn, w, preferred_element_type=F32)
        up = jnp.where(first, 0.0, up)
        acc = u * cw_ref[CONV_WIDTH - 1:CONV_WIDTH, cs]
        for k in range(1, CONV_WIDTH):
            rolled = pltpu.roll(u, k, 0)
            prev = pltpu.roll(up, k, 0)
            head = jnp.where(row8 < k, prev, rolled[:8])
            shifted = jnp.concatenate([head, rolled[8:]], axis=0)
            acc = acc + shifted * cw_ref[CONV_WIDTH - 1 - k:CONV_WIDTH - k, cs]
        y = acc * jax.nn.sigmoid(acc)
        if c < ML_PAD // FF_CHUNK:
            q_ref[:, cs] = (y * (ML_HEAD_DIM ** -0.5)).astype(BF16)
        else:
            ks = slice(c * FF_CHUNK - ML_PAD, (c + 1) * FF_CHUNK - ML_PAD)
            kt_ref[ks, :] = y.T.astype(BF16)
    zvo = jnp.dot(xn, wvo_ref[...], preferred_element_type=F32)
    v_ref[...] = zvo[:, :ML_PAD].astype(BF16)
    so_ref[...] = jax.nn.sigmoid(zvo[:, ML_PAD:]).astype(BF16)
    mq_ref[...] = jnp.dot(xn, wmq_ref[...], preferred_element_type=F32).astype(BF16)

    def gate_act(z, is_input_gate):
        logsig = jnp.minimum(z, 0.0) - jnp.log(1.0 + jnp.exp(-jnp.abs(z)))
        return jnp.where(is_input_gate, z, logsig)

    zg = jnp.dot(xn, wg_ref[...], preferred_element_type=F32) + bias_ref[...]
    lane = lax.broadcasted_iota(jnp.int32, (tm, GATE_PAD), 1)
    gates_ref[...] = gate_act(zg, lane < ML_HEADS)
    zgt = lax.dot_general(wgt_ref[...], xn, (((1,), (1,)), ((), ())),
                          preferred_element_type=F32) + biast_ref[...]
    row = lax.broadcasted_iota(jnp.int32, (8, tm), 0)
    gatest_ref[...] = gate_act(zgt, row < ML_HEADS)


def _in0(h2d, ln, wqk, wvo, wmq, wg, wgt, cw, bias, biast, seq):
    t = h2d.shape[0]
    tm = TM_IN0
    tiles_per_seq = seq // tm
    tok = lambda n, dt: jax.ShapeDtypeStruct((t, n), dt)
    row = lambda n: pl.BlockSpec((tm, n), lambda i: (i, 0))
    col = lambda n: pl.BlockSpec((n, tm), lambda i: (0, i))
    return pl.pallas_call(
        functools.partial(_in0_kernel, tiles_per_seq),
        out_shape=(tok(ML_PAD, BF16), jax.ShapeDtypeStruct((ML_PAD, t), BF16), tok(ML_PAD, BF16),
                   tok(ML_PAD, BF16), tok(MEM_WIDTH, BF16), tok(GATE_PAD, F32), jax.ShapeDtypeStruct((8, t), F32)),
        grid=(t // tm,),
        in_specs=[row(D_MODEL),
                  pl.BlockSpec((8, D_MODEL), lambda i: (jnp.maximum(i * (tm // 8) - 1, 0), 0)),
                  _const_spec((1, D_MODEL)), _const_spec(wqk.shape), _const_spec(wvo.shape),
                  _const_spec(wmq.shape), _const_spec(wg.shape), _const_spec(wgt.shape),
                  _const_spec(cw.shape), _const_spec(bias.shape), _const_spec(biast.shape)],
        out_specs=(row(ML_PAD), col(ML_PAD), row(ML_PAD), row(ML_PAD), row(MEM_WIDTH), row(GATE_PAD), col(8)),
        compiler_params=_cparams(("parallel",)),
        name="in_proj0",
    )(h2d, h2d, ln, wqk, wvo, wmq, wg, wgt, cw, bias, biast)


def _mlstm_kernel(batch, cast_blocks, q_ref, v_ref, so_ref, gates_ref, *rest):
    n_cast = len(cast_blocks)
    kt_refs, gatest_refs = rest[:batch], rest[batch:2 * batch]
    hg_ref = rest[2 * batch]
    cast_src = rest[2 * batch + 1:2 * batch + 1 + n_cast]
    out_ref = rest[2 * batch + 1 + n_cast]
    cast_dst = rest[2 * batch + 2 + n_cast:2 * batch + 2 + 2 * n_cast]
    c_ref, m_ref = rest[2 * batch + 2 + 2 * n_cast:]
    L = q_ref.shape[1]
    for src, dst, n_blocks in zip(cast_src, cast_dst, cast_blocks):
        _cast_block(pl.program_id(0), src, dst, n_blocks)

    @pl.when(pl.program_id(0) == 0)
    def _():
        c_ref[...] = jnp.zeros_like(c_ref)
        m_ref[...] = jnp.zeros_like(m_ref)

    r = lax.broadcasted_iota(jnp.int32, (L, L), 0)
    c = lax.broadcasted_iota(jnp.int32, (L, L), 1)
    causal = r >= c
    tri_lo = jnp.where(causal, 1.0, 0.0).astype(BF16)
    tri_up = jnp.where(r <= c, 1.0, 0.0).astype(BF16)
    lane = lax.broadcasted_iota(jnp.int32, (L, HEAD_PAD), 1)
    hg = hg_ref[...]
    for b in range(batch):
        gts = gates_ref[b]
        gtt = gatest_refs[b][...]
        hi = gts.astype(BF16)
        lo = (gts - hi.astype(F32)).astype(BF16)
        b_cols = (jnp.dot(tri_lo, hi, preferred_element_type=F32)
                  + jnp.dot(tri_lo, lo, preferred_element_type=F32))
        b_rows = _split_dot(gtt, tri_up)
        for h in range(ML_HEADS):
            hs = slice(h * HEAD_PAD, (h + 1) * HEAD_PAD)
            state = b * ML_HEADS + h
            m_h = m_ref[state][0:1, 0:1]
            b_c = b_cols[:, ML_HEADS + h:ML_HEADS + h + 1]
            c_r = b_rows[ML_HEADS + h:ML_HEADS + h + 1, :] - gtt[h:h + 1, :]
            log_d = jnp.where(causal, b_c - c_r, NEG)
            inter = b_c + m_h
            m_row = jnp.maximum(inter, jnp.max(log_d, axis=-1, keepdims=True))
            d = jnp.exp(log_d - m_row)
            s_inter = jnp.exp(inter - m_row)
            q_h = q_ref[b, :, hs]
            kt_h = kt_refs[b][hs, :]
            v_aug = jnp.where(lane == ML_HEAD_DIM, 1.0, v_ref[b, :, hs].astype(F32)).astype(BF16)
            s = jnp.dot(q_h, kt_h, preferred_element_type=F32)
            qkd = (s * d).astype(BF16)
            c_old = c_ref[state]
            num = (jnp.dot(qkd, v_aug, preferred_element_type=F32)
                   + s_inter * jnp.dot(q_h, c_old.astype(BF16), preferred_element_type=F32))
            den = jnp.sum(jnp.where(lane == ML_HEAD_DIM, num, 0.0), axis=-1, keepdims=True)
            ht = num * (1.0 / jnp.maximum(jnp.abs(den), jnp.exp(-m_row)))
            ht = jnp.where(lane < ML_HEAD_DIM, ht, 0.0)
            ms = jnp.sum(ht * ht, axis=-1, keepdims=True) * (1.0 / ML_HEAD_DIM)
            hn = ht * lax.rsqrt(ms + EPS) * hg * so_ref[b, :, hs].astype(F32)
            out_ref[b, :, hs] = hn.astype(BF16)
            b_last = b_c[L - 1:L, :]
            g_r = b_last - c_r
            m_new = jnp.maximum(b_last + m_h, jnp.max(g_r, axis=-1, keepdims=True))
            w_r = jnp.exp(g_r - m_new)
            decay = jnp.exp(b_last + m_h - m_new)
            ktw = (kt_h.astype(F32) * w_r).astype(BF16)
            c_ref[state] = decay * c_old + jnp.dot(ktw, v_aug, preferred_element_type=F32)
            m_ref[state] = jnp.broadcast_to(m_new, m_ref.shape[1:])


def _mlstm(q, kt, v, so, gates, gatest, hgain, casts, batch, seq):
    L = ML_CHUNK
    nc = seq // L
    seq3 = lambda a: a.reshape(batch, seq, a.shape[-1])
    row = lambda n: pl.BlockSpec((batch, L, n), lambda c: (0, c, 0))
    col = lambda rows: [pl.BlockSpec((rows, L), functools.partial(lambda b, c: (0, b * nc + c), b))
                        for b in range(batch)]
    cast_blocks = tuple(a.shape[0] * CAST_PARTS for a in casts)
    assert all(n <= nc for n in cast_blocks)
    cast_specs = [_cast_spec(a.shape, lambda c: c) for a in casts]
    outs = pl.pallas_call(
        functools.partial(_mlstm_kernel, batch, cast_blocks),
        out_shape=[jax.ShapeDtypeStruct((batch, seq, ML_PAD), BF16)]
                  + [jax.ShapeDtypeStruct(a.shape, BF16) for a in casts],
        grid=(nc,),
        in_specs=[row(ML_PAD), row(ML_PAD), row(ML_PAD), row(GATE_PAD)] + col(ML_PAD) + col(8)
                 + [_const_spec((1, HEAD_PAD))] + cast_specs,
        out_specs=[row(ML_PAD)] + cast_specs,
        scratch_shapes=[pltpu.VMEM((batch * ML_HEADS, HEAD_PAD, HEAD_PAD), F32),
                        pltpu.VMEM((batch * ML_HEADS, 8, LANE), F32)],
        compiler_params=_cparams(("arbitrary",)),
        name="mlstm",
    )(seq3(q), seq3(v), seq3(so), seq3(gates), *([kt] * batch), *([gatest] * batch), hgain, *casts)
    return outs[0].reshape(batch * seq, ML_PAD), outs[1:]


def _out_kernel(main_ref, mq_ref, h_ref, k4_ref, v4_ref, gq_ref, wm_ref, wmem_ref, out_ref):
    q = mq_ref[...].astype(F32)
    bd = _block_diag_ones(MEM_WIDTH, MEM_HEAD_DIM)
    ms = _split_dot(q * q, bd) * (1.0 / MEM_HEAD_DIM)
    qn = (q * lax.rsqrt(ms + EPS) * gq_ref[...] * (MEM_HEAD_DIM ** -0.5)).astype(BF16)
    ymem = jnp.zeros(q.shape, F32)
    for h in range(MEM_HEADS):
        s = lax.dot_general(qn, k4_ref[h], (((1,), (1,)), ((), ())), preferred_element_type=F32)
        e = jnp.exp(s - jnp.max(s, axis=-1, keepdims=True))
        inv = 1.0 / jnp.sum(e, axis=-1, keepdims=True)
        ymem = ymem + jnp.dot(e.astype(BF16), v4_ref[h], preferred_element_type=F32) * inv
    y = (jnp.dot(main_ref[...], wm_ref[...], preferred_element_type=F32)
         + jnp.dot(ymem.astype(BF16), wmem_ref[...], preferred_element_type=F32))
    out_ref[...] = h_ref[...] + y


def _out_proj(main, mq, h2d, k4, v4, gq, wm, wmem, seq):
    t = h2d.shape[0]
    tm = TM_OUT
    tiles_per_seq = seq // tm
    row = lambda n: pl.BlockSpec((tm, n), lambda i: (i, 0))
    memspec = pl.BlockSpec((MEM_HEADS, MEM_TOKENS, MEM_WIDTH), lambda i: (i // tiles_per_seq, 0, 0))
    return pl.pallas_call(
        _out_kernel,
        out_shape=jax.ShapeDtypeStruct((t, D_MODEL), F32),
        grid=(t // tm,),
        in_specs=[row(main.shape[1]), row(MEM_WIDTH), row(D_MODEL), memspec, memspec,
                  _const_spec((1, MEM_WIDTH)), _const_spec(wm.shape), _const_spec(wmem.shape)],
        out_specs=row(D_MODEL),
        compiler_params=_cparams(("parallel",)),
        name="out_proj",
    )(main, mq, h2d, k4, v4, gq, wm, wmem)


def _cast_spec(shape, step_of):
    e, r, c = shape
    n_blocks = e * CAST_PARTS

    def index_map(*grid_idx):
        s = jnp.minimum(step_of(*grid_idx), n_blocks - 1)
        return (s // CAST_PARTS, s % CAST_PARTS, 0)

    return pl.BlockSpec((1, r // CAST_PARTS, c), index_map)


def _cast_block(step, src_ref, dst_ref, n_blocks):
    @pl.when(step < n_blocks)
    def _():
        dst_ref[...] = src_ref[...].astype(dst_ref.dtype)


def _ffn_kernel(h_ref, g_ref, wg_ref, wu_ref, wd_ref, cast_src_ref, out_ref, cast_dst_ref):
    _cast_block(pl.program_id(0), cast_src_ref, cast_dst_ref, N_EXPERTS * CAST_PARTS)
    x = h_ref[...]
    xn = _rms(x, g_ref[...]).astype(BF16)
    acc = x
    for c in range(D_FF // FF_CHUNK):
        cs = slice(c * FF_CHUNK, (c + 1) * FF_CHUNK)
        gt = jnp.dot(xn, wg_ref[:, cs], preferred_element_type=F32)
        up = jnp.dot(xn, wu_ref[:, cs], preferred_element_type=F32)
        a = (gt * jax.nn.sigmoid(gt) * up).astype(BF16)
        acc = acc + jnp.dot(a, wd_ref[cs, :], preferred_element_type=F32)
    out_ref[...] = acc


def _ffn(h2d, ln, wg, wu, wd, cast_src):
    t = h2d.shape[0]
    tm = TM_FFN
    assert t // tm >= cast_src.shape[0] * CAST_PARTS
    row = pl.BlockSpec((tm, D_MODEL), lambda i: (i, 0))
    res = lambda shape: pl.BlockSpec(shape, lambda i: (0, 0), pipeline_mode=pl.Buffered(1))
    cast = _cast_spec(cast_src.shape, lambda i: i)
    return pl.pallas_call(
        _ffn_kernel,
        out_shape=(jax.ShapeDtypeStruct((t, D_MODEL), F32), jax.ShapeDtypeStruct(cast_src.shape, BF16)),
        grid=(t // tm,),
        in_specs=[row, _const_spec((1, D_MODEL)), res(wg.shape), res(wu.shape), res(wd.shape), cast],
        out_specs=(row, cast),
        compiler_params=_cparams(("arbitrary",)),
        name="ffn_dense",
    )(h2d, ln, wg, wu, wd, cast_src)


def _in1_kernel(h_ref, pos_ref, g_ref, w1_ref, gcq_ref, gckv_ref, wuq_ref, wuk_ref, wuvt_ref,
                gq_ref, gkn_ref, gkr_ref, invf_ref,
                q_ref, k_ref, vt_ref, mq_ref):
    tm = h_ref.shape[0]
    xn = _rms(h_ref[...], g_ref[...]).astype(BF16)
    z = jnp.dot(xn, w1_ref[...], preferred_element_type=F32)
    cq = z[:, :Q_LORA]
    ckv = z[:, Q_LORA:Q_LORA + KV_LORA]
    kr = z[:, Q_LORA + KV_LORA:Q_LORA + KV_LORA + LANE]
    mq_ref[...] = z[:, Q_LORA + KV_LORA + LANE:].astype(BF16)
    cqn = _rms(cq, gcq_ref[...]).astype(BF16)
    ckvn = _rms(ckv, gckv_ref[...]).astype(BF16)
    q = jnp.dot(cqn, wuq_ref[...], preferred_element_type=F32)
    kn = jnp.dot(ckvn, wuk_ref[...], preferred_element_type=F32)
    vt_ref[...] = lax.dot_general(wuvt_ref[...], ckvn, (((1,), (1,)), ((), ())),
                                  preferred_element_type=F32).astype(BF16)

    half = MLA_ROPE // 2
    eye = lax.broadcasted_iota(jnp.int32, (tm, tm), 0) == lax.broadcasted_iota(jnp.int32, (tm, tm), 1)
    pos = jnp.sum(jnp.where(eye, pos_ref[...].astype(F32), 0.0), axis=-1, keepdims=True)
    ang = pos * invf_ref[...]
    lane = lax.broadcasted_iota(jnp.int32, (tm, LANE), 1)
    cos = jnp.where(lane < MLA_ROPE, jnp.cos(ang), 0.0)
    sin = jnp.sin(ang)
    sin_hi = jnp.where((lane >= half) & (lane < MLA_ROPE), sin, 0.0)
    sin_lo = jnp.where(lane < half, -sin, 0.0)

    def rope(t):
        return t * cos + pltpu.roll(t, half, 1) * sin_hi + pltpu.roll(t, LANE - half, 1) * sin_lo

    scale = MLA_QK ** -0.5 * LOG2E
    gq = gq_ref[...]
    for h in range(MLA_HEADS):
        qh = q[:, h * HEAD_PAD:(h + 1) * HEAD_PAD]
        rs = lax.rsqrt(jnp.sum(qh * qh, axis=-1, keepdims=True) * (1.0 / MLA_QK) + EPS) * scale
        qn = qh * rs * gq
        q_ref[:, h * HEAD_PAD:h * HEAD_PAD + LANE] = qn[:, :LANE].astype(BF16)
        q_ref[:, h * HEAD_PAD + LANE:(h + 1) * HEAD_PAD] = rope(qn[:, LANE:]).astype(BF16)
    ss_r = jnp.sum(kr * kr, axis=-1, keepdims=True)
    krr = rope(kr * gkr_ref[...])
    gkn = gkn_ref[...]
    for h in range(MLA_HEADS):
        kh = kn[:, h * MLA_NOPE:(h + 1) * MLA_NOPE]
        rs = lax.rsqrt((jnp.sum(kh * kh, axis=-1, keepdims=True) + ss_r) * (1.0 / MLA_QK) + EPS)
        k_ref[:, h * HEAD_PAD:h * HEAD_PAD + LANE] = (kh * rs * gkn).astype(BF16)
        k_ref[:, h * HEAD_PAD + LANE:(h + 1) * HEAD_PAD] = (krr * rs).astype(BF16)


def _in1(h2d, pos2d, ln, w1, gcq, gckv, wuq, wuk, wuvt, gq, gkn, gkr, invf):
    t = h2d.shape[0]
    tm = TM_IN1
    tok = lambda n: jax.ShapeDtypeStruct((t, n), BF16)
    row = lambda n: pl.BlockSpec((tm, n), lambda i: (i, 0))
    consts = [ln, w1, gcq, gckv, wuq, wuk, wuvt, gq, gkn, gkr, invf]
    return pl.pallas_call(
        _in1_kernel,
        out_shape=(tok(MLA_PAD), tok(MLA_PAD), jax.ShapeDtypeStruct((MAIN_WIDTH, t), BF16), tok(MEM_WIDTH)),
        grid=(t // tm,),
        in_specs=[row(D_MODEL), pl.BlockSpec((1, tm), lambda i: (0, i))] + [_const_spec(a.shape) for a in consts],
        out_specs=(row(MLA_PAD), row(MLA_PAD), pl.BlockSpec((MAIN_WIDTH, tm), lambda i: (0, i)), row(MEM_WIDTH)),
        compiler_params=_cparams(("parallel",)),
        name="in_proj1",
    )(h2d, pos2d, *consts)


def _attn_kernel(q_ref, k_ref, vt_ref, cast_a_ref, cast_b_ref, o_ref, cast_a_out_ref, cast_b_out_ref,
                 m_ref, l_ref, acc_ref, st_ref, mt_ref):
    qi = pl.program_id(2)
    step = (pl.program_id(0) * pl.num_programs(1) + pl.program_id(1)) * pl.num_programs(2) + qi
    _cast_block(step, cast_a_ref, cast_a_out_ref, N_EXPERTS * CAST_PARTS)
    _cast_block(step, cast_b_ref, cast_b_out_ref, N_EXPERTS * CAST_PARTS)
    m_ref[...] = jnp.full_like(m_ref, NEG)
    l_ref[...] = jnp.zeros_like(l_ref)
    acc_ref[...] = jnp.zeros_like(acc_ref)
    n_diag = TQ // TK
    n_full = qi * n_diag
    nt_dims = (((1,), (1,)), ((), ()))

    def scores(j, slot):
        koff = pl.multiple_of(j * TK, TK)
        st = lax.dot_general(k_ref[pl.ds(koff, TK), :], q_ref[...], nt_dims, preferred_element_type=F32)
        st_ref[slot] = st
        mt_ref[slot] = jnp.max(st, axis=0, keepdims=True)

    def absorb(st, m_tile, vt, cols):
        m_old = m_ref[:, cols]
        m_new = jnp.maximum(m_old, m_tile)
        alpha = jnp.exp2(m_old - m_new)
        e = jnp.exp2(st - m_new)
        l_ref[:, cols] = alpha * l_ref[:, cols] + jnp.sum(e, axis=0, keepdims=True)
        acc_ref[:, cols] = alpha * acc_ref[:, cols] + jnp.dot(vt, e.astype(BF16), preferred_element_type=F32)
        m_ref[:, cols] = m_new

    def finish(j, slot):
        koff = pl.multiple_of(j * TK, TK)
        absorb(st_ref[slot], mt_ref[slot], vt_ref[:, pl.ds(koff, TK)], slice(None))

    @pl.when(n_full > 0)
    def _():
        scores(0, 0)
        scores(1, 1)

        def body(p, carry):
            j = 2 * p
            finish(j, 0)
            scores(j + 2, 0)
            finish(j + 1, 1)
            scores(j + 3, 1)
            return carry

        lax.fori_loop(0, n_full // 2 - 1, body, 0)
        finish(n_full - 2, 0)
        finish(n_full - 1, 1)

    kd0 = pl.multiple_of(n_full * TK, TK)
    for u in range(n_diag):
        nk = (u + 1) * TK
        cols = slice(u * TK, (u + 1) * TK)
        st = lax.dot_general(k_ref[pl.ds(kd0, nk), :], q_ref[cols, :], nt_dims, preferred_element_type=F32)
        kpos = lax.broadcasted_iota(jnp.int32, st.shape, 0)
        qpos = u * TK + lax.broadcasted_iota(jnp.int32, st.shape, 1)
        st = jnp.where(kpos <= qpos, st, NEG)
        absorb(st, jnp.max(st, axis=0, keepdims=True), vt_ref[:, pl.ds(kd0, nk)], cols)
    o_ref[...] = (acc_ref[...] * (1.0 / l_ref[...])).T.astype(o_ref.dtype)


def _attention(q, k, v, cast_a, cast_b, batch, seq):
    t = q.shape[0]
    nq = seq // TQ
    assert batch * MLA_HEADS * nq >= cast_a.shape[0] * CAST_PARTS
    step_of = lambda b, h, i: (b * MLA_HEADS + h) * nq + i
    cast_specs = [_cast_spec(cast_a.shape, step_of), _cast_spec(cast_b.shape, step_of)]
    return pl.pallas_call(
        _attn_kernel,
        out_shape=(jax.ShapeDtypeStruct((t, MAIN_WIDTH), BF16), jax.ShapeDtypeStruct(cast_a.shape, BF16),
                   jax.ShapeDtypeStruct(cast_b.shape, BF16)),
        grid=(batch, MLA_HEADS, nq),
        in_specs=[pl.BlockSpec((TQ, HEAD_PAD), lambda b, h, i: (b * nq + i, h)),
                  pl.BlockSpec((seq, HEAD_PAD), lambda b, h, i: (b, h)),
                  pl.BlockSpec((MLA_V, seq), lambda b, h, i: (h, b))] + cast_specs,
        out_specs=[pl.BlockSpec((TQ, MLA_V), lambda b, h, i: (b * nq + i, h))] + cast_specs,
        scratch_shapes=[pltpu.VMEM((1, TQ), F32), pltpu.VMEM((1, TQ), F32), pltpu.VMEM((MLA_V, TQ), F32),
                        pltpu.VMEM((2, TK, TQ), F32), pltpu.VMEM((2, 1, TQ), F32)],
        compiler_params=_cparams(("arbitrary", "arbitrary", "arbitrary")),
        name="mla_attention",
    )(q, k, v, cast_a, cast_b)


def _router_kernel(h_ref, g_ref, whi_ref, wlo_ref,
                   xn_ref, gatest_ref, rank_ref, rankt_ref, starts_ref, counts_ref, run_ref):
    @pl.when(pl.program_id(0) == 0)
    def _():
        run_ref[...] = jnp.zeros_like(run_ref)

    tm = h_ref.shape[0]
    xn = _rms(h_ref[...], g_ref[...])
    hi = xn.astype(BF16)
    lo = (xn - hi.astype(F32)).astype(BF16)
    xn_ref[...] = hi
    whi = whi_ref[...]
    logits = (jnp.dot(hi, whi, preferred_element_type=F32) + jnp.dot(lo, whi, preferred_element_type=F32)
              + jnp.dot(hi, wlo_ref[...], preferred_element_type=F32))
    lane = lax.broadcasted_iota(jnp.int32, logits.shape, 1).astype(F32)
    logits = jnp.where(lane < N_EXPERTS, logits, NEG)
    m1 = jnp.max(logits, axis=-1, keepdims=True)
    i1 = jnp.min(jnp.where(logits == m1, lane, float(LANE)), axis=-1, keepdims=True)
    rest = jnp.where(lane == i1, NEG, logits)
    m2 = jnp.max(rest, axis=-1, keepdims=True)
    i2 = jnp.min(jnp.where(rest == m2, lane, float(LANE)), axis=-1, keepdims=True)
    e2 = jnp.exp(m2 - m1)
    w1 = 1.0 / (1.0 + e2)
    w2 = e2 * w1
    gates = jnp.where(lane == i1, w1, jnp.where(lane == i2, w2, 0.0))
    gatest_ref[...] = gates.T[:N_EXPERTS, :]
    routed = (lane == i1) | (lane == i2)
    oh = jnp.where(routed, 1.0, 0.0)
    r = lax.broadcasted_iota(jnp.int32, (TM_PROJ, TM_PROJ), 0)
    c = lax.broadcasted_iota(jnp.int32, (TM_PROJ, TM_PROJ), 1)
    before = jnp.where(r > c, 1.0, 0.0).astype(BF16)
    run = run_ref[0:1, :]
    ranks = []
    for ch in range(tm // TM_PROJ):
        rows = slice(ch * TM_PROJ, (ch + 1) * TM_PROJ)
        starts_ref[8 * ch:8 * ch + 8, :] = jnp.broadcast_to(run, (8, LANE))
        prefix = jnp.dot(before, oh[rows].astype(BF16), preferred_element_type=F32)
        ranks.append(jnp.where(routed[rows], run + prefix, -1.0))
        run = run + jnp.sum(oh[rows], axis=0, keepdims=True)
    rank = jnp.concatenate(ranks, axis=0)
    rank_ref[...] = rank
    rankt_ref[...] = rank.T[:N_EXPERTS, :]
    total = jnp.broadcast_to(run, run_ref.shape)
    run_ref[...] = total
    counts_ref[...] = total


def _router(h2d, ln, whi, wlo):
    t = h2d.shape[0]
    tm = min(TM_ROUTER, t)
    nt = t // tm
    chunk_rows = 8 * (tm // TM_PROJ)
    row = lambda n: pl.BlockSpec((tm, n), lambda i: (i, 0))
    return pl.pallas_call(
        _router_kernel,
        out_shape=(jax.ShapeDtypeStruct((t, D_MODEL), BF16), jax.ShapeDtypeStruct((N_EXPERTS, t), F32),
                   jax.ShapeDtypeStruct((t, LANE), F32), jax.ShapeDtypeStruct((N_EXPERTS, t), F32),
                   jax.ShapeDtypeStruct((nt * chunk_rows, LANE), F32), jax.ShapeDtypeStruct((8, LANE), F32)),
        grid=(nt,),
        in_specs=[row(D_MODEL), _const_spec((1, D_MODEL)), _const_spec(whi.shape), _const_spec(wlo.shape)],
        out_specs=(row(D_MODEL), pl.BlockSpec((N_EXPERTS, tm), lambda i: (0, i)), row(LANE),
                   pl.BlockSpec((N_EXPERTS, tm), lambda i: (0, i)),
                   pl.BlockSpec((chunk_rows, LANE), lambda i: (i, 0)), _const_spec((8, LANE))),
        scratch_shapes=[pltpu.VMEM((8, LANE), F32)],
        compiler_params=_cparams(("arbitrary",)),
        name="moe_router",
    )(h2d, ln, whi, wlo)


def _dispatch_kernel(ge_ref, gr0_ref, gstart_ref, gn_ref, xn_ref, rankt_ref, gatest_ref, xs_ref, gs_ref,
                     acc_ref, gacc_ref):
    groups = [pl.program_id(0) * DISPATCH_GROUPS + u for u in range(DISPATCH_GROUPS)]
    eye = (lax.broadcasted_iota(jnp.int32, (SLOT_GROUP, SLOT_GROUP), 0)
           == lax.broadcasted_iota(jnp.int32, (SLOT_GROUP, SLOT_GROUP), 1))

    def gathered(g, start):
        start = pl.multiple_of(start, TM_PROJ)
        want = (gr0_ref[g] + lax.broadcasted_iota(jnp.int32, (SLOT_GROUP, DISPATCH_K), 0)).astype(F32)
        span = (pl.ds(ge_ref[g], 1), pl.ds(start, DISPATCH_K))
        hit = rankt_ref[span] == want
        rows = jnp.dot(jnp.where(hit, 1.0, 0.0).astype(BF16), xn_ref[pl.ds(start, DISPATCH_K), :],
                       preferred_element_type=F32)
        gate = jnp.sum(jnp.where(hit, gatest_ref[span], 0.0), axis=-1, keepdims=True)
        return rows, gate

    def as_row(col):
        return jnp.sum(jnp.where(eye, col, 0.0), axis=0, keepdims=True)

    single = gn_ref[groups[0]] <= 1
    for g in groups[1:]:
        single = single & (gn_ref[g] <= 1)

    @pl.when(single)
    def _():
        gate_rows = []
        for u, g in enumerate(groups):
            rows, gate = gathered(g, gstart_ref[g])
            xs_ref[u * SLOT_GROUP:(u + 1) * SLOT_GROUP, :] = rows.astype(BF16)
            gate_rows.append(as_row(gate))
        gs_ref[...] = jnp.broadcast_to(jnp.concatenate(gate_rows, axis=1), gs_ref.shape)

    @pl.when(jnp.logical_not(single))
    def _():
        gate_rows = []
        for u, g in enumerate(groups):
            acc_ref[...] = jnp.zeros_like(acc_ref)
            gacc_ref[...] = jnp.zeros_like(gacc_ref)

            def body(k, carry, g=g):
                rows, gate = gathered(g, gstart_ref[g] + k * DISPATCH_K)
                acc_ref[...] += rows
                gacc_ref[...] += gate
                return carry

            lax.fori_loop(0, gn_ref[g], body, 0)
            xs_ref[u * SLOT_GROUP:(u + 1) * SLOT_GROUP, :] = acc_ref[...].astype(BF16)
            gate_rows.append(as_row(gacc_ref[...]))
        gs_ref[...] = jnp.broadcast_to(jnp.concatenate(gate_rows, axis=1), gs_ref.shape)


def _dispatch(xn, rankt, gatest, g_expert, g_rank0, g_start, g_nspans):
    t = xn.shape[0]
    n_steps = g_expert.shape[0] // DISPATCH_GROUPS
    rows = DISPATCH_GROUPS * SLOT_GROUP
    assert rows == TM_MOE
    res = lambda shape: pl.BlockSpec(shape, lambda g, *_: (0, 0), pipeline_mode=pl.Buffered(1))
    grid_spec = pltpu.PrefetchScalarGridSpec(
        num_scalar_prefetch=4,
        grid=(n_steps,),
        in_specs=[res((t, D_MODEL)), res((N_EXPERTS, t)), res((N_EXPERTS, t))],
        out_specs=(pl.BlockSpec((rows, D_MODEL), lambda g, *_: (g, 0)),
                   pl.BlockSpec((8, rows), lambda g, *_: (g, 0))),
        scratch_shapes=[pltpu.VMEM((SLOT_GROUP, D_MODEL), F32), pltpu.VMEM((SLOT_GROUP, 1), F32)],
    )
    return pl.pallas_call(
        _dispatch_kernel,
        out_shape=(jax.ShapeDtypeStruct((n_steps * rows, D_MODEL), BF16),
                   jax.ShapeDtypeStruct((n_steps * 8, rows), F32)),
        grid_spec=grid_spec,
        compiler_params=_cparams(("arbitrary",)),
        name="moe_dispatch",
    )(g_expert, g_rank0, g_start, g_nspans, xn, rankt, gatest)


def _expert_kernel(te_ref, ta_ref, x_ref, gs_ref, wg_ref, wu_ref, wd_ref, out_ref):
    active = ta_ref[pl.program_id(0)] > 0

    @pl.when(active)
    def _():
        x = x_ref[...]
        acc = None
        for c in range(D_FF // FF_CHUNK):
            cs = slice(c * FF_CHUNK, (c + 1) * FF_CHUNK)
            gt = jnp.dot(x, wg_ref[0, :, cs], preferred_element_type=F32)
            up = jnp.dot(x, wu_ref[0, :, cs], preferred_element_type=F32)
            a = (gt * jax.nn.sigmoid(gt) * up).astype(BF16)
            y = jnp.dot(a, wd_ref[0, cs, :], preferred_element_type=F32)
            acc = y if acc is None else acc + y
        tm = x.shape[0]
        eye = lax.broadcasted_iota(jnp.int32, (tm, tm), 0) == lax.broadcasted_iota(jnp.int32, (tm, tm), 1)
        gate = jnp.sum(jnp.where(eye, gs_ref[0:1, :], 0.0), axis=-1, keepdims=True)
        out_ref[...] = (acc * gate).astype(out_ref.dtype)

    @pl.when(jnp.logical_not(active))
    def _():
        out_ref[...] = jnp.zeros_like(out_ref)


def _experts(x, gate_rows, tile_expert, tile_active, wg, wu, wd):
    n_tiles = tile_expert.shape[0]
    tm = TM_MOE
    wmap = lambda j, te, ta: (te[j], 0, 0)
    tile = lambda j, te, ta: (jnp.where(ta[j] > 0, j, 0), 0)
    grid_spec = pltpu.PrefetchScalarGridSpec(
        num_scalar_prefetch=2,
        grid=(n_tiles,),
        in_specs=[pl.BlockSpec((tm, D_MODEL), tile),
                  pl.BlockSpec((8, tm), tile),
                  pl.BlockSpec((1, D_MODEL, D_FF), wmap),
                  pl.BlockSpec((1, D_MODEL, D_FF), wmap),
                  pl.BlockSpec((1, D_FF, D_MODEL), wmap)],
        out_specs=pl.BlockSpec((tm, D_MODEL), lambda j, te, ta: (j, 0)),
    )
    return pl.pallas_call(
        _expert_kernel,
        out_shape=jax.ShapeDtypeStruct((n_tiles * tm, D_MODEL), BF16),
        grid_spec=grid_spec,
        compiler_params=_cparams(("arbitrary",), EXPERT_VMEM_LIMIT),
        name="moe_experts",
    )(tile_expert, tile_active, x, gate_rows, wg, wu, wd)


def _combine_window_copies(ys_ref, buf_ref, sem_ref, ws_ref, tile, buf_slot, row_offset):
    copies = []
    for e in range(N_EXPERTS):
        start = pl.multiple_of(ws_ref[tile * N_EXPERTS + e] + row_offset, SLOT_ALIGN)
        copies.append(pltpu.make_async_copy(ys_ref.at[pl.ds(start, COMBINE_WIN), :],
                                            buf_ref.at[buf_slot, pl.ds(e * COMBINE_WIN, COMBINE_WIN), :],
                                            sem_ref.at[buf_slot, e]))
    return copies


def _combine_kernel(ws_ref, second_ref, h_ref, rank_ref, base_ref, ys_ref, out_ref, buf_ref, sem_ref):
    i = pl.program_id(0)
    n = pl.num_programs(0)
    tm = h_ref.shape[0]

    @pl.when(i == 0)
    def _():
        for cp in _combine_window_copies(ys_ref, buf_ref, sem_ref, ws_ref, 0, 0, 0):
            cp.start()

    @pl.when(i + 1 < n)
    def _():
        for cp in _combine_window_copies(ys_ref, buf_ref, sem_ref, ws_ref, i + 1, (i + 1) % 2, 0):
            cp.start()

    rank = rank_ref[...]
    slots = jnp.where(rank >= 0.0, rank + base_ref[...], -1.0)
    lane_iota = lax.broadcasted_iota(jnp.int32, (tm, COMBINE_WIN), 1)

    def gathered(buf_slot, row_offset):
        parts = []
        for e in range(N_EXPERTS):
            first = ws_ref[i * N_EXPERTS + e] + row_offset
            hit = slots[:, e:e + 1] == (first + lane_iota).astype(F32)
            parts.append(jnp.where(hit, 1.0, 0.0).astype(BF16))
        return jnp.dot(jnp.concatenate(parts, axis=1), buf_ref[buf_slot], preferred_element_type=F32)

    for cp in _combine_window_copies(ys_ref, buf_ref, sem_ref, ws_ref, i, i % 2, 0):
        cp.wait()
    out_ref[...] = h_ref[...] + gathered(i % 2, 0)

    for extra in range(1, COMBINE_MAX_WINDOWS):
        @pl.when(second_ref[i] > extra)
        def _(extra=extra):
            copies = _combine_window_copies(ys_ref, buf_ref, sem_ref, ws_ref, i, 2, extra * COMBINE_WIN)
            for cp in copies:
                cp.start()
            for cp in copies:
                cp.wait()
            out_ref[...] += gathered(2, extra * COMBINE_WIN)


def _combine(h2d, rank, base_row, ys, win_start, tile_second):
    t = h2d.shape[0]
    tm = TM_PROJ
    row = lambda n: pl.BlockSpec((tm, n), lambda i, *_: (i, 0))
    grid_spec = pltpu.PrefetchScalarGridSpec(
        num_scalar_prefetch=2,
        grid=(t // tm,),
        in_specs=[row(D_MODEL), row(LANE), pl.BlockSpec((1, LANE), lambda i, *_: (0, 0)),
                  pl.BlockSpec(memory_space=pl.ANY)],
        out_specs=row(D_MODEL),
        scratch_shapes=[pltpu.VMEM((3, N_EXPERTS * COMBINE_WIN, D_MODEL), BF16),
                        pltpu.SemaphoreType.DMA((3, N_EXPERTS))],
    )
    return pl.pallas_call(
        _combine_kernel,
        out_shape=jax.ShapeDtypeStruct((t, D_MODEL), F32),
        grid_spec=grid_spec,
        compiler_params=_cparams(("arbitrary",)),
        name="moe_combine",
    )(win_start, tile_second, h2d, rank, base_row, ys)


def _moe_plan(counts, starts, t):
    n_tok_tiles = t // TM_PROJ
    n_tiles = 2 * t // TM_MOE + N_EXPERTS + 1
    n_slots = n_tiles * TM_MOE
    counts = counts.astype(jnp.int32)
    cum = jnp.concatenate([starts.astype(jnp.int32), counts[None, :]], axis=0)
    padded = (counts + TM_MOE - 1) // TM_MOE * TM_MOE
    ends = jnp.cumsum(padded)
    base = ends - padded
    total = ends[-1]
    tile_first = jnp.arange(n_tiles, dtype=jnp.int32) * TM_MOE
    tile_expert = jnp.minimum(jnp.sum(ends[None, :] <= tile_first[:, None], axis=1), N_EXPERTS - 1).astype(jnp.int32)
    tile_active = (tile_first < total).astype(jnp.int32)
    n_groups = n_slots // SLOT_GROUP
    g_first = jnp.arange(n_groups, dtype=jnp.int32) * SLOT_GROUP
    g_expert = tile_expert[g_first // TM_MOE]
    g_r0 = g_first - base[g_expert]
    cum_e = cum.T[g_expert]
    lo = jnp.sum(cum_e <= g_r0[:, None], axis=1) - 1
    hi = jnp.sum(cum_e < (g_r0 + SLOT_GROUP)[:, None], axis=1)
    g_valid = (g_first < total) & (g_r0 < counts[g_expert])
    lo = jnp.clip(lo, 0, n_tok_tiles - 1)
    hi = jnp.clip(hi, 0, n_tok_tiles)
    g_n = jnp.where(g_valid, (jnp.maximum(hi - lo, 0) * TM_PROJ + DISPATCH_K - 1) // DISPATCH_K, 0)
    lo = jnp.minimum(lo * TM_PROJ, t - jnp.maximum(g_n, 1) * DISPATCH_K)
    first = base[None, :] + cum[:-1]
    cnt = cum[1:] - cum[:-1]
    ws = first // SLOT_ALIGN * SLOT_ALIGN
    pieces = jnp.where(cnt > 0, (first + cnt - ws + COMBINE_WIN - 1) // COMBINE_WIN, 0)
    return dict(tile_expert=tile_expert, tile_active=tile_active, g_expert=g_expert, g_r0=g_r0.astype(jnp.int32),
                g_lo=lo.astype(jnp.int32), g_n=g_n.astype(jnp.int32), base=base,
                ws=ws.reshape(-1).astype(jnp.int32), second=jnp.max(pieces, axis=1).astype(jnp.int32))


def _pad_heads_cols(w, heads, dim, pad):
    lead = w.shape[:-1]
    w = w.reshape(lead + (heads, dim))
    w = jnp.pad(w, [(0, 0)] * len(lead) + [(0, 0), (0, pad - dim)])
    return w.reshape(lead + (heads * pad,))


def _row(v):
    return v.reshape(1, -1).astype(F32)


def kernel(x, mem, positions, mem_norm, w_mem_kv, ln_mix0, w_in0, conv_w0, b_i0, b_f0, h_norm0, mq_norm0, mk_norm0, w_out0, ln_ffn0, w_gate0, w_up0, w_down0, ln_mix1, w_in1, cq_norm1, ckv_norm1, w_uq1, w_ukv1, q_norm1, k_norm1, mq_norm1, mk_norm1, w_out1, ln_ffn1, w_router1, we_gate1, we_up1, we_down1):
    batch, seq, _ = x.shape
    t = batch * seq
    h = x.reshape(t, D_MODEL)
    tile4 = lambda g: _row(jnp.tile(g, MEM_HEADS))

    k4_0, k4_1, v4 = _memkv(mem.reshape(batch * MEM_TOKENS, D_MODEL), _row(mem_norm), w_mem_kv.astype(BF16),
                            tile4(mk_norm0), tile4(mk_norm1), batch)

    mw = MAIN_WIDTH
    w_q, w_k, w_v, w_o = (w_in0[:, i * mw:(i + 1) * mw] for i in range(4))
    w_gates = w_in0[:, 4 * mw:4 * mw + 2 * ML_HEADS]
    w_mq = w_in0[:, 4 * mw + 2 * ML_HEADS:]
    padh = lambda w: _pad_heads_cols(w, ML_HEADS, ML_HEAD_DIM, HEAD_PAD)
    wqk = jnp.concatenate([padh(w_q), padh(w_k)], axis=1).astype(BF16)
    wvo = jnp.concatenate([padh(w_v), padh(w_o)], axis=1).astype(BF16)
    wg = jnp.pad(w_gates, ((0, 0), (0, GATE_PAD - 2 * ML_HEADS))).astype(BF16)
    wgt = w_gates.T.astype(BF16)
    cw = jnp.concatenate([padh(conv_w0[:, :mw]), padh(conv_w0[:, mw:])], axis=1).astype(F32)
    bias = jnp.concatenate([b_i0, b_f0]).astype(F32)
    bias_row = jnp.pad(bias, (0, GATE_PAD - 2 * ML_HEADS)).reshape(1, GATE_PAD)
    bias_col = bias.reshape(2 * ML_HEADS, 1)
    q0, kt0, v0, so, mq0, gates, gatest = _in0(h, _row(ln_mix0), wqk, wvo, w_mq.astype(BF16), wg, wgt, cw,
                                               bias_row, bias_col, seq)
    hgain = jnp.pad(h_norm0.astype(F32), (0, HEAD_PAD - ML_HEAD_DIM)).reshape(1, HEAD_PAD)
    ffn_groups = lambda w: w.reshape(N_EXPERTS, w.shape[0] // N_EXPERTS, w.shape[1])
    hn, (wg0, wu0, wd0) = _mlstm(q0, kt0, v0, so, gates, gatest, hgain,
                                 [ffn_groups(w_gate0), ffn_groups(w_up0), ffn_groups(w_down0)], batch, seq)
    wm0 = jnp.pad(w_out0[:mw].reshape(ML_HEADS, ML_HEAD_DIM, D_MODEL),
                  ((0, 0), (0, HEAD_PAD - ML_HEAD_DIM), (0, 0))).reshape(ML_PAD, D_MODEL).astype(BF16)
    h = _out_proj(hn, mq0, h, k4_0, v4, tile4(mq_norm0), wm0, w_out0[mw:].astype(BF16), seq)
    h, we_gate_bf = _ffn(h, _row(ln_ffn0), wg0.reshape(w_gate0.shape), wu0.reshape(w_up0.shape),
                         wd0.reshape(w_down0.shape), we_gate1)

    w_cq = w_in1[:, :Q_LORA]
    w_ckv = w_in1[:, Q_LORA:Q_LORA + KV_LORA]
    w_kr = jnp.pad(w_in1[:, Q_LORA + KV_LORA:Q_LORA + KV_LORA + MLA_ROPE], ((0, 0), (0, LANE - MLA_ROPE)))
    w_mq1 = w_in1[:, Q_LORA + KV_LORA + MLA_ROPE:]
    w1 = jnp.concatenate([w_cq, w_ckv, w_kr, w_mq1], axis=1).astype(BF16)
    wuq = _pad_heads_cols(w_uq1, MLA_HEADS, MLA_QK, HEAD_PAD).astype(BF16)
    wukv = w_ukv1.reshape(KV_LORA, MLA_HEADS, MLA_NOPE + MLA_V)
    wuk = wukv[:, :, :MLA_NOPE].reshape(KV_LORA, MLA_HEADS * MLA_NOPE).astype(BF16)
    wuvt = wukv[:, :, MLA_NOPE:].reshape(KV_LORA, MLA_HEADS * MLA_V).T.astype(BF16)
    gq = jnp.pad(q_norm1.astype(F32), (0, HEAD_PAD - MLA_QK)).reshape(1, HEAD_PAD)
    gkn = _row(k_norm1[:MLA_NOPE])
    gkr = jnp.pad(k_norm1[MLA_NOPE:].astype(F32), (0, LANE - MLA_ROPE)).reshape(1, LANE)
    half = MLA_ROPE // 2
    inv_freq = ROPE_THETA ** (-jnp.arange(half, dtype=F32) / half)
    invf = jnp.concatenate([inv_freq, inv_freq, jnp.zeros((LANE - MLA_ROPE,), F32)]).reshape(1, LANE)
    q1, k1, v1, mq1 = _in1(h, positions.reshape(1, t).astype(jnp.int32), _row(ln_mix1), w1, _row(cq_norm1),
                           _row(ckv_norm1), wuq, wuk, wuvt, gq, gkn, gkr, invf)
    o1, we_up_bf, we_down_bf = _attention(q1, k1, v1, we_up1, we_down1, batch, seq)
    h = _out_proj(o1, mq1, h, k4_1, v4, tile4(mq_norm1), w_out1[:mw].astype(BF16), w_out1[mw:].astype(BF16), seq)

    wr = jnp.pad(w_router1.astype(F32), ((0, 0), (0, LANE - N_EXPERTS)))
    wr_hi = wr.astype(BF16)
    wr_lo = (wr - wr_hi.astype(F32)).astype(BF16)
    xn, gatest, rank, rankt, starts, counts = _router(h, _row(ln_ffn1), wr_hi, wr_lo)
    plan = _moe_plan(counts[0, :N_EXPERTS], starts[::8, :N_EXPERTS], t)
    xs, gate_rows = _dispatch(xn, rankt, gatest, plan["g_expert"], plan["g_r0"], plan["g_lo"], plan["g_n"])
    ys = _experts(xs, gate_rows, plan["tile_expert"], plan["tile_active"], we_gate_bf, we_up_bf, we_down_bf)
    base_row = jnp.pad(plan["base"].astype(F32), (0, LANE - N_EXPERTS)).reshape(1, LANE)
    h = _combine(h, rank, base_row, ys, plan["ws"], plan["second"])
    return h.reshape(batch, seq, D_MODEL)
```

```python
import functools

import jax
import jax.numpy as jnp
import numpy as np
from jax import lax
from jax.experimental import pallas as pl
from jax.experimental.pallas import tpu as pltpu

F32 = jnp.float32
BF16 = jnp.bfloat16

D_MODEL = 1024
MEM_TOKENS = 256
MEM_HEADS = 4
MEM_WIDTH = 256
MEM_HEAD_DIM = 64
MAIN_WIDTH = 768
ML_HEADS = 4
ML_HEAD_DIM = 192
CONV_WIDTH = 4
MLA_HEADS = 6
MLA_NOPE = 128
MLA_ROPE = 64
MLA_QK = 192
MLA_V = 128
Q_LORA = 384
KV_LORA = 128
ROPE_THETA = 10000.0
D_FF = 3584
N_EXPERTS = 8
EPS = 1e-6

LANE = 128
HEAD_PAD = 256
ML_PAD = ML_HEADS * HEAD_PAD
MLA_PAD = MLA_HEADS * HEAD_PAD
GATE_PAD = 128
ML_CHUNK = 256
TM_PROJ = 256
TM_ROUTER = 1024
TM_IN0 = 256
TM_IN1 = 256
TM_OUT = 1024
TM_FFN = 512
FF_CHUNK = 512
TM_MOE = 512
SLOT_GROUP = 128
DISPATCH_K = 1024
DISPATCH_GROUPS = 4
CAST_PARTS = 4
SLOT_ALIGN = 16
COMBINE_WIN = 128
COMBINE_MAX_WINDOWS = -(-(TM_PROJ + SLOT_ALIGN - 1) // COMBINE_WIN)
TQ = 2048
TK = 512
NEG = -1e30
LOG2E = 1.4426950408889634
VMEM_LIMIT = 56 * 1024 * 1024
EXPERT_VMEM_LIMIT = 62 * 1024 * 1024


def _cparams(sem, vmem=VMEM_LIMIT):
    return pltpu.CompilerParams(dimension_semantics=sem, vmem_limit_bytes=vmem)


def _rms(x, g):
    return x * lax.rsqrt(jnp.mean(x * x, axis=-1, keepdims=True) + EPS) * g


def _split_dot(x, w_bf16):
    hi = x.astype(BF16)
    lo = (x - hi.astype(F32)).astype(BF16)
    return (jnp.dot(hi, w_bf16, preferred_element_type=F32)
            + jnp.dot(lo, w_bf16, preferred_element_type=F32))


def _block_diag_ones(n, blk):
    r = lax.broadcasted_iota(jnp.int32, (n, n), 0) // blk
    c = lax.broadcasted_iota(jnp.int32, (n, n), 1) // blk
    return jnp.where(r == c, 1.0, 0.0).astype(BF16)


def _const_spec(shape):
    nd = len(shape)
    return pl.BlockSpec(shape, lambda *_: (0,) * nd)


def _memkv_kernel(mem_ref, g_ref, w_ref, gk0_ref, gk1_ref, k0_ref, k1_ref, v_ref):
    xn = _rms(mem_ref[...], g_ref[...]).astype(BF16)
    kv = jnp.dot(xn, w_ref[...], preferred_element_type=F32)
    k = kv[:, :MEM_WIDTH]
    v = kv[:, MEM_WIDTH:]
    bd = _block_diag_ones(MEM_WIDTH, MEM_HEAD_DIM)
    ms = _split_dot(k * k, bd) * (1.0 / MEM_HEAD_DIM)
    kn = k * lax.rsqrt(ms + EPS)
    lane_head = lax.broadcasted_iota(jnp.int32, (MEM_TOKENS, MEM_WIDTH), 1) // MEM_HEAD_DIM
    for h in range(MEM_HEADS):
        sel = lane_head == h
        k0_ref[h] = jnp.where(sel, kn * gk0_ref[...], 0.0).astype(BF16)
        k1_ref[h] = jnp.where(sel, kn * gk1_ref[...], 0.0).astype(BF16)
        v_ref[h] = jnp.where(sel, v, 0.0).astype(BF16)


def _memkv(mem2d, mem_norm, w_kv, gk0, gk1, batch):
    out = jax.ShapeDtypeStruct((batch * MEM_HEADS, MEM_TOKENS, MEM_WIDTH), BF16)
    hspec = pl.BlockSpec((MEM_HEADS, MEM_TOKENS, MEM_WIDTH), lambda b: (b, 0, 0))
    return pl.pallas_call(
        _memkv_kernel,
        out_shape=(out, out, out),
        grid=(batch,),
        in_specs=[pl.BlockSpec((MEM_TOKENS, D_MODEL), lambda b: (b, 0)),
                  _const_spec((1, D_MODEL)), _const_spec((D_MODEL, 2 * MEM_WIDTH)),
                  _const_spec((1, MEM_WIDTH)), _const_spec((1, MEM_WIDTH))],
        out_specs=(hspec, hspec, hspec),
        compiler_params=_cparams(("parallel",)),
        name="memkv",
    )(mem2d, mem_norm, w_kv, gk0, gk1)


def _in0_kernel(tiles_per_seq, h_ref, hp_ref, g_ref, wqk_ref, wvo_ref, wmq_ref, wg_ref, wgt_ref,
                cw_ref, bias_ref, biast_ref,
                q_ref, kt_ref, v_ref, so_ref, mq_ref, gates_ref, gatest_ref):
    i = pl.program_id(0)
    tm = h_ref.shape[0]
    g = g_ref[...]
    xn = _rms(h_ref[...], g).astype(BF16)
    first = (i % tiles_per_seq) == 0
    xpn = _rms(hp_ref[...], g).astype(BF16)
    row8 = lax.broadcasted_iota(jnp.int32, (8, FF_CHUNK), 0)
    ncol = wqk_ref.shape[1]
    for c in range(ncol // FF_CHUNK):
        cs = slice(c * FF_CHUNK, (c + 1) * FF_CHUNK)
        w = wqk_ref[:, cs]
        u = jnp.dot(xn, w, preferred_element_type=F32)
        up = jnp.dot(xpn, w, preferred_element_type=F32)
        up = jnp.where(first, 0.0, up)
        acc = u * cw_ref[CONV_WIDTH - 1:CONV_WIDTH, cs]
        for k in range(1, CONV_WIDTH):
            rolled = pltpu.roll(u, k, 0)
            prev = pltpu.roll(up, k, 0)
            head = jnp.where(row8 < k, prev, rolled[:8])
            shifted = jnp.concatenate([head, rolled[8:]], axis=0)
            acc = acc + shifted * cw_ref[CONV_WIDTH - 1 - k:CONV_WIDTH - k, cs]
        y = acc * jax.nn.sigmoid(acc)
        if c < ML_PAD // FF_CHUNK:
            q_ref[:, cs] = (y * (ML_HEAD_DIM ** -0.5)).astype(BF16)
        else:
            ks = slice(c * FF_CHUNK - ML_PAD, (c + 1) * FF_CHUNK - ML_PAD)
            kt_ref[ks, :] = y.T.astype(BF16)
    zvo = jnp.dot(xn, wvo_ref[...], preferred_element_type=F32)
    v_ref[...] = zvo[:, :ML_PAD].astype(BF16)
    so_ref[...] = jax.nn.sigmoid(zvo[:, ML_PAD:]).astype(BF16)
    mq_ref[...] = jnp.dot(xn, wmq_ref[...], preferred_element_type=F32).astype(BF16)

    def gate_act(z, is_input_gate):
        logsig = jnp.minimum(z, 0.0) - jnp.log(1.0 + jnp.exp(-jnp.abs(z)))
        return jnp.where(is_input_gate, z, logsig)

    zg = jnp.dot(xn, wg_ref[...], preferred_element_type=F32) + bias_ref[...]
    lane = lax.broadcasted_iota(jnp.int32, (tm, GATE_PAD), 1)
    gates_ref[...] = gate_act(zg, lane < ML_HEADS)
    zgt = lax.dot_general(wgt_ref[...], xn, (((1,), (1,)), ((), ())),
                          preferred_element_type=F32) + biast_ref[...]
    row = lax.broadcasted_iota(jnp.int32, (8, tm), 0)
    gatest_ref[...] = gate_act(zgt, row < ML_HEADS)


def _in0(h2d, ln, wqk, wvo, wmq, wg, wgt, cw, bias, biast, seq):
    t = h2d.shape[0]
    tm = TM_IN0
    tiles_per_seq = seq // tm
    tok = lambda n, dt: jax.ShapeDtypeStruct((t, n), dt)
    row = lambda n: pl.BlockSpec((tm, n), lambda i: (i, 0))
    col = lambda n: pl.BlockSpec((n, tm), lambda i: (0, i))
    return pl.pallas_call(
        functools.partial(_in0_kernel, tiles_per_seq),
        out_shape=(tok(ML_PAD, BF16), jax.ShapeDtypeStruct((ML_PAD, t), BF16), tok(ML_PAD, BF16),
                   tok(ML_PAD, BF16), tok(MEM_WIDTH, BF16), tok(GATE_PAD, F32), jax.ShapeDtypeStruct((8, t), F32)),
        grid=(t // tm,),
        in_specs=[row(D_MODEL),
                  pl.BlockSpec((8, D_MODEL), lambda i: (jnp.maximum(i * (tm // 8) - 1, 0), 0)),
                  _const_spec((1, D_MODEL)), _const_spec(wqk.shape), _const_spec(wvo.shape),
                  _const_spec(wmq.shape), _const_spec(wg.shape), _const_spec(wgt.shape),
                  _const_spec(cw.shape), _const_spec(bias.shape), _const_spec(biast.shape)],
        out_specs=(row(ML_PAD), col(ML_PAD), row(ML_PAD), row(ML_PAD), row(MEM_WIDTH), row(GATE_PAD), col(8)),
        compiler_params=_cparams(("parallel",)),
        name="in_proj0",
    )(h2d, h2d, ln, wqk, wvo, wmq, wg, wgt, cw, bias, biast)


def _mlstm_kernel(batch, cast_blocks, q_ref, v_ref, so_ref, gates_ref, *rest):
    n_cast = len(cast_blocks)
    kt_refs, gatest_refs = rest[:batch], rest[batch:2 * batch]
    hg_ref = rest[2 * batch]
    cast_src = rest[2 * batch + 1:2 * batch + 1 + n_cast]
    out_ref = rest[2 * batch + 1 + n_cast]
    cast_dst = rest[2 * batch + 2 + n_cast:2 * batch + 2 + 2 * n_cast]
    c_ref, m_ref = rest[2 * batch + 2 + 2 * n_cast:]
    L = q_ref.shape[1]
    for src, dst, n_blocks in zip(cast_src, cast_dst, cast_blocks):
        _cast_block(pl.program_id(0), src, dst, n_blocks)

    @pl.when(pl.program_id(0) == 0)
    def _():
        c_ref[...] = jnp.zeros_like(c_ref)
        m_ref[...] = jnp.zeros_like(m_ref)

    r = lax.broadcasted_iota(jnp.int32, (L, L), 0)
    c = lax.broadcasted_iota(jnp.int32, (L, L), 1)
    causal = r >= c
    tri_lo = jnp.where(causal, 1.0, 0.0).astype(BF16)
    tri_up = jnp.where(r <= c, 1.0, 0.0).astype(BF16)
    lane = lax.broadcasted_iota(jnp.int32, (L, HEAD_PAD), 1)
    hg = hg_ref[...]
    for b in range(batch):
        gts = gates_ref[b]
        gtt = gatest_refs[b][...]
        hi = gts.astype(BF16)
        lo = (gts - hi.astype(F32)).astype(BF16)
        b_cols = (jnp.dot(tri_lo, hi, preferred_element_type=F32)
                  + jnp.dot(tri_lo, lo, preferred_element_type=F32))
        b_rows = _split_dot(gtt, tri_up)
        for h in range(ML_HEADS):
            hs = slice(h * HEAD_PAD, (h + 1) * HEAD_PAD)
            state = b * ML_HEADS + h
            m_h = m_ref[state][0:1, 0:1]
            b_c = b_cols[:, ML_HEADS + h:ML_HEADS + h + 1]
            c_r = b_rows[ML_HEADS + h:ML_HEADS + h + 1, :] - gtt[h:h + 1, :]
            log_d = jnp.where(causal, b_c - c_r, NEG)
            inter = b_c + m_h
            m_row = jnp.maximum(inter, jnp.max(log_d, axis=-1, keepdims=True))
            d = jnp.exp(log_d - m_row)
            s_inter = jnp.exp(inter - m_row)
            q_h = q_ref[b, :, hs]
            kt_h = kt_refs[b][hs, :]
            v_aug = jnp.where(lane == ML_HEAD_DIM, 1.0, v_ref[b, :, hs].astype(F32)).astype(BF16)
            s = jnp.dot(q_h, kt_h, preferred_element_type=F32)
            qkd = (s * d).astype(BF16)
            c_old = c_ref[state]
            num = (jnp.dot(qkd, v_aug, preferred_element_type=F32)
                   + s_inter * jnp.dot(q_h, c_old.astype(BF16), preferred_element_type=F32))
            den = jnp.sum(jnp.where(lane == ML_HEAD_DIM, num, 0.0), axis=-1, keepdims=True)
            ht = num * (1.0 / jnp.maximum(jnp.abs(den), jnp.exp(-m_row)))
            ht = jnp.where(lane < ML_HEAD_DIM, ht, 0.0)
            ms = jnp.sum(ht * ht, axis=-1, keepdims=True) * (1.0 / ML_HEAD_DIM)
            hn = ht * lax.rsqrt(ms + EPS) * hg * so_ref[b, :, hs].astype(F32)
            out_ref[b, :, hs] = hn.astype(BF16)
            b_last = b_c[L - 1:L, :]
            g_r = b_last - c_r
            m_new = jnp.maximum(b_last + m_h, jnp.max(g_r, axis=-1, keepdims=True))
            w_r = jnp.exp(g_r - m_new)
            decay = jnp.exp(b_last + m_h - m_new)
            ktw = (kt_h.astype(F32) * w_r).astype(BF16)
            c_ref[state] = decay * c_old + jnp.dot(ktw, v_aug, preferred_element_type=F32)
            m_ref[state] = jnp.broadcast_to(m_new, m_ref.shape[1:])


def _mlstm(q, kt, v, so, gates, gatest, hgain, casts, batch, seq):
    L = ML_CHUNK
    nc = seq // L
    seq3 = lambda a: a.reshape(batch, seq, a.shape[-1])
    row = lambda n: pl.BlockSpec((batch, L, n), lambda c: (0, c, 0))
    col = lambda rows: [pl.BlockSpec((rows, L), functools.partial(lambda b, c: (0, b * nc + c), b))
                        for b in range(batch)]
    cast_blocks = tuple(a.shape[0] * CAST_PARTS for a in casts)
    assert all(n <= nc for n in cast_blocks)
    cast_specs = [_cast_spec(a.shape, lambda c: c) for a in casts]
    outs = pl.pallas_call(
        functools.partial(_mlstm_kernel, batch, cast_blocks),
        out_shape=[jax.ShapeDtypeStruct((batch, seq, ML_PAD), BF16)]
                  + [jax.ShapeDtypeStruct(a.shape, BF16) for a in casts],
        grid=(nc,),
        in_specs=[row(ML_PAD), row(ML_PAD), row(ML_PAD), row(GATE_PAD)] + col(ML_PAD) + col(8)
                 + [_const_spec((1, HEAD_PAD))] + cast_specs,
        out_specs=[row(ML_PAD)] + cast_specs,
        scratch_shapes=[pltpu.VMEM((batch * ML_HEADS, HEAD_PAD, HEAD_PAD), F32),
                        pltpu.VMEM((batch * ML_HEADS, 8, LANE), F32)],
        compiler_params=_cparams(("arbitrary",)),
        name="mlstm",
    )(seq3(q), seq3(v), seq3(so), seq3(gates), *([kt] * batch), *([gatest] * batch), hgain, *casts)
    return outs[0].reshape(batch * seq, ML_PAD), outs[1:]


def _cast_spec(shape, step_of):
    e, r, c = shape
    n_blocks = e * CAST_PARTS

    def index_map(*grid_idx):
        s = jnp.minimum(step_of(*grid_idx), n_blocks - 1)
        return (s // CAST_PARTS, s % CAST_PARTS, 0)

    return pl.BlockSpec((1, r // CAST_PARTS, c), index_map)


def _cast_block(step, src_ref, dst_ref, n_blocks):
    @pl.when(step < n_blocks)
    def _():
        dst_ref[...] = src_ref[...].astype(dst_ref.dtype)


def _mixer_residual(main_ref, mq_ref, h_ref, k4_ref, v4_ref, gq_ref, wm_ref, wmem_ref):
    q = mq_ref[...].astype(F32)
    bd = _block_diag_ones(MEM_WIDTH, MEM_HEAD_DIM)
    ms = _split_dot(q * q, bd) * (1.0 / MEM_HEAD_DIM)
    qn = (q * lax.rsqrt(ms + EPS) * gq_ref[...] * (MEM_HEAD_DIM ** -0.5)).astype(BF16)
    ymem = jnp.zeros(q.shape, F32)
    for h in range(MEM_HEADS):
        s = lax.dot_general(qn, k4_ref[h], (((1,), (1,)), ((), ())), preferred_element_type=F32)
        e = jnp.exp(s - jnp.max(s, axis=-1, keepdims=True))
        inv = 1.0 / jnp.sum(e, axis=-1, keepdims=True)
        ymem = ymem + jnp.dot(e.astype(BF16), v4_ref[h], preferred_element_type=F32) * inv
    y = (jnp.dot(main_ref[...], wm_ref[...], preferred_element_type=F32)
         + jnp.dot(ymem.astype(BF16), wmem_ref[...], preferred_element_type=F32))
    return h_ref[...] + y


def _out_kernel(main_ref, mq_ref, h_ref, k4_ref, v4_ref, gq_ref, wm_ref, wmem_ref, out_ref):
    out_ref[...] = _mixer_residual(main_ref, mq_ref, h_ref, k4_ref, v4_ref, gq_ref, wm_ref, wmem_ref)


def _out_ffn_kernel(main_ref, mq_ref, h_ref, k4_ref, v4_ref, gq_ref, wm_ref, wmem_ref,
                    g_ref, wg_ref, wu_ref, wd_ref, cast_src_ref, out_ref, cast_dst_ref):
    _cast_block(pl.program_id(0), cast_src_ref, cast_dst_ref, N_EXPERTS * CAST_PARTS)
    x = _mixer_residual(main_ref, mq_ref, h_ref, k4_ref, v4_ref, gq_ref, wm_ref, wmem_ref)
    xn = _rms(x, g_ref[...]).astype(BF16)
    acc = x
    for c in range(D_FF // FF_CHUNK):
        cs = slice(c * FF_CHUNK, (c + 1) * FF_CHUNK)
        gt = jnp.dot(xn, wg_ref[:, cs], preferred_element_type=F32)
        up = jnp.dot(xn, wu_ref[:, cs], preferred_element_type=F32)
        a = (gt * jax.nn.sigmoid(gt) * up).astype(BF16)
        acc = acc + jnp.dot(a, wd_ref[cs, :], preferred_element_type=F32)
    out_ref[...] = acc


def _out_proj(main, mq, h2d, k4, v4, gq, wm, wmem, seq, ffn=None):
    t = h2d.shape[0]
    tm = TM_OUT if ffn is None else TM_FFN
    tiles_per_seq = seq // tm
    row = lambda n: pl.BlockSpec((tm, n), lambda i: (i, 0))
    memspec = pl.BlockSpec((MEM_HEADS, MEM_TOKENS, MEM_WIDTH), lambda i: (i // tiles_per_seq, 0, 0))
    in_specs = [row(main.shape[1]), row(MEM_WIDTH), row(D_MODEL), memspec, memspec,
                _const_spec((1, MEM_WIDTH)), _const_spec(wm.shape), _const_spec(wmem.shape)]
    if ffn is None:
        return pl.pallas_call(
            _out_kernel,
            out_shape=jax.ShapeDtypeStruct((t, D_MODEL), F32),
            grid=(t // tm,),
            in_specs=in_specs,
            out_specs=row(D_MODEL),
            compiler_params=_cparams(("parallel",)),
            name="out_proj",
        )(main, mq, h2d, k4, v4, gq, wm, wmem)
    ln, wg, wu, wd, cast_src = ffn
    assert t // tm >= cast_src.shape[0] * CAST_PARTS
    res = lambda shape: pl.BlockSpec(shape, lambda i: (0, 0), pipeline_mode=pl.Buffered(1))
    cast = _cast_spec(cast_src.shape, lambda i: i)
    return pl.pallas_call(
        _out_ffn_kernel,
        out_shape=(jax.ShapeDtypeStruct((t, D_MODEL), F32), jax.ShapeDtypeStruct(cast_src.shape, BF16)),
        grid=(t // tm,),
        in_specs=in_specs + [_const_spec((1, D_MODEL)), res(wg.shape), res(wu.shape), res(wd.shape), cast],
        out_specs=(row(D_MODEL), cast),
        compiler_params=_cparams(("arbitrary",)),
        name="out_proj_ffn",
    )(main, mq, h2d, k4, v4, gq, wm, wmem, ln, wg, wu, wd, cast_src)


def _in1_kernel(h_ref, pos_ref, g_ref, w1_ref, gcq_ref, gckv_ref, wuq_ref, wuk_ref, wuvt_ref,
                gq_ref, gkn_ref, gkr_ref, invf_ref,
                q_ref, k_ref, vt_ref, mq_ref):
    tm = h_ref.shape[0]
    xn = _rms(h_ref[...], g_ref[...]).astype(BF16)
    z = jnp.dot(xn, w1_ref[...], preferred_element_type=F32)
    cq = z[:, :Q_LORA]
    ckv = z[:, Q_LORA:Q_LORA + KV_LORA]
    kr = z[:, Q_LORA + KV_LORA:Q_LORA + KV_LORA + LANE]
    mq_ref[...] = z[:, Q_LORA + KV_LORA + LANE:].astype(BF16)
    cqn = _rms(cq, gcq_ref[...]).astype(BF16)
    ckvn = _rms(ckv, gckv_ref[...]).astype(BF16)
    q = jnp.dot(cqn, wuq_ref[...], preferred_element_type=F32)
    kn = jnp.dot(ckvn, wuk_ref[...], preferred_element_type=F32)
    vt_ref[...] = lax.dot_general(wuvt_ref[...], ckvn, (((1,), (1,)), ((), ())),
                                  preferred_element_type=F32).astype(BF16)

    half = MLA_ROPE // 2
    eye = lax.broadcasted_iota(jnp.int32, (tm, tm), 0) == lax.broadcasted_iota(jnp.int32, (tm, tm), 1)
    pos = jnp.sum(jnp.where(eye, pos_ref[...].astype(F32), 0.0), axis=-1, keepdims=True)
    ang = pos * invf_ref[...]
    lane = lax.broadcasted_iota(jnp.int32, (tm, LANE), 1)
    cos = jnp.where(lane < MLA_ROPE, jnp.cos(ang), 0.0)
    sin = jnp.sin(ang)
    sin_hi = jnp.where((lane >= half) & (lane < MLA_ROPE), sin, 0.0)
    sin_lo = jnp.where(lane < half, -sin, 0.0)

    def rope(t):
        return t * cos + pltpu.roll(t, half, 1) * sin_hi + pltpu.roll(t, LANE - half, 1) * sin_lo

    scale = MLA_QK ** -0.5 * LOG2E
    gq = gq_ref[...]
    for h in range(MLA_HEADS):
        qh = q[:, h * HEAD_PAD:(h + 1) * HEAD_PAD]
        rs = lax.rsqrt(jnp.sum(qh * qh, axis=-1, keepdims=True) * (1.0 / MLA_QK) + EPS) * scale
        qn = qh * rs * gq
        q_ref[:, h * HEAD_PAD:h * HEAD_PAD + LANE] = qn[:, :LANE].astype(BF16)
        q_ref[:, h * HEAD_PAD + LANE:(h + 1) * HEAD_PAD] = rope(qn[:, LANE:]).astype(BF16)
    ss_r = jnp.sum(kr * kr, axis=-1, keepdims=True)
    krr = rope(kr * gkr_ref[...])
    gkn = gkn_ref[...]
    for h in range(MLA_HEADS):
        kh = kn[:, h * MLA_NOPE:(h + 1) * MLA_NOPE]
        rs = lax.rsqrt((jnp.sum(kh * kh, axis=-1, keepdims=True) + ss_r) * (1.0 / MLA_QK) + EPS)
        k_ref[:, h * HEAD_PAD:h * HEAD_PAD + LANE] = (kh * rs * gkn).astype(BF16)
        k_ref[:, h * HEAD_PAD + LANE:(h + 1) * HEAD_PAD] = (krr * rs).astype(BF16)


def _in1(h2d, pos2d, ln, w1, gcq, gckv, wuq, wuk, wuvt, gq, gkn, gkr, invf):
    t = h2d.shape[0]
    tm = TM_IN1
    tok = lambda n: jax.ShapeDtypeStruct((t, n), BF16)
    row = lambda n: pl.BlockSpec((tm, n), lambda i: (i, 0))
    consts = [ln, w1, gcq, gckv, wuq, wuk, wuvt, gq, gkn, gkr, invf]
    return pl.pallas_call(
        _in1_kernel,
        out_shape=(tok(MLA_PAD), tok(MLA_PAD), jax.ShapeDtypeStruct((MAIN_WIDTH, t), BF16), tok(MEM_WIDTH)),
        grid=(t // tm,),
        in_specs=[row(D_MODEL), pl.BlockSpec((1, tm), lambda i: (0, i))] + [_const_spec(a.shape) for a in consts],
        out_specs=(row(MLA_PAD), row(MLA_PAD), pl.BlockSpec((MAIN_WIDTH, tm), lambda i: (0, i)), row(MEM_WIDTH)),
        compiler_params=_cparams(("parallel",)),
        name="in_proj1",
    )(h2d, pos2d, *consts)


def _attn_kernel(q_ref, k_ref, vt_ref, cast_a_ref, cast_b_ref, o_ref, cast_a_out_ref, cast_b_out_ref,
                 m_ref, l_ref, acc_ref, st_ref, mt_ref):
    qi = pl.program_id(2)
    step = (pl.program_id(0) * pl.num_programs(1) + pl.program_id(1)) * pl.num_programs(2) + qi
    _cast_block(step, cast_a_ref, cast_a_out_ref, N_EXPERTS * CAST_PARTS)
    _cast_block(step, cast_b_ref, cast_b_out_ref, N_EXPERTS * CAST_PARTS)
    m_ref[...] = jnp.full_like(m_ref, NEG)
    l_ref[...] = jnp.zeros_like(l_ref)
    acc_ref[...] = jnp.zeros_like(acc_ref)
    n_diag = TQ // TK
    n_full = qi * n_diag
    nt_dims = (((1,), (1,)), ((), ()))

    def scores(j, slot):
        koff = pl.multiple_of(j * TK, TK)
        st = lax.dot_general(k_ref[pl.ds(koff, TK), :], q_ref[...], nt_dims, preferred_element_type=F32)
        st_ref[slot] = st
        mt_ref[slot] = jnp.max(st, axis=0, keepdims=True)

    def absorb(st, m_tile, vt, cols):
        m_old = m_ref[:, cols]
        m_new = jnp.maximum(m_old, m_tile)
        alpha = jnp.exp2(m_old - m_new)
        e = jnp.exp2(st - m_new)
        l_ref[:, cols] = alpha * l_ref[:, cols] + jnp.sum(e, axis=0, keepdims=True)
        acc_ref[:, cols] = alpha * acc_ref[:, cols] + jnp.dot(vt, e.astype(BF16), preferred_element_type=F32)
        m_ref[:, cols] = m_new

    def finish(j, slot):
        koff = pl.multiple_of(j * TK, TK)
        absorb(st_ref[slot], mt_ref[slot], vt_ref[:, pl.ds(koff, TK)], slice(None))

    def pair(j):
        finish(j, 0)
        scores(j + 2, 0)
        finish(j + 1, 1)
        scores(j + 3, 1)

    @pl.when(n_full > 0)
    def _():
        scores(0, 0)
        scores(1, 1)

        def body(p, carry):
            pair(4 * p)
            pair(4 * p + 2)
            return carry

        lax.fori_loop(0, (n_full - 2) // 4, body, 0)
        pair(n_full - 4)
        finish(n_full - 2, 0)
        finish(n_full - 1, 1)

    kd0 = pl.multiple_of(n_full * TK, TK)
    for u in range(n_diag):
        nk = (u + 1) * TK
        cols = slice(u * TK, (u + 1) * TK)
        st = lax.dot_general(k_ref[pl.ds(kd0, nk), :], q_ref[cols, :], nt_dims, preferred_element_type=F32)
        kpos = lax.broadcasted_iota(jnp.int32, st.shape, 0)
        qpos = u * TK + lax.broadcasted_iota(jnp.int32, st.shape, 1)
        st = jnp.where(kpos <= qpos, st, NEG)
        absorb(st, jnp.max(st, axis=0, keepdims=True), vt_ref[:, pl.ds(kd0, nk)], cols)
    o_ref[...] = (acc_ref[...] * (1.0 / l_ref[...])).T.astype(o_ref.dtype)


def _attention(q, k, v, cast_a, cast_b, batch, seq):
    t = q.shape[0]
    nq = seq // TQ
    assert batch * MLA_HEADS * nq >= cast_a.shape[0] * CAST_PARTS
    assert (TQ // TK) % 4 == 0
    step_of = lambda b, h, i: (b * MLA_HEADS + h) * nq + i
    cast_specs = [_cast_spec(cast_a.shape, step_of), _cast_spec(cast_b.shape, step_of)]
    return pl.pallas_call(
        _attn_kernel,
        out_shape=(jax.ShapeDtypeStruct((t, MAIN_WIDTH), BF16), jax.ShapeDtypeStruct(cast_a.shape, BF16),
                   jax.ShapeDtypeStruct(cast_b.shape, BF16)),
        grid=(batch, MLA_HEADS, nq),
        in_specs=[pl.BlockSpec((TQ, HEAD_PAD), lambda b, h, i: (b * nq + i, h)),
                  pl.BlockSpec((seq, HEAD_PAD), lambda b, h, i: (b, h)),
                  pl.BlockSpec((MLA_V, seq), lambda b, h, i: (h, b))] + cast_specs,
        out_specs=[pl.BlockSpec((TQ, MLA_V), lambda b, h, i: (b * nq + i, h))] + cast_specs,
        scratch_shapes=[pltpu.VMEM((1, TQ), F32), pltpu.VMEM((1, TQ), F32), pltpu.VMEM((MLA_V, TQ), F32),
                        pltpu.VMEM((2, TK, TQ), F32), pltpu.VMEM((2, 1, TQ), F32)],
        compiler_params=_cparams(("arbitrary", "arbitrary", "arbitrary")),
        name="mla_attention",
    )(q, k, v, cast_a, cast_b)


def _router_kernel(h_ref, g_ref, whi_ref, wlo_ref,
                   xn_ref, gatest_ref, rank_ref, rankt_ref, starts_ref, counts_ref, run_ref):
    @pl.when(pl.program_id(0) == 0)
    def _():
        run_ref[...] = jnp.zeros_like(run_ref)

    tm = h_ref.shape[0]
    xn = _rms(h_ref[...], g_ref[...])
    hi = xn.astype(BF16)
    lo = (xn - hi.astype(F32)).astype(BF16)
    xn_ref[...] = hi
    whi = whi_ref[...]
    logits = (jnp.dot(hi, whi, preferred_element_type=F32) + jnp.dot(lo, whi, preferred_element_type=F32)
              + jnp.dot(hi, wlo_ref[...], preferred_element_type=F32))
    lane = lax.broadcasted_iota(jnp.int32, logits.shape, 1).astype(F32)
    logits = jnp.where(lane < N_EXPERTS, logits, NEG)
    m1 = jnp.max(logits, axis=-1, keepdims=True)
    i1 = jnp.min(jnp.where(logits == m1, lane, float(LANE)), axis=-1, keepdims=True)
    rest = jnp.where(lane == i1, NEG, logits)
    m2 = jnp.max(rest, axis=-1, keepdims=True)
    i2 = jnp.min(jnp.where(rest == m2, lane, float(LANE)), axis=-1, keepdims=True)
    e2 = jnp.exp(m2 - m1)
    w1 = 1.0 / (1.0 + e2)
    w2 = e2 * w1
    gates = jnp.where(lane == i1, w1, jnp.where(lane == i2, w2, 0.0))
    gatest_ref[...] = gates.T[:N_EXPERTS, :]
    routed = (lane == i1) | (lane == i2)
    oh = jnp.where(routed, 1.0, 0.0)
    r = lax.broadcasted_iota(jnp.int32, (TM_PROJ, TM_PROJ), 0)
    c = lax.broadcasted_iota(jnp.int32, (TM_PROJ, TM_PROJ), 1)
    before = jnp.where(r > c, 1.0, 0.0).astype(BF16)
    run = run_ref[0:1, :]
    ranks = []
    for ch in range(tm // TM_PROJ):
        rows = slice(ch * TM_PROJ, (ch + 1) * TM_PROJ)
        starts_ref[8 * ch:8 * ch + 8, :] = jnp.broadcast_to(run, (8, LANE))
        prefix = jnp.dot(before, oh[rows].astype(BF16), preferred_element_type=F32)
        ranks.append(jnp.where(routed[rows], run + prefix, -1.0))
        run = run + jnp.sum(oh[rows], axis=0, keepdims=True)
    rank = jnp.concatenate(ranks, axis=0)
    rank_ref[...] = rank
    rankt_ref[...] = rank.T[:N_EXPERTS, :]
    total = jnp.broadcast_to(run, run_ref.shape)
    run_ref[...] = total
    counts_ref[...] = total


def _router(h2d, ln, whi, wlo):
    t = h2d.shape[0]
    tm = min(TM_ROUTER, t)
    nt = t // tm
    chunk_rows = 8 * (tm // TM_PROJ)
    row = lambda n: pl.BlockSpec((tm, n), lambda i: (i, 0))
    return pl.pallas_call(
        _router_kernel,
        out_shape=(jax.ShapeDtypeStruct((t, D_MODEL), BF16), jax.ShapeDtypeStruct((N_EXPERTS, t), F32),
                   jax.ShapeDtypeStruct((t, LANE), F32), jax.ShapeDtypeStruct((N_EXPERTS, t), F32),
                   jax.ShapeDtypeStruct((nt * chunk_rows, LANE), F32), jax.ShapeDtypeStruct((8, LANE), F32)),
        grid=(nt,),
        in_specs=[row(D_MODEL), _const_spec((1, D_MODEL)), _const_spec(whi.shape), _const_spec(wlo.shape)],
        out_specs=(row(D_MODEL), pl.BlockSpec((N_EXPERTS, tm), lambda i: (0, i)), row(LANE),
                   pl.BlockSpec((N_EXPERTS, tm), lambda i: (0, i)),
                   pl.BlockSpec((chunk_rows, LANE), lambda i: (i, 0)), _const_spec((8, LANE))),
        scratch_shapes=[pltpu.VMEM((8, LANE), F32)],
        compiler_params=_cparams(("arbitrary",)),
        name="moe_router",
    )(h2d, ln, whi, wlo)


def _dispatch_kernel(ge_ref, gr0_ref, gstart_ref, gn_ref, xn_ref, rankt_ref, gatest_ref, xs_ref, gs_ref,
                     acc_ref, gacc_ref):
    groups = [pl.program_id(0) * DISPATCH_GROUPS + u for u in range(DISPATCH_GROUPS)]
    eye = (lax.broadcasted_iota(jnp.int32, (SLOT_GROUP, SLOT_GROUP), 0)
           == lax.broadcasted_iota(jnp.int32, (SLOT_GROUP, SLOT_GROUP), 1))

    def gathered(g, start):
        start = pl.multiple_of(start, TM_PROJ)
        want = (gr0_ref[g] + lax.broadcasted_iota(jnp.int32, (SLOT_GROUP, DISPATCH_K), 0)).astype(F32)
        span = (pl.ds(ge_ref[g], 1), pl.ds(start, DISPATCH_K))
        hit = rankt_ref[span] == want
        rows = jnp.dot(jnp.where(hit, 1.0, 0.0).astype(BF16), xn_ref[pl.ds(start, DISPATCH_K), :],
                       preferred_element_type=F32)
        gate = jnp.sum(jnp.where(hit, gatest_ref[span], 0.0), axis=-1, keepdims=True)
        return rows, gate

    def as_row(col):
        return jnp.sum(jnp.where(eye, col, 0.0), axis=0, keepdims=True)

    single = gn_ref[groups[0]] <= 1
    for g in groups[1:]:
        single = single & (gn_ref[g] <= 1)

    @pl.when(single)
    def _():
        gate_rows = []
        for u, g in enumerate(groups):
            rows, gate = gathered(g, gstart_ref[g])
            xs_ref[u * SLOT_GROUP:(u + 1) * SLOT_GROUP, :] = rows.astype(BF16)
            gate_rows.append(as_row(gate))
        gs_ref[...] = jnp.broadcast_to(jnp.concatenate(gate_rows, axis=1), gs_ref.shape)

    @pl.when(jnp.logical_not(single))
    def _():
        gate_rows = []
        for u, g in enumerate(groups):
            acc_ref[...] = jnp.zeros_like(acc_ref)
            gacc_ref[...] = jnp.zeros_like(gacc_ref)

            def body(k, carry, g=g):
                rows, gate = gathered(g, gstart_ref[g] + k * DISPATCH_K)
                acc_ref[...] += rows
                gacc_ref[...] += gate
                return carry

            lax.fori_loop(0, gn_ref[g], body, 0)
            xs_ref[u * SLOT_GROUP:(u + 1) * SLOT_GROUP, :] = acc_ref[...].astype(BF16)
            gate_rows.append(as_row(gacc_ref[...]))
        gs_ref[...] = jnp.broadcast_to(jnp.concatenate(gate_rows, axis=1), gs_ref.shape)


def _dispatch(xn, rankt, gatest, g_expert, g_rank0, g_start, g_nspans):
    t = xn.shape[0]
    n_steps = g_expert.shape[0] // DISPATCH_GROUPS
    rows = DISPATCH_GROUPS * SLOT_GROUP
    assert rows == TM_MOE
    res = lambda shape: pl.BlockSpec(shape, lambda g, *_: (0, 0), pipeline_mode=pl.Buffered(1))
    grid_spec = pltpu.PrefetchScalarGridSpec(
        num_scalar_prefetch=4,
        grid=(n_steps,),
        in_specs=[res((t, D_MODEL)), res((N_EXPERTS, t)), res((N_EXPERTS, t))],
        out_specs=(pl.BlockSpec((rows, D_MODEL), lambda g, *_: (g, 0)),
                   pl.BlockSpec((8, rows), lambda g, *_: (g, 0))),
        scratch_shapes=[pltpu.VMEM((SLOT_GROUP, D_MODEL), F32), pltpu.VMEM((SLOT_GROUP, 1), F32)],
    )
    return pl.pallas_call(
        _dispatch_kernel,
        out_shape=(jax.ShapeDtypeStruct((n_steps * rows, D_MODEL), BF16),
                   jax.ShapeDtypeStruct((n_steps * 8, rows), F32)),
        grid_spec=grid_spec,
        compiler_params=_cparams(("arbitrary",)),
        name="moe_dispatch",
    )(g_expert, g_rank0, g_start, g_nspans, xn, rankt, gatest)


def _expert_kernel(te_ref, ta_ref, x_ref, gs_ref, wg_ref, wu_ref, wd_ref, out_ref):
    active = ta_ref[pl.program_id(0)] > 0

    @pl.when(active)
    def _():
        x = x_ref[...]
        acc = None
        for c in range(D_FF // FF_CHUNK):
            cs = slice(c * FF_CHUNK, (c + 1) * FF_CHUNK)
            gt = jnp.dot(x, wg_ref[0, :, cs], preferred_element_type=F32)
            up = jnp.dot(x, wu_ref[0, :, cs], preferred_element_type=F32)
            a = (gt * jax.nn.sigmoid(gt) * up).astype(BF16)
            y = jnp.dot(a, wd_ref[0, cs, :], preferred_element_type=F32)
            acc = y if acc is None else acc + y
        tm = x.shape[0]
        eye = lax.broadcasted_iota(jnp.int32, (tm, tm), 0) == lax.broadcasted_iota(jnp.int32, (tm, tm), 1)
        gate = jnp.sum(jnp.where(eye, gs_ref[0:1, :], 0.0), axis=-1, keepdims=True)
        out_ref[...] = (acc * gate).astype(out_ref.dtype)

    @pl.when(jnp.logical_not(active))
    def _():
        out_ref[...] = jnp.zeros_like(out_ref)


def _experts(x, gate_rows, tile_expert, tile_active, wg, wu, wd):
    n_tiles = tile_expert.shape[0]
    tm = TM_MOE
    wmap = lambda j, te, ta: (te[j], 0, 0)
    tile = lambda j, te, ta: (jnp.where(ta[j] > 0, j, 0), 0)
    grid_spec = pltpu.PrefetchScalarGridSpec(
        num_scalar_prefetch=2,
        grid=(n_tiles,),
        in_specs=[pl.BlockSpec((tm, D_MODEL), tile),
                  pl.BlockSpec((8, tm), tile),
                  pl.BlockSpec((1, D_MODEL, D_FF), wmap),
                  pl.BlockSpec((1, D_MODEL, D_FF), wmap),
                  pl.BlockSpec((1, D_FF, D_MODEL), wmap)],
        out_specs=pl.BlockSpec((tm, D_MODEL), lambda j, te, ta: (j, 0)),
    )
    return pl.pallas_call(
        _expert_kernel,
        out_shape=jax.ShapeDtypeStruct((n_tiles * tm, D_MODEL), BF16),
        grid_spec=grid_spec,
        compiler_params=_cparams(("arbitrary",), EXPERT_VMEM_LIMIT),
        name="moe_experts",
    )(tile_expert, tile_active, x, gate_rows, wg, wu, wd)


def _combine_window_copies(ys_ref, buf_ref, sem_ref, ws_ref, tile, buf_slot, row_offset):
    copies = []
    for e in range(N_EXPERTS):
        start = pl.multiple_of(ws_ref[tile * N_EXPERTS + e] + row_offset, SLOT_ALIGN)
        copies.append(pltpu.make_async_copy(ys_ref.at[pl.ds(start, COMBINE_WIN), :],
                                            buf_ref.at[buf_slot, pl.ds(e * COMBINE_WIN, COMBINE_WIN), :],
                                            sem_ref.at[buf_slot, e]))
    return copies


def _combine_kernel(ws_ref, second_ref, h_ref, rank_ref, base_ref, ys_ref, out_ref, buf_ref, sem_ref):
    i = pl.program_id(0)
    n = pl.num_programs(0)
    tm = h_ref.shape[0]

    @pl.when(i == 0)
    def _():
        for cp in _combine_window_copies(ys_ref, buf_ref, sem_ref, ws_ref, 0, 0, 0):
            cp.start()

    @pl.when(i + 1 < n)
    def _():
        for cp in _combine_window_copies(ys_ref, buf_ref, sem_ref, ws_ref, i + 1, (i + 1) % 2, 0):
            cp.start()

    rank = rank_ref[...]
    slots = jnp.where(rank >= 0.0, rank + base_ref[...], -1.0)
    lane_iota = lax.broadcasted_iota(jnp.int32, (tm, COMBINE_WIN), 1)

    def gathered(buf_slot, row_offset):
        parts = []
        for e in range(N_EXPERTS):
            first = ws_ref[i * N_EXPERTS + e] + row_offset
            hit = slots[:, e:e + 1] == (first + lane_iota).astype(F32)
            parts.append(jnp.where(hit, 1.0, 0.0).astype(BF16))
        return jnp.dot(jnp.concatenate(parts, axis=1), buf_ref[buf_slot], preferred_element_type=F32)

    for cp in _combine_window_copies(ys_ref, buf_ref, sem_ref, ws_ref, i, i % 2, 0):
        cp.wait()
    out_ref[...] = h_ref[...] + gathered(i % 2, 0)

    for extra in range(1, COMBINE_MAX_WINDOWS):
        @pl.when(second_ref[i] > extra)
        def _(extra=extra):
            copies = _combine_window_copies(ys_ref, buf_ref, sem_ref, ws_ref, i, 2, extra * COMBINE_WIN)
            for cp in copies:
                cp.start()
            for cp in copies:
                cp.wait()
            out_ref[...] += gathered(2, extra * COMBINE_WIN)


def _combine(h2d, rank, base_row, ys, win_start, tile_second):
    t = h2d.shape[0]
    tm = TM_PROJ
    row = lambda n: pl.BlockSpec((tm, n), lambda i, *_: (i, 0))
    grid_spec = pltpu.PrefetchScalarGridSpec(
        num_scalar_prefetch=2,
        grid=(t // tm,),
        in_specs=[row(D_MODEL), row(LANE), pl.BlockSpec((1, LANE), lambda i, *_: (0, 0)),
                  pl.BlockSpec(memory_space=pl.ANY)],
        out_specs=row(D_MODEL),
        scratch_shapes=[pltpu.VMEM((3, N_EXPERTS * COMBINE_WIN, D_MODEL), BF16),
                        pltpu.SemaphoreType.DMA((3, N_EXPERTS))],
    )
    return pl.pallas_call(
        _combine_kernel,
        out_shape=jax.ShapeDtypeStruct((t, D_MODEL), F32),
        grid_spec=grid_spec,
        compiler_params=_cparams(("arbitrary",)),
        name="moe_combine",
    )(win_start, tile_second, h2d, rank, base_row, ys)


def _moe_plan(counts, starts, t):
    n_tok_tiles = t // TM_PROJ
    n_tiles = 2 * t // TM_MOE + N_EXPERTS + 1
    n_slots = n_tiles * TM_MOE
    counts = counts.astype(jnp.int32)
    cum = jnp.concatenate([starts.astype(jnp.int32), counts[None, :]], axis=0)
    padded = (counts + TM_MOE - 1) // TM_MOE * TM_MOE
    ends = jnp.cumsum(padded)
    base = ends - padded
    total = ends[-1]
    tile_first = jnp.arange(n_tiles, dtype=jnp.int32) * TM_MOE
    tile_expert = jnp.minimum(jnp.sum(ends[None, :] <= tile_first[:, None], axis=1), N_EXPERTS - 1).astype(jnp.int32)
    tile_active = (tile_first < total).astype(jnp.int32)
    n_groups = n_slots // SLOT_GROUP
    g_first = jnp.arange(n_groups, dtype=jnp.int32) * SLOT_GROUP
    g_expert = tile_expert[g_first // TM_MOE]
    g_r0 = g_first - base[g_expert]
    cum_e = cum.T[g_expert]
    lo = jnp.sum(cum_e <= g_r0[:, None], axis=1) - 1
    hi = jnp.sum(cum_e < (g_r0 + SLOT_GROUP)[:, None], axis=1)
    g_valid = (g_first < total) & (g_r0 < counts[g_expert])
    lo = jnp.clip(lo, 0, n_tok_tiles - 1)
    hi = jnp.clip(hi, 0, n_tok_tiles)
    g_n = jnp.where(g_valid, (jnp.maximum(hi - lo, 0) * TM_PROJ + DISPATCH_K - 1) // DISPATCH_K, 0)
    lo = jnp.minimum(lo * TM_PROJ, t - jnp.maximum(g_n, 1) * DISPATCH_K)
    first = base[None, :] + cum[:-1]
    cnt = cum[1:] - cum[:-1]
    ws = first // SLOT_ALIGN * SLOT_ALIGN
    pieces = jnp.where(cnt > 0, (first + cnt - ws + COMBINE_WIN - 1) // COMBINE_WIN, 0)
    return dict(tile_expert=tile_expert, tile_active=tile_active, g_expert=g_expert, g_r0=g_r0.astype(jnp.int32),
                g_lo=lo.astype(jnp.int32), g_n=g_n.astype(jnp.int32), base=base,
                ws=ws.reshape(-1).astype(jnp.int32), second=jnp.max(pieces, axis=1).astype(jnp.int32))


def _pad_heads_cols(w, heads, dim, pad):
    lead = w.shape[:-1]
    w = w.reshape(lead + (heads, dim))
    w = jnp.pad(w, [(0, 0)] * len(lead) + [(0, 0), (0, pad - dim)])
    return w.reshape(lead + (heads * pad,))


def _row(v):
    return v.reshape(1, -1).astype(F32)


def kernel(x, mem, positions, mem_norm, w_mem_kv, ln_mix0, w_in0, conv_w0, b_i0, b_f0, h_norm0, mq_norm0, mk_norm0, w_out0, ln_ffn0, w_gate0, w_up0, w_down0, ln_mix1, w_in1, cq_norm1, ckv_norm1, w_uq1, w_ukv1, q_norm1, k_norm1, mq_norm1, mk_norm1, w_out1, ln_ffn1, w_router1, we_gate1, we_up1, we_down1):
    batch, seq, _ = x.shape
    t = batch * seq
    h = x.reshape(t, D_MODEL)
    tile4 = lambda g: _row(jnp.tile(g, MEM_HEADS))

    k4_0, k4_1, v4 = _memkv(mem.reshape(batch * MEM_TOKENS, D_MODEL), _row(mem_norm), w_mem_kv.astype(BF16),
                            tile4(mk_norm0), tile4(mk_norm1), batch)

    mw = MAIN_WIDTH
    w_q, w_k, w_v, w_o = (w_in0[:, i * mw:(i + 1) * mw] for i in range(4))
    w_gates = w_in0[:, 4 * mw:4 * mw + 2 * ML_HEADS]
    w_mq = w_in0[:, 4 * mw + 2 * ML_HEADS:]
    padh = lambda w: _pad_heads_cols(w, ML_HEADS, ML_HEAD_DIM, HEAD_PAD)
    wqk = jnp.concatenate([padh(w_q), padh(w_k)], axis=1).astype(BF16)
    wvo = jnp.concatenate([padh(w_v), padh(w_o)], axis=1).astype(BF16)
    wg = jnp.pad(w_gates, ((0, 0), (0, GATE_PAD - 2 * ML_HEADS))).astype(BF16)
    wgt = w_gates.T.astype(BF16)
    cw = jnp.concatenate([padh(conv_w0[:, :mw]), padh(conv_w0[:, mw:])], axis=1).astype(F32)
    bias = jnp.concatenate([b_i0, b_f0]).astype(F32)
    bias_row = jnp.pad(bias, (0, GATE_PAD - 2 * ML_HEADS)).reshape(1, GATE_PAD)
    bias_col = bias.reshape(2 * ML_HEADS, 1)
    q0, kt0, v0, so, mq0, gates, gatest = _in0(h, _row(ln_mix0), wqk, wvo, w_mq.astype(BF16), wg, wgt, cw,
                                               bias_row, bias_col, seq)
    hgain = jnp.pad(h_norm0.astype(F32), (0, HEAD_PAD - ML_HEAD_DIM)).reshape(1, HEAD_PAD)
    ffn_groups = lambda w: w.reshape(N_EXPERTS, w.shape[0] // N_EXPERTS, w.shape[1])
    hn, (wg0, wu0, wd0) = _mlstm(q0, kt0, v0, so, gates, gatest, hgain,
                                 [ffn_groups(w_gate0), ffn_groups(w_up0), ffn_groups(w_down0)], batch, seq)
    wm0 = jnp.pad(w_out0[:mw].reshape(ML_HEADS, ML_HEAD_DIM, D_MODEL),
                  ((0, 0), (0, HEAD_PAD - ML_HEAD_DIM), (0, 0))).reshape(ML_PAD, D_MODEL).astype(BF16)
    h, we_gate_bf = _out_proj(hn, mq0, h, k4_0, v4, tile4(mq_norm0), wm0, w_out0[mw:].astype(BF16), seq,
                              ffn=(_row(ln_ffn0), wg0.reshape(w_gate0.shape), wu0.reshape(w_up0.shape),
                                   wd0.reshape(w_down0.shape), we_gate1))

    w_cq = w_in1[:, :Q_LORA]
    w_ckv = w_in1[:, Q_LORA:Q_LORA + KV_LORA]
    w_kr = jnp.pad(w_in1[:, Q_LORA + KV_LORA:Q_LORA + KV_LORA + MLA_ROPE], ((0, 0), (0, LANE - MLA_ROPE)))
    w_mq1 = w_in1[:, Q_LORA + KV_LORA + MLA_ROPE:]
    w1 = jnp.concatenate([w_cq, w_ckv, w_kr, w_mq1], axis=1).astype(BF16)
    wuq = _pad_heads_cols(w_uq1, MLA_HEADS, MLA_QK, HEAD_PAD).astype(BF16)
    wukv = w_ukv1.reshape(KV_LORA, MLA_HEADS, MLA_NOPE + MLA_V)
    wuk = wukv[:, :, :MLA_NOPE].reshape(KV_LORA, MLA_HEADS * MLA_NOPE).astype(BF16)
    wuvt = wukv[:, :, MLA_NOPE:].reshape(KV_LORA, MLA_HEADS * MLA_V).T.astype(BF16)
    gq = jnp.pad(q_norm1.astype(F32), (0, HEAD_PAD - MLA_QK)).reshape(1, HEAD_PAD)
    gkn = _row(k_norm1[:MLA_NOPE])
    gkr = jnp.pad(k_norm1[MLA_NOPE:].astype(F32), (0, LANE - MLA_ROPE)).reshape(1, LANE)
    half = MLA_ROPE // 2
    inv_freq = ROPE_THETA ** (-jnp.arange(half, dtype=F32) / half)
    invf = jnp.concatenate([inv_freq, inv_freq, jnp.zeros((LANE - MLA_ROPE,), F32)]).reshape(1, LANE)
    q1, k1, v1, mq1 = _in1(h, positions.reshape(1, t).astype(jnp.int32), _row(ln_mix1), w1, _row(cq_norm1),
                           _row(ckv_norm1), wuq, wuk, wuvt, gq, gkn, gkr, invf)
    o1, we_up_bf, we_down_bf = _attention(q1, k1, v1, we_up1, we_down1, batch, seq)
    h = _out_proj(o1, mq1, h, k4_1, v4, tile4(mq_norm1), w_out1[:mw].astype(BF16), w_out1[mw:].astype(BF16), seq)

    wr = jnp.pad(w_router1.astype(F32), ((0, 0), (0, LANE - N_EXPERTS)))
    wr_hi = wr.astype(BF16)
    wr_lo = (wr - wr_hi.astype(F32)).astype(BF16)
    xn, gatest, rank, rankt, starts, counts = _router(h, _row(ln_ffn1), wr_hi, wr_lo)
    plan = _moe_plan(counts[0, :N_EXPERTS], starts[::8, :N_EXPERTS], t)
    xs, gate_rows = _dispatch(xn, rankt, gatest, plan["g_expert"], plan["g_r0"], plan["g_lo"], plan["g_n"])
    ys = _experts(xs, gate_rows, plan["tile_expert"], plan["tile_active"], we_gate_bf, we_up_bf, we_down_bf)
    base_row = jnp.pad(plan["base"].astype(F32), (0, LANE - N_EXPERTS)).reshape(1, LANE)
    h = _combine(h, rank, base_row, ys, plan["ws"], plan["second"])
    return h.reshape(batch, seq, D_MODEL)
```

```python
import functools

import jax
import jax.numpy as jnp
import numpy as np
from jax import lax
from jax.experimental import pallas as pl
from jax.experimental.pallas import tpu as pltpu

F32 = jnp.float32
BF16 = jnp.bfloat16

D_MODEL = 1024
MEM_TOKENS = 256
MEM_HEADS = 4
MEM_WIDTH = 256
MEM_HEAD_DIM = 64
MAIN_WIDTH = 768
ML_HEADS = 4
ML_HEAD_DIM = 192
CONV_WIDTH = 4
MLA_HEADS = 6
MLA_NOPE = 128
MLA_ROPE = 64
MLA_QK = 192
MLA_V = 128
Q_LORA = 384
KV_LORA = 128
ROPE_THETA = 10000.0
D_FF = 3584
N_EXPERTS = 8
EPS = 1e-6

LANE = 128
HEAD_PAD = 256
ML_PAD = ML_HEADS * HEAD_PAD
MLA_PAD = MLA_HEADS * HEAD_PAD
GATE_PAD = 128
ML_CHUNK = 256
TM_PROJ = 256
TM_ROUTER = 1024
TM_IN0 = 256
TM_IN1 = 256
TM_OUT = 1024
TM_FFN = 512
FF_CHUNK = 512
TM_MOE = 512
SLOT_GROUP = 128
DISPATCH_K = 1024
DISPATCH_GROUPS = 4
CAST_PARTS = 4
SLOT_ALIGN = 16
COMBINE_WIN = 128
COMBINE_MAX_WINDOWS = -(-(TM_PROJ + SLOT_ALIGN - 1) // COMBINE_WIN)
TQ = 2048
TK = 512
NEG = -1e30
LOG2E = 1.4426950408889634
VMEM_LIMIT = 56 * 1024 * 1024
EXPERT_VMEM_LIMIT = 62 * 1024 * 1024


def _cparams(sem, vmem=VMEM_LIMIT):
    return pltpu.CompilerParams(dimension_semantics=sem, vmem_limit_bytes=vmem)


def _rms(x, g):
    return x * lax.rsqrt(jnp.mean(x * x, axis=-1, keepdims=True) + EPS) * g


def _split_dot(x, w_bf16):
    hi = x.astype(BF16)
    lo = (x - hi.astype(F32)).astype(BF16)
    return (jnp.dot(hi, w_bf16, preferred_element_type=F32)
            + jnp.dot(lo, w_bf16, preferred_element_type=F32))


def _block_diag_ones(n, blk):
    r = lax.broadcasted_iota(jnp.int32, (n, n), 0) // blk
    c = lax.broadcasted_iota(jnp.int32, (n, n), 1) // blk
    return jnp.where(r == c, 1.0, 0.0).astype(BF16)


def _const_spec(shape):
    nd = len(shape)
    return pl.BlockSpec(shape, lambda *_: (0,) * nd)


def _memkv_kernel(mem_ref, g_ref, w_ref, gk0_ref, gk1_ref, k0_ref, k1_ref, v_ref):
    xn = _rms(mem_ref[...], g_ref[...]).astype(BF16)
    kv = jnp.dot(xn, w_ref[...], preferred_element_type=F32)
    k = kv[:, :MEM_WIDTH]
    v = kv[:, MEM_WIDTH:]
    bd = _block_diag_ones(MEM_WIDTH, MEM_HEAD_DIM)
    ms = _split_dot(k * k, bd) * (1.0 / MEM_HEAD_DIM)
    kn = k * lax.rsqrt(ms + EPS)
    lane_head = lax.broadcasted_iota(jnp.int32, (MEM_TOKENS, MEM_WIDTH), 1) // MEM_HEAD_DIM
    for h in range(MEM_HEADS):
        sel = lane_head == h
        k0_ref[h] = jnp.where(sel, kn * gk0_ref[...], 0.0).astype(BF16)
        k1_ref[h] = jnp.where(sel, kn * gk1_ref[...], 0.0).astype(BF16)
        v_ref[h] = jnp.where(sel, v, 0.0).astype(BF16)


def _memkv(mem2d, mem_norm, w_kv, gk0, gk1, batch):
    out = jax.ShapeDtypeStruct((batch * MEM_HEADS, MEM_TOKENS, MEM_WIDTH), BF16)
    hspec = pl.BlockSpec((MEM_HEADS, MEM_TOKENS, MEM_WIDTH), lambda b: (b, 0, 0))
    return pl.pallas_call(
        _memkv_kernel,
        out_shape=(out, out, out),
        grid=(batch,),
        in_specs=[pl.BlockSpec((MEM_TOKENS, D_MODEL), lambda b: (b, 0)),
                  _const_spec((1, D_MODEL)), _const_spec((D_MODEL, 2 * MEM_WIDTH)),
                  _const_spec((1, MEM_WIDTH)), _const_spec((1, MEM_WIDTH))],
        out_specs=(hspec, hspec, hspec),
        compiler_params=_cparams(("parallel",)),
        name="memkv",
    )(mem2d, mem_norm, w_kv, gk0, gk1)


def _in0_kernel(tiles_per_seq, h_ref, hp_ref, g_ref, wqk_ref, wvo_ref, wmq_ref, wg_ref, wgt_ref,
                cw_ref, bias_ref, biast_ref,
                q_ref, kt_ref, v_ref, so_ref, mq_ref, gates_ref, gatest_ref):
    i = pl.program_id(0)
    tm = h_ref.shape[0]
    g = g_ref[...]
    xn = _rms(h_ref[...], g).astype(BF16)
    first = (i % tiles_per_seq) == 0
    xpn = _rms(hp_ref[...], g).astype(BF16)
    row8 = lax.broadcasted_iota(jnp.int32, (8, FF_CHUNK), 0)
    ncol = wqk_ref.shape[1]
    for c in range(ncol // FF_CHUNK):
        cs = slice(c * FF_CHUNK, (c + 1) * FF_CHUNK)
        w = wqk_ref[:, cs]
        u = jnp.dot(xn, w, preferred_element_type=F32)
        up = jnp.dot(xpn, w, preferred_element_type=F32)
        up = jnp.where(first, 0.0, up)
        acc = u * cw_ref[CONV_WIDTH - 1:CONV_WIDTH, cs]
        for k in range(1, CONV_WIDTH):
            rolled = pltpu.roll(u, k, 0)
            prev = pltpu.roll(up, k, 0)
            head = jnp.where(row8 < k, prev, rolled[:8])
            shifted = jnp.concatenate([head, rolled[8:]], axis=0)
            acc = acc + shifted * cw_ref[CONV_WIDTH - 1 - k:CONV_WIDTH - k, cs]
        y = acc * jax.nn.sigmoid(acc)
        if c < ML_PAD // FF_CHUNK:
            q_ref[:, cs] = (y * (ML_HEAD_DIM ** -0.5)).astype(BF16)
        else:
            ks = slice(c * FF_CHUNK - ML_PAD, (c + 1) * FF_CHUNK - ML_PAD)
            kt_ref[ks, :] = y.T.astype(BF16)
    zvo = jnp.dot(xn, wvo_ref[...], preferred_element_type=F32)
    v_ref[...] = zvo[:, :ML_PAD].astype(BF16)
    so_ref[...] = jax.nn.sigmoid(zvo[:, ML_PAD:]).astype(BF16)
    mq_ref[...] = jnp.dot(xn, wmq_ref[...], preferred_element_type=F32).astype(BF16)

    def gate_act(z, is_input_gate):
        logsig = jnp.minimum(z, 0.0) - jnp.log(1.0 + jnp.exp(-jnp.abs(z)))
        return jnp.where(is_input_gate, z, logsig)

    zg = jnp.dot(xn, wg_ref[...], preferred_element_type=F32) + bias_ref[...]
    lane = lax.broadcasted_iota(jnp.int32, (tm, GATE_PAD), 1)
    gates_ref[...] = gate_act(zg, lane < ML_HEADS)
    zgt = lax.dot_general(wgt_ref[...], xn, (((1,), (1,)), ((), ())),
                          preferred_element_type=F32) + biast_ref[...]
    row = lax.broadcasted_iota(jnp.int32, (8, tm), 0)
    gatest_ref[...] = gate_act(zgt, row < ML_HEADS)


def _in0(h2d, ln, wqk, wvo, wmq, wg, wgt, cw, bias, biast, seq):
    t = h2d.shape[0]
    tm = TM_IN0
    tiles_per_seq = seq // tm
    tok = lambda n, dt: jax.ShapeDtypeStruct((t, n), dt)
    row = lambda n: pl.BlockSpec((tm, n), lambda i: (i, 0))
    col = lambda n: pl.BlockSpec((n, tm), lambda i: (0, i))
    return pl.pallas_call(
        functools.partial(_in0_kernel, tiles_per_seq),
        out_shape=(tok(ML_PAD, BF16), jax.ShapeDtypeStruct((ML_PAD, t), BF16), tok(ML_PAD, BF16),
                   tok(ML_PAD, BF16), tok(MEM_WIDTH, BF16), tok(GATE_PAD, F32), jax.ShapeDtypeStruct((8, t), F32)),
        grid=(t // tm,),
        in_specs=[row(D_MODEL),
                  pl.BlockSpec((8, D_MODEL), lambda i: (jnp.maximum(i * (tm // 8) - 1, 0), 0)),
                  _const_spec((1, D_MODEL)), _const_spec(wqk.shape), _const_spec(wvo.shape),
                  _const_spec(wmq.shape), _const_spec(wg.shape), _const_spec(wgt.shape),
                  _const_spec(cw.shape), _const_spec(bias.shape), _const_spec(biast.shape)],
        out_specs=(row(ML_PAD), col(ML_PAD), row(ML_PAD), row(ML_PAD), row(MEM_WIDTH), row(GATE_PAD), col(8)),
        compiler_params=_cparams(("parallel",)),
        name="in_proj0",
    )(h2d, h2d, ln, wqk, wvo, wmq, wg, wgt, cw, bias, biast)


def _mlstm_kernel(batch, cast_blocks, q_ref, v_ref, so_ref, gates_ref, *rest):
    n_cast = len(cast_blocks)
    kt_refs, gatest_refs = rest[:batch], rest[batch:2 * batch]
    hg_ref = rest[2 * batch]
    cast_src = rest[2 * batch + 1:2 * batch + 1 + n_cast]
    out_ref = rest[2 * batch + 1 + n_cast]
    cast_dst = rest[2 * batch + 2 + n_cast:2 * batch + 2 + 2 * n_cast]
    c_ref, m_ref = rest[2 * batch + 2 + 2 * n_cast:]
    L = q_ref.shape[1]
    for src, dst, n_blocks in zip(cast_src, cast_dst, cast_blocks):
        _cast_block(pl.program_id(0), src, dst, n_blocks)

    @pl.when(pl.program_id(0) == 0)
    def _():
        c_ref[...] = jnp.zeros_like(c_ref)
        m_ref[...] = jnp.zeros_like(m_ref)

    r = lax.broadcasted_iota(jnp.int32, (L, L), 0)
    c = lax.broadcasted_iota(jnp.int32, (L, L), 1)
    causal = r >= c
    tri_lo = jnp.where(causal, 1.0, 0.0).astype(BF16)
    tri_up = jnp.where(r <= c, 1.0, 0.0).astype(BF16)
    lane = lax.broadcasted_iota(jnp.int32, (L, HEAD_PAD), 1)
    hg = hg_ref[...]
    for b in range(batch):
        gts = gates_ref[b]
        gtt = gatest_refs[b][...]
        hi = gts.astype(BF16)
        lo = (gts - hi.astype(F32)).astype(BF16)
        b_cols = (jnp.dot(tri_lo, hi, preferred_element_type=F32)
                  + jnp.dot(tri_lo, lo, preferred_element_type=F32))
        b_rows = _split_dot(gtt, tri_up)
        for h in range(ML_HEADS):
            hs = slice(h * HEAD_PAD, (h + 1) * HEAD_PAD)
            state = b * ML_HEADS + h
            m_h = m_ref[state][0:1, 0:1]
            b_c = b_cols[:, ML_HEADS + h:ML_HEADS + h + 1]
            c_r = b_rows[ML_HEADS + h:ML_HEADS + h + 1, :] - gtt[h:h + 1, :]
            log_d = jnp.where(causal, b_c - c_r, NEG)
            inter = b_c + m_h
            m_row = jnp.maximum(inter, jnp.max(log_d, axis=-1, keepdims=True))
            d = jnp.exp(log_d - m_row)
            s_inter = jnp.exp(inter - m_row)
            q_h = q_ref[b, :, hs]
            kt_h = kt_refs[b][hs, :]
            v_aug = jnp.where(lane == ML_HEAD_DIM, 1.0, v_ref[b, :, hs].astype(F32)).astype(BF16)
            s = jnp.dot(q_h, kt_h, preferred_element_type=F32)
            qkd = (s * d).astype(BF16)
            c_old = c_ref[state]
            num = (jnp.dot(qkd, v_aug, preferred_element_type=F32)
                   + s_inter * jnp.dot(q_h, c_old.astype(BF16), preferred_element_type=F32))
            den = jnp.sum(jnp.where(lane == ML_HEAD_DIM, num, 0.0), axis=-1, keepdims=True)
            ht = num * (1.0 / jnp.maximum(jnp.abs(den), jnp.exp(-m_row)))
            ht = jnp.where(lane < ML_HEAD_DIM, ht, 0.0)
            ms = jnp.sum(ht * ht, axis=-1, keepdims=True) * (1.0 / ML_HEAD_DIM)
            hn = ht * lax.rsqrt(ms + EPS) * hg * so_ref[b, :, hs].astype(F32)
            out_ref[b, :, hs] = hn.astype(BF16)
            b_last = b_c[L - 1:L, :]
            g_r = b_last - c_r
            m_new = jnp.maximum(b_last + m_h, jnp.max(g_r, axis=-1, keepdims=True))
            w_r = jnp.exp(g_r - m_new)
            decay = jnp.exp(b_last + m_h - m_new)
            ktw = (kt_h.astype(F32) * w_r).astype(BF16)
            c_ref[state] = decay * c_old + jnp.dot(ktw, v_aug, preferred_element_type=F32)
            m_ref[state] = jnp.broadcast_to(m_new, m_ref.shape[1:])


def _mlstm(q, kt, v, so, gates, gatest, hgain, casts, batch, seq):
    L = ML_CHUNK
    nc = seq // L
    seq3 = lambda a: a.reshape(batch, seq, a.shape[-1])
    row = lambda n: pl.BlockSpec((batch, L, n), lambda c: (0, c, 0))
    col = lambda rows: [pl.BlockSpec((rows, L), functools.partial(lambda b, c: (0, b * nc + c), b))
                        for b in range(batch)]
    cast_blocks = tuple(a.shape[0] * CAST_PARTS for a in casts)
    assert all(n <= nc for n in cast_blocks)
    cast_specs = [_cast_spec(a.shape, lambda c: c) for a in casts]
    outs = pl.pallas_call(
        functools.partial(_mlstm_kernel, batch, cast_blocks),
        out_shape=[jax.ShapeDtypeStruct((batch, seq, ML_PAD), BF16)]
                  + [jax.ShapeDtypeStruct(a.shape, BF16) for a in casts],
        grid=(nc,),
        in_specs=[row(ML_PAD), row(ML_PAD), row(ML_PAD), row(GATE_PAD)] + col(ML_PAD) + col(8)
                 + [_const_spec((1, HEAD_PAD))] + cast_specs,
        out_specs=[row(ML_PAD)] + cast_specs,
        scratch_shapes=[pltpu.VMEM((batch * ML_HEADS, HEAD_PAD, HEAD_PAD), F32),
                        pltpu.VMEM((batch * ML_HEADS, 8, LANE), F32)],
        compiler_params=_cparams(("arbitrary",)),
        name="mlstm",
    )(seq3(q), seq3(v), seq3(so), seq3(gates), *([kt] * batch), *([gatest] * batch), hgain, *casts)
    return outs[0].reshape(batch * seq, ML_PAD), outs[1:]


def _cast_spec(shape, step_of):
    e, r, c = shape
    n_blocks = e * CAST_PARTS

    def index_map(*grid_idx):
        s = jnp.minimum(step_of(*grid_idx), n_blocks - 1)
        return (s // CAST_PARTS, s % CAST_PARTS, 0)

    return pl.BlockSpec((1, r // CAST_PARTS, c), index_map)


def _cast_block(step, src_ref, dst_ref, n_blocks):
    @pl.when(step < n_blocks)
    def _():
        dst_ref[...] = src_ref[...].astype(dst_ref.dtype)


def _mixer_residual(main_ref, mq_ref, h_ref, k4_ref, v4_ref, gq_ref, wm_ref, wmem_ref):
    q = mq_ref[...].astype(F32)
    bd = _block_diag_ones(MEM_WIDTH, MEM_HEAD_DIM)
    ms = _split_dot(q * q, bd) * (1.0 / MEM_HEAD_DIM)
    qn = (q * lax.rsqrt(ms + EPS) * gq_ref[...] * (MEM_HEAD_DIM ** -0.5)).astype(BF16)
    ymem = jnp.zeros(q.shape, F32)
    for h in range(MEM_HEADS):
        s = lax.dot_general(qn, k4_ref[h], (((1,), (1,)), ((), ())), preferred_element_type=F32)
        e = jnp.exp(s - jnp.max(s, axis=-1, keepdims=True))
        inv = 1.0 / jnp.sum(e, axis=-1, keepdims=True)
        ymem = ymem + jnp.dot(e.astype(BF16), v4_ref[h], preferred_element_type=F32) * inv
    y = (jnp.dot(main_ref[...], wm_ref[...], preferred_element_type=F32)
         + jnp.dot(ymem.astype(BF16), wmem_ref[...], preferred_element_type=F32))
    return h_ref[...] + y


def _out_kernel(main_ref, mq_ref, h_ref, k4_ref, v4_ref, gq_ref, wm_ref, wmem_ref, out_ref):
    out_ref[...] = _mixer_residual(main_ref, mq_ref, h_ref, k4_ref, v4_ref, gq_ref, wm_ref, wmem_ref)


def _out_ffn_kernel(main_ref, mq_ref, h_ref, k4_ref, v4_ref, gq_ref, wm_ref, wmem_ref,
                    g_ref, wg_ref, wu_ref, wd_ref, cast_src_ref, out_ref, cast_dst_ref):
    _cast_block(pl.program_id(0), cast_src_ref, cast_dst_ref, N_EXPERTS * CAST_PARTS)
    x = _mixer_residual(main_ref, mq_ref, h_ref, k4_ref, v4_ref, gq_ref, wm_ref, wmem_ref)
    xn = _rms(x, g_ref[...]).astype(BF16)
    acc = x
    for c in range(D_FF // FF_CHUNK):
        cs = slice(c * FF_CHUNK, (c + 1) * FF_CHUNK)
        gt = jnp.dot(xn, wg_ref[:, cs], preferred_element_type=F32)
        up = jnp.dot(xn, wu_ref[:, cs], preferred_element_type=F32)
        a = (gt * jax.nn.sigmoid(gt) * up).astype(BF16)
        acc = acc + jnp.dot(a, wd_ref[cs, :], preferred_element_type=F32)
    out_ref[...] = acc


def _out_proj(main, mq, h2d, k4, v4, gq, wm, wmem, seq, ffn=None):
    t = h2d.shape[0]
    tm = TM_OUT if ffn is None else TM_FFN
    tiles_per_seq = seq // tm
    row = lambda n: pl.BlockSpec((tm, n), lambda i: (i, 0))
    memspec = pl.BlockSpec((MEM_HEADS, MEM_TOKENS, MEM_WIDTH), lambda i: (i // tiles_per_seq, 0, 0))
    in_specs = [row(main.shape[1]), row(MEM_WIDTH), row(D_MODEL), memspec, memspec,
                _const_spec((1, MEM_WIDTH)), _const_spec(wm.shape), _const_spec(wmem.shape)]
    if ffn is None:
        return pl.pallas_call(
            _out_kernel,
            out_shape=jax.ShapeDtypeStruct((t, D_MODEL), F32),
            grid=(t // tm,),
            in_specs=in_specs,
            out_specs=row(D_MODEL),
            compiler_params=_cparams(("parallel",)),
            name="out_proj",
        )(main, mq, h2d, k4, v4, gq, wm, wmem)
    ln, wg, wu, wd, cast_src = ffn
    assert t // tm >= cast_src.shape[0] * CAST_PARTS
    res = lambda shape: pl.BlockSpec(shape, lambda i: (0, 0), pipeline_mode=pl.Buffered(1))
    cast = _cast_spec(cast_src.shape, lambda i: i)
    return pl.pallas_call(
        _out_ffn_kernel,
        out_shape=(jax.ShapeDtypeStruct((t, D_MODEL), F32), jax.ShapeDtypeStruct(cast_src.shape, BF16)),
        grid=(t // tm,),
        in_specs=in_specs + [_const_spec((1, D_MODEL)), res(wg.shape), res(wu.shape), res(wd.shape), cast],
        out_specs=(row(D_MODEL), cast),
        compiler_params=_cparams(("arbitrary",)),
        name="out_proj_ffn",
    )(main, mq, h2d, k4, v4, gq, wm, wmem, ln, wg, wu, wd, cast_src)


def _in1_kernel(h_ref, pos_ref, g_ref, w1_ref, gcq_ref, gckv_ref, wuq_ref, wuk_ref, wuvt_ref,
                gq_ref, gkn_ref, gkr_ref, invf_ref,
                q_ref, k_ref, vt_ref, mq_ref):
    tm = h_ref.shape[0]
    xn = _rms(h_ref[...], g_ref[...]).astype(BF16)
    z = jnp.dot(xn, w1_ref[...], preferred_element_type=F32)
    cq = z[:, :Q_LORA]
    ckv = z[:, Q_LORA:Q_LORA + KV_LORA]
    kr = z[:, Q_LORA + KV_LORA:Q_LORA + KV_LORA + LANE]
    mq_ref[...] = z[:, Q_LORA + KV_LORA + LANE:].astype(BF16)
    cqn = _rms(cq, gcq_ref[...]).astype(BF16)
    ckvn = _rms(ckv, gckv_ref[...]).astype(BF16)
    q = jnp.dot(cqn, wuq_ref[...], preferred_element_type=F32)
    kn = jnp.dot(ckvn, wuk_ref[...], preferred_element_type=F32)
    vt_ref[...] = lax.dot_general(wuvt_ref[...], ckvn, (((1,), (1,)), ((), ())),
                                  preferred_element_type=F32).astype(BF16)

    half = MLA_ROPE // 2
    eye = lax.broadcasted_iota(jnp.int32, (tm, tm), 0) == lax.broadcasted_iota(jnp.int32, (tm, tm), 1)
    pos = jnp.sum(jnp.where(eye, pos_ref[...].astype(F32), 0.0), axis=-1, keepdims=True)
    ang = pos * invf_ref[...]
    lane = lax.broadcasted_iota(jnp.int32, (tm, LANE), 1)
    cos = jnp.where(lane < MLA_ROPE, jnp.cos(ang), 0.0)
    sin = jnp.sin(ang)
    sin_hi = jnp.where((lane >= half) & (lane < MLA_ROPE), sin, 0.0)
    sin_lo = jnp.where(lane < half, -sin, 0.0)

    def rope(t):
        return t * cos + pltpu.roll(t, half, 1) * sin_hi + pltpu.roll(t, LANE - half, 1) * sin_lo

    scale = MLA_QK ** -0.5 * LOG2E
    gq = gq_ref[...]
    for h in range(MLA_HEADS):
        qh = q[:, h * HEAD_PAD:(h + 1) * HEAD_PAD]
        rs = lax.rsqrt(jnp.sum(qh * qh, axis=-1, keepdims=True) * (1.0 / MLA_QK) + EPS) * scale
        qn = qh * rs * gq
        q_ref[:, h * HEAD_PAD:h * HEAD_PAD + LANE] = qn[:, :LANE].astype(BF16)
        q_ref[:, h * HEAD_PAD + LANE:(h + 1) * HEAD_PAD] = rope(qn[:, LANE:]).astype(BF16)
    ss_r = jnp.sum(kr * kr, axis=-1, keepdims=True)
    krr = rope(kr * gkr_ref[...])
    gkn = gkn_ref[...]
    for h in range(MLA_HEADS):
        kh = kn[:, h * MLA_NOPE:(h + 1) * MLA_NOPE]
        rs = lax.rsqrt((jnp.sum(kh * kh, axis=-1, keepdims=True) + ss_r) * (1.0 / MLA_QK) + EPS)
        k_ref[:, h * HEAD_PAD:h * HEAD_PAD + LANE] = (kh * rs * gkn).astype(BF16)
        k_ref[:, h * HEAD_PAD + LANE:(h + 1) * HEAD_PAD] = (krr * rs).astype(BF16)


def _in1(h2d, pos2d, ln, w1, gcq, gckv, wuq, wuk, wuvt, gq, gkn, gkr, invf):
    t = h2d.shape[0]
    tm = TM_IN1
    tok = lambda n: jax.ShapeDtypeStruct((t, n), BF16)
    row = lambda n: pl.BlockSpec((tm, n), lambda i: (i, 0))
    consts = [ln, w1, gcq, gckv, wuq, wuk, wuvt, gq, gkn, gkr, invf]
    return pl.pallas_call(
        _in1_kernel,
        out_shape=(tok(MLA_PAD), tok(MLA_PAD), jax.ShapeDtypeStruct((MAIN_WIDTH, t), BF16), tok(MEM_WIDTH)),
        grid=(t // tm,),
        in_specs=[row(D_MODEL), pl.BlockSpec((1, tm), lambda i: (0, i))] + [_const_spec(a.shape) for a in consts],
        out_specs=(row(MLA_PAD), row(MLA_PAD), pl.BlockSpec((MAIN_WIDTH, tm), lambda i: (0, i)), row(MEM_WIDTH)),
        compiler_params=_cparams(("parallel",)),
        name="in_proj1",
    )(h2d, pos2d, *consts)


def _attn_kernel(n_q_tiles, q_ref, k_ref, vt_ref, cast_a_ref, cast_b_ref, o_ref, cast_a_out_ref, cast_b_out_ref,
                 m_ref, l_ref, acc_ref, st_ref, mt_ref):
    qi = pl.program_id(2)
    step = (pl.program_id(0) * pl.num_programs(1) + pl.program_id(1)) * pl.num_programs(2) + qi
    _cast_block(step, cast_a_ref, cast_a_out_ref, N_EXPERTS * CAST_PARTS)
    _cast_block(step, cast_b_ref, cast_b_out_ref, N_EXPERTS * CAST_PARTS)
    n_diag = TQ // TK
    nt_dims = (((1,), (1,)), ((), ()))

    def scores(j, slot):
        koff = j * TK
        st = lax.dot_general(k_ref[pl.ds(koff, TK), :], q_ref[...], nt_dims, preferred_element_type=F32)
        st_ref[slot] = st
        mt_ref[slot] = jnp.max(st, axis=0, keepdims=True)

    def absorb(st, m_tile, vt, cols):
        m_old = m_ref[:, cols]
        m_new = jnp.maximum(m_old, m_tile)
        alpha = jnp.exp2(m_old - m_new)
        e = jnp.exp2(st - m_new)
        l_ref[:, cols] = alpha * l_ref[:, cols] + jnp.sum(e, axis=0, keepdims=True)
        acc_ref[:, cols] = alpha * acc_ref[:, cols] + jnp.dot(vt, e.astype(BF16), preferred_element_type=F32)
        m_ref[:, cols] = m_new

    def finish(j, slot):
        absorb(st_ref[slot], mt_ref[slot], vt_ref[:, pl.ds(j * TK, TK)], slice(None))

    def query_tile(qt):
        m_ref[...] = jnp.full_like(m_ref, NEG)
        l_ref[...] = jnp.zeros_like(l_ref)
        acc_ref[...] = jnp.zeros_like(acc_ref)
        n_full = qt * n_diag
        for j in range(min(2, n_full)):
            scores(j, j % 2)
        for j in range(n_full):
            finish(j, j % 2)
            if j + 2 < n_full:
                scores(j + 2, j % 2)
        kd0 = n_full * TK
        for u in range(n_diag):
            nk = (u + 1) * TK
            cols = slice(u * TK, (u + 1) * TK)
            st = lax.dot_general(k_ref[pl.ds(kd0, nk), :], q_ref[cols, :], nt_dims,
                                 preferred_element_type=F32)
            kpos = lax.broadcasted_iota(jnp.int32, st.shape, 0)
            qpos = u * TK + lax.broadcasted_iota(jnp.int32, st.shape, 1)
            st = jnp.where(kpos <= qpos, st, NEG)
            absorb(st, jnp.max(st, axis=0, keepdims=True), vt_ref[:, pl.ds(kd0, nk)], cols)
        o_ref[...] = (acc_ref[...] * (1.0 / l_ref[...])).T.astype(o_ref.dtype)

    for qt in range(n_q_tiles):
        pl.when(qi == qt)(functools.partial(query_tile, qt))


def _attention(q, k, v, cast_a, cast_b, batch, seq):
    t = q.shape[0]
    nq = seq // TQ
    assert batch * MLA_HEADS * nq >= cast_a.shape[0] * CAST_PARTS
    step_of = lambda b, h, i: (b * MLA_HEADS + h) * nq + i
    cast_specs = [_cast_spec(cast_a.shape, step_of), _cast_spec(cast_b.shape, step_of)]
    return pl.pallas_call(
        functools.partial(_attn_kernel, nq),
        out_shape=(jax.ShapeDtypeStruct((t, MAIN_WIDTH), BF16), jax.ShapeDtypeStruct(cast_a.shape, BF16),
                   jax.ShapeDtypeStruct(cast_b.shape, BF16)),
        grid=(batch, MLA_HEADS, nq),
        in_specs=[pl.BlockSpec((TQ, HEAD_PAD), lambda b, h, i: (b * nq + i, h)),
                  pl.BlockSpec((seq, HEAD_PAD), lambda b, h, i: (b, h)),
                  pl.BlockSpec((MLA_V, seq), lambda b, h, i: (h, b))] + cast_specs,
        out_specs=[pl.BlockSpec((TQ, MLA_V), lambda b, h, i: (b * nq + i, h))] + cast_specs,
        scratch_shapes=[pltpu.VMEM((1, TQ), F32), pltpu.VMEM((1, TQ), F32), pltpu.VMEM((MLA_V, TQ), F32),
                        pltpu.VMEM((2, TK, TQ), F32), pltpu.VMEM((2, 1, TQ), F32)],
        compiler_params=_cparams(("arbitrary", "arbitrary", "arbitrary")),
        name="mla_attention",
    )(q, k, v, cast_a, cast_b)


def _router_kernel(h_ref, g_ref, whi_ref, wlo_ref,
                   xn_ref, gatest_ref, rank_ref, rankt_ref, starts_ref, counts_ref, run_ref):
    @pl.when(pl.program_id(0) == 0)
    def _():
        run_ref[...] = jnp.zeros_like(run_ref)

    tm = h_ref.shape[0]
    xn = _rms(h_ref[...], g_ref[...])
    hi = xn.astype(BF16)
    lo = (xn - hi.astype(F32)).astype(BF16)
    xn_ref[...] = hi
    whi = whi_ref[...]
    logits = (jnp.dot(hi, whi, preferred_element_type=F32) + jnp.dot(lo, whi, preferred_element_type=F32)
              + jnp.dot(hi, wlo_ref[...], preferred_element_type=F32))
    lane = lax.broadcasted_iota(jnp.int32, logits.shape, 1).astype(F32)
    logits = jnp.where(lane < N_EXPERTS, logits, NEG)
    m1 = jnp.max(logits, axis=-1, keepdims=True)
    i1 = jnp.min(jnp.where(logits == m1, lane, float(LANE)), axis=-1, keepdims=True)
    rest = jnp.where(lane == i1, NEG, logits)
    m2 = jnp.max(rest, axis=-1, keepdims=True)
    i2 = jnp.min(jnp.where(rest == m2, lane, float(LANE)), axis=-1, keepdims=True)
    e2 = jnp.exp(m2 - m1)
    w1 = 1.0 / (1.0 + e2)
    w2 = e2 * w1
    gates = jnp.where(lane == i1, w1, jnp.where(lane == i2, w2, 0.0))
    gatest_ref[...] = gates.T[:N_EXPERTS, :]
    routed = (lane == i1) | (lane == i2)
    oh = jnp.where(routed, 1.0, 0.0)
    r = lax.broadcasted_iota(jnp.int32, (TM_PROJ, TM_PROJ), 0)
    c = lax.broadcasted_iota(jnp.int32, (TM_PROJ, TM_PROJ), 1)
    before = jnp.where(r > c, 1.0, 0.0).astype(BF16)
    run = run_ref[0:1, :]
    ranks = []
    for ch in range(tm // TM_PROJ):
        rows = slice(ch * TM_PROJ, (ch + 1) * TM_PROJ)
        starts_ref[8 * ch:8 * ch + 8, :] = jnp.broadcast_to(run, (8, LANE))
        prefix = jnp.dot(before, oh[rows].astype(BF16), preferred_element_type=F32)
        ranks.append(jnp.where(routed[rows], run + prefix, -1.0))
        run = run + jnp.sum(oh[rows], axis=0, keepdims=True)
    rank = jnp.concatenate(ranks, axis=0)
    rank_ref[...] = rank
    rankt_ref[...] = rank.T[:N_EXPERTS, :]
    total = jnp.broadcast_to(run, run_ref.shape)
    run_ref[...] = total
    counts_ref[...] = total


def _router(h2d, ln, whi, wlo):
    t = h2d.shape[0]
    tm = min(TM_ROUTER, t)
    nt = t // tm
    chunk_rows = 8 * (tm // TM_PROJ)
    row = lambda n: pl.BlockSpec((tm, n), lambda i: (i, 0))
    return pl.pallas_call(
        _router_kernel,
        out_shape=(jax.ShapeDtypeStruct((t, D_MODEL), BF16), jax.ShapeDtypeStruct((N_EXPERTS, t), F32),
                   jax.ShapeDtypeStruct((t, LANE), F32), jax.ShapeDtypeStruct((N_EXPERTS, t), F32),
                   jax.ShapeDtypeStruct((nt * chunk_rows, LANE), F32), jax.ShapeDtypeStruct((8, LANE), F32)),
        grid=(nt,),
        in_specs=[row(D_MODEL), _const_spec((1, D_MODEL)), _const_spec(whi.shape), _const_spec(wlo.shape)],
        out_specs=(row(D_MODEL), pl.BlockSpec((N_EXPERTS, tm), lambda i: (0, i)), row(LANE),
                   pl.BlockSpec((N_EXPERTS, tm), lambda i: (0, i)),
                   pl.BlockSpec((chunk_rows, LANE), lambda i: (i, 0)), _const_spec((8, LANE))),
        scratch_shapes=[pltpu.VMEM((8, LANE), F32)],
        compiler_params=_cparams(("arbitrary",)),
        name="moe_router",
    )(h2d, ln, whi, wlo)


def _dispatch_kernel(ge_ref, gr0_ref, gstart_ref, gn_ref, xn_ref, rankt_ref, gatest_ref, xs_ref, gs_ref,
                     acc_ref, gacc_ref):
    groups = [pl.program_id(0) * DISPATCH_GROUPS + u for u in range(DISPATCH_GROUPS)]
    eye = (lax.broadcasted_iota(jnp.int32, (SLOT_GROUP, SLOT_GROUP), 0)
           == lax.broadcasted_iota(jnp.int32, (SLOT_GROUP, SLOT_GROUP), 1))

    def gathered(g, start):
        start = pl.multiple_of(start, TM_PROJ)
        want = (gr0_ref[g] + lax.broadcasted_iota(jnp.int32, (SLOT_GROUP, DISPATCH_K), 0)).astype(F32)
        span = (pl.ds(ge_ref[g], 1), pl.ds(start, DISPATCH_K))
        hit = rankt_ref[span] == want
        rows = jnp.dot(jnp.where(hit, 1.0, 0.0).astype(BF16), xn_ref[pl.ds(start, DISPATCH_K), :],
                       preferred_element_type=F32)
        gate = jnp.sum(jnp.where(hit, gatest_ref[span], 0.0), axis=-1, keepdims=True)
        return rows, gate

    def as_row(col):
        return jnp.sum(jnp.where(eye, col, 0.0), axis=0, keepdims=True)

    single = gn_ref[groups[0]] <= 1
    for g in groups[1:]:
        single = single & (gn_ref[g] <= 1)

    @pl.when(single)
    def _():
        gate_rows = []
        for u, g in enumerate(groups):
            rows, gate = gathered(g, gstart_ref[g])
            xs_ref[u * SLOT_GROUP:(u + 1) * SLOT_GROUP, :] = rows.astype(BF16)
            gate_rows.append(as_row(gate))
        gs_ref[...] = jnp.broadcast_to(jnp.concatenate(gate_rows, axis=1), gs_ref.shape)

    @pl.when(jnp.logical_not(single))
    def _():
        gate_rows = []
        for u, g in enumerate(groups):
            acc_ref[...] = jnp.zeros_like(acc_ref)
            gacc_ref[...] = jnp.zeros_like(gacc_ref)

            def body(k, carry, g=g):
                rows, gate = gathered(g, gstart_ref[g] + k * DISPATCH_K)
                acc_ref[...] += rows
                gacc_ref[...] += gate
                return carry

            lax.fori_loop(0, gn_ref[g], body, 0)
            xs_ref[u * SLOT_GROUP:(u + 1) * SLOT_GROUP, :] = acc_ref[...].astype(BF16)
            gate_rows.append(as_row(gacc_ref[...]))
        gs_ref[...] = jnp.broadcast_to(jnp.concatenate(gate_rows, axis=1), gs_ref.shape)


def _dispatch(xn, rankt, gatest, g_expert, g_rank0, g_start, g_nspans):
    t = xn.shape[0]
    n_steps = g_expert.shape[0] // DISPATCH_GROUPS
    rows = DISPATCH_GROUPS * SLOT_GROUP
    assert rows == TM_MOE
    res = lambda shape: pl.BlockSpec(shape, lambda g, *_: (0, 0), pipeline_mode=pl.Buffered(1))
    grid_spec = pltpu.PrefetchScalarGridSpec(
        num_scalar_prefetch=4,
        grid=(n_steps,),
        in_specs=[res((t, D_MODEL)), res((N_EXPERTS, t)), res((N_EXPERTS, t))],
        out_specs=(pl.BlockSpec((rows, D_MODEL), lambda g, *_: (g, 0)),
                   pl.BlockSpec((8, rows), lambda g, *_: (g, 0))),
        scratch_shapes=[pltpu.VMEM((SLOT_GROUP, D_MODEL), F32), pltpu.VMEM((SLOT_GROUP, 1), F32)],
    )
    return pl.pallas_call(
        _dispatch_kernel,
        out_shape=(jax.ShapeDtypeStruct((n_steps * rows, D_MODEL), BF16),
                   jax.ShapeDtypeStruct((n_steps * 8, rows), F32)),
        grid_spec=grid_spec,
        compiler_params=_cparams(("arbitrary",)),
        name="moe_dispatch",
    )(g_expert, g_rank0, g_start, g_nspans, xn, rankt, gatest)


def _expert_kernel(te_ref, ta_ref, x_ref, gs_ref, wg_ref, wu_ref, wd_ref, out_ref):
    active = ta_ref[pl.program_id(0)] > 0

    @pl.when(active)
    def _():
        x = x_ref[...]
        acc = None
        for c in range(D_FF // FF_CHUNK):
            cs = slice(c * FF_CHUNK, (c + 1) * FF_CHUNK)
            gt = jnp.dot(x, wg_ref[0, :, cs], preferred_element_type=F32)
            up = jnp.dot(x, wu_ref[0, :, cs], preferred_element_type=F32)
            a = (gt * jax.nn.sigmoid(gt) * up).astype(BF16)
            y = jnp.dot(a, wd_ref[0, cs, :], preferred_element_type=F32)
            acc = y if acc is None else acc + y
        tm = x.shape[0]
        eye = lax.broadcasted_iota(jnp.int32, (tm, tm), 0) == lax.broadcasted_iota(jnp.int32, (tm, tm), 1)
        gate = jnp.sum(jnp.where(eye, gs_ref[0:1, :], 0.0), axis=-1, keepdims=True)
        out_ref[...] = (acc * gate).astype(out_ref.dtype)

    @pl.when(jnp.logical_not(active))
    def _():
        out_ref[...] = jnp.zeros_like(out_ref)


def _experts(x, gate_rows, tile_expert, tile_active, wg, wu, wd):
    n_tiles = tile_expert.shape[0]
    tm = TM_MOE
    wmap = lambda j, te, ta: (te[j], 0, 0)
    tile = lambda j, te, ta: (jnp.where(ta[j] > 0, j, 0), 0)
    grid_spec = pltpu.PrefetchScalarGridSpec(
        num_scalar_prefetch=2,
        grid=(n_tiles,),
        in_specs=[pl.BlockSpec((tm, D_MODEL), tile),
                  pl.BlockSpec((8, tm), tile),
                  pl.BlockSpec((1, D_MODEL, D_FF), wmap),
                  pl.BlockSpec((1, D_MODEL, D_FF), wmap),
                  pl.BlockSpec((1, D_FF, D_MODEL), wmap)],
        out_specs=pl.BlockSpec((tm, D_MODEL), lambda j, te, ta: (j, 0)),
    )
    return pl.pallas_call(
        _expert_kernel,
        out_shape=jax.ShapeDtypeStruct((n_tiles * tm, D_MODEL), BF16),
        grid_spec=grid_spec,
        compiler_params=_cparams(("arbitrary",), EXPERT_VMEM_LIMIT),
        name="moe_experts",
    )(tile_expert, tile_active, x, gate_rows, wg, wu, wd)


def _combine_window_copies(ys_ref, buf_ref, sem_ref, ws_ref, tile, buf_slot, row_offset):
    copies = []
    for e in range(N_EXPERTS):
        start = pl.multiple_of(ws_ref[tile * N_EXPERTS + e] + row_offset, SLOT_ALIGN)
        copies.append(pltpu.make_async_copy(ys_ref.at[pl.ds(start, COMBINE_WIN), :],
                                            buf_ref.at[buf_slot, pl.ds(e * COMBINE_WIN, COMBINE_WIN), :],
                                            sem_ref.at[buf_slot, e]))
    return copies


def _combine_kernel(ws_ref, second_ref, h_ref, rank_ref, base_ref, ys_ref, out_ref, buf_ref, sem_ref):
    i = pl.program_id(0)
    n = pl.num_programs(0)
    tm = h_ref.shape[0]

    @pl.when(i == 0)
    def _():
        for cp in _combine_window_copies(ys_ref, buf_ref, sem_ref, ws_ref, 0, 0, 0):
            cp.start()

    @pl.when(i + 1 < n)
    def _():
        for cp in _combine_window_copies(ys_ref, buf_ref, sem_ref, ws_ref, i + 1, (i + 1) % 2, 0):
            cp.start()

    rank = rank_ref[...]
    slots = jnp.where(rank >= 0.0, rank + base_ref[...], -1.0)
    lane_iota = lax.broadcasted_iota(jnp.int32, (tm, COMBINE_WIN), 1)

    def gathered(buf_slot, row_offset):
        parts = []
        for e in range(N_EXPERTS):
            first = ws_ref[i * N_EXPERTS + e] + row_offset
            hit = slots[:, e:e + 1] == (first + lane_iota).astype(F32)
            parts.append(jnp.where(hit, 1.0, 0.0).astype(BF16))
        return jnp.dot(jnp.concatenate(parts, axis=1), buf_ref[buf_slot], preferred_element_type=F32)

    for cp in _combine_window_copies(ys_ref, buf_ref, sem_ref, ws_ref, i, i % 2, 0):
        cp.wait()
    out_ref[...] = h_ref[...] + gathered(i % 2, 0)

    for extra in range(1, COMBINE_MAX_WINDOWS):
        @pl.when(second_ref[i] > extra)
        def _(extra=extra):
            copies = _combine_window_copies(ys_ref, buf_ref, sem_ref, ws_ref, i, 2, extra * COMBINE_WIN)
            for cp in copies:
                cp.start()
            for cp in copies:
                cp.wait()
            out_ref[...] += gathered(2, extra * COMBINE_WIN)


def _combine(h2d, rank, base_row, ys, win_start, tile_second):
    t = h2d.shape[0]
    tm = TM_PROJ
    row = lambda n: pl.BlockSpec((tm, n), lambda i, *_: (i, 0))
    grid_spec = pltpu.PrefetchScalarGridSpec(
        num_scalar_prefetch=2,
        grid=(t // tm,),
        in_specs=[row(D_MODEL), row(LANE), pl.BlockSpec((1, LANE), lambda i, *_: (0, 0)),
                  pl.BlockSpec(memory_space=pl.ANY)],
        out_specs=row(D_MODEL),
        scratch_shapes=[pltpu.VMEM((3, N_EXPERTS * COMBINE_WIN, D_MODEL), BF16),
                        pltpu.SemaphoreType.DMA((3, N_EXPERTS))],
    )
    return pl.pallas_call(
        _combine_kernel,
        out_shape=jax.ShapeDtypeStruct((t, D_MODEL), F32),
        grid_spec=grid_spec,
        compiler_params=_cparams(("arbitrary",)),
        name="moe_combine",
    )(win_start, tile_second, h2d, rank, base_row, ys)


def _moe_plan(counts, starts, t):
    n_tok_tiles = t // TM_PROJ
    n_tiles = 2 * t // TM_MOE + N_EXPERTS + 1
    n_slots = n_tiles * TM_MOE
    counts = counts.astype(jnp.int32)
    cum = jnp.concatenate([starts.astype(jnp.int32), counts[None, :]], axis=0)
    padded = (counts + TM_MOE - 1) // TM_MOE * TM_MOE
    ends = jnp.cumsum(padded)
    base = ends - padded
    total = ends[-1]
    tile_first = jnp.arange(n_tiles, dtype=jnp.int32) * TM_MOE
    tile_expert = jnp.minimum(jnp.sum(ends[None, :] <= tile_first[:, None], axis=1), N_EXPERTS - 1).astype(jnp.int32)
    tile_active = (tile_first < total).astype(jnp.int32)
    n_groups = n_slots // SLOT_GROUP
    g_first = jnp.arange(n_groups, dtype=jnp.int32) * SLOT_GROUP
    g_expert = tile_expert[g_first // TM_MOE]
    g_r0 = g_first - base[g_expert]
    cum_e = cum.T[g_expert]
    lo = jnp.sum(cum_e <= g_r0[:, None], axis=1) - 1
    hi = jnp.sum(cum_e < (g_r0 + SLOT_GROUP)[:, None], axis=1)
    g_valid = (g_first < total) & (g_r0 < counts[g_expert])
    lo = jnp.clip(lo, 0, n_tok_tiles - 1)
    hi = jnp.clip(hi, 0, n_tok_tiles)
    g_n = jnp.where(g_valid, (jnp.maximum(hi - lo, 0) * TM_PROJ + DISPATCH_K - 1) // DISPATCH_K, 0)
    lo = jnp.minimum(lo * TM_PROJ, t - jnp.maximum(g_n, 1) * DISPATCH_K)
    first = base[None, :] + cum[:-1]
    cnt = cum[1:] - cum[:-1]
    ws = first // SLOT_ALIGN * SLOT_ALIGN
    pieces = jnp.where(cnt > 0, (first + cnt - ws + COMBINE_WIN - 1) // COMBINE_WIN, 0)
    return dict(tile_expert=tile_expert, tile_active=tile_active, g_expert=g_expert, g_r0=g_r0.astype(jnp.int32),
                g_lo=lo.astype(jnp.int32), g_n=g_n.astype(jnp.int32), base=base,
                ws=ws.reshape(-1).astype(jnp.int32), second=jnp.max(pieces, axis=1).astype(jnp.int32))


def _pad_heads_cols(w, heads, dim, pad):
    lead = w.shape[:-1]
    w = w.reshape(lead + (heads, dim))
    w = jnp.pad(w, [(0, 0)] * len(lead) + [(0, 0), (0, pad - dim)])
    return w.reshape(lead + (heads * pad,))


def _row(v):
    return v.reshape(1, -1).astype(F32)


def kernel(x, mem, positions, mem_norm, w_mem_kv, ln_mix0, w_in0, conv_w0, b_i0, b_f0, h_norm0, mq_norm0, mk_norm0, w_out0, ln_ffn0, w_gate0, w_up0, w_down0, ln_mix1, w_in1, cq_norm1, ckv_norm1, w_uq1, w_ukv1, q_norm1, k_norm1, mq_norm1, mk_norm1, w_out1, ln_ffn1, w_router1, we_gate1, we_up1, we_down1):
    batch, seq, _ = x.shape
    t = batch * seq
    h = x.reshape(t, D_MODEL)
    tile4 = lambda g: _row(jnp.tile(g, MEM_HEADS))

    k4_0, k4_1, v4 = _memkv(mem.reshape(batch * MEM_TOKENS, D_MODEL), _row(mem_norm), w_mem_kv.astype(BF16),
                            tile4(mk_norm0), tile4(mk_norm1), batch)

    mw = MAIN_WIDTH
    w_q, w_k, w_v, w_o = (w_in0[:, i * mw:(i + 1) * mw] for i in range(4))
    w_gates = w_in0[:, 4 * mw:4 * mw + 2 * ML_HEADS]
    w_mq = w_in0[:, 4 * mw + 2 * ML_HEADS:]
    padh = lambda w: _pad_heads_cols(w, ML_HEADS, ML_HEAD_DIM, HEAD_PAD)
    wqk = jnp.concatenate([padh(w_q), padh(w_k)], axis=1).astype(BF16)
    wvo = jnp.concatenate([padh(w_v), padh(w_o)], axis=1).astype(BF16)
    wg = jnp.pad(w_gates, ((0, 0), (0, GATE_PAD - 2 * ML_HEADS))).astype(BF16)
    wgt = w_gates.T.astype(BF16)
    cw = jnp.concatenate([padh(conv_w0[:, :mw]), padh(conv_w0[:, mw:])], axis=1).astype(F32)
    bias = jnp.concatenate([b_i0, b_f0]).astype(F32)
    bias_row = jnp.pad(bias, (0, GATE_PAD - 2 * ML_HEADS)).reshape(1, GATE_PAD)
    bias_col = bias.reshape(2 * ML_HEADS, 1)
    q0, kt0, v0, so, mq0, gates, gatest = _in0(h, _row(ln_mix0), wqk, wvo, w_mq.astype(BF16), wg, wgt, cw,
                                               bias_row, bias_col, seq)
    hgain = jnp.pad(h_norm0.astype(F32), (0, HEAD_PAD - ML_HEAD_DIM)).reshape(1, HEAD_PAD)
    ffn_groups = lambda w: w.reshape(N_EXPERTS, w.shape[0] // N_EXPERTS, w.shape[1])
    hn, (wg0, wu0, wd0) = _mlstm(q0, kt0, v0, so, gates, gatest, hgain,
                                 [ffn_groups(w_gate0), ffn_groups(w_up0), ffn_groups(w_down0)], batch, seq)
    wm0 = jnp.pad(w_out0[:mw].reshape(ML_HEADS, ML_HEAD_DIM, D_MODEL),
                  ((0, 0), (0, HEAD_PAD - ML_HEAD_DIM), (0, 0))).reshape(ML_PAD, D_MODEL).astype(BF16)
    h, we_gate_bf = _out_proj(hn, mq0, h, k4_0, v4, tile4(mq_norm0), wm0, w_out0[mw:].astype(BF16), seq,
                              ffn=(_row(ln_ffn0), wg0.reshape(w_gate0.shape), wu0.reshape(w_up0.shape),
                                   wd0.reshape(w_down0.shape), we_gate1))

    w_cq = w_in1[:, :Q_LORA]
    w_ckv = w_in1[:, Q_LORA:Q_LORA + KV_LORA]
    w_kr = jnp.pad(w_in1[:, Q_LORA + KV_LORA:Q_LORA + KV_LORA + MLA_ROPE], ((0, 0), (0, LANE - MLA_ROPE)))
    w_mq1 = w_in1[:, Q_LORA + KV_LORA + MLA_ROPE:]
    w1 = jnp.concatenate([w_cq, w_ckv, w_kr, w_mq1], axis=1).astype(BF16)
    wuq = _pad_heads_cols(w_uq1, MLA_HEADS, MLA_QK, HEAD_PAD).astype(BF16)
    wukv = w_ukv1.reshape(KV_LORA, MLA_HEADS, MLA_NOPE + MLA_V)
    wuk = wukv[:, :, :MLA_NOPE].reshape(KV_LORA, MLA_HEADS * MLA_NOPE).astype(BF16)
    wuvt = wukv[:, :, MLA_NOPE:].reshape(KV_LORA, MLA_HEADS * MLA_V).T.astype(BF16)
    gq = jnp.pad(q_norm1.astype(F32), (0, HEAD_PAD - MLA_QK)).reshape(1, HEAD_PAD)
    gkn = _row(k_norm1[:MLA_NOPE])
    gkr = jnp.pad(k_norm1[MLA_NOPE:].astype(F32), (0, LANE - MLA_ROPE)).reshape(1, LANE)
    half = MLA_ROPE // 2
    inv_freq = ROPE_THETA ** (-jnp.arange(half, dtype=F32) / half)
    invf = jnp.concatenate([inv_freq, inv_freq, jnp.zeros((LANE - MLA_ROPE,), F32)]).reshape(1, LANE)
    q1, k1, v1, mq1 = _in1(h, positions.reshape(1, t).astype(jnp.int32), _row(ln_mix1), w1, _row(cq_norm1),
                           _row(ckv_norm1), wuq, wuk, wuvt, gq, gkn, gkr, invf)
    o1, we_up_bf, we_down_bf = _attention(q1, k1, v1, we_up1, we_down1, batch, seq)
    h = _out_proj(o1, mq1, h, k4_1, v4, tile4(mq_norm1), w_out1[:mw].astype(BF16), w_out1[mw:].astype(BF16), seq)

    wr = jnp.pad(w_router1.astype(F32), ((0, 0), (0, LANE - N_EXPERTS)))
    wr_hi = wr.astype(BF16)
    wr_lo = (wr - wr_hi.astype(F32)).astype(BF16)
    xn, gatest, rank, rankt, starts, counts = _router(h, _row(ln_ffn1), wr_hi, wr_lo)
    plan = _moe_plan(counts[0, :N_EXPERTS], starts[::8, :N_EXPERTS], t)
    xs, gate_rows = _dispatch(xn, rankt, gatest, plan["g_expert"], plan["g_r0"], plan["g_lo"], plan["g_n"])
    ys = _experts(xs, gate_rows, plan["tile_expert"], plan["tile_active"], we_gate_bf, we_up_bf, we_down_bf)
    base_row = jnp.pad(plan["base"].astype(F32), (0, LANE - N_EXPERTS)).reshape(1, LANE)
    h = _combine(h, rank, base_row, ys, plan["ws"], plan["second"])
    return h.reshape(batch, seq, D_MODEL)
```

```python
import functools

import jax
import jax.numpy as jnp
import numpy as np
from jax import lax
from jax.experimental import pallas as pl
from jax.experimental.pallas import tpu as pltpu

F32 = jnp.float32
BF16 = jnp.bfloat16

D_MODEL = 1024
MEM_TOKENS = 256
MEM_HEADS = 4
MEM_WIDTH = 256
MEM_HEAD_DIM = 64
MAIN_WIDTH = 768
ML_HEADS = 4
ML_HEAD_DIM = 192
CONV_WIDTH = 4
MLA_HEADS = 6
MLA_NOPE = 128
MLA_ROPE = 64
MLA_QK = 192
MLA_V = 128
Q_LORA = 384
KV_LORA = 128
ROPE_THETA = 10000.0
D_FF = 3584
N_EXPERTS = 8
EPS = 1e-6

LANE = 128
HEAD_PAD = 256
ML_PAD = ML_HEADS * HEAD_PAD
MLA_PAD = MLA_HEADS * HEAD_PAD
GATE_PAD = 128
ML_CHUNK = 256
TM_PROJ = 256
TM_ROUTER = 1024
TM_IN0 = 1024
TM_IN1 = 1024
TM_OUT = 1024
TM_FFN = 512
FF_CHUNK = 512
TM_MOE = 512
SLOT_GROUP = 128
DISPATCH_K = 1024
DISPATCH_GROUPS = 4
CAST_PARTS = 4
IN0_ROW_SPLIT = 4
IN1_ROW_SPLIT = 4
SLOT_ALIGN = 16
COMBINE_WIN = 128
COMBINE_MAX_WINDOWS = -(-(TM_PROJ + SLOT_ALIGN - 1) // COMBINE_WIN)
TQ = 2048
TK = 512
NEG = -1e30
LOG2E = 1.4426950408889634
VMEM_LIMIT = 56 * 1024 * 1024
EXPERT_VMEM_LIMIT = 62 * 1024 * 1024


def _cparams(sem, vmem=VMEM_LIMIT):
    return pltpu.CompilerParams(dimension_semantics=sem, vmem_limit_bytes=vmem)


def _rms(x, g):
    return x * lax.rsqrt(jnp.mean(x * x, axis=-1, keepdims=True) + EPS) * g


def _split_dot(x, w_bf16):
    hi = x.astype(BF16)
    lo = (x - hi.astype(F32)).astype(BF16)
    return (jnp.dot(hi, w_bf16, preferred_element_type=F32)
            + jnp.dot(lo, w_bf16, preferred_element_type=F32))


def _block_diag_ones(n, blk):
    r = lax.broadcasted_iota(jnp.int32, (n, n), 0) // blk
    c = lax.broadcasted_iota(jnp.int32, (n, n), 1) // blk
    return jnp.where(r == c, 1.0, 0.0).astype(BF16)


def _const_spec(shape):
    nd = len(shape)
    return pl.BlockSpec(shape, lambda *_: (0,) * nd)


def _interleave(stage_generators):
    live = list(stage_generators)
    while live:
        for gen in list(live):
            try:
                next(gen)
            except StopIteration:
                live.remove(gen)


def _memkv_kernel(mem_ref, g_ref, w_ref, gk0_ref, gk1_ref, k0_ref, k1_ref, v_ref):
    xn = _rms(mem_ref[...], g_ref[...]).astype(BF16)
    kv = jnp.dot(xn, w_ref[...], preferred_element_type=F32)
    k = kv[:, :MEM_WIDTH]
    v = kv[:, MEM_WIDTH:]
    bd = _block_diag_ones(MEM_WIDTH, MEM_HEAD_DIM)
    ms = _split_dot(k * k, bd) * (1.0 / MEM_HEAD_DIM)
    kn = k * lax.rsqrt(ms + EPS)
    lane_head = lax.broadcasted_iota(jnp.int32, (MEM_TOKENS, MEM_WIDTH), 1) // MEM_HEAD_DIM
    for h in range(MEM_HEADS):
        sel = lane_head == h
        k0_ref[h] = jnp.where(sel, kn * gk0_ref[...], 0.0).astype(BF16)
        k1_ref[h] = jnp.where(sel, kn * gk1_ref[...], 0.0).astype(BF16)
        v_ref[h] = jnp.where(sel, v, 0.0).astype(BF16)


def _memkv(mem2d, mem_norm, w_kv, gk0, gk1, batch):
    out = jax.ShapeDtypeStruct((batch * MEM_HEADS, MEM_TOKENS, MEM_WIDTH), BF16)
    hspec = pl.BlockSpec((MEM_HEADS, MEM_TOKENS, MEM_WIDTH), lambda b: (b, 0, 0))
    return pl.pallas_call(
        _memkv_kernel,
        out_shape=(out, out, out),
        grid=(batch,),
        in_specs=[pl.BlockSpec((MEM_TOKENS, D_MODEL), lambda b: (b, 0)),
                  _const_spec((1, D_MODEL)), _const_spec((D_MODEL, 2 * MEM_WIDTH)),
                  _const_spec((1, MEM_WIDTH)), _const_spec((1, MEM_WIDTH))],
        out_specs=(hspec, hspec, hspec),
        compiler_params=_cparams(("parallel",)),
        name="memkv",
    )(mem2d, mem_norm, w_kv, gk0, gk1)


def _in0_kernel(tiles_per_seq, h_ref, hp_ref, g_ref, wqk_ref, wvo_ref, wmq_ref, wg_ref, wgt_ref,
                cw_ref, bias_ref, biast_ref,
                q_ref, kt_ref, v_ref, so_ref, mq_ref, gates_ref, gatest_ref):
    i = pl.program_id(0)
    tm = h_ref.shape[0] // IN0_ROW_SPLIT
    g = g_ref[...]
    first = (i % tiles_per_seq) == 0
    row8 = lax.broadcasted_iota(jnp.int32, (8, FF_CHUNK), 0)
    ncol = wqk_ref.shape[1]

    def gate_act(z, is_input_gate):
        logsig = jnp.minimum(z, 0.0) - jnp.log(1.0 + jnp.exp(-jnp.abs(z)))
        return jnp.where(is_input_gate, z, logsig)

    def rows_body(r):
        rows = slice(r * tm, (r + 1) * tm)
        xn = _rms(h_ref[rows, :], g).astype(BF16)
        prev_rows = hp_ref[...] if r == 0 else h_ref[r * tm - 8:r * tm, :]
        xpn = _rms(prev_rows, g).astype(BF16)
        yield
        for c in range(ncol // FF_CHUNK):
            cs = slice(c * FF_CHUNK, (c + 1) * FF_CHUNK)
            w = wqk_ref[:, cs]
            u = jnp.dot(xn, w, preferred_element_type=F32)
            up = jnp.dot(xpn, w, preferred_element_type=F32)
            if r == 0:
                up = jnp.where(first, 0.0, up)
            yield
            acc = u * cw_ref[CONV_WIDTH - 1:CONV_WIDTH, cs]
            for k in range(1, CONV_WIDTH):
                rolled = pltpu.roll(u, k, 0)
                prev = pltpu.roll(up, k, 0)
                head = jnp.where(row8 < k, prev, rolled[:8])
                shifted = jnp.concatenate([head, rolled[8:]], axis=0)
                acc = acc + shifted * cw_ref[CONV_WIDTH - 1 - k:CONV_WIDTH - k, cs]
            y = acc * jax.nn.sigmoid(acc)
            if c < ML_PAD // FF_CHUNK:
                q_ref[rows, cs] = (y * (ML_HEAD_DIM ** -0.5)).astype(BF16)
            else:
                ks = slice(c * FF_CHUNK - ML_PAD, (c + 1) * FF_CHUNK - ML_PAD)
                kt_ref[ks, rows] = y.T.astype(BF16)
            yield
        zvo = jnp.dot(xn, wvo_ref[...], preferred_element_type=F32)
        yield
        v_ref[rows, :] = zvo[:, :ML_PAD].astype(BF16)
        so_ref[rows, :] = jax.nn.sigmoid(zvo[:, ML_PAD:]).astype(BF16)
        yield
        mq_ref[rows, :] = jnp.dot(xn, wmq_ref[...], preferred_element_type=F32).astype(BF16)
        zg = jnp.dot(xn, wg_ref[...], preferred_element_type=F32) + bias_ref[...]
        lane = lax.broadcasted_iota(jnp.int32, (tm, GATE_PAD), 1)
        gates_ref[rows, :] = gate_act(zg, lane < ML_HEADS)
        zgt = lax.dot_general(wgt_ref[...], xn, (((1,), (1,)), ((), ())),
                              preferred_element_type=F32) + biast_ref[...]
        row = lax.broadcasted_iota(jnp.int32, (8, tm), 0)
        gatest_ref[:, rows] = gate_act(zgt, row < ML_HEADS)

    _interleave(rows_body(r) for r in range(IN0_ROW_SPLIT))


def _in0(h2d, ln, wqk, wvo, wmq, wg, wgt, cw, bias, biast, seq):
    t = h2d.shape[0]
    tm = TM_IN0
    tiles_per_seq = seq // tm
    tok = lambda n, dt: jax.ShapeDtypeStruct((t, n), dt)
    row = lambda n: pl.BlockSpec((tm, n), lambda i: (i, 0))
    col = lambda n: pl.BlockSpec((n, tm), lambda i: (0, i))
    return pl.pallas_call(
        functools.partial(_in0_kernel, tiles_per_seq),
        out_shape=(tok(ML_PAD, BF16), jax.ShapeDtypeStruct((ML_PAD, t), BF16), tok(ML_PAD, BF16),
                   tok(ML_PAD, BF16), tok(MEM_WIDTH, BF16), tok(GATE_PAD, F32), jax.ShapeDtypeStruct((8, t), F32)),
        grid=(t // tm,),
        in_specs=[row(D_MODEL),
                  pl.BlockSpec((8, D_MODEL), lambda i: (jnp.maximum(i * (tm // 8) - 1, 0), 0)),
                  _const_spec((1, D_MODEL)), _const_spec(wqk.shape), _const_spec(wvo.shape),
                  _const_spec(wmq.shape), _const_spec(wg.shape), _const_spec(wgt.shape),
                  _const_spec(cw.shape), _const_spec(bias.shape), _const_spec(biast.shape)],
        out_specs=(row(ML_PAD), col(ML_PAD), row(ML_PAD), row(ML_PAD), row(MEM_WIDTH), row(GATE_PAD), col(8)),
        compiler_params=_cparams(("parallel",)),
        name="in_proj0",
    )(h2d, h2d, ln, wqk, wvo, wmq, wg, wgt, cw, bias, biast)


def _mlstm_kernel(batch, cast_blocks, q_ref, v_ref, so_ref, gates_ref, *rest):
    n_cast = len(cast_blocks)
    kt_refs, gatest_refs = rest[:batch], rest[batch:2 * batch]
    hg_ref = rest[2 * batch]
    cast_src = rest[2 * batch + 1:2 * batch + 1 + n_cast]
    out_ref = rest[2 * batch + 1 + n_cast]
    cast_dst = rest[2 * batch + 2 + n_cast:2 * batch + 2 + 2 * n_cast]
    c_ref, m_ref = rest[2 * batch + 2 + 2 * n_cast:]
    L = q_ref.shape[1]
    for src, dst, n_blocks in zip(cast_src, cast_dst, cast_blocks):
        _cast_block(pl.program_id(0), src, dst, n_blocks)

    @pl.when(pl.program_id(0) == 0)
    def _():
        c_ref[...] = jnp.zeros_like(c_ref)
        m_ref[...] = jnp.zeros_like(m_ref)

    r = lax.broadcasted_iota(jnp.int32, (L, L), 0)
    c = lax.broadcasted_iota(jnp.int32, (L, L), 1)
    causal = r >= c
    tri_lo = jnp.where(causal, 1.0, 0.0).astype(BF16)
    tri_up = jnp.where(r <= c, 1.0, 0.0).astype(BF16)
    lane = lax.broadcasted_iota(jnp.int32, (L, HEAD_PAD), 1)
    hg = hg_ref[...]
    def sequence_body(b):
        gts = gates_ref[b]
        gtt = gatest_refs[b][...]
        hi = gts.astype(BF16)
        lo = (gts - hi.astype(F32)).astype(BF16)
        b_cols = (jnp.dot(tri_lo, hi, preferred_element_type=F32)
                  + jnp.dot(tri_lo, lo, preferred_element_type=F32))
        b_rows = _split_dot(gtt, tri_up)
        yield
        heads = range(ML_HEADS)
        hs = [slice(h * HEAD_PAD, (h + 1) * HEAD_PAD) for h in heads]
        state = [b * ML_HEADS + h for h in heads]
        m_h = [m_ref[state[h]][0:1, 0:1] for h in heads]
        b_c = [b_cols[:, ML_HEADS + h:ML_HEADS + h + 1] for h in heads]
        c_r = [b_rows[ML_HEADS + h:ML_HEADS + h + 1, :] - gtt[h:h + 1, :] for h in heads]
        q_h = [q_ref[b, :, hs[h]] for h in heads]
        kt_h = [kt_refs[b][hs[h], :] for h in heads]
        s = [jnp.dot(q_h[h], kt_h[h], preferred_element_type=F32) for h in heads]
        c_old = [c_ref[state[h]] for h in heads]
        qc = [jnp.dot(q_h[h], c_old[h].astype(BF16), preferred_element_type=F32) for h in heads]
        yield
        log_d = [jnp.where(causal, b_c[h] - c_r[h], NEG) for h in heads]
        inter = [b_c[h] + m_h[h] for h in heads]
        m_row = [jnp.maximum(inter[h], jnp.max(log_d[h], axis=-1, keepdims=True)) for h in heads]
        yield
        d = [jnp.exp(log_d[h] - m_row[h]) for h in heads]
        s_inter = [jnp.exp(inter[h] - m_row[h]) for h in heads]
        v_aug = [jnp.where(lane == ML_HEAD_DIM, 1.0, v_ref[b, :, hs[h]].astype(F32)).astype(BF16) for h in heads]
        qkd = [(s[h] * d[h]).astype(BF16) for h in heads]
        yield
        num = [jnp.dot(qkd[h], v_aug[h], preferred_element_type=F32) + s_inter[h] * qc[h] for h in heads]
        yield
        b_last = [b_c[h][L - 1:L, :] for h in heads]
        g_r = [b_last[h] - c_r[h] for h in heads]
        m_new = [jnp.maximum(b_last[h] + m_h[h], jnp.max(g_r[h], axis=-1, keepdims=True)) for h in heads]
        w_r = [jnp.exp(g_r[h] - m_new[h]) for h in heads]
        decay = [jnp.exp(b_last[h] + m_h[h] - m_new[h]) for h in heads]
        ktw = [(kt_h[h].astype(F32) * w_r[h]).astype(BF16) for h in heads]
        yield
        for h in heads:
            c_ref[state[h]] = decay[h] * c_old[h] + jnp.dot(ktw[h], v_aug[h], preferred_element_type=F32)
            m_ref[state[h]] = jnp.broadcast_to(m_new[h], m_ref.shape[1:])
        yield
        den = [jnp.sum(jnp.where(lane == ML_HEAD_DIM, num[h], 0.0), axis=-1, keepdims=True) for h in heads]
        ht = [num[h] * (1.0 / jnp.maximum(jnp.abs(den[h]), jnp.exp(-m_row[h]))) for h in heads]
        ht = [jnp.where(lane < ML_HEAD_DIM, ht[h], 0.0) for h in heads]
        ms = [jnp.sum(ht[h] * ht[h], axis=-1, keepdims=True) * (1.0 / ML_HEAD_DIM) for h in heads]
        yield
        for h in heads:
            hn = ht[h] * lax.rsqrt(ms[h] + EPS) * hg * so_ref[b, :, hs[h]].astype(F32)
            out_ref[b, :, hs[h]] = hn.astype(BF16)

    _interleave(sequence_body(b) for b in range(batch))


def _mlstm(q, kt, v, so, gates, gatest, hgain, casts, batch, seq):
    L = ML_CHUNK
    nc = seq // L
    seq3 = lambda a: a.reshape(batch, seq, a.shape[-1])
    row = lambda n: pl.BlockSpec((batch, L, n), lambda c: (0, c, 0))
    col = lambda rows: [pl.BlockSpec((rows, L), functools.partial(lambda b, c: (0, b * nc + c), b))
                        for b in range(batch)]
    cast_blocks = tuple(a.shape[0] * CAST_PARTS for a in casts)
    assert all(n <= nc for n in cast_blocks)
    cast_specs = [_cast_spec(a.shape, lambda c: c) for a in casts]
    outs = pl.pallas_call(
        functools.partial(_mlstm_kernel, batch, cast_blocks),
        out_shape=[jax.ShapeDtypeStruct((batch, seq, ML_PAD), BF16)]
                  + [jax.ShapeDtypeStruct(a.shape, BF16) for a in casts],
        grid=(nc,),
        in_specs=[row(ML_PAD), row(ML_PAD), row(ML_PAD), row(GATE_PAD)] + col(ML_PAD) + col(8)
                 + [_const_spec((1, HEAD_PAD))] + cast_specs,
        out_specs=[row(ML_PAD)] + cast_specs,
        scratch_shapes=[pltpu.VMEM((batch * ML_HEADS, HEAD_PAD, HEAD_PAD), F32),
                        pltpu.VMEM((batch * ML_HEADS, 8, LANE), F32)],
        compiler_params=_cparams(("arbitrary",)),
        name="mlstm",
    )(seq3(q), seq3(v), seq3(so), seq3(gates), *([kt] * batch), *([gatest] * batch), hgain, *casts)
    return outs[0].reshape(batch * seq, ML_PAD), outs[1:]


def _cast_spec(shape, step_of):
    e, r, c = shape
    n_blocks = e * CAST_PARTS

    def index_map(*grid_idx):
        s = jnp.minimum(step_of(*grid_idx), n_blocks - 1)
        return (s // CAST_PARTS, s % CAST_PARTS, 0)

    return pl.BlockSpec((1, r // CAST_PARTS, c), index_map)


def _cast_block(step, src_ref, dst_ref, n_blocks):
    @pl.when(step < n_blocks)
    def _():
        dst_ref[...] = src_ref[...].astype(dst_ref.dtype)


def _mixer_residual(main_ref, mq_ref, h_ref, k4_ref, v4_ref, gq_ref, wm_ref, wmem_ref):
    heads = range(MEM_HEADS)
    nt_dims = (((1,), (1,)), ((), ()))
    q = mq_ref[...].astype(F32)
    bd = _block_diag_ones(MEM_WIDTH, MEM_HEAD_DIM)
    ms = _split_dot(q * q, bd) * (1.0 / MEM_HEAD_DIM)
    qn = (q * lax.rsqrt(ms + EPS) * gq_ref[...] * (MEM_HEAD_DIM ** -0.5)).astype(BF16)
    ymem = jnp.zeros(q.shape, F32)
    for h in heads:
        s = lax.dot_general(qn, k4_ref[h], nt_dims, preferred_element_type=F32)
        e = jnp.exp(s - jnp.max(s, axis=-1, keepdims=True))
        inv = 1.0 / jnp.sum(e, axis=-1, keepdims=True)
        ymem = ymem + jnp.dot(e.astype(BF16), v4_ref[h], preferred_element_type=F32) * inv
    y = (jnp.dot(main_ref[...], wm_ref[...], preferred_element_type=F32)
         + jnp.dot(ymem.astype(BF16), wmem_ref[...], preferred_element_type=F32))
    return h_ref[...] + y


def _out_kernel(main_ref, mq_ref, h_ref, k4_ref, v4_ref, gq_ref, wm_ref, wmem_ref, out_ref):
    out_ref[...] = _mixer_residual(main_ref, mq_ref, h_ref, k4_ref, v4_ref, gq_ref, wm_ref, wmem_ref)


def _out_ffn_kernel(main_ref, mq_ref, h_ref, k4_ref, v4_ref, gq_ref, wm_ref, wmem_ref,
                    g_ref, wg_ref, wu_ref, wd_ref, cast_src_ref, out_ref, cast_dst_ref):
    _cast_block(pl.program_id(0), cast_src_ref, cast_dst_ref, N_EXPERTS * CAST_PARTS)
    x = _mixer_residual(main_ref, mq_ref, h_ref, k4_ref, v4_ref, gq_ref, wm_ref, wmem_ref)
    xn = _rms(x, g_ref[...]).astype(BF16)
    acc = x
    for c in range(D_FF // FF_CHUNK):
        cs = slice(c * FF_CHUNK, (c + 1) * FF_CHUNK)
        gt = jnp.dot(xn, wg_ref[:, cs], preferred_element_type=F32)
        up = jnp.dot(xn, wu_ref[:, cs], preferred_element_type=F32)
        a = (gt * jax.nn.sigmoid(gt) * up).astype(BF16)
        acc = acc + jnp.dot(a, wd_ref[cs, :], preferred_element_type=F32)
    out_ref[...] = acc


def _out_proj(main, mq, h2d, k4, v4, gq, wm, wmem, seq, ffn=None):
    t = h2d.shape[0]
    tm = TM_OUT if ffn is None else TM_FFN
    tiles_per_seq = seq // tm
    row = lambda n: pl.BlockSpec((tm, n), lambda i: (i, 0))
    memspec = pl.BlockSpec((MEM_HEADS, MEM_TOKENS, MEM_WIDTH), lambda i: (i // tiles_per_seq, 0, 0))
    in_specs = [row(main.shape[1]), row(MEM_WIDTH), row(D_MODEL), memspec, memspec,
                _const_spec((1, MEM_WIDTH)), _const_spec(wm.shape), _const_spec(wmem.shape)]
    if ffn is None:
        return pl.pallas_call(
            _out_kernel,
            out_shape=jax.ShapeDtypeStruct((t, D_MODEL), F32),
            grid=(t // tm,),
            in_specs=in_specs,
            out_specs=row(D_MODEL),
            compiler_params=_cparams(("parallel",)),
            name="out_proj",
        )(main, mq, h2d, k4, v4, gq, wm, wmem)
    ln, wg, wu, wd, cast_src = ffn
    assert t // tm >= cast_src.shape[0] * CAST_PARTS
    res = lambda shape: pl.BlockSpec(shape, lambda i: (0, 0), pipeline_mode=pl.Buffered(1))
    cast = _cast_spec(cast_src.shape, lambda i: i)
    return pl.pallas_call(
        _out_ffn_kernel,
        out_shape=(jax.ShapeDtypeStruct((t, D_MODEL), F32), jax.ShapeDtypeStruct(cast_src.shape, BF16)),
        grid=(t // tm,),
        in_specs=in_specs + [_const_spec((1, D_MODEL)), res(wg.shape), res(wu.shape), res(wd.shape), cast],
        out_specs=(row(D_MODEL), cast),
        compiler_params=_cparams(("arbitrary",)),
        name="out_proj_ffn",
    )(main, mq, h2d, k4, v4, gq, wm, wmem, ln, wg, wu, wd, cast_src)


def _in1_kernel(h_ref, pos_ref, g_ref, w1_ref, gcq_ref, gckv_ref, wuq_ref, wuk_ref, wuvt_ref,
                gq_ref, gkn_ref, gkr_ref, invf_ref,
                q_ref, k_ref, vt_ref, mq_ref):
    tm = h_ref.shape[0] // IN1_ROW_SPLIT
    half = MLA_ROPE // 2
    scale = MLA_QK ** -0.5 * LOG2E
    heads = range(MLA_HEADS)

    def rows_body(rows):
        xn = _rms(h_ref[rows, :], g_ref[...]).astype(BF16)
        yield
        z = jnp.dot(xn, w1_ref[...], preferred_element_type=F32)
        yield
        cq = z[:, :Q_LORA]
        ckv = z[:, Q_LORA:Q_LORA + KV_LORA]
        kr = z[:, Q_LORA + KV_LORA:Q_LORA + KV_LORA + LANE]
        mq_ref[rows, :] = z[:, Q_LORA + KV_LORA + LANE:].astype(BF16)
        cqn = _rms(cq, gcq_ref[...]).astype(BF16)
        ckvn = _rms(ckv, gckv_ref[...]).astype(BF16)
        yield
        q = jnp.dot(cqn, wuq_ref[...], preferred_element_type=F32)
        kn = jnp.dot(ckvn, wuk_ref[...], preferred_element_type=F32)
        vt_ref[:, rows] = lax.dot_general(wuvt_ref[...], ckvn, (((1,), (1,)), ((), ())),
                                          preferred_element_type=F32).astype(BF16)
        yield
        eye = lax.broadcasted_iota(jnp.int32, (tm, tm), 0) == lax.broadcasted_iota(jnp.int32, (tm, tm), 1)
        pos = jnp.sum(jnp.where(eye, pos_ref[:, rows].astype(F32), 0.0), axis=-1, keepdims=True)
        ang = pos * invf_ref[...]
        lane = lax.broadcasted_iota(jnp.int32, (tm, LANE), 1)
        cos = jnp.where(lane < MLA_ROPE, jnp.cos(ang), 0.0)
        sin = jnp.sin(ang)
        sin_hi = jnp.where((lane >= half) & (lane < MLA_ROPE), sin, 0.0)
        sin_lo = jnp.where(lane < half, -sin, 0.0)

        def rope(t):
            return t * cos + pltpu.roll(t, half, 1) * sin_hi + pltpu.roll(t, LANE - half, 1) * sin_lo

        yield
        qh = [q[:, h * HEAD_PAD:(h + 1) * HEAD_PAD] for h in heads]
        kh = [kn[:, h * MLA_NOPE:(h + 1) * MLA_NOPE] for h in heads]
        q_ss = [jnp.sum(qh[h] * qh[h], axis=-1, keepdims=True) for h in heads]
        k_ss = [jnp.sum(kh[h] * kh[h], axis=-1, keepdims=True) for h in heads]
        ss_r = jnp.sum(kr * kr, axis=-1, keepdims=True)
        yield
        krr = rope(kr * gkr_ref[...])
        q_rs = [lax.rsqrt(q_ss[h] * (1.0 / MLA_QK) + EPS) * scale for h in heads]
        k_rs = [lax.rsqrt((k_ss[h] + ss_r) * (1.0 / MLA_QK) + EPS) for h in heads]
        yield
        qn = [qh[h] * q_rs[h] * gq_ref[...] for h in heads]
        q_rot = [rope(qn[h][:, LANE:]) for h in heads]
        yield
        for h in heads:
            q_ref[rows, h * HEAD_PAD:h * HEAD_PAD + LANE] = qn[h][:, :LANE].astype(BF16)
            q_ref[rows, h * HEAD_PAD + LANE:(h + 1) * HEAD_PAD] = q_rot[h].astype(BF16)
        yield
        for h in heads:
            k_ref[rows, h * HEAD_PAD:h * HEAD_PAD + LANE] = (kh[h] * k_rs[h] * gkn_ref[...]).astype(BF16)
            k_ref[rows, h * HEAD_PAD + LANE:(h + 1) * HEAD_PAD] = (krr * k_rs[h]).astype(BF16)

    _interleave(rows_body(slice(r * tm, (r + 1) * tm)) for r in range(IN1_ROW_SPLIT))


def _in1(h2d, pos2d, ln, w1, gcq, gckv, wuq, wuk, wuvt, gq, gkn, gkr, invf):
    t = h2d.shape[0]
    tm = TM_IN1
    tok = lambda n: jax.ShapeDtypeStruct((t, n), BF16)
    row = lambda n: pl.BlockSpec((tm, n), lambda i: (i, 0))
    consts = [ln, w1, gcq, gckv, wuq, wuk, wuvt, gq, gkn, gkr, invf]
    return pl.pallas_call(
        _in1_kernel,
        out_shape=(tok(MLA_PAD), tok(MLA_PAD), jax.ShapeDtypeStruct((MAIN_WIDTH, t), BF16), tok(MEM_WIDTH)),
        grid=(t // tm,),
        in_specs=[row(D_MODEL), pl.BlockSpec((1, tm), lambda i: (0, i))] + [_const_spec(a.shape) for a in consts],
        out_specs=(row(MLA_PAD), row(MLA_PAD), pl.BlockSpec((MAIN_WIDTH, tm), lambda i: (0, i)), row(MEM_WIDTH)),
        compiler_params=_cparams(("parallel",)),
        name="in_proj1",
    )(h2d, pos2d, *consts)


def _attn_kernel(q_ref, k_ref, vt_ref, cast_a_ref, cast_b_ref, o_ref, cast_a_out_ref, cast_b_out_ref,
                 m_ref, l_ref, acc_ref, st_ref, mt_ref):
    qi = pl.program_id(2)
    step = (pl.program_id(0) * pl.num_programs(1) + pl.program_id(1)) * pl.num_programs(2) + qi
    _cast_block(step, cast_a_ref, cast_a_out_ref, N_EXPERTS * CAST_PARTS)
    _cast_block(step, cast_b_ref, cast_b_out_ref, N_EXPERTS * CAST_PARTS)
    m_ref[...] = jnp.full_like(m_ref, NEG)
    l_ref[...] = jnp.zeros_like(l_ref)
    acc_ref[...] = jnp.zeros_like(acc_ref)
    n_diag = TQ // TK
    n_full = qi * n_diag
    nt_dims = (((1,), (1,)), ((), ()))

    def scores(j, slot):
        koff = pl.multiple_of(j * TK, TK)
        st = lax.dot_general(k_ref[pl.ds(koff, TK), :], q_ref[...], nt_dims, preferred_element_type=F32)
        st_ref[slot] = st
        mt_ref[slot] = jnp.max(st, axis=0, keepdims=True)

    def absorb(st, m_tile, vt, cols):
        m_old = m_ref[:, cols]
        m_new = jnp.maximum(m_old, m_tile)
        alpha = jnp.exp2(m_old - m_new)
        e = jnp.exp2(st - m_new)
        l_ref[:, cols] = alpha * l_ref[:, cols] + jnp.sum(e, axis=0, keepdims=True)
        acc_ref[:, cols] = alpha * acc_ref[:, cols] + jnp.dot(vt, e.astype(BF16), preferred_element_type=F32)
        m_ref[:, cols] = m_new

    def finish(j, slot):
        koff = pl.multiple_of(j * TK, TK)
        absorb(st_ref[slot], mt_ref[slot], vt_ref[:, pl.ds(koff, TK)], slice(None))

    def pair(j):
        finish(j, 0)
        scores(j + 2, 0)
        finish(j + 1, 1)
        scores(j + 3, 1)

    @pl.when(n_full > 0)
    def _():
        scores(0, 0)
        scores(1, 1)

        def body(p, carry):
            pair(4 * p)
            pair(4 * p + 2)
            return carry

        lax.fori_loop(0, (n_full - 2) // 4, body, 0)
        pair(n_full - 4)
        finish(n_full - 2, 0)
        finish(n_full - 1, 1)

    kd0 = pl.multiple_of(n_full * TK, TK)

    def diagonal_block(u):
        nk = (u + 1) * TK
        cols = slice(u * TK, (u + 1) * TK)
        st = lax.dot_general(k_ref[pl.ds(kd0, nk), :], q_ref[cols, :], nt_dims, preferred_element_type=F32)
        yield
        kpos = lax.broadcasted_iota(jnp.int32, st.shape, 0)
        qpos = u * TK + lax.broadcasted_iota(jnp.int32, st.shape, 1)
        st = jnp.where(kpos <= qpos, st, NEG)
        m_tile = jnp.max(st, axis=0, keepdims=True)
        yield
        absorb(st, m_tile, vt_ref[:, pl.ds(kd0, nk)], cols)

    for u in range(n_diag // 2):
        _interleave(diagonal_block(v) for v in (u, n_diag - 1 - u))
    o_ref[...] = (acc_ref[...] * (1.0 / l_ref[...])).T.astype(o_ref.dtype)


def _attention(q, k, v, cast_a, cast_b, batch, seq):
    t = q.shape[0]
    nq = seq // TQ
    assert batch * MLA_HEADS * nq >= cast_a.shape[0] * CAST_PARTS
    assert (TQ // TK) % 4 == 0
    step_of = lambda b, h, i: (b * MLA_HEADS + h) * nq + i
    cast_specs = [_cast_spec(cast_a.shape, step_of), _cast_spec(cast_b.shape, step_of)]
    return pl.pallas_call(
        _attn_kernel,
        out_shape=(jax.ShapeDtypeStruct((t, MAIN_WIDTH), BF16), jax.ShapeDtypeStruct(cast_a.shape, BF16),
                   jax.ShapeDtypeStruct(cast_b.shape, BF16)),
        grid=(batch, MLA_HEADS, nq),
        in_specs=[pl.BlockSpec((TQ, HEAD_PAD), lambda b, h, i: (b * nq + i, h)),
                  pl.BlockSpec((seq, HEAD_PAD), lambda b, h, i: (b, h)),
                  pl.BlockSpec((MLA_V, seq), lambda b, h, i: (h, b))] + cast_specs,
        out_specs=[pl.BlockSpec((TQ, MLA_V), lambda b, h, i: (b * nq + i, h))] + cast_specs,
        scratch_shapes=[pltpu.VMEM((1, TQ), F32), pltpu.VMEM((1, TQ), F32), pltpu.VMEM((MLA_V, TQ), F32),
                        pltpu.VMEM((2, TK, TQ), F32), pltpu.VMEM((2, 1, TQ), F32)],
        compiler_params=_cparams(("arbitrary", "arbitrary", "arbitrary")),
        name="mla_attention",
    )(q, k, v, cast_a, cast_b)


def _router_kernel(h_ref, g_ref, whi_ref, wlo_ref,
                   xn_ref, gatest_ref, rank_ref, rankt_ref, starts_ref, counts_ref, run_ref):
    @pl.when(pl.program_id(0) == 0)
    def _():
        run_ref[...] = jnp.zeros_like(run_ref)

    tm = h_ref.shape[0]
    xn = _rms(h_ref[...], g_ref[...])
    hi = xn.astype(BF16)
    lo = (xn - hi.astype(F32)).astype(BF16)
    xn_ref[...] = hi
    whi = whi_ref[...]
    logits = (jnp.dot(hi, whi, preferred_element_type=F32) + jnp.dot(lo, whi, preferred_element_type=F32)
              + jnp.dot(hi, wlo_ref[...], preferred_element_type=F32))
    lane = lax.broadcasted_iota(jnp.int32, logits.shape, 1).astype(F32)
    logits = jnp.where(lane < N_EXPERTS, logits, NEG)
    m1 = jnp.max(logits, axis=-1, keepdims=True)
    i1 = jnp.min(jnp.where(logits == m1, lane, float(LANE)), axis=-1, keepdims=True)
    rest = jnp.where(lane == i1, NEG, logits)
    m2 = jnp.max(rest, axis=-1, keepdims=True)
    i2 = jnp.min(jnp.where(rest == m2, lane, float(LANE)), axis=-1, keepdims=True)
    e2 = jnp.exp(m2 - m1)
    w1 = 1.0 / (1.0 + e2)
    w2 = e2 * w1
    gates = jnp.where(lane == i1, w1, jnp.where(lane == i2, w2, 0.0))
    gatest_ref[...] = gates.T[:N_EXPERTS, :]
    routed = (lane == i1) | (lane == i2)
    oh = jnp.where(routed, 1.0, 0.0)
    r = lax.broadcasted_iota(jnp.int32, (TM_PROJ, TM_PROJ), 0)
    c = lax.broadcasted_iota(jnp.int32, (TM_PROJ, TM_PROJ), 1)
    before = jnp.where(r > c, 1.0, 0.0).astype(BF16)
    run = run_ref[0:1, :]
    ranks = []
    for ch in range(tm // TM_PROJ):
        rows = slice(ch * TM_PROJ, (ch + 1) * TM_PROJ)
        starts_ref[8 * ch:8 * ch + 8, :] = jnp.broadcast_to(run, (8, LANE))
        prefix = jnp.dot(before, oh[rows].astype(BF16), preferred_element_type=F32)
        ranks.append(jnp.where(routed[rows], run + prefix, -1.0))
        run = run + jnp.sum(oh[rows], axis=0, keepdims=True)
    rank = jnp.concatenate(ranks, axis=0)
    rank_ref[...] = rank
    rankt_ref[...] = rank.T[:N_EXPERTS, :]
    total = jnp.broadcast_to(run, run_ref.shape)
    run_ref[...] = total
    counts_ref[...] = total


def _router(h2d, ln, whi, wlo):
    t = h2d.shape[0]
    tm = min(TM_ROUTER, t)
    nt = t // tm
    chunk_rows = 8 * (tm // TM_PROJ)
    row = lambda n: pl.BlockSpec((tm, n), lambda i: (i, 0))
    return pl.pallas_call(
        _router_kernel,
        out_shape=(jax.ShapeDtypeStruct((t, D_MODEL), BF16), jax.ShapeDtypeStruct((N_EXPERTS, t), F32),
                   jax.ShapeDtypeStruct((t, LANE), F32), jax.ShapeDtypeStruct((N_EXPERTS, t), F32),
                   jax.ShapeDtypeStruct((nt * chunk_rows, LANE), F32), jax.ShapeDtypeStruct((8, LANE), F32)),
        grid=(nt,),
        in_specs=[row(D_MODEL), _const_spec((1, D_MODEL)), _const_spec(whi.shape), _const_spec(wlo.shape)],
        out_specs=(row(D_MODEL), pl.BlockSpec((N_EXPERTS, tm), lambda i: (0, i)), row(LANE),
                   pl.BlockSpec((N_EXPERTS, tm), lambda i: (0, i)),
                   pl.BlockSpec((chunk_rows, LANE), lambda i: (i, 0)), _const_spec((8, LANE))),
        scratch_shapes=[pltpu.VMEM((8, LANE), F32)],
        compiler_params=_cparams(("arbitrary",)),
        name="moe_router",
    )(h2d, ln, whi, wlo)


def _dispatch_kernel(ge_ref, gr0_ref, gstart_ref, gn_ref, xn_ref, rankt_ref, gatest_ref, xs_ref, gs_ref,
                     acc_ref, gacc_ref):
    groups = [pl.program_id(0) * DISPATCH_GROUPS + u for u in range(DISPATCH_GROUPS)]
    eye = (lax.broadcasted_iota(jnp.int32, (SLOT_GROUP, SLOT_GROUP), 0)
           == lax.broadcasted_iota(jnp.int32, (SLOT_GROUP, SLOT_GROUP), 1))

    def gathered(g, start):
        start = pl.multiple_of(start, TM_PROJ)
        want = (gr0_ref[g] + lax.broadcasted_iota(jnp.int32, (SLOT_GROUP, DISPATCH_K), 0)).astype(F32)
        span = (pl.ds(ge_ref[g], 1), pl.ds(start, DISPATCH_K))
        hit = rankt_ref[span] == want
        rows = jnp.dot(jnp.where(hit, 1.0, 0.0).astype(BF16), xn_ref[pl.ds(start, DISPATCH_K), :],
                       preferred_element_type=F32)
        gate = jnp.sum(jnp.where(hit, gatest_ref[span], 0.0), axis=-1, keepdims=True)
        return rows, gate

    def as_row(col):
        return jnp.sum(jnp.where(eye, col, 0.0), axis=0, keepdims=True)

    single = gn_ref[groups[0]] <= 1
    for g in groups[1:]:
        single = single & (gn_ref[g] <= 1)

    @pl.when(single)
    def _():
        gate_rows = [None] * DISPATCH_GROUPS

        def group_body(u, g):
            rows, gate = gathered(g, gstart_ref[g])
            yield
            xs_ref[u * SLOT_GROUP:(u + 1) * SLOT_GROUP, :] = rows.astype(BF16)
            gate_rows[u] = as_row(gate)

        _interleave(group_body(u, g) for u, g in enumerate(groups))
        gs_ref[...] = jnp.broadcast_to(jnp.concatenate(gate_rows, axis=1), gs_ref.shape)

    @pl.when(jnp.logical_not(single))
    def _():
        gate_rows = []
        for u, g in enumerate(groups):
            acc_ref[...] = jnp.zeros_like(acc_ref)
            gacc_ref[...] = jnp.zeros_like(gacc_ref)

            def body(k, carry, g=g):
                rows, gate = gathered(g, gstart_ref[g] + k * DISPATCH_K)
                acc_ref[...] += rows
                gacc_ref[...] += gate
                return carry

            lax.fori_loop(0, gn_ref[g], body, 0)
            xs_ref[u * SLOT_GROUP:(u + 1) * SLOT_GROUP, :] = acc_ref[...].astype(BF16)
            gate_rows.append(as_row(gacc_ref[...]))
        gs_ref[...] = jnp.broadcast_to(jnp.concatenate(gate_rows, axis=1), gs_ref.shape)


def _dispatch(xn, rankt, gatest, g_expert, g_rank0, g_start, g_nspans):
    t = xn.shape[0]
    n_steps = g_expert.shape[0] // DISPATCH_GROUPS
    rows = DISPATCH_GROUPS * SLOT_GROUP
    assert rows == TM_MOE
    res = lambda shape: pl.BlockSpec(shape, lambda g, *_: (0, 0), pipeline_mode=pl.Buffered(1))
    grid_spec = pltpu.PrefetchScalarGridSpec(
        num_scalar_prefetch=4,
        grid=(n_steps,),
        in_specs=[res((t, D_MODEL)), res((N_EXPERTS, t)), res((N_EXPERTS, t))],
        out_specs=(pl.BlockSpec((rows, D_MODEL), lambda g, *_: (g, 0)),
                   pl.BlockSpec((8, rows), lambda g, *_: (g, 0))),
        scratch_shapes=[pltpu.VMEM((SLOT_GROUP, D_MODEL), F32), pltpu.VMEM((SLOT_GROUP, 1), F32)],
    )
    return pl.pallas_call(
        _dispatch_kernel,
        out_shape=(jax.ShapeDtypeStruct((n_steps * rows, D_MODEL), BF16),
                   jax.ShapeDtypeStruct((n_steps * 8, rows), F32)),
        grid_spec=grid_spec,
        compiler_params=_cparams(("arbitrary",)),
        name="moe_dispatch",
    )(g_expert, g_rank0, g_start, g_nspans, xn, rankt, gatest)


def _expert_kernel(te_ref, ta_ref, x_ref, gs_ref, wg_ref, wu_ref, wd_ref, out_ref):
    active = ta_ref[pl.program_id(0)] > 0

    @pl.when(active)
    def _():
        x = x_ref[...]
        acc = None
        for c in range(D_FF // FF_CHUNK):
            cs = slice(c * FF_CHUNK, (c + 1) * FF_CHUNK)
            gt = jnp.dot(x, wg_ref[0, :, cs], preferred_element_type=F32)
            up = jnp.dot(x, wu_ref[0, :, cs], preferred_element_type=F32)
            a = (gt * jax.nn.sigmoid(gt) * up).astype(BF16)
            y = jnp.dot(a, wd_ref[0, cs, :], preferred_element_type=F32)
            acc = y if acc is None else acc + y
        tm = x.shape[0]
        eye = lax.broadcasted_iota(jnp.int32, (tm, tm), 0) == lax.broadcasted_iota(jnp.int32, (tm, tm), 1)
        gate = jnp.sum(jnp.where(eye, gs_ref[0:1, :], 0.0), axis=-1, keepdims=True)
        out_ref[...] = (acc * gate).astype(out_ref.dtype)

    @pl.when(jnp.logical_not(active))
    def _():
        out_ref[...] = jnp.zeros_like(out_ref)


def _experts(x, gate_rows, tile_expert, tile_active, wg, wu, wd):
    n_tiles = tile_expert.shape[0]
    tm = TM_MOE
    wmap = lambda j, te, ta: (te[j], 0, 0)
    tile = lambda j, te, ta: (jnp.where(ta[j] > 0, j, 0), 0)
    grid_spec = pltpu.PrefetchScalarGridSpec(
        num_scalar_prefetch=2,
        grid=(n_tiles,),
        in_specs=[pl.BlockSpec((tm, D_MODEL), tile),
                  pl.BlockSpec((8, tm), tile),
                  pl.BlockSpec((1, D_MODEL, D_FF), wmap),
                  pl.BlockSpec((1, D_MODEL, D_FF), wmap),
                  pl.BlockSpec((1, D_FF, D_MODEL), wmap)],
        out_specs=pl.BlockSpec((tm, D_MODEL), lambda j, te, ta: (j, 0)),
    )
    return pl.pallas_call(
        _expert_kernel,
        out_shape=jax.ShapeDtypeStruct((n_tiles * tm, D_MODEL), BF16),
        grid_spec=grid_spec,
        compiler_params=_cparams(("arbitrary",), EXPERT_VMEM_LIMIT),
        name="moe_experts",
    )(tile_expert, tile_active, x, gate_rows, wg, wu, wd)


def _combine_window_copies(ys_ref, buf_ref, sem_ref, ws_ref, tile, buf_slot, row_offset):
    copies = []
    for e in range(N_EXPERTS):
        start = pl.multiple_of(ws_ref[tile * N_EXPERTS + e] + row_offset, SLOT_ALIGN)
        copies.append(pltpu.make_async_copy(ys_ref.at[pl.ds(start, COMBINE_WIN), :],
                                            buf_ref.at[buf_slot, pl.ds(e * COMBINE_WIN, COMBINE_WIN), :],
                                            sem_ref.at[buf_slot, e]))
    return copies


def _combine_kernel(ws_ref, second_ref, h_ref, rank_ref, base_ref, ys_ref, out_ref, buf_ref, sem_ref):
    i = pl.program_id(0)
    n = pl.num_programs(0)
    tm = h_ref.shape[0]

    @pl.when(i == 0)
    def _():
        for cp in _combine_window_copies(ys_ref, buf_ref, sem_ref, ws_ref, 0, 0, 0):
            cp.start()

    @pl.when(i + 1 < n)
    def _():
        for cp in _combine_window_copies(ys_ref, buf_ref, sem_ref, ws_ref, i + 1, (i + 1) % 2, 0):
            cp.start()

    rank = rank_ref[...]
    slots = jnp.where(rank >= 0.0, rank + base_ref[...], -1.0)
    lane_iota = lax.broadcasted_iota(jnp.int32, (tm, COMBINE_WIN), 1)

    def gathered(buf_slot, row_offset):
        parts = []
        for e in range(N_EXPERTS):
            first = ws_ref[i * N_EXPERTS + e] + row_offset
            hit = slots[:, e:e + 1] == (first + lane_iota).astype(F32)
            parts.append(jnp.where(hit, 1.0, 0.0).astype(BF16))
        return jnp.dot(jnp.concatenate(parts, axis=1), buf_ref[buf_slot], preferred_element_type=F32)

    for cp in _combine_window_copies(ys_ref, buf_ref, sem_ref, ws_ref, i, i % 2, 0):
        cp.wait()
    out_ref[...] = h_ref[...] + gathered(i % 2, 0)

    for extra in range(1, COMBINE_MAX_WINDOWS):
        @pl.when(second_ref[i] > extra)
        def _(extra=extra):
            copies = _combine_window_copies(ys_ref, buf_ref, sem_ref, ws_ref, i, 2, extra * COMBINE_WIN)
            for cp in copies:
                cp.start()
            for cp in copies:
                cp.wait()
            out_ref[...] += gathered(2, extra * COMBINE_WIN)


def _combine(h2d, rank, base_row, ys, win_start, tile_second):
    t = h2d.shape[0]
    tm = TM_PROJ
    row = lambda n: pl.BlockSpec((tm, n), lambda i, *_: (i, 0))
    grid_spec = pltpu.PrefetchScalarGridSpec(
        num_scalar_prefetch=2,
        grid=(t // tm,),
        in_specs=[row(D_MODEL), row(LANE), pl.BlockSpec((1, LANE), lambda i, *_: (0, 0)),
                  pl.BlockSpec(memory_space=pl.ANY)],
        out_specs=row(D_MODEL),
        scratch_shapes=[pltpu.VMEM((3, N_EXPERTS * COMBINE_WIN, D_MODEL), BF16),
                        pltpu.SemaphoreType.DMA((3, N_EXPERTS))],
    )
    return pl.pallas_call(
        _combine_kernel,
        out_shape=jax.ShapeDtypeStruct((t, D_MODEL), F32),
        grid_spec=grid_spec,
        compiler_params=_cparams(("arbitrary",)),
        name="moe_combine",
    )(win_start, tile_second, h2d, rank, base_row, ys)


def _moe_plan(counts, starts, t):
    n_tok_tiles = t // TM_PROJ
    n_tiles = 2 * t // TM_MOE + N_EXPERTS + 1
    n_slots = n_tiles * TM_MOE
    counts = counts.astype(jnp.int32)
    cum = jnp.concatenate([starts.astype(jnp.int32), counts[None, :]], axis=0)
    padded = (counts + TM_MOE - 1) // TM_MOE * TM_MOE
    ends = jnp.cumsum(padded)
    base = ends - padded
    total = ends[-1]
    tile_first = jnp.arange(n_tiles, dtype=jnp.int32) * TM_MOE
    tile_expert = jnp.minimum(jnp.sum(ends[None, :] <= tile_first[:, None], axis=1), N_EXPERTS - 1).astype(jnp.int32)
    tile_active = (tile_first < total).astype(jnp.int32)
    n_groups = n_slots // SLOT_GROUP
    g_first = jnp.arange(n_groups, dtype=jnp.int32) * SLOT_GROUP
    g_expert = tile_expert[g_first // TM_MOE]
    g_r0 = g_first - base[g_expert]
    cum_e = cum.T[g_expert]
    lo = jnp.sum(cum_e <= g_r0[:, None], axis=1) - 1
    hi = jnp.sum(cum_e < (g_r0 + SLOT_GROUP)[:, None], axis=1)
    g_valid = (g_first < total) & (g_r0 < counts[g_expert])
    lo = jnp.clip(lo, 0, n_tok_tiles - 1)
    hi = jnp.clip(hi, 0, n_tok_tiles)
    g_n = jnp.where(g_valid, (jnp.maximum(hi - lo, 0) * TM_PROJ + DISPATCH_K - 1) // DISPATCH_K, 0)
    lo = jnp.minimum(lo * TM_PROJ, t - jnp.maximum(g_n, 1) * DISPATCH_K)
    first = base[None, :] + cum[:-1]
    cnt = cum[1:] - cum[:-1]
    ws = first // SLOT_ALIGN * SLOT_ALIGN
    pieces = jnp.where(cnt > 0, (first + cnt - ws + COMBINE_WIN - 1) // COMBINE_WIN, 0)
    return dict(tile_expert=tile_expert, tile_active=tile_active, g_expert=g_expert, g_r0=g_r0.astype(jnp.int32),
                g_lo=lo.astype(jnp.int32), g_n=g_n.astype(jnp.int32), base=base,
                ws=ws.reshape(-1).astype(jnp.int32), second=jnp.max(pieces, axis=1).astype(jnp.int32))


def _pad_heads_cols(w, heads, dim, pad):
    lead = w.shape[:-1]
    w = w.reshape(lead + (heads, dim))
    w = jnp.pad(w, [(0, 0)] * len(lead) + [(0, 0), (0, pad - dim)])
    return w.reshape(lead + (heads * pad,))


def _row(v):
    return v.reshape(1, -1).astype(F32)


def kernel(x, mem, positions, mem_norm, w_mem_kv, ln_mix0, w_in0, conv_w0, b_i0, b_f0, h_norm0, mq_norm0, mk_norm0, w_out0, ln_ffn0, w_gate0, w_up0, w_down0, ln_mix1, w_in1, cq_norm1, ckv_norm1, w_uq1, w_ukv1, q_norm1, k_norm1, mq_norm1, mk_norm1, w_out1, ln_ffn1, w_router1, we_gate1, we_up1, we_down1):
    batch, seq, _ = x.shape
    t = batch * seq
    h = x.reshape(t, D_MODEL)
    tile4 = lambda g: _row(jnp.tile(g, MEM_HEADS))

    k4_0, k4_1, v4 = _memkv(mem.reshape(batch * MEM_TOKENS, D_MODEL), _row(mem_norm), w_mem_kv.astype(BF16),
                            tile4(mk_norm0), tile4(mk_norm1), batch)

    mw = MAIN_WIDTH
    w_q, w_k, w_v, w_o = (w_in0[:, i * mw:(i + 1) * mw] for i in range(4))
    w_gates = w_in0[:, 4 * mw:4 * mw + 2 * ML_HEADS]
    w_mq = w_in0[:, 4 * mw + 2 * ML_HEADS:]
    padh = lambda w: _pad_heads_cols(w, ML_HEADS, ML_HEAD_DIM, HEAD_PAD)
    wqk = jnp.concatenate([padh(w_q), padh(w_k)], axis=1).astype(BF16)
    wvo = jnp.concatenate([padh(w_v), padh(w_o)], axis=1).astype(BF16)
    wg = jnp.pad(w_gates, ((0, 0), (0, GATE_PAD - 2 * ML_HEADS))).astype(BF16)
    wgt = w_gates.T.astype(BF16)
    cw = jnp.concatenate([padh(conv_w0[:, :mw]), padh(conv_w0[:, mw:])], axis=1).astype(F32)
    bias = jnp.concatenate([b_i0, b_f0]).astype(F32)
    bias_row = jnp.pad(bias, (0, GATE_PAD - 2 * ML_HEADS)).reshape(1, GATE_PAD)
    bias_col = bias.reshape(2 * ML_HEADS, 1)
    q0, kt0, v0, so, mq0, gates, gatest = _in0(h, _row(ln_mix0), wqk, wvo, w_mq.astype(BF16), wg, wgt, cw,
                                               bias_row, bias_col, seq)
    hgain = jnp.pad(h_norm0.astype(F32), (0, HEAD_PAD - ML_HEAD_DIM)).reshape(1, HEAD_PAD)
    ffn_groups = lambda w: w.reshape(N_EXPERTS, w.shape[0] // N_EXPERTS, w.shape[1])
    hn, (wg0, wu0, wd0) = _mlstm(q0, kt0, v0, so, gates, gatest, hgain,
                                 [ffn_groups(w_gate0), ffn_groups(w_up0), ffn_groups(w_down0)], batch, seq)
    wm0 = jnp.pad(w_out0[:mw].reshape(ML_HEADS, ML_HEAD_DIM, D_MODEL),
                  ((0, 0), (0, HEAD_PAD - ML_HEAD_DIM), (0, 0))).reshape(ML_PAD, D_MODEL).astype(BF16)
    h, we_gate_bf = _out_proj(hn, mq0, h, k4_0, v4, tile4(mq_norm0), wm0, w_out0[mw:].astype(BF16), seq,
                              ffn=(_row(ln_ffn0), wg0.reshape(w_gate0.shape), wu0.reshape(w_up0.shape),
                                   wd0.reshape(w_down0.shape), we_gate1))

    w_cq = w_in1[:, :Q_LORA]
    w_ckv = w_in1[:, Q_LORA:Q_LORA + KV_LORA]
    w_kr = jnp.pad(w_in1[:, Q_LORA + KV_LORA:Q_LORA + KV_LORA + MLA_ROPE], ((0, 0), (0, LANE - MLA_ROPE)))
    w_mq1 = w_in1[:, Q_LORA + KV_LORA + MLA_ROPE:]
    w1 = jnp.concatenate([w_cq, w_ckv, w_kr, w_mq1], axis=1).astype(BF16)
    wuq = _pad_heads_cols(w_uq1, MLA_HEADS, MLA_QK, HEAD_PAD).astype(BF16)
    wukv = w_ukv1.reshape(KV_LORA, MLA_HEADS, MLA_NOPE + MLA_V)
    wuk = wukv[:, :, :MLA_NOPE].reshape(KV_LORA, MLA_HEADS * MLA_NOPE).astype(BF16)
    wuvt = wukv[:, :, MLA_NOPE:].reshape(KV_LORA, MLA_HEADS * MLA_V).T.astype(BF16)
    gq = jnp.pad(q_norm1.astype(F32), (0, HEAD_PAD - MLA_QK)).reshape(1, HEAD_PAD)
    gkn = _row(k_norm1[:MLA_NOPE])
    gkr = jnp.pad(k_norm1[MLA_NOPE:].astype(F32), (0, LANE - MLA_ROPE)).reshape(1, LANE)
    half = MLA_ROPE // 2
    inv_freq = ROPE_THETA ** (-jnp.arange(half, dtype=F32) / half)
    invf = jnp.concatenate([inv_freq, inv_freq, jnp.zeros((LANE - MLA_ROPE,), F32)]).reshape(1, LANE)
    q1, k1, v1, mq1 = _in1(h, positions.reshape(1, t).astype(jnp.int32), _row(ln_mix1), w1, _row(cq_norm1),
                           _row(ckv_norm1), wuq, wuk, wuvt, gq, gkn, gkr, invf)
    o1, we_up_bf, we_down_bf = _attention(q1, k1, v1, we_up1, we_down1, batch, seq)
    h = _out_proj(o1, mq1, h, k4_1, v4, tile4(mq_norm1), w_out1[:mw].astype(BF16), w_out1[mw:].astype(BF16), seq)

    wr = jnp.pad(w_router1.astype(F32), ((0, 0), (0, LANE - N_EXPERTS)))
    wr_hi = wr.astype(BF16)
    wr_lo = (wr - wr_hi.astype(F32)).astype(BF16)
    xn, gatest, rank, rankt, starts, counts = _router(h, _row(ln_ffn1), wr_hi, wr_lo)
    plan = _moe_plan(counts[0, :N_EXPERTS], starts[::8, :N_EXPERTS], t)
    xs, gate_rows = _dispatch(xn, rankt, gatest, plan["g_expert"], plan["g_r0"], plan["g_lo"], plan["g_n"])
    ys = _experts(xs, gate_rows, plan["tile_expert"], plan["tile_active"], we_gate_bf, we_up_bf, we_down_bf)
    base_row = jnp.pad(plan["base"].astype(F32), (0, LANE - N_EXPERTS)).reshape(1, LANE)
    h = _combine(h, rank, base_row, ys, plan["ws"], plan["second"])
    return h.reshape(batch, seq, D_MODEL)
```

```python
import functools

import jax
import jax.numpy as jnp
import numpy as np
from jax import lax
from jax.experimental import pallas as pl
from jax.experimental.pallas import tpu as pltpu

F32 = jnp.float32
BF16 = jnp.bfloat16

D_MODEL = 1024
MEM_TOKENS = 256
MEM_HEADS = 4
MEM_WIDTH = 256
MEM_HEAD_DIM = 64
MAIN_WIDTH = 768
ML_HEADS = 4
ML_HEAD_DIM = 192
CONV_WIDTH = 4
MLA_HEADS = 6
MLA_NOPE = 128
MLA_ROPE = 64
MLA_QK = 192
MLA_V = 128
Q_LORA = 384
KV_LORA = 128
ROPE_THETA = 10000.0
D_FF = 3584
N_EXPERTS = 8
EPS = 1e-6

LANE = 128
HEAD_PAD = 256
ML_PAD = ML_HEADS * HEAD_PAD
MLA_PAD = MLA_HEADS * HEAD_PAD
GATE_PAD = 128
ML_CHUNK = 256
TM_PROJ = 256
TM_ROUTER = 2048
TM_IN0 = 1024
TM_IN1 = 1024
TM_OUT = 1024
TM_FFN = 512
FF_CHUNK = 512
TM_MOE = 512
SLOT_GROUP = 128
DISPATCH_K = 1024
DISPATCH_GROUPS = 4
CAST_PARTS = 4
IN0_ROW_SPLIT = 4
IN1_ROW_SPLIT = 4
SLOT_ALIGN = 16
COMBINE_WIN = 128
COMBINE_MAX_WINDOWS = -(-(TM_PROJ + SLOT_ALIGN - 1) // COMBINE_WIN)
TQ = 2048
TK = 512
NEG = -1e30
LOG2E = 1.4426950408889634
VMEM_LIMIT = 56 * 1024 * 1024
EXPERT_VMEM_LIMIT = 62 * 1024 * 1024


def _cparams(sem, vmem=VMEM_LIMIT):
    return pltpu.CompilerParams(dimension_semantics=sem, vmem_limit_bytes=vmem)


def _rms(x, g):
    return x * lax.rsqrt(jnp.mean(x * x, axis=-1, keepdims=True) + EPS) * g


def _split_dot(x, w_bf16):
    hi = x.astype(BF16)
    lo = (x - hi.astype(F32)).astype(BF16)
    return (jnp.dot(hi, w_bf16, preferred_element_type=F32)
            + jnp.dot(lo, w_bf16, preferred_element_type=F32))


def _block_diag_ones(n, blk):
    r = lax.broadcasted_iota(jnp.int32, (n, n), 0) // blk
    c = lax.broadcasted_iota(jnp.int32, (n, n), 1) // blk
    return jnp.where(r == c, 1.0, 0.0).astype(BF16)


def _const_spec(shape):
    nd = len(shape)
    return pl.BlockSpec(shape, lambda *_: (0,) * nd)


def _interleave(stage_generators):
    live = list(stage_generators)
    while live:
        for gen in list(live):
            try:
                next(gen)
            except StopIteration:
                live.remove(gen)


def _memkv_kernel(mem_ref, g_ref, w_ref, gk0_ref, gk1_ref, k0_ref, k1_ref, v_ref):
    xn = _rms(mem_ref[...], g_ref[...]).astype(BF16)
    kv = jnp.dot(xn, w_ref[...], preferred_element_type=F32)
    k = kv[:, :MEM_WIDTH]
    v = kv[:, MEM_WIDTH:]
    bd = _block_diag_ones(MEM_WIDTH, MEM_HEAD_DIM)
    ms = _split_dot(k * k, bd) * (1.0 / MEM_HEAD_DIM)
    kn = k * lax.rsqrt(ms + EPS)
    lane_head = lax.broadcasted_iota(jnp.int32, (MEM_TOKENS, MEM_WIDTH), 1) // MEM_HEAD_DIM
    for h in range(MEM_HEADS):
        sel = lane_head == h
        k0_ref[h] = jnp.where(sel, kn * gk0_ref[...], 0.0).astype(BF16)
        k1_ref[h] = jnp.where(sel, kn * gk1_ref[...], 0.0).astype(BF16)
        v_ref[h] = jnp.where(sel, v, 0.0).astype(BF16)


def _memkv(mem2d, mem_norm, w_kv, gk0, gk1, batch):
    out = jax.ShapeDtypeStruct((batch * MEM_HEADS, MEM_TOKENS, MEM_WIDTH), BF16)
    hspec = pl.BlockSpec((MEM_HEADS, MEM_TOKENS, MEM_WIDTH), lambda b: (b, 0, 0))
    return pl.pallas_call(
        _memkv_kernel,
        out_shape=(out, out, out),
        grid=(batch,),
        in_specs=[pl.BlockSpec((MEM_TOKENS, D_MODEL), lambda b: (b, 0)),
                  _const_spec((1, D_MODEL)), _const_spec((D_MODEL, 2 * MEM_WIDTH)),
                  _const_spec((1, MEM_WIDTH)), _const_spec((1, MEM_WIDTH))],
        out_specs=(hspec, hspec, hspec),
        compiler_params=_cparams(("parallel",)),
        name="memkv",
    )(mem2d, mem_norm, w_kv, gk0, gk1)


def _in0_kernel(tiles_per_seq, h_ref, hp_ref, g_ref, wqk_ref, wvo_ref, wmq_ref, wg_ref, wgt_ref,
                cw_ref, bias_ref, biast_ref,
                q_ref, kt_ref, v_ref, so_ref, mq_ref, gates_ref, gatest_ref):
    i = pl.program_id(0)
    tm = h_ref.shape[0] // IN0_ROW_SPLIT
    g = g_ref[...]
    first = (i % tiles_per_seq) == 0
    row8 = lax.broadcasted_iota(jnp.int32, (8, FF_CHUNK), 0)
    ncol = wqk_ref.shape[1]

    def gate_act(z, is_input_gate):
        logsig = jnp.minimum(z, 0.0) - jnp.log(1.0 + jnp.exp(-jnp.abs(z)))
        return jnp.where(is_input_gate, z, logsig)

    def rows_body(r):
        rows = slice(r * tm, (r + 1) * tm)
        xn = _rms(h_ref[rows, :], g).astype(BF16)
        prev_rows = hp_ref[...] if r == 0 else h_ref[r * tm - 8:r * tm, :]
        xpn = _rms(prev_rows, g).astype(BF16)
        yield
        for c in range(ncol // FF_CHUNK):
            cs = slice(c * FF_CHUNK, (c + 1) * FF_CHUNK)
            w = wqk_ref[:, cs]
            u = jnp.dot(xn, w, preferred_element_type=F32)
            up = jnp.dot(xpn, w, preferred_element_type=F32)
            if r == 0:
                up = jnp.where(first, 0.0, up)
            yield
            acc = u * cw_ref[CONV_WIDTH - 1:CONV_WIDTH, cs]
            for k in range(1, CONV_WIDTH):
                rolled = pltpu.roll(u, k, 0)
                prev = pltpu.roll(up, k, 0)
                head = jnp.where(row8 < k, prev, rolled[:8])
                shifted = jnp.concatenate([head, rolled[8:]], axis=0)
                acc = acc + shifted * cw_ref[CONV_WIDTH - 1 - k:CONV_WIDTH - k, cs]
            y = acc * jax.nn.sigmoid(acc)
            if c < ML_PAD // FF_CHUNK:
                q_ref[rows, cs] = (y * (ML_HEAD_DIM ** -0.5)).astype(BF16)
            else:
                ks = slice(c * FF_CHUNK - ML_PAD, (c + 1) * FF_CHUNK - ML_PAD)
                kt_ref[ks, rows] = y.T.astype(BF16)
            yield
        zvo = jnp.dot(xn, wvo_ref[...], preferred_element_type=F32)
        yield
        v_ref[rows, :] = zvo[:, :ML_PAD].astype(BF16)
        so_ref[rows, :] = jax.nn.sigmoid(zvo[:, ML_PAD:]).astype(BF16)
        yield
        mq_ref[rows, :] = jnp.dot(xn, wmq_ref[...], preferred_element_type=F32).astype(BF16)
        zg = jnp.dot(xn, wg_ref[...], preferred_element_type=F32) + bias_ref[...]
        lane = lax.broadcasted_iota(jnp.int32, (tm, GATE_PAD), 1)
        gates_ref[rows, :] = gate_act(zg, lane < ML_HEADS)
        zgt = lax.dot_general(wgt_ref[...], xn, (((1,), (1,)), ((), ())),
                              preferred_element_type=F32) + biast_ref[...]
        row = lax.broadcasted_iota(jnp.int32, (8, tm), 0)
        gatest_ref[:, rows] = gate_act(zgt, row < ML_HEADS)

    _interleave(rows_body(r) for r in range(IN0_ROW_SPLIT))


def _in0(h2d, ln, wqk, wvo, wmq, wg, wgt, cw, bias, biast, seq):
    t = h2d.shape[0]
    tm = TM_IN0
    tiles_per_seq = seq // tm
    tok = lambda n, dt: jax.ShapeDtypeStruct((t, n), dt)
    row = lambda n: pl.BlockSpec((tm, n), lambda i: (i, 0))
    col = lambda n: pl.BlockSpec((n, tm), lambda i: (0, i))
    return pl.pallas_call(
        functools.partial(_in0_kernel, tiles_per_seq),
        out_shape=(tok(ML_PAD, BF16), jax.ShapeDtypeStruct((ML_PAD, t), BF16), tok(ML_PAD, BF16),
                   tok(ML_PAD, BF16), tok(MEM_WIDTH, BF16), tok(GATE_PAD, F32), jax.ShapeDtypeStruct((8, t), F32)),
        grid=(t // tm,),
        in_specs=[row(D_MODEL),
                  pl.BlockSpec((8, D_MODEL), lambda i: (jnp.maximum(i * (tm // 8) - 1, 0), 0)),
                  _const_spec((1, D_MODEL)), _const_spec(wqk.shape), _const_spec(wvo.shape),
                  _const_spec(wmq.shape), _const_spec(wg.shape), _const_spec(wgt.shape),
                  _const_spec(cw.shape), _const_spec(bias.shape), _const_spec(biast.shape)],
        out_specs=(row(ML_PAD), col(ML_PAD), row(ML_PAD), row(ML_PAD), row(MEM_WIDTH), row(GATE_PAD), col(8)),
        compiler_params=_cparams(("parallel",)),
        name="in_proj0",
    )(h2d, h2d, ln, wqk, wvo, wmq, wg, wgt, cw, bias, biast)


def _mlstm_kernel(batch, cast_blocks, q_ref, v_ref, so_ref, gates_ref, *rest):
    n_cast = len(cast_blocks)
    kt_refs, gatest_refs = rest[:batch], rest[batch:2 * batch]
    hg_ref = rest[2 * batch]
    cast_src = rest[2 * batch + 1:2 * batch + 1 + n_cast]
    out_ref = rest[2 * batch + 1 + n_cast]
    cast_dst = rest[2 * batch + 2 + n_cast:2 * batch + 2 + 2 * n_cast]
    c_ref, m_ref = rest[2 * batch + 2 + 2 * n_cast:]
    L = q_ref.shape[1]
    for src, dst, n_blocks in zip(cast_src, cast_dst, cast_blocks):
        _cast_block(pl.program_id(0), src, dst, n_blocks)

    @pl.when(pl.program_id(0) == 0)
    def _():
        c_ref[...] = jnp.zeros_like(c_ref)
        m_ref[...] = jnp.zeros_like(m_ref)

    r = lax.broadcasted_iota(jnp.int32, (L, L), 0)
    c = lax.broadcasted_iota(jnp.int32, (L, L), 1)
    causal = r >= c
    tri_lo = jnp.where(causal, 1.0, 0.0).astype(BF16)
    tri_up = jnp.where(r <= c, 1.0, 0.0).astype(BF16)
    lane = lax.broadcasted_iota(jnp.int32, (L, HEAD_PAD), 1)
    hg = hg_ref[...]
    def sequence_body(b):
        gts = gates_ref[b]
        gtt = gatest_refs[b][...]
        hi = gts.astype(BF16)
        lo = (gts - hi.astype(F32)).astype(BF16)
        b_cols = (jnp.dot(tri_lo, hi, preferred_element_type=F32)
                  + jnp.dot(tri_lo, lo, preferred_element_type=F32))
        b_rows = _split_dot(gtt, tri_up)
        yield
        heads = range(ML_HEADS)
        hs = [slice(h * HEAD_PAD, (h + 1) * HEAD_PAD) for h in heads]
        state = [b * ML_HEADS + h for h in heads]
        m_h = [m_ref[state[h]][0:1, 0:1] for h in heads]
        b_c = [b_cols[:, ML_HEADS + h:ML_HEADS + h + 1] for h in heads]
        c_r = [b_rows[ML_HEADS + h:ML_HEADS + h + 1, :] - gtt[h:h + 1, :] for h in heads]
        q_h = [q_ref[b, :, hs[h]] for h in heads]
        kt_h = [kt_refs[b][hs[h], :] for h in heads]
        s = [jnp.dot(q_h[h], kt_h[h], preferred_element_type=F32) for h in heads]
        c_old = [c_ref[state[h]] for h in heads]
        qc = [jnp.dot(q_h[h], c_old[h].astype(BF16), preferred_element_type=F32) for h in heads]
        yield
        log_d = [jnp.where(causal, b_c[h] - c_r[h], NEG) for h in heads]
        inter = [b_c[h] + m_h[h] for h in heads]
        m_row = [jnp.maximum(inter[h], jnp.max(log_d[h], axis=-1, keepdims=True)) for h in heads]
        yield
        d = [jnp.exp(log_d[h] - m_row[h]) for h in heads]
        s_inter = [jnp.exp(inter[h] - m_row[h]) for h in heads]
        v_aug = [jnp.where(lane == ML_HEAD_DIM, 1.0, v_ref[b, :, hs[h]].astype(F32)).astype(BF16) for h in heads]
        qkd = [(s[h] * d[h]).astype(BF16) for h in heads]
        yield
        num = [jnp.dot(qkd[h], v_aug[h], preferred_element_type=F32) + s_inter[h] * qc[h] for h in heads]
        yield
        b_last = [b_c[h][L - 1:L, :] for h in heads]
        g_r = [b_last[h] - c_r[h] for h in heads]
        m_new = [jnp.maximum(b_last[h] + m_h[h], jnp.max(g_r[h], axis=-1, keepdims=True)) for h in heads]
        w_r = [jnp.exp(g_r[h] - m_new[h]) for h in heads]
        decay = [jnp.exp(b_last[h] + m_h[h] - m_new[h]) for h in heads]
        ktw = [(kt_h[h].astype(F32) * w_r[h]).astype(BF16) for h in heads]
        yield
        for h in heads:
            c_ref[state[h]] = decay[h] * c_old[h] + jnp.dot(ktw[h], v_aug[h], preferred_element_type=F32)
            m_ref[state[h]] = jnp.broadcast_to(m_new[h], m_ref.shape[1:])
        yield
        den = [jnp.sum(jnp.where(lane == ML_HEAD_DIM, num[h], 0.0), axis=-1, keepdims=True) for h in heads]
        ht = [num[h] * (1.0 / jnp.maximum(jnp.abs(den[h]), jnp.exp(-m_row[h]))) for h in heads]
        ht = [jnp.where(lane < ML_HEAD_DIM, ht[h], 0.0) for h in heads]
        ms = [jnp.sum(ht[h] * ht[h], axis=-1, keepdims=True) * (1.0 / ML_HEAD_DIM) for h in heads]
        yield
        for h in heads:
            hn = ht[h] * lax.rsqrt(ms[h] + EPS) * hg * so_ref[b, :, hs[h]].astype(F32)
            out_ref[b, :, hs[h]] = hn.astype(BF16)

    _interleave(sequence_body(b) for b in range(batch))


def _mlstm(q, kt, v, so, gates, gatest, hgain, casts, batch, seq):
    L = ML_CHUNK
    nc = seq // L
    seq3 = lambda a: a.reshape(batch, seq, a.shape[-1])
    row = lambda n: pl.BlockSpec((batch, L, n), lambda c: (0, c, 0))
    col = lambda rows: [pl.BlockSpec((rows, L), functools.partial(lambda b, c: (0, b * nc + c), b))
                        for b in range(batch)]
    cast_blocks = tuple(a.shape[0] * CAST_PARTS for a in casts)
    assert all(n <= nc for n in cast_blocks)
    cast_specs = [_cast_spec(a.shape, lambda c: c) for a in casts]
    outs = pl.pallas_call(
        functools.partial(_mlstm_kernel, batch, cast_blocks),
        out_shape=[jax.ShapeDtypeStruct((batch, seq, ML_PAD), BF16)]
                  + [jax.ShapeDtypeStruct(a.shape, BF16) for a in casts],
        grid=(nc,),
        in_specs=[row(ML_PAD), row(ML_PAD), row(ML_PAD), row(GATE_PAD)] + col(ML_PAD) + col(8)
                 + [_const_spec((1, HEAD_PAD))] + cast_specs,
        out_specs=[row(ML_PAD)] + cast_specs,
        scratch_shapes=[pltpu.VMEM((batch * ML_HEADS, HEAD_PAD, HEAD_PAD), F32),
                        pltpu.VMEM((batch * ML_HEADS, 8, LANE), F32)],
        compiler_params=_cparams(("arbitrary",)),
        name="mlstm",
    )(seq3(q), seq3(v), seq3(so), seq3(gates), *([kt] * batch), *([gatest] * batch), hgain, *casts)
    return outs[0].reshape(batch * seq, ML_PAD), outs[1:]


def _cast_spec(shape, step_of):
    e, r, c = shape
    n_blocks = e * CAST_PARTS

    def index_map(*grid_idx):
        s = jnp.minimum(step_of(*grid_idx), n_blocks - 1)
        return (s // CAST_PARTS, s % CAST_PARTS, 0)

    return pl.BlockSpec((1, r // CAST_PARTS, c), index_map)


def _cast_block(step, src_ref, dst_ref, n_blocks):
    @pl.when(step < n_blocks)
    def _():
        dst_ref[...] = src_ref[...].astype(dst_ref.dtype)


def _mixer_residual(main_ref, mq_ref, h_ref, k4_ref, v4_ref, gq_ref, wm_ref, wmem_ref):
    heads = range(MEM_HEADS)
    nt_dims = (((1,), (1,)), ((), ()))
    q = mq_ref[...].astype(F32)
    bd = _block_diag_ones(MEM_WIDTH, MEM_HEAD_DIM)
    ms = _split_dot(q * q, bd) * (1.0 / MEM_HEAD_DIM)
    qn = (q * lax.rsqrt(ms + EPS) * gq_ref[...] * (MEM_HEAD_DIM ** -0.5)).astype(BF16)
    ymem = jnp.zeros(q.shape, F32)
    for h in heads:
        s = lax.dot_general(qn, k4_ref[h], nt_dims, preferred_element_type=F32)
        e = jnp.exp(s - jnp.max(s, axis=-1, keepdims=True))
        inv = 1.0 / jnp.sum(e, axis=-1, keepdims=True)
        ymem = ymem + jnp.dot(e.astype(BF16), v4_ref[h], preferred_element_type=F32) * inv
    y = (jnp.dot(main_ref[...], wm_ref[...], preferred_element_type=F32)
         + jnp.dot(ymem.astype(BF16), wmem_ref[...], preferred_element_type=F32))
    return h_ref[...] + y


def _out_kernel(main_ref, mq_ref, h_ref, k4_ref, v4_ref, gq_ref, wm_ref, wmem_ref, out_ref):
    out_ref[...] = _mixer_residual(main_ref, mq_ref, h_ref, k4_ref, v4_ref, gq_ref, wm_ref, wmem_ref)


def _out_ffn_kernel(main_ref, mq_ref, h_ref, k4_ref, v4_ref, gq_ref, wm_ref, wmem_ref,
                    g_ref, wg_ref, wu_ref, wd_ref, cast_src_ref, out_ref, cast_dst_ref):
    _cast_block(pl.program_id(0), cast_src_ref, cast_dst_ref, N_EXPERTS * CAST_PARTS)
    x = _mixer_residual(main_ref, mq_ref, h_ref, k4_ref, v4_ref, gq_ref, wm_ref, wmem_ref)
    xn = _rms(x, g_ref[...]).astype(BF16)
    acc = x
    for c in range(D_FF // FF_CHUNK):
        cs = slice(c * FF_CHUNK, (c + 1) * FF_CHUNK)
        gt = jnp.dot(xn, wg_ref[:, cs], preferred_element_type=F32)
        up = jnp.dot(xn, wu_ref[:, cs], preferred_element_type=F32)
        a = (gt * jax.nn.sigmoid(gt) * up).astype(BF16)
        acc = acc + jnp.dot(a, wd_ref[cs, :], preferred_element_type=F32)
    out_ref[...] = acc


def _out_proj(main, mq, h2d, k4, v4, gq, wm, wmem, seq, ffn=None):
    t = h2d.shape[0]
    tm = TM_OUT if ffn is None else TM_FFN
    tiles_per_seq = seq // tm
    row = lambda n: pl.BlockSpec((tm, n), lambda i: (i, 0))
    memspec = pl.BlockSpec((MEM_HEADS, MEM_TOKENS, MEM_WIDTH), lambda i: (i // tiles_per_seq, 0, 0))
    in_specs = [row(main.shape[1]), row(MEM_WIDTH), row(D_MODEL), memspec, memspec,
                _const_spec((1, MEM_WIDTH)), _const_spec(wm.shape), _const_spec(wmem.shape)]
    if ffn is None:
        return pl.pallas_call(
            _out_kernel,
            out_shape=jax.ShapeDtypeStruct((t, D_MODEL), F32),
            grid=(t // tm,),
            in_specs=in_specs,
            out_specs=row(D_MODEL),
            compiler_params=_cparams(("parallel",)),
            name="out_proj",
        )(main, mq, h2d, k4, v4, gq, wm, wmem)
    ln, wg, wu, wd, cast_src = ffn
    assert t // tm >= cast_src.shape[0] * CAST_PARTS
    res = lambda shape: pl.BlockSpec(shape, lambda i: (0, 0), pipeline_mode=pl.Buffered(1))
    cast = _cast_spec(cast_src.shape, lambda i: i)
    return pl.pallas_call(
        _out_ffn_kernel,
        out_shape=(jax.ShapeDtypeStruct((t, D_MODEL), F32), jax.ShapeDtypeStruct(cast_src.shape, BF16)),
        grid=(t // tm,),
        in_specs=in_specs + [_const_spec((1, D_MODEL)), res(wg.shape), res(wu.shape), res(wd.shape), cast],
        out_specs=(row(D_MODEL), cast),
        compiler_params=_cparams(("arbitrary",)),
        name="out_proj_ffn",
    )(main, mq, h2d, k4, v4, gq, wm, wmem, ln, wg, wu, wd, cast_src)


def _in1_kernel(h_ref, pos_ref, g_ref, w1_ref, gcq_ref, gckv_ref, wuq_ref, wuk_ref, wuvt_ref,
                gq_ref, gkn_ref, gkr_ref, invf_ref,
                q_ref, k_ref, vt_ref, mq_ref):
    tm = h_ref.shape[0] // IN1_ROW_SPLIT
    half = MLA_ROPE // 2
    scale = MLA_QK ** -0.5 * LOG2E
    heads = range(MLA_HEADS)

    def rows_body(rows):
        xn = _rms(h_ref[rows, :], g_ref[...]).astype(BF16)
        yield
        z = jnp.dot(xn, w1_ref[...], preferred_element_type=F32)
        yield
        cq = z[:, :Q_LORA]
        ckv = z[:, Q_LORA:Q_LORA + KV_LORA]
        kr = z[:, Q_LORA + KV_LORA:Q_LORA + KV_LORA + LANE]
        mq_ref[rows, :] = z[:, Q_LORA + KV_LORA + LANE:].astype(BF16)
        cqn = _rms(cq, gcq_ref[...]).astype(BF16)
        ckvn = _rms(ckv, gckv_ref[...]).astype(BF16)
        yield
        q = jnp.dot(cqn, wuq_ref[...], preferred_element_type=F32)
        kn = jnp.dot(ckvn, wuk_ref[...], preferred_element_type=F32)
        vt_ref[:, rows] = lax.dot_general(wuvt_ref[...], ckvn, (((1,), (1,)), ((), ())),
                                          preferred_element_type=F32).astype(BF16)
        yield
        eye = lax.broadcasted_iota(jnp.int32, (tm, tm), 0) == lax.broadcasted_iota(jnp.int32, (tm, tm), 1)
        pos = jnp.sum(jnp.where(eye, pos_ref[:, rows].astype(F32), 0.0), axis=-1, keepdims=True)
        ang = pos * invf_ref[...]
        lane = lax.broadcasted_iota(jnp.int32, (tm, LANE), 1)
        cos = jnp.where(lane < MLA_ROPE, jnp.cos(ang), 0.0)
        sin = jnp.sin(ang)
        sin_hi = jnp.where((lane >= half) & (lane < MLA_ROPE), sin, 0.0)
        sin_lo = jnp.where(lane < half, -sin, 0.0)

        def rope(t):
            return t * cos + pltpu.roll(t, half, 1) * sin_hi + pltpu.roll(t, LANE - half, 1) * sin_lo

        yield
        qh = [q[:, h * HEAD_PAD:(h + 1) * HEAD_PAD] for h in heads]
        kh = [kn[:, h * MLA_NOPE:(h + 1) * MLA_NOPE] for h in heads]
        q_ss = [jnp.sum(qh[h] * qh[h], axis=-1, keepdims=True) for h in heads]
        k_ss = [jnp.sum(kh[h] * kh[h], axis=-1, keepdims=True) for h in heads]
        ss_r = jnp.sum(kr * kr, axis=-1, keepdims=True)
        yield
        krr = rope(kr * gkr_ref[...])
        q_rs = [lax.rsqrt(q_ss[h] * (1.0 / MLA_QK) + EPS) * scale for h in heads]
        k_rs = [lax.rsqrt((k_ss[h] + ss_r) * (1.0 / MLA_QK) + EPS) for h in heads]
        yield
        qn = [qh[h] * q_rs[h] * gq_ref[...] for h in heads]
        q_rot = [rope(qn[h][:, LANE:]) for h in heads]
        yield
        for h in heads:
            q_ref[rows, h * HEAD_PAD:h * HEAD_PAD + LANE] = qn[h][:, :LANE].astype(BF16)
            q_ref[rows, h * HEAD_PAD + LANE:(h + 1) * HEAD_PAD] = q_rot[h].astype(BF16)
        yield
        for h in heads:
            k_ref[rows, h * HEAD_PAD:h * HEAD_PAD + LANE] = (kh[h] * k_rs[h] * gkn_ref[...]).astype(BF16)
            k_ref[rows, h * HEAD_PAD + LANE:(h + 1) * HEAD_PAD] = (krr * k_rs[h]).astype(BF16)

    _interleave(rows_body(slice(r * tm, (r + 1) * tm)) for r in range(IN1_ROW_SPLIT))


def _in1(h2d, pos2d, ln, w1, gcq, gckv, wuq, wuk, wuvt, gq, gkn, gkr, invf):
    t = h2d.shape[0]
    tm = TM_IN1
    tok = lambda n: jax.ShapeDtypeStruct((t, n), BF16)
    row = lambda n: pl.BlockSpec((tm, n), lambda i: (i, 0))
    consts = [ln, w1, gcq, gckv, wuq, wuk, wuvt, gq, gkn, gkr, invf]
    return pl.pallas_call(
        _in1_kernel,
        out_shape=(tok(MLA_PAD), tok(MLA_PAD), jax.ShapeDtypeStruct((MAIN_WIDTH, t), BF16), tok(MEM_WIDTH)),
        grid=(t // tm,),
        in_specs=[row(D_MODEL), pl.BlockSpec((1, tm), lambda i: (0, i))] + [_const_spec(a.shape) for a in consts],
        out_specs=(row(MLA_PAD), row(MLA_PAD), pl.BlockSpec((MAIN_WIDTH, tm), lambda i: (0, i)), row(MEM_WIDTH)),
        compiler_params=_cparams(("parallel",)),
        name="in_proj1",
    )(h2d, pos2d, *consts)


def _attn_kernel(q_ref, k_ref, vt_ref, cast_a_ref, cast_b_ref, o_ref, cast_a_out_ref, cast_b_out_ref,
                 m_ref, l_ref, acc_ref, st_ref, mt_ref):
    qi = pl.program_id(2)
    step = (pl.program_id(0) * pl.num_programs(1) + pl.program_id(1)) * pl.num_programs(2) + qi
    _cast_block(step, cast_a_ref, cast_a_out_ref, N_EXPERTS * CAST_PARTS)
    _cast_block(step, cast_b_ref, cast_b_out_ref, N_EXPERTS * CAST_PARTS)
    m_ref[...] = jnp.full_like(m_ref, NEG)
    l_ref[...] = jnp.zeros_like(l_ref)
    acc_ref[...] = jnp.zeros_like(acc_ref)
    n_diag = TQ // TK
    n_full = qi * n_diag
    nt_dims = (((1,), (1,)), ((), ()))

    def scores(j, slot):
        koff = pl.multiple_of(j * TK, TK)
        st = lax.dot_general(k_ref[pl.ds(koff, TK), :], q_ref[...], nt_dims, preferred_element_type=F32)
        st_ref[slot] = st
        mt_ref[slot] = jnp.max(st, axis=0, keepdims=True)

    def absorb(st, m_tile, vt, cols):
        m_old = m_ref[:, cols]
        m_new = jnp.maximum(m_old, m_tile)
        alpha = jnp.exp2(m_old - m_new)
        e = jnp.exp2(st - m_new)
        l_ref[:, cols] = alpha * l_ref[:, cols] + jnp.sum(e, axis=0, keepdims=True)
        acc_ref[:, cols] = alpha * acc_ref[:, cols] + jnp.dot(vt, e.astype(BF16), preferred_element_type=F32)
        m_ref[:, cols] = m_new

    def finish(j, slot):
        koff = pl.multiple_of(j * TK, TK)
        absorb(st_ref[slot], mt_ref[slot], vt_ref[:, pl.ds(koff, TK)], slice(None))

    def pair(j):
        finish(j, 0)
        scores(j + 2, 0)
        finish(j + 1, 1)
        scores(j + 3, 1)

    @pl.when(n_full > 0)
    def _():
        scores(0, 0)
        scores(1, 1)

        def body(p, carry):
            pair(4 * p)
            pair(4 * p + 2)
            return carry

        lax.fori_loop(0, (n_full - 2) // 4, body, 0)
        pair(n_full - 4)
        finish(n_full - 2, 0)
        finish(n_full - 1, 1)

    kd0 = pl.multiple_of(n_full * TK, TK)

    def diagonal_block(u):
        nk = (u + 1) * TK
        cols = slice(u * TK, (u + 1) * TK)
        st = lax.dot_general(k_ref[pl.ds(kd0, nk), :], q_ref[cols, :], nt_dims, preferred_element_type=F32)
        yield
        kpos = lax.broadcasted_iota(jnp.int32, st.shape, 0)
        qpos = u * TK + lax.broadcasted_iota(jnp.int32, st.shape, 1)
        st = jnp.where(kpos <= qpos, st, NEG)
        m_tile = jnp.max(st, axis=0, keepdims=True)
        yield
        absorb(st, m_tile, vt_ref[:, pl.ds(kd0, nk)], cols)

    for u in range(n_diag // 2):
        _interleave(diagonal_block(v) for v in (u, n_diag - 1 - u))
    o_ref[...] = (acc_ref[...] * (1.0 / l_ref[...])).T.astype(o_ref.dtype)


def _attention(q, k, v, cast_a, cast_b, batch, seq):
    t = q.shape[0]
    nq = seq // TQ
    assert batch * MLA_HEADS * nq >= cast_a.shape[0] * CAST_PARTS
    assert (TQ // TK) % 4 == 0
    step_of = lambda b, h, i: (b * MLA_HEADS + h) * nq + i
    cast_specs = [_cast_spec(cast_a.shape, step_of), _cast_spec(cast_b.shape, step_of)]
    return pl.pallas_call(
        _attn_kernel,
        out_shape=(jax.ShapeDtypeStruct((t, MAIN_WIDTH), BF16), jax.ShapeDtypeStruct(cast_a.shape, BF16),
                   jax.ShapeDtypeStruct(cast_b.shape, BF16)),
        grid=(batch, MLA_HEADS, nq),
        in_specs=[pl.BlockSpec((TQ, HEAD_PAD), lambda b, h, i: (b * nq + i, h)),
                  pl.BlockSpec((seq, HEAD_PAD), lambda b, h, i: (b, h)),
                  pl.BlockSpec((MLA_V, seq), lambda b, h, i: (h, b))] + cast_specs,
        out_specs=[pl.BlockSpec((TQ, MLA_V), lambda b, h, i: (b * nq + i, h))] + cast_specs,
        scratch_shapes=[pltpu.VMEM((1, TQ), F32), pltpu.VMEM((1, TQ), F32), pltpu.VMEM((MLA_V, TQ), F32),
                        pltpu.VMEM((2, TK, TQ), F32), pltpu.VMEM((2, 1, TQ), F32)],
        compiler_params=_cparams(("arbitrary", "arbitrary", "arbitrary")),
        name="mla_attention",
    )(q, k, v, cast_a, cast_b)


def _router_kernel(h_ref, g_ref, whi_ref, wlo_ref,
                   xn_ref, gatest_ref, rank_ref, rankt_ref, starts_ref, counts_ref, run_ref):
    @pl.when(pl.program_id(0) == 0)
    def _():
        run_ref[...] = jnp.zeros_like(run_ref)

    n_chunks = h_ref.shape[0] // TM_PROJ
    r = lax.broadcasted_iota(jnp.int32, (TM_PROJ, TM_PROJ), 0)
    c = lax.broadcasted_iota(jnp.int32, (TM_PROJ, TM_PROJ), 1)
    before = jnp.where(r > c, 1.0, 0.0).astype(BF16)
    lane = lax.broadcasted_iota(jnp.int32, (TM_PROJ, LANE), 1).astype(F32)
    routed, prefix, count = [None] * n_chunks, [None] * n_chunks, [None] * n_chunks

    def chunk_body(ch):
        rows = slice(ch * TM_PROJ, (ch + 1) * TM_PROJ)
        xn = _rms(h_ref[rows, :], g_ref[...])
        hi = xn.astype(BF16)
        lo = (xn - hi.astype(F32)).astype(BF16)
        xn_ref[rows, :] = hi
        yield
        whi = whi_ref[...]
        logits = (jnp.dot(hi, whi, preferred_element_type=F32) + jnp.dot(lo, whi, preferred_element_type=F32)
                  + jnp.dot(hi, wlo_ref[...], preferred_element_type=F32))
        logits = jnp.where(lane < N_EXPERTS, logits, NEG)
        yield
        m1 = jnp.max(logits, axis=-1, keepdims=True)
        yield
        i1 = jnp.min(jnp.where(logits == m1, lane, float(LANE)), axis=-1, keepdims=True)
        yield
        rest = jnp.where(lane == i1, NEG, logits)
        m2 = jnp.max(rest, axis=-1, keepdims=True)
        yield
        i2 = jnp.min(jnp.where(rest == m2, lane, float(LANE)), axis=-1, keepdims=True)
        yield
        e2 = jnp.exp(m2 - m1)
        w1 = 1.0 / (1.0 + e2)
        w2 = e2 * w1
        gates = jnp.where(lane == i1, w1, jnp.where(lane == i2, w2, 0.0))
        gatest_ref[:, rows] = gates.T[:N_EXPERTS, :]
        routed[ch] = (lane == i1) | (lane == i2)
        oh = jnp.where(routed[ch], 1.0, 0.0)
        yield
        prefix[ch] = jnp.dot(before, oh.astype(BF16), preferred_element_type=F32)
        count[ch] = jnp.sum(oh, axis=0, keepdims=True)

    _interleave(chunk_body(ch) for ch in range(n_chunks))
    run = run_ref[0:1, :]
    for ch in range(n_chunks):
        rows = slice(ch * TM_PROJ, (ch + 1) * TM_PROJ)
        starts_ref[8 * ch:8 * ch + 8, :] = jnp.broadcast_to(run, (8, LANE))
        rank = jnp.where(routed[ch], run + prefix[ch], -1.0)
        rank_ref[rows, :] = rank
        rankt_ref[:, rows] = rank.T[:N_EXPERTS, :]
        run = run + count[ch]
    total = jnp.broadcast_to(run, run_ref.shape)
    run_ref[...] = total
    counts_ref[...] = total


def _router(h2d, ln, whi, wlo):
    t = h2d.shape[0]
    tm = min(TM_ROUTER, t)
    nt = t // tm
    chunk_rows = 8 * (tm // TM_PROJ)
    row = lambda n: pl.BlockSpec((tm, n), lambda i: (i, 0))
    return pl.pallas_call(
        _router_kernel,
        out_shape=(jax.ShapeDtypeStruct((t, D_MODEL), BF16), jax.ShapeDtypeStruct((N_EXPERTS, t), F32),
                   jax.ShapeDtypeStruct((t, LANE), F32), jax.ShapeDtypeStruct((N_EXPERTS, t), F32),
                   jax.ShapeDtypeStruct((nt * chunk_rows, LANE), F32), jax.ShapeDtypeStruct((8, LANE), F32)),
        grid=(nt,),
        in_specs=[row(D_MODEL), _const_spec((1, D_MODEL)), _const_spec(whi.shape), _const_spec(wlo.shape)],
        out_specs=(row(D_MODEL), pl.BlockSpec((N_EXPERTS, tm), lambda i: (0, i)), row(LANE),
                   pl.BlockSpec((N_EXPERTS, tm), lambda i: (0, i)),
                   pl.BlockSpec((chunk_rows, LANE), lambda i: (i, 0)), _const_spec((8, LANE))),
        scratch_shapes=[pltpu.VMEM((8, LANE), F32)],
        compiler_params=_cparams(("arbitrary",)),
        name="moe_router",
    )(h2d, ln, whi, wlo)


def _dispatch_kernel(ge_ref, gr0_ref, gstart_ref, gn_ref, xn_ref, rankt_ref, gatest_ref, xs_ref, gs_ref,
                     acc_ref, gacc_ref):
    groups = [pl.program_id(0) * DISPATCH_GROUPS + u for u in range(DISPATCH_GROUPS)]
    eye = (lax.broadcasted_iota(jnp.int32, (SLOT_GROUP, SLOT_GROUP), 0)
           == lax.broadcasted_iota(jnp.int32, (SLOT_GROUP, SLOT_GROUP), 1))

    def gathered(g, start):
        start = pl.multiple_of(start, TM_PROJ)
        want = (gr0_ref[g] + lax.broadcasted_iota(jnp.int32, (SLOT_GROUP, DISPATCH_K), 0)).astype(F32)
        span = (pl.ds(ge_ref[g], 1), pl.ds(start, DISPATCH_K))
        hit = rankt_ref[span] == want
        rows = jnp.dot(jnp.where(hit, 1.0, 0.0).astype(BF16), xn_ref[pl.ds(start, DISPATCH_K), :],
                       preferred_element_type=F32)
        gate = jnp.sum(jnp.where(hit, gatest_ref[span], 0.0), axis=-1, keepdims=True)
        return rows, gate

    def as_row(col):
        return jnp.sum(jnp.where(eye, col, 0.0), axis=0, keepdims=True)

    single = gn_ref[groups[0]] <= 1
    for g in groups[1:]:
        single = single & (gn_ref[g] <= 1)

    @pl.when(single)
    def _():
        gate_rows = [None] * DISPATCH_GROUPS

        def group_body(u, g):
            rows, gate = gathered(g, gstart_ref[g])
            yield
            xs_ref[u * SLOT_GROUP:(u + 1) * SLOT_GROUP, :] = rows.astype(BF16)
            gate_rows[u] = as_row(gate)

        _interleave(group_body(u, g) for u, g in enumerate(groups))
        gs_ref[...] = jnp.broadcast_to(jnp.concatenate(gate_rows, axis=1), gs_ref.shape)

    @pl.when(jnp.logical_not(single))
    def _():
        gate_rows = []
        for u, g in enumerate(groups):
            acc_ref[...] = jnp.zeros_like(acc_ref)
            gacc_ref[...] = jnp.zeros_like(gacc_ref)

            def body(k, carry, g=g):
                rows, gate = gathered(g, gstart_ref[g] + k * DISPATCH_K)
                acc_ref[...] += rows
                gacc_ref[...] += gate
                return carry

            lax.fori_loop(0, gn_ref[g], body, 0)
            xs_ref[u * SLOT_GROUP:(u + 1) * SLOT_GROUP, :] = acc_ref[...].astype(BF16)
            gate_rows.append(as_row(gacc_ref[...]))
        gs_ref[...] = jnp.broadcast_to(jnp.concatenate(gate_rows, axis=1), gs_ref.shape)


def _dispatch(xn, rankt, gatest, g_expert, g_rank0, g_start, g_nspans):
    t = xn.shape[0]
    n_steps = g_expert.shape[0] // DISPATCH_GROUPS
    rows = DISPATCH_GROUPS * SLOT_GROUP
    assert rows == TM_MOE
    res = lambda shape: pl.BlockSpec(shape, lambda g, *_: (0, 0), pipeline_mode=pl.Buffered(1))
    grid_spec = pltpu.PrefetchScalarGridSpec(
        num_scalar_prefetch=4,
        grid=(n_steps,),
        in_specs=[res((t, D_MODEL)), res((N_EXPERTS, t)), res((N_EXPERTS, t))],
        out_specs=(pl.BlockSpec((rows, D_MODEL), lambda g, *_: (g, 0)),
                   pl.BlockSpec((8, rows), lambda g, *_: (g, 0))),
        scratch_shapes=[pltpu.VMEM((SLOT_GROUP, D_MODEL), F32), pltpu.VMEM((SLOT_GROUP, 1), F32)],
    )
    return pl.pallas_call(
        _dispatch_kernel,
        out_shape=(jax.ShapeDtypeStruct((n_steps * rows, D_MODEL), BF16),
                   jax.ShapeDtypeStruct((n_steps * 8, rows), F32)),
        grid_spec=grid_spec,
        compiler_params=_cparams(("arbitrary",)),
        name="moe_dispatch",
    )(g_expert, g_rank0, g_start, g_nspans, xn, rankt, gatest)


def _expert_kernel(te_ref, ta_ref, x_ref, gs_ref, wg_ref, wu_ref, wd_ref, out_ref):
    active = ta_ref[pl.program_id(0)] > 0

    @pl.when(active)
    def _():
        x = x_ref[...]
        acc = None
        for c in range(D_FF // FF_CHUNK):
            cs = slice(c * FF_CHUNK, (c + 1) * FF_CHUNK)
            gt = jnp.dot(x, wg_ref[0, :, cs], preferred_element_type=F32)
            up = jnp.dot(x, wu_ref[0, :, cs], preferred_element_type=F32)
            a = (gt * jax.nn.sigmoid(gt) * up).astype(BF16)
            y = jnp.dot(a, wd_ref[0, cs, :], preferred_element_type=F32)
            acc = y if acc is None else acc + y
        tm = x.shape[0]
        eye = lax.broadcasted_iota(jnp.int32, (tm, tm), 0) == lax.broadcasted_iota(jnp.int32, (tm, tm), 1)
        gate = jnp.sum(jnp.where(eye, gs_ref[0:1, :], 0.0), axis=-1, keepdims=True)
        out_ref[...] = (acc * gate).astype(out_ref.dtype)

    @pl.when(jnp.logical_not(active))
    def _():
        out_ref[...] = jnp.zeros_like(out_ref)


def _experts(x, gate_rows, tile_expert, tile_active, wg, wu, wd):
    n_tiles = tile_expert.shape[0]
    tm = TM_MOE
    wmap = lambda j, te, ta: (te[j], 0, 0)
    tile = lambda j, te, ta: (jnp.where(ta[j] > 0, j, 0), 0)
    grid_spec = pltpu.PrefetchScalarGridSpec(
        num_scalar_prefetch=2,
        grid=(n_tiles,),
        in_specs=[pl.BlockSpec((tm, D_MODEL), tile),
                  pl.BlockSpec((8, tm), tile),
                  pl.BlockSpec((1, D_MODEL, D_FF), wmap),
                  pl.BlockSpec((1, D_MODEL, D_FF), wmap),
                  pl.BlockSpec((1, D_FF, D_MODEL), wmap)],
        out_specs=pl.BlockSpec((tm, D_MODEL), lambda j, te, ta: (j, 0)),
    )
    return pl.pallas_call(
        _expert_kernel,
        out_shape=jax.ShapeDtypeStruct((n_tiles * tm, D_MODEL), BF16),
        grid_spec=grid_spec,
        compiler_params=_cparams(("arbitrary",), EXPERT_VMEM_LIMIT),
        name="moe_experts",
    )(tile_expert, tile_active, x, gate_rows, wg, wu, wd)


def _combine_window_copies(ys_ref, buf_ref, sem_ref, ws_ref, tile, buf_slot, row_offset):
    copies = []
    for e in range(N_EXPERTS):
        start = pl.multiple_of(ws_ref[tile * N_EXPERTS + e] + row_offset, SLOT_ALIGN)
        copies.append(pltpu.make_async_copy(ys_ref.at[pl.ds(start, COMBINE_WIN), :],
                                            buf_ref.at[buf_slot, pl.ds(e * COMBINE_WIN, COMBINE_WIN), :],
                                            sem_ref.at[buf_slot, e]))
    return copies


def _combine_kernel(ws_ref, second_ref, h_ref, rank_ref, base_ref, ys_ref, out_ref, buf_ref, sem_ref):
    i = pl.program_id(0)
    n = pl.num_programs(0)
    tm = h_ref.shape[0]

    @pl.when(i == 0)
    def _():
        for cp in _combine_window_copies(ys_ref, buf_ref, sem_ref, ws_ref, 0, 0, 0):
            cp.start()

    @pl.when(i + 1 < n)
    def _():
        for cp in _combine_window_copies(ys_ref, buf_ref, sem_ref, ws_ref, i + 1, (i + 1) % 2, 0):
            cp.start()

    rank = rank_ref[...]
    slots = jnp.where(rank >= 0.0, rank + base_ref[...], -1.0)
    lane_iota = lax.broadcasted_iota(jnp.int32, (tm, COMBINE_WIN), 1)

    def gathered(buf_slot, row_offset):
        parts = []
        for e in range(N_EXPERTS):
            first = ws_ref[i * N_EXPERTS + e] + row_offset
            hit = slots[:, e:e + 1] == (first + lane_iota).astype(F32)
            parts.append(jnp.where(hit, 1.0, 0.0).astype(BF16))
        return jnp.dot(jnp.concatenate(parts, axis=1), buf_ref[buf_slot], preferred_element_type=F32)

    for cp in _combine_window_copies(ys_ref, buf_ref, sem_ref, ws_ref, i, i % 2, 0):
        cp.wait()
    out_ref[...] = h_ref[...] + gathered(i % 2, 0)

    for extra in range(1, COMBINE_MAX_WINDOWS):
        @pl.when(second_ref[i] > extra)
        def _(extra=extra):
            copies = _combine_window_copies(ys_ref, buf_ref, sem_ref, ws_ref, i, 2, extra * COMBINE_WIN)
            for cp in copies:
                cp.start()
            for cp in copies:
                cp.wait()
            out_ref[...] += gathered(2, extra * COMBINE_WIN)


def _combine(h2d, rank, base_row, ys, win_start, tile_second):
    t = h2d.shape[0]
    tm = TM_PROJ
    row = lambda n: pl.BlockSpec((tm, n), lambda i, *_: (i, 0))
    grid_spec = pltpu.PrefetchScalarGridSpec(
        num_scalar_prefetch=2,
        grid=(t // tm,),
        in_specs=[row(D_MODEL), row(LANE), pl.BlockSpec((1, LANE), lambda i, *_: (0, 0)),
                  pl.BlockSpec(memory_space=pl.ANY)],
        out_specs=row(D_MODEL),
        scratch_shapes=[pltpu.VMEM((3, N_EXPERTS * COMBINE_WIN, D_MODEL), BF16),
                        pltpu.SemaphoreType.DMA((3, N_EXPERTS))],
    )
    return pl.pallas_call(
        _combine_kernel,
        out_shape=jax.ShapeDtypeStruct((t, D_MODEL), F32),
        grid_spec=grid_spec,
        compiler_params=_cparams(("arbitrary",)),
        name="moe_combine",
    )(win_start, tile_second, h2d, rank, base_row, ys)


def _moe_plan(counts, starts, t):
    n_tok_tiles = t // TM_PROJ
    n_tiles = 2 * t // TM_MOE + N_EXPERTS + 1
    n_slots = n_tiles * TM_MOE
    counts = counts.astype(jnp.int32)
    cum = jnp.concatenate([starts.astype(jnp.int32), counts[None, :]], axis=0)
    padded = (counts + TM_MOE - 1) // TM_MOE * TM_MOE
    ends = jnp.cumsum(padded)
    base = ends - padded
    total = ends[-1]
    tile_first = jnp.arange(n_tiles, dtype=jnp.int32) * TM_MOE
    tile_expert = jnp.minimum(jnp.sum(ends[None, :] <= tile_first[:, None], axis=1), N_EXPERTS - 1).astype(jnp.int32)
    tile_active = (tile_first < total).astype(jnp.int32)
    n_groups = n_slots // SLOT_GROUP
    g_first = jnp.arange(n_groups, dtype=jnp.int32) * SLOT_GROUP
    g_expert = tile_expert[g_first // TM_MOE]
    g_r0 = g_first - base[g_expert]
    cum_e = cum.T[g_expert]
    lo = jnp.sum(cum_e <= g_r0[:, None], axis=1) - 1
    hi = jnp.sum(cum_e < (g_r0 + SLOT_GROUP)[:, None], axis=1)
    g_valid = (g_first < total) & (g_r0 < counts[g_expert])
    lo = jnp.clip(lo, 0, n_tok_tiles - 1)
    hi = jnp.clip(hi, 0, n_tok_tiles)
    g_n = jnp.where(g_valid, (jnp.maximum(hi - lo, 0) * TM_PROJ + DISPATCH_K - 1) // DISPATCH_K, 0)
    lo = jnp.minimum(lo * TM_PROJ, t - jnp.maximum(g_n, 1) * DISPATCH_K)
    first = base[None, :] + cum[:-1]
    cnt = cum[1:] - cum[:-1]
    ws = first // SLOT_ALIGN * SLOT_ALIGN
    pieces = jnp.where(cnt > 0, (first + cnt - ws + COMBINE_WIN - 1) // COMBINE_WIN, 0)
    return dict(tile_expert=tile_expert, tile_active=tile_active, g_expert=g_expert, g_r0=g_r0.astype(jnp.int32),
                g_lo=lo.astype(jnp.int32), g_n=g_n.astype(jnp.int32), base=base,
                ws=ws.reshape(-1).astype(jnp.int32), second=jnp.max(pieces, axis=1).astype(jnp.int32))


def _pad_heads_cols(w, heads, dim, pad):
    lead = w.shape[:-1]
    w = w.reshape(lead + (heads, dim))
    w = jnp.pad(w, [(0, 0)] * len(lead) + [(0, 0), (0, pad - dim)])
    return w.reshape(lead + (heads * pad,))


def _row(v):
    return v.reshape(1, -1).astype(F32)


def kernel(x, mem, positions, mem_norm, w_mem_kv, ln_mix0, w_in0, conv_w0, b_i0, b_f0, h_norm0, mq_norm0, mk_norm0, w_out0, ln_ffn0, w_gate0, w_up0, w_down0, ln_mix1, w_in1, cq_norm1, ckv_norm1, w_uq1, w_ukv1, q_norm1, k_norm1, mq_norm1, mk_norm1, w_out1, ln_ffn1, w_router1, we_gate1, we_up1, we_down1):
    batch, seq, _ = x.shape
    t = batch * seq
    h = x.reshape(t, D_MODEL)
    tile4 = lambda g: _row(jnp.tile(g, MEM_HEADS))

    k4_0, k4_1, v4 = _memkv(mem.reshape(batch * MEM_TOKENS, D_MODEL), _row(mem_norm), w_mem_kv.astype(BF16),
                            tile4(mk_norm0), tile4(mk_norm1), batch)

    mw = MAIN_WIDTH
    w_q, w_k, w_v, w_o = (w_in0[:, i * mw:(i + 1) * mw] for i in range(4))
    w_gates = w_in0[:, 4 * mw:4 * mw + 2 * ML_HEADS]
    w_mq = w_in0[:, 4 * mw + 2 * ML_HEADS:]
    padh = lambda w: _pad_heads_cols(w, ML_HEADS, ML_HEAD_DIM, HEAD_PAD)
    wqk = jnp.concatenate([padh(w_q), padh(w_k)], axis=1).astype(BF16)
    wvo = jnp.concatenate([padh(w_v), padh(w_o)], axis=1).astype(BF16)
    wg = jnp.pad(w_gates, ((0, 0), (0, GATE_PAD - 2 * ML_HEADS))).astype(BF16)
    wgt = w_gates.T.astype(BF16)
    cw = jnp.concatenate([padh(conv_w0[:, :mw]), padh(conv_w0[:, mw:])], axis=1).astype(F32)
    bias = jnp.concatenate([b_i0, b_f0]).astype(F32)
    bias_row = jnp.pad(bias, (0, GATE_PAD - 2 * ML_HEADS)).reshape(1, GATE_PAD)
    bias_col = bias.reshape(2 * ML_HEADS, 1)
    q0, kt0, v0, so, mq0, gates, gatest = _in0(h, _row(ln_mix0), wqk, wvo, w_mq.astype(BF16), wg, wgt, cw,
                                               bias_row, bias_col, seq)
    hgain = jnp.pad(h_norm0.astype(F32), (0, HEAD_PAD - ML_HEAD_DIM)).reshape(1, HEAD_PAD)
    ffn_groups = lambda w: w.reshape(N_EXPERTS, w.shape[0] // N_EXPERTS, w.shape[1])
    hn, (wg0, wu0, wd0) = _mlstm(q0, kt0, v0, so, gates, gatest, hgain,
                                 [ffn_groups(w_gate0), ffn_groups(w_up0), ffn_groups(w_down0)], batch, seq)
    wm0 = jnp.pad(w_out0[:mw].reshape(ML_HEADS, ML_HEAD_DIM, D_MODEL),
                  ((0, 0), (0, HEAD_PAD - ML_HEAD_DIM), (0, 0))).reshape(ML_PAD, D_MODEL).astype(BF16)
    h, we_gate_bf = _out_proj(hn, mq0, h, k4_0, v4, tile4(mq_norm0), wm0, w_out0[mw:].astype(BF16), seq,
                              ffn=(_row(ln_ffn0), wg0.reshape(w_gate0.shape), wu0.reshape(w_up0.shape),
                                   wd0.reshape(w_down0.shape), we_gate1))

    w_cq = w_in1[:, :Q_LORA]
    w_ckv = w_in1[:, Q_LORA:Q_LORA + KV_LORA]
    w_kr = jnp.pad(w_in1[:, Q_LORA + KV_LORA:Q_LORA + KV_LORA + MLA_ROPE], ((0, 0), (0, LANE - MLA_ROPE)))
    w_mq1 = w_in1[:, Q_LORA + KV_LORA + MLA_ROPE:]
    w1 = jnp.concatenate([w_cq, w_ckv, w_kr, w_mq1], axis=1).astype(BF16)
    wuq = _pad_heads_cols(w_uq1, MLA_HEADS, MLA_QK, HEAD_PAD).astype(BF16)
    wukv = w_ukv1.reshape(KV_LORA, MLA_HEADS, MLA_NOPE + MLA_V)
    wuk = wukv[:, :, :MLA_NOPE].reshape(KV_LORA, MLA_HEADS * MLA_NOPE).astype(BF16)
    wuvt = wukv[:, :, MLA_NOPE:].reshape(KV_LORA, MLA_HEADS * MLA_V).T.astype(BF16)
    gq = jnp.pad(q_norm1.astype(F32), (0, HEAD_PAD - MLA_QK)).reshape(1, HEAD_PAD)
    gkn = _row(k_norm1[:MLA_NOPE])
    gkr = jnp.pad(k_norm1[MLA_NOPE:].astype(F32), (0, LANE - MLA_ROPE)).reshape(1, LANE)
    half = MLA_ROPE // 2
    inv_freq = ROPE_THETA ** (-jnp.arange(half, dtype=F32) / half)
    invf = jnp.concatenate([inv_freq, inv_freq, jnp.zeros((LANE - MLA_ROPE,), F32)]).reshape(1, LANE)
    q1, k1, v1, mq1 = _in1(h, positions.reshape(1, t).astype(jnp.int32), _row(ln_mix1), w1, _row(cq_norm1),
                           _row(ckv_norm1), wuq, wuk, wuvt, gq, gkn, gkr, invf)
    o1, we_up_bf, we_down_bf = _attention(q1, k1, v1, we_up1, we_down1, batch, seq)
    h = _out_proj(o1, mq1, h, k4_1, v4, tile4(mq_norm1), w_out1[:mw].astype(BF16), w_out1[mw:].astype(BF16), seq)

    wr = jnp.pad(w_router1.astype(F32), ((0, 0), (0, LANE - N_EXPERTS)))
    wr_hi = wr.astype(BF16)
    wr_lo = (wr - wr_hi.astype(F32)).astype(BF16)
    xn, gatest, rank, rankt, starts, counts = _router(h, _row(ln_ffn1), wr_hi, wr_lo)
    plan = _moe_plan(counts[0, :N_EXPERTS], starts[::8, :N_EXPERTS], t)
    xs, gate_rows = _dispatch(xn, rankt, gatest, plan["g_expert"], plan["g_r0"], plan["g_lo"], plan["g_n"])
    ys = _experts(xs, gate_rows, plan["tile_expert"], plan["tile_active"], we_gate_bf, we_up_bf, we_down_bf)
    base_row = jnp.pad(plan["base"].astype(F32), (0, LANE - N_EXPERTS)).reshape(1, LANE)
    h = _combine(h, rank, base_row, ys, plan["ws"], plan["second"])
    return h.reshape(batch, seq, D_MODEL)
```

```python
import functools

import jax
import jax.numpy as jnp
import numpy as np
from jax import lax
from jax.experimental import pallas as pl
from jax.experimental.pallas import tpu as pltpu

F32 = jnp.float32
BF16 = jnp.bfloat16

D_MODEL = 1024
MEM_TOKENS = 256
MEM_HEADS = 4
MEM_WIDTH = 256
MEM_HEAD_DIM = 64
MAIN_WIDTH = 768
ML_HEADS = 4
ML_HEAD_DIM = 192
CONV_WIDTH = 4
MLA_HEADS = 6
MLA_NOPE = 128
MLA_ROPE = 64
MLA_QK = 192
MLA_V = 128
Q_LORA = 384
KV_LORA = 128
ROPE_THETA = 10000.0
D_FF = 3584
N_EXPERTS = 8
EPS = 1e-6

LANE = 128
HEAD_PAD = 256
ML_PAD = ML_HEADS * HEAD_PAD
MLA_PAD = MLA_HEADS * HEAD_PAD
GATE_PAD = 128
ML_CHUNK = 256
TM_PROJ = 256
TM_ROUTER = 2048
TM_IN0 = 1024
TM_IN1 = 1024
TM_OUT = 1024
TM_FFN = 512
FF_CHUNK = 512
TM_MOE = 512
SLOT_GROUP = 128
DISPATCH_K = 1024
DISPATCH_GROUPS = 4
CAST_PARTS = 4
IN0_ROW_SPLIT = 4
IN1_ROW_SPLIT = 4
SLOT_ALIGN = 16
COMBINE_WIN = 128
COMBINE_MAX_WINDOWS = -(-(TM_PROJ + SLOT_ALIGN - 1) // COMBINE_WIN)
TQ = 2048
TK = 512
NEG = -1e30
LOG2E = 1.4426950408889634
VMEM_LIMIT = 56 * 1024 * 1024
EXPERT_VMEM_LIMIT = 62 * 1024 * 1024


def _cparams(sem, vmem=VMEM_LIMIT):
    return pltpu.CompilerParams(dimension_semantics=sem, vmem_limit_bytes=vmem)


def _rms(x, g):
    return x * lax.rsqrt(jnp.mean(x * x, axis=-1, keepdims=True) + EPS) * g


def _split_dot(x, w_bf16):
    hi = x.astype(BF16)
    lo = (x - hi.astype(F32)).astype(BF16)
    return (jnp.dot(hi, w_bf16, preferred_element_type=F32)
            + jnp.dot(lo, w_bf16, preferred_element_type=F32))


def _block_diag_ones(n, blk):
    r = lax.broadcasted_iota(jnp.int32, (n, n), 0) // blk
    c = lax.broadcasted_iota(jnp.int32, (n, n), 1) // blk
    return jnp.where(r == c, 1.0, 0.0).astype(BF16)


def _const_spec(shape):
    nd = len(shape)
    return pl.BlockSpec(shape, lambda *_: (0,) * nd)


def _interleave(stage_generators):
    live = list(stage_generators)
    while live:
        for gen in list(live):
            try:
                next(gen)
            except StopIteration:
                live.remove(gen)


def _memkv_kernel(mem_ref, g_ref, w_ref, gk0_ref, gk1_ref, k0_ref, k1_ref, v_ref):
    xn = _rms(mem_ref[...], g_ref[...]).astype(BF16)
    kv = jnp.dot(xn, w_ref[...], preferred_element_type=F32)
    k = kv[:, :MEM_WIDTH]
    v = kv[:, MEM_WIDTH:]
    bd = _block_diag_ones(MEM_WIDTH, MEM_HEAD_DIM)
    ms = _split_dot(k * k, bd) * (1.0 / MEM_HEAD_DIM)
    kn = k * lax.rsqrt(ms + EPS)
    lane_head = lax.broadcasted_iota(jnp.int32, (MEM_TOKENS, MEM_WIDTH), 1) // MEM_HEAD_DIM
    for h in range(MEM_HEADS):
        sel = lane_head == h
        k0_ref[h] = jnp.where(sel, kn * gk0_ref[...], 0.0).astype(BF16)
        k1_ref[h] = jnp.where(sel, kn * gk1_ref[...], 0.0).astype(BF16)
        v_ref[h] = jnp.where(sel, v, 0.0).astype(BF16)


def _memkv(mem2d, mem_norm, w_kv, gk0, gk1, batch):
    out = jax.ShapeDtypeStruct((batch * MEM_HEADS, MEM_TOKENS, MEM_WIDTH), BF16)
    hspec = pl.BlockSpec((MEM_HEADS, MEM_TOKENS, MEM_WIDTH), lambda b: (b, 0, 0))
    return pl.pallas_call(
        _memkv_kernel,
        out_shape=(out, out, out),
        grid=(batch,),
        in_specs=[pl.BlockSpec((MEM_TOKENS, D_MODEL), lambda b: (b, 0)),
                  _const_spec((1, D_MODEL)), _const_spec((D_MODEL, 2 * MEM_WIDTH)),
                  _const_spec((1, MEM_WIDTH)), _const_spec((1, MEM_WIDTH))],
        out_specs=(hspec, hspec, hspec),
        compiler_params=_cparams(("parallel",)),
        name="memkv",
    )(mem2d, mem_norm, w_kv, gk0, gk1)


def _in0_kernel(tiles_per_seq, h_ref, hp_ref, g_ref, wqk_ref, wvo_ref, wmq_ref, wg_ref, wgt_ref,
                cw_ref, bias_ref, biast_ref,
                q_ref, kt_ref, v_ref, so_ref, mq_ref, gates_ref, gatest_ref):
    i = pl.program_id(0)
    tm = h_ref.shape[0] // IN0_ROW_SPLIT
    g = g_ref[...]
    first = (i % tiles_per_seq) == 0
    row8 = lax.broadcasted_iota(jnp.int32, (8, FF_CHUNK), 0)
    ncol = wqk_ref.shape[1]

    def gate_act(z, is_input_gate):
        logsig = jnp.minimum(z, 0.0) - jnp.log(1.0 + jnp.exp(-jnp.abs(z)))
        return jnp.where(is_input_gate, z, logsig)

    def rows_body(r):
        rows = slice(r * tm, (r + 1) * tm)
        xn = _rms(h_ref[rows, :], g).astype(BF16)
        prev_rows = hp_ref[...] if r == 0 else h_ref[r * tm - 8:r * tm, :]
        xpn = _rms(prev_rows, g).astype(BF16)
        yield
        for c in range(ncol // FF_CHUNK):
            cs = slice(c * FF_CHUNK, (c + 1) * FF_CHUNK)
            w = wqk_ref[:, cs]
            u = jnp.dot(xn, w, preferred_element_type=F32)
            up = jnp.dot(xpn, w, preferred_element_type=F32)
            if r == 0:
                up = jnp.where(first, 0.0, up)
            yield
            acc = u * cw_ref[CONV_WIDTH - 1:CONV_WIDTH, cs]
            for k in range(1, CONV_WIDTH):
                rolled = pltpu.roll(u, k, 0)
                prev = pltpu.roll(up, k, 0)
                head = jnp.where(row8 < k, prev, rolled[:8])
                shifted = jnp.concatenate([head, rolled[8:]], axis=0)
                acc = acc + shifted * cw_ref[CONV_WIDTH - 1 - k:CONV_WIDTH - k, cs]
            y = acc * jax.nn.sigmoid(acc)
            if c < ML_PAD // FF_CHUNK:
                q_ref[rows, cs] = (y * (ML_HEAD_DIM ** -0.5)).astype(BF16)
            else:
                ks = slice(c * FF_CHUNK - ML_PAD, (c + 1) * FF_CHUNK - ML_PAD)
                kt_ref[ks, rows] = y.T.astype(BF16)
            yield
        zvo = jnp.dot(xn, wvo_ref[...], preferred_element_type=F32)
        yield
        v_ref[rows, :] = zvo[:, :ML_PAD].astype(BF16)
        so_ref[rows, :] = jax.nn.sigmoid(zvo[:, ML_PAD:]).astype(BF16)
        yield
        mq_ref[rows, :] = jnp.dot(xn, wmq_ref[...], preferred_element_type=F32).astype(BF16)
        zg = jnp.dot(xn, wg_ref[...], preferred_element_type=F32) + bias_ref[...]
        lane = lax.broadcasted_iota(jnp.int32, (tm, GATE_PAD), 1)
        gates_ref[rows, :] = gate_act(zg, lane < ML_HEADS)
        zgt = lax.dot_general(wgt_ref[...], xn, (((1,), (1,)), ((), ())),
                              preferred_element_type=F32) + biast_ref[...]
        row = lax.broadcasted_iota(jnp.int32, (8, tm), 0)
        gatest_ref[:, rows] = gate_act(zgt, row < ML_HEADS)

    _interleave(rows_body(r) for r in range(IN0_ROW_SPLIT))


def _in0(h2d, ln, wqk, wvo, wmq, wg, wgt, cw, bias, biast, seq):
    t = h2d.shape[0]
    tm = TM_IN0
    tiles_per_seq = seq // tm
    tok = lambda n, dt: jax.ShapeDtypeStruct((t, n), dt)
    row = lambda n: pl.BlockSpec((tm, n), lambda i: (i, 0))
    col = lambda n: pl.BlockSpec((n, tm), lambda i: (0, i))
    return pl.pallas_call(
        functools.partial(_in0_kernel, tiles_per_seq),
        out_shape=(tok(ML_PAD, BF16), jax.ShapeDtypeStruct((ML_PAD, t), BF16), tok(ML_PAD, BF16),
                   tok(ML_PAD, BF16), tok(MEM_WIDTH, BF16), tok(GATE_PAD, F32), jax.ShapeDtypeStruct((8, t), F32)),
        grid=(t // tm,),
        in_specs=[row(D_MODEL),
                  pl.BlockSpec((8, D_MODEL), lambda i: (jnp.maximum(i * (tm // 8) - 1, 0), 0)),
                  _const_spec((1, D_MODEL)), _const_spec(wqk.shape), _const_spec(wvo.shape),
                  _const_spec(wmq.shape), _const_spec(wg.shape), _const_spec(wgt.shape),
                  _const_spec(cw.shape), _const_spec(bias.shape), _const_spec(biast.shape)],
        out_specs=(row(ML_PAD), col(ML_PAD), row(ML_PAD), row(ML_PAD), row(MEM_WIDTH), row(GATE_PAD), col(8)),
        compiler_params=_cparams(("parallel",)),
        name="in_proj0",
    )(h2d, h2d, ln, wqk, wvo, wmq, wg, wgt, cw, bias, biast)


def _mlstm_kernel(batch, cast_blocks, q_ref, v_ref, so_ref, gates_ref, *rest):
    n_cast = len(cast_blocks)
    kt_refs, gatest_refs = rest[:batch], rest[batch:2 * batch]
    hg_ref = rest[2 * batch]
    cast_src = rest[2 * batch + 1:2 * batch + 1 + n_cast]
    out_ref = rest[2 * batch + 1 + n_cast]
    cast_dst = rest[2 * batch + 2 + n_cast:2 * batch + 2 + 2 * n_cast]
    c_ref, m_ref = rest[2 * batch + 2 + 2 * n_cast:]
    L = q_ref.shape[1]
    for src, dst, n_blocks in zip(cast_src, cast_dst, cast_blocks):
        _cast_block(pl.program_id(0), src, dst, n_blocks)

    @pl.when(pl.program_id(0) == 0)
    def _():
        c_ref[...] = jnp.zeros_like(c_ref)
        m_ref[...] = jnp.zeros_like(m_ref)

    r = lax.broadcasted_iota(jnp.int32, (L, L), 0)
    c = lax.broadcasted_iota(jnp.int32, (L, L), 1)
    causal = r >= c
    tri_lo = jnp.where(causal, 1.0, 0.0).astype(BF16)
    tri_up = jnp.where(r <= c, 1.0, 0.0).astype(BF16)
    lane = lax.broadcasted_iota(jnp.int32, (L, HEAD_PAD), 1)
    hg = hg_ref[...]
    def sequence_body(b):
        gts = gates_ref[b]
        gtt = gatest_refs[b][...]
        hi = gts.astype(BF16)
        lo = (gts - hi.astype(F32)).astype(BF16)
        b_cols = (jnp.dot(tri_lo, hi, preferred_element_type=F32)
                  + jnp.dot(tri_lo, lo, preferred_element_type=F32))
        b_rows = _split_dot(gtt, tri_up)
        yield
        heads = range(ML_HEADS)
        hs = [slice(h * HEAD_PAD, (h + 1) * HEAD_PAD) for h in heads]
        state = [b * ML_HEADS + h for h in heads]
        m_h = [m_ref[state[h]][0:1, 0:1] for h in heads]
        b_c = [b_cols[:, ML_HEADS + h:ML_HEADS + h + 1] for h in heads]
        c_r = [b_rows[ML_HEADS + h:ML_HEADS + h + 1, :] - gtt[h:h + 1, :] for h in heads]
        q_h = [q_ref[b, :, hs[h]] for h in heads]
        kt_h = [kt_refs[b][hs[h], :] for h in heads]
        s = [jnp.dot(q_h[h], kt_h[h], preferred_element_type=F32) for h in heads]
        c_old = [c_ref[state[h]] for h in heads]
        qc = [jnp.dot(q_h[h], c_old[h].astype(BF16), preferred_element_type=F32) for h in heads]
        yield
        log_d = [jnp.where(causal, b_c[h] - c_r[h], NEG) for h in heads]
        inter = [b_c[h] + m_h[h] for h in heads]
        m_row = [jnp.maximum(inter[h], jnp.max(log_d[h], axis=-1, keepdims=True)) for h in heads]
        yield
        d = [jnp.exp(log_d[h] - m_row[h]) for h in heads]
        s_inter = [jnp.exp(inter[h] - m_row[h]) for h in heads]
        v_aug = [jnp.where(lane == ML_HEAD_DIM, 1.0, v_ref[b, :, hs[h]].astype(F32)).astype(BF16) for h in heads]
        qkd = [(s[h] * d[h]).astype(BF16) for h in heads]
        yield
        num = [jnp.dot(qkd[h], v_aug[h], preferred_element_type=F32) + s_inter[h] * qc[h] for h in heads]
        yield
        b_last = [b_c[h][L - 1:L, :] for h in heads]
        g_r = [b_last[h] - c_r[h] for h in heads]
        m_new = [jnp.maximum(b_last[h] + m_h[h], jnp.max(g_r[h], axis=-1, keepdims=True)) for h in heads]
        w_r = [jnp.exp(g_r[h] - m_new[h]) for h in heads]
        decay = [jnp.exp(b_last[h] + m_h[h] - m_new[h]) for h in heads]
        ktw = [(kt_h[h].astype(F32) * w_r[h]).astype(BF16) for h in heads]
        yield
        for h in heads:
            c_ref[state[h]] = decay[h] * c_old[h] + jnp.dot(ktw[h], v_aug[h], preferred_element_type=F32)
            m_ref[state[h]] = jnp.broadcast_to(m_new[h], m_ref.shape[1:])
        yield
        den = [jnp.sum(jnp.where(lane == ML_HEAD_DIM, num[h], 0.0), axis=-1, keepdims=True) for h in heads]
        ht = [num[h] * (1.0 / jnp.maximum(jnp.abs(den[h]), jnp.exp(-m_row[h]))) for h in heads]
        ht = [jnp.where(lane < ML_HEAD_DIM, ht[h], 0.0) for h in heads]
        ms = [jnp.sum(ht[h] * ht[h], axis=-1, keepdims=True) * (1.0 / ML_HEAD_DIM) for h in heads]
        yield
        for h in heads:
            hn = ht[h] * lax.rsqrt(ms[h] + EPS) * hg * so_ref[b, :, hs[h]].astype(F32)
            out_ref[b, :, hs[h]] = hn.astype(BF16)

    _interleave(sequence_body(b) for b in range(batch))


def _mlstm(q, kt, v, so, gates, gatest, hgain, casts, batch, seq):
    L = ML_CHUNK
    nc = seq // L
    seq3 = lambda a: a.reshape(batch, seq, a.shape[-1])
    row = lambda n: pl.BlockSpec((batch, L, n), lambda c: (0, c, 0))
    col = lambda rows: [pl.BlockSpec((rows, L), functools.partial(lambda b, c: (0, b * nc + c), b))
                        for b in range(batch)]
    cast_blocks = tuple(a.shape[0] * CAST_PARTS for a in casts)
    assert all(n <= nc for n in cast_blocks)
    cast_specs = [_cast_spec(a.shape, lambda c: c) for a in casts]
    outs = pl.pallas_call(
        functools.partial(_mlstm_kernel, batch, cast_blocks),
        out_shape=[jax.ShapeDtypeStruct((batch, seq, ML_PAD), BF16)]
                  + [jax.ShapeDtypeStruct(a.shape, BF16) for a in casts],
        grid=(nc,),
        in_specs=[row(ML_PAD), row(ML_PAD), row(ML_PAD), row(GATE_PAD)] + col(ML_PAD) + col(8)
                 + [_const_spec((1, HEAD_PAD))] + cast_specs,
        out_specs=[row(ML_PAD)] + cast_specs,
        scratch_shapes=[pltpu.VMEM((batch * ML_HEADS, HEAD_PAD, HEAD_PAD), F32),
                        pltpu.VMEM((batch * ML_HEADS, 8, LANE), F32)],
        compiler_params=_cparams(("arbitrary",)),
        name="mlstm",
    )(seq3(q), seq3(v), seq3(so), seq3(gates), *([kt] * batch), *([gatest] * batch), hgain, *casts)
    return outs[0].reshape(batch * seq, ML_PAD), outs[1:]


def _cast_spec(shape, step_of):
    e, r, c = shape
    n_blocks = e * CAST_PARTS

    def index_map(*grid_idx):
        s = jnp.minimum(step_of(*grid_idx), n_blocks - 1)
        return (s // CAST_PARTS, s % CAST_PARTS, 0)

    return pl.BlockSpec((1, r // CAST_PARTS, c), index_map)


def _cast_block(step, src_ref, dst_ref, n_blocks):
    @pl.when(step < n_blocks)
    def _():
        dst_ref[...] = src_ref[...].astype(dst_ref.dtype)


def _mixer_residual(main_ref, mq_ref, h_ref, k4_ref, v4_ref, gq_ref, wm_ref, wmem_ref):
    heads = range(MEM_HEADS)
    nt_dims = (((1,), (1,)), ((), ()))
    q = mq_ref[...].astype(F32)
    bd = _block_diag_ones(MEM_WIDTH, MEM_HEAD_DIM)
    ms = _split_dot(q * q, bd) * (1.0 / MEM_HEAD_DIM)
    qn = (q * lax.rsqrt(ms + EPS) * gq_ref[...] * (MEM_HEAD_DIM ** -0.5)).astype(BF16)
    ymem = jnp.zeros(q.shape, F32)
    for h in heads:
        s = lax.dot_general(qn, k4_ref[h], nt_dims, preferred_element_type=F32)
        e = jnp.exp(s - jnp.max(s, axis=-1, keepdims=True))
        inv = 1.0 / jnp.sum(e, axis=-1, keepdims=True)
        ymem = ymem + jnp.dot(e.astype(BF16), v4_ref[h], preferred_element_type=F32) * inv
    y = (jnp.dot(main_ref[...], wm_ref[...], preferred_element_type=F32)
         + jnp.dot(ymem.astype(BF16), wmem_ref[...], preferred_element_type=F32))
    return h_ref[...] + y


def _out_kernel(main_ref, mq_ref, h_ref, k4_ref, v4_ref, gq_ref, wm_ref, wmem_ref, out_ref):
    out_ref[...] = _mixer_residual(main_ref, mq_ref, h_ref, k4_ref, v4_ref, gq_ref, wm_ref, wmem_ref)


def _out_ffn_kernel(main_ref, mq_ref, h_ref, k4_ref, v4_ref, gq_ref, wm_ref, wmem_ref,
                    g_ref, wg_ref, wu_ref, wd_ref, cast_src_ref, out_ref, cast_dst_ref):
    _cast_block(pl.program_id(0), cast_src_ref, cast_dst_ref, N_EXPERTS * CAST_PARTS)
    x = _mixer_residual(main_ref, mq_ref, h_ref, k4_ref, v4_ref, gq_ref, wm_ref, wmem_ref)
    xn = _rms(x, g_ref[...]).astype(BF16)
    acc = x
    for c in range(D_FF // FF_CHUNK):
        cs = slice(c * FF_CHUNK, (c + 1) * FF_CHUNK)
        gt = jnp.dot(xn, wg_ref[:, cs], preferred_element_type=F32)
        up = jnp.dot(xn, wu_ref[:, cs], preferred_element_type=F32)
        a = (gt * jax.nn.sigmoid(gt) * up).astype(BF16)
        acc = acc + jnp.dot(a, wd_ref[cs, :], preferred_element_type=F32)
    out_ref[...] = acc


def _out_proj(main, mq, h2d, k4, v4, gq, wm, wmem, seq, ffn=None):
    t = h2d.shape[0]
    tm = TM_OUT if ffn is None else TM_FFN
    tiles_per_seq = seq // tm
    row = lambda n: pl.BlockSpec((tm, n), lambda i: (i, 0))
    memspec = pl.BlockSpec((MEM_HEADS, MEM_TOKENS, MEM_WIDTH), lambda i: (i // tiles_per_seq, 0, 0))
    in_specs = [row(main.shape[1]), row(MEM_WIDTH), row(D_MODEL), memspec, memspec,
                _const_spec((1, MEM_WIDTH)), _const_spec(wm.shape), _const_spec(wmem.shape)]
    if ffn is None:
        return pl.pallas_call(
            _out_kernel,
            out_shape=jax.ShapeDtypeStruct((t, D_MODEL), F32),
            grid=(t // tm,),
            in_specs=in_specs,
            out_specs=row(D_MODEL),
            compiler_params=_cparams(("parallel",)),
            name="out_proj",
        )(main, mq, h2d, k4, v4, gq, wm, wmem)
    ln, wg, wu, wd, cast_src = ffn
    assert t // tm >= cast_src.shape[0] * CAST_PARTS
    res = lambda shape: pl.BlockSpec(shape, lambda i: (0, 0), pipeline_mode=pl.Buffered(1))
    cast = _cast_spec(cast_src.shape, lambda i: i)
    return pl.pallas_call(
        _out_ffn_kernel,
        out_shape=(jax.ShapeDtypeStruct((t, D_MODEL), F32), jax.ShapeDtypeStruct(cast_src.shape, BF16)),
        grid=(t // tm,),
        in_specs=in_specs + [_const_spec((1, D_MODEL)), res(wg.shape), res(wu.shape), res(wd.shape), cast],
        out_specs=(row(D_MODEL), cast),
        compiler_params=_cparams(("arbitrary",)),
        name="out_proj_ffn",
    )(main, mq, h2d, k4, v4, gq, wm, wmem, ln, wg, wu, wd, cast_src)


def _in1_kernel(h_ref, pos_ref, g_ref, w1_ref, gcq_ref, gckv_ref, wuq_ref, wuk_ref, wuvt_ref,
                gq_ref, gkn_ref, gkr_ref, invf_ref,
                q_ref, k_ref, vt_ref, mq_ref):
    tm = h_ref.shape[0] // IN1_ROW_SPLIT
    half = MLA_ROPE // 2
    scale = MLA_QK ** -0.5 * LOG2E
    heads = range(MLA_HEADS)

    def rows_body(rows):
        xn = _rms(h_ref[rows, :], g_ref[...]).astype(BF16)
        yield
        z = jnp.dot(xn, w1_ref[...], preferred_element_type=F32)
        yield
        cq = z[:, :Q_LORA]
        ckv = z[:, Q_LORA:Q_LORA + KV_LORA]
        kr = z[:, Q_LORA + KV_LORA:Q_LORA + KV_LORA + LANE]
        mq_ref[rows, :] = z[:, Q_LORA + KV_LORA + LANE:].astype(BF16)
        cqn = _rms(cq, gcq_ref[...]).astype(BF16)
        ckvn = _rms(ckv, gckv_ref[...]).astype(BF16)
        yield
        q = jnp.dot(cqn, wuq_ref[...], preferred_element_type=F32)
        kn = jnp.dot(ckvn, wuk_ref[...], preferred_element_type=F32)
        vt_ref[:, rows] = lax.dot_general(wuvt_ref[...], ckvn, (((1,), (1,)), ((), ())),
                                          preferred_element_type=F32).astype(BF16)
        yield
        eye = lax.broadcasted_iota(jnp.int32, (tm, tm), 0) == lax.broadcasted_iota(jnp.int32, (tm, tm), 1)
        pos = jnp.sum(jnp.where(eye, pos_ref[:, rows].astype(F32), 0.0), axis=-1, keepdims=True)
        ang = pos * invf_ref[...]
        lane = lax.broadcasted_iota(jnp.int32, (tm, LANE), 1)
        cos = jnp.where(lane < MLA_ROPE, jnp.cos(ang), 0.0)
        sin = jnp.sin(ang)
        sin_hi = jnp.where((lane >= half) & (lane < MLA_ROPE), sin, 0.0)
        sin_lo = jnp.where(lane < half, -sin, 0.0)

        def rope(t):
            return t * cos + pltpu.roll(t, half, 1) * sin_hi + pltpu.roll(t, LANE - half, 1) * sin_lo

        yield
        qh = [q[:, h * HEAD_PAD:(h + 1) * HEAD_PAD] for h in heads]
        kh = [kn[:, h * MLA_NOPE:(h + 1) * MLA_NOPE] for h in heads]
        q_ss = [jnp.sum(qh[h] * qh[h], axis=-1, keepdims=True) for h in heads]
        k_ss = [jnp.sum(kh[h] * kh[h], axis=-1, keepdims=True) for h in heads]
        ss_r = jnp.sum(kr * kr, axis=-1, keepdims=True)
        yield
        krr = rope(kr * gkr_ref[...])
        q_rs = [lax.rsqrt(q_ss[h] * (1.0 / MLA_QK) + EPS) * scale for h in heads]
        k_rs = [lax.rsqrt((k_ss[h] + ss_r) * (1.0 / MLA_QK) + EPS) for h in heads]
        yield
        qn = [qh[h] * q_rs[h] * gq_ref[...] for h in heads]
        q_rot = [rope(qn[h][:, LANE:]) for h in heads]
        yield
        for h in heads:
            q_ref[rows, h * HEAD_PAD:h * HEAD_PAD + LANE] = qn[h][:, :LANE].astype(BF16)
            q_ref[rows, h * HEAD_PAD + LANE:(h + 1) * HEAD_PAD] = q_rot[h].astype(BF16)
        yield
        for h in heads:
            k_ref[rows, h * HEAD_PAD:h * HEAD_PAD + LANE] = (kh[h] * k_rs[h] * gkn_ref[...]).astype(BF16)
            k_ref[rows, h * HEAD_PAD + LANE:(h + 1) * HEAD_PAD] = (krr * k_rs[h]).astype(BF16)

    _interleave(rows_body(slice(r * tm, (r + 1) * tm)) for r in range(IN1_ROW_SPLIT))


def _in1(h2d, pos2d, ln, w1, gcq, gckv, wuq, wuk, wuvt, gq, gkn, gkr, invf):
    t = h2d.shape[0]
    tm = TM_IN1
    tok = lambda n: jax.ShapeDtypeStruct((t, n), BF16)
    row = lambda n: pl.BlockSpec((tm, n), lambda i: (i, 0))
    consts = [ln, w1, gcq, gckv, wuq, wuk, wuvt, gq, gkn, gkr, invf]
    return pl.pallas_call(
        _in1_kernel,
        out_shape=(tok(MLA_PAD), tok(MLA_PAD), jax.ShapeDtypeStruct((MAIN_WIDTH, t), BF16), tok(MEM_WIDTH)),
        grid=(t // tm,),
        in_specs=[row(D_MODEL), pl.BlockSpec((1, tm), lambda i: (0, i))] + [_const_spec(a.shape) for a in consts],
        out_specs=(row(MLA_PAD), row(MLA_PAD), pl.BlockSpec((MAIN_WIDTH, tm), lambda i: (0, i)), row(MEM_WIDTH)),
        compiler_params=_cparams(("parallel",)),
        name="in_proj1",
    )(h2d, pos2d, *consts)


def _attn_kernel(q_ref, k_ref, vt_ref, cast_a_ref, cast_b_ref, o_ref, cast_a_out_ref, cast_b_out_ref,
                 m_ref, l_ref, acc_ref, st_ref, mt_ref):
    qi = pl.program_id(2)
    step = (pl.program_id(0) * pl.num_programs(1) + pl.program_id(1)) * pl.num_programs(2) + qi
    _cast_block(step, cast_a_ref, cast_a_out_ref, N_EXPERTS * CAST_PARTS)
    _cast_block(step, cast_b_ref, cast_b_out_ref, N_EXPERTS * CAST_PARTS)
    m_ref[...] = jnp.full_like(m_ref, NEG)
    l_ref[...] = jnp.zeros_like(l_ref)
    acc_ref[...] = jnp.zeros_like(acc_ref)
    n_diag = TQ // TK
    n_full = qi * n_diag
    nt_dims = (((1,), (1,)), ((), ()))

    def scores(j, slot):
        koff = pl.multiple_of(j * TK, TK)
        st = lax.dot_general(k_ref[pl.ds(koff, TK), :], q_ref[...], nt_dims, preferred_element_type=F32)
        st_ref[slot] = st
        mt_ref[slot] = jnp.max(st, axis=0, keepdims=True)

    def absorb(st, m_tile, vt, cols):
        m_old = m_ref[:, cols]
        m_new = jnp.maximum(m_old, m_tile)
        alpha = jnp.exp2(m_old - m_new)
        e = jnp.exp2(st - m_new)
        l_ref[:, cols] = alpha * l_ref[:, cols] + jnp.sum(e, axis=0, keepdims=True)
        acc_ref[:, cols] = alpha * acc_ref[:, cols] + jnp.dot(vt, e.astype(BF16), preferred_element_type=F32)
        m_ref[:, cols] = m_new

    def finish(j, slot):
        koff = pl.multiple_of(j * TK, TK)
        absorb(st_ref[slot], mt_ref[slot], vt_ref[:, pl.ds(koff, TK)], slice(None))

    strips = [slice(u * TK, (u + 1) * TK) for u in range(n_diag)]

    def scores_stages(j, slot):
        koff = pl.multiple_of(j * TK, TK)
        for cols in strips:
            st = lax.dot_general(k_ref[pl.ds(koff, TK), :], q_ref[cols, :], nt_dims, preferred_element_type=F32)
            st_ref[slot, :, cols] = st
            mt_ref[slot, :, cols] = jnp.max(st, axis=0, keepdims=True)
            yield

    def finish_stages(j, slot):
        koff = pl.multiple_of(j * TK, TK)
        for cols in strips:
            absorb(st_ref[slot, :, cols], mt_ref[slot, :, cols], vt_ref[:, pl.ds(koff, TK)], cols)
            yield

    @pl.when(n_full > 0)
    def _():
        scores(0, 0)
        scores(1, 1)

        def body(p, carry):
            j = 4 * p
            finish(j, 0)
            _interleave([scores_stages(j + 2, 0), finish_stages(j + 1, 1)])
            _interleave([scores_stages(j + 3, 1), finish_stages(j + 2, 0)])
            _interleave([scores_stages(j + 4, 0), finish_stages(j + 3, 1)])
            scores(j + 5, 1)
            return carry

        lax.fori_loop(0, (n_full - 2) // 4, body, 0)
        finish(n_full - 4, 0)
        _interleave([scores_stages(n_full - 2, 0), finish_stages(n_full - 3, 1)])
        _interleave([scores_stages(n_full - 1, 1), finish_stages(n_full - 2, 0)])
        finish(n_full - 1, 1)

    kd0 = pl.multiple_of(n_full * TK, TK)

    def diagonal_block(u):
        nk = (u + 1) * TK
        cols = slice(u * TK, (u + 1) * TK)
        st = lax.dot_general(k_ref[pl.ds(kd0, nk), :], q_ref[cols, :], nt_dims, preferred_element_type=F32)
        yield
        kpos = lax.broadcasted_iota(jnp.int32, st.shape, 0)
        qpos = u * TK + lax.broadcasted_iota(jnp.int32, st.shape, 1)
        st = jnp.where(kpos <= qpos, st, NEG)
        m_tile = jnp.max(st, axis=0, keepdims=True)
        yield
        absorb(st, m_tile, vt_ref[:, pl.ds(kd0, nk)], cols)

    for u in range(n_diag // 2):
        _interleave(diagonal_block(v) for v in (u, n_diag - 1 - u))
    o_ref[...] = (acc_ref[...] * (1.0 / l_ref[...])).T.astype(o_ref.dtype)


def _attention(q, k, v, cast_a, cast_b, batch, seq):
    t = q.shape[0]
    nq = seq // TQ
    assert batch * MLA_HEADS * nq >= cast_a.shape[0] * CAST_PARTS
    assert (TQ // TK) % 4 == 0
    step_of = lambda b, h, i: (b * MLA_HEADS + h) * nq + i
    cast_specs = [_cast_spec(cast_a.shape, step_of), _cast_spec(cast_b.shape, step_of)]
    return pl.pallas_call(
        _attn_kernel,
        out_shape=(jax.ShapeDtypeStruct((t, MAIN_WIDTH), BF16), jax.ShapeDtypeStruct(cast_a.shape, BF16),
                   jax.ShapeDtypeStruct(cast_b.shape, BF16)),
        grid=(batch, MLA_HEADS, nq),
        in_specs=[pl.BlockSpec((TQ, HEAD_PAD), lambda b, h, i: (b * nq + i, h)),
                  pl.BlockSpec((seq, HEAD_PAD), lambda b, h, i: (b, h)),
                  pl.BlockSpec((MLA_V, seq), lambda b, h, i: (h, b))] + cast_specs,
        out_specs=[pl.BlockSpec((TQ, MLA_V), lambda b, h, i: (b * nq + i, h))] + cast_specs,
        scratch_shapes=[pltpu.VMEM((1, TQ), F32), pltpu.VMEM((1, TQ), F32), pltpu.VMEM((MLA_V, TQ), F32),
                        pltpu.VMEM((2, TK, TQ), F32), pltpu.VMEM((2, 1, TQ), F32)],
        compiler_params=_cparams(("arbitrary", "arbitrary", "arbitrary")),
        name="mla_attention",
    )(q, k, v, cast_a, cast_b)


def _router_kernel(h_ref, g_ref, whi_ref, wlo_ref,
                   xn_ref, gatest_ref, rank_ref, rankt_ref, starts_ref, counts_ref, run_ref):
    @pl.when(pl.program_id(0) == 0)
    def _():
        run_ref[...] = jnp.zeros_like(run_ref)

    n_chunks = h_ref.shape[0] // TM_PROJ
    r = lax.broadcasted_iota(jnp.int32, (TM_PROJ, TM_PROJ), 0)
    c = lax.broadcasted_iota(jnp.int32, (TM_PROJ, TM_PROJ), 1)
    before = jnp.where(r > c, 1.0, 0.0).astype(BF16)
    lane = lax.broadcasted_iota(jnp.int32, (TM_PROJ, LANE), 1).astype(F32)
    routed, prefix, count = [None] * n_chunks, [None] * n_chunks, [None] * n_chunks

    def chunk_body(ch):
        rows = slice(ch * TM_PROJ, (ch + 1) * TM_PROJ)
        xn = _rms(h_ref[rows, :], g_ref[...])
        hi = xn.astype(BF16)
        lo = (xn - hi.astype(F32)).astype(BF16)
        xn_ref[rows, :] = hi
        yield
        whi = whi_ref[...]
        logits = (jnp.dot(hi, whi, preferred_element_type=F32) + jnp.dot(lo, whi, preferred_element_type=F32)
                  + jnp.dot(hi, wlo_ref[...], preferred_element_type=F32))
        logits = jnp.where(lane < N_EXPERTS, logits, NEG)
        yield
        m1 = jnp.max(logits, axis=-1, keepdims=True)
        yield
        i1 = jnp.min(jnp.where(logits == m1, lane, float(LANE)), axis=-1, keepdims=True)
        yield
        rest = jnp.where(lane == i1, NEG, logits)
        m2 = jnp.max(rest, axis=-1, keepdims=True)
        yield
        i2 = jnp.min(jnp.where(rest == m2, lane, float(LANE)), axis=-1, keepdims=True)
        yield
        e2 = jnp.exp(m2 - m1)
        w1 = 1.0 / (1.0 + e2)
        w2 = e2 * w1
        gates = jnp.where(lane == i1, w1, jnp.where(lane == i2, w2, 0.0))
        gatest_ref[:, rows] = gates.T[:N_EXPERTS, :]
        routed[ch] = (lane == i1) | (lane == i2)
        oh = jnp.where(routed[ch], 1.0, 0.0)
        yield
        prefix[ch] = jnp.dot(before, oh.astype(BF16), preferred_element_type=F32)
        count[ch] = jnp.sum(oh, axis=0, keepdims=True)

    _interleave(chunk_body(ch) for ch in range(n_chunks))
    run = run_ref[0:1, :]
    for ch in range(n_chunks):
        rows = slice(ch * TM_PROJ, (ch + 1) * TM_PROJ)
        starts_ref[8 * ch:8 * ch + 8, :] = jnp.broadcast_to(run, (8, LANE))
        rank = jnp.where(routed[ch], run + prefix[ch], -1.0)
        rank_ref[rows, :] = rank
        rankt_ref[:, rows] = rank.T[:N_EXPERTS, :]
        run = run + count[ch]
    total = jnp.broadcast_to(run, run_ref.shape)
    run_ref[...] = total
    counts_ref[...] = total


def _router(h2d, ln, whi, wlo):
    t = h2d.shape[0]
    tm = min(TM_ROUTER, t)
    nt = t // tm
    chunk_rows = 8 * (tm // TM_PROJ)
    row = lambda n: pl.BlockSpec((tm, n), lambda i: (i, 0))
    return pl.pallas_call(
        _router_kernel,
        out_shape=(jax.ShapeDtypeStruct((t, D_MODEL), BF16), jax.ShapeDtypeStruct((N_EXPERTS, t), F32),
                   jax.ShapeDtypeStruct((t, LANE), F32), jax.ShapeDtypeStruct((N_EXPERTS, t), F32),
                   jax.ShapeDtypeStruct((nt * chunk_rows, LANE), F32), jax.ShapeDtypeStruct((8, LANE), F32)),
        grid=(nt,),
        in_specs=[row(D_MODEL), _const_spec((1, D_MODEL)), _const_spec(whi.shape), _const_spec(wlo.shape)],
        out_specs=(row(D_MODEL), pl.BlockSpec((N_EXPERTS, tm), lambda i: (0, i)), row(LANE),
                   pl.BlockSpec((N_EXPERTS, tm), lambda i: (0, i)),
                   pl.BlockSpec((chunk_rows, LANE), lambda i: (i, 0)), _const_spec((8, LANE))),
        scratch_shapes=[pltpu.VMEM((8, LANE), F32)],
        compiler_params=_cparams(("arbitrary",)),
        name="moe_router",
    )(h2d, ln, whi, wlo)


def _dispatch_kernel(ge_ref, gr0_ref, gstart_ref, gn_ref, xn_ref, rankt_ref, gatest_ref, xs_ref, gs_ref,
                     acc_ref, gacc_ref):
    groups = [pl.program_id(0) * DISPATCH_GROUPS + u for u in range(DISPATCH_GROUPS)]
    eye = (lax.broadcasted_iota(jnp.int32, (SLOT_GROUP, SLOT_GROUP), 0)
           == lax.broadcasted_iota(jnp.int32, (SLOT_GROUP, SLOT_GROUP), 1))

    def gathered(g, start):
        start = pl.multiple_of(start, TM_PROJ)
        want = (gr0_ref[g] + lax.broadcasted_iota(jnp.int32, (SLOT_GROUP, DISPATCH_K), 0)).astype(F32)
        span = (pl.ds(ge_ref[g], 1), pl.ds(start, DISPATCH_K))
        hit = rankt_ref[span] == want
        rows = jnp.dot(jnp.where(hit, 1.0, 0.0).astype(BF16), xn_ref[pl.ds(start, DISPATCH_K), :],
                       preferred_element_type=F32)
        gate = jnp.sum(jnp.where(hit, gatest_ref[span], 0.0), axis=-1, keepdims=True)
        return rows, gate

    def as_row(col):
        return jnp.sum(jnp.where(eye, col, 0.0), axis=0, keepdims=True)

    single = gn_ref[groups[0]] <= 1
    for g in groups[1:]:
        single = single & (gn_ref[g] <= 1)

    @pl.when(single)
    def _():
        gate_rows = [None] * DISPATCH_GROUPS

        def group_body(u, g):
            rows, gate = gathered(g, gstart_ref[g])
            yield
            xs_ref[u * SLOT_GROUP:(u + 1) * SLOT_GROUP, :] = rows.astype(BF16)
            gate_rows[u] = as_row(gate)

        _interleave(group_body(u, g) for u, g in enumerate(groups))
        gs_ref[...] = jnp.broadcast_to(jnp.concatenate(gate_rows, axis=1), gs_ref.shape)

    @pl.when(jnp.logical_not(single))
    def _():
        gate_rows = []
        for u, g in enumerate(groups):
            acc_ref[...] = jnp.zeros_like(acc_ref)
            gacc_ref[...] = jnp.zeros_like(gacc_ref)

            def body(k, carry, g=g):
                rows, gate = gathered(g, gstart_ref[g] + k * DISPATCH_K)
                acc_ref[...] += rows
                gacc_ref[...] += gate
                return carry

            lax.fori_loop(0, gn_ref[g], body, 0)
            xs_ref[u * SLOT_GROUP:(u + 1) * SLOT_GROUP, :] = acc_ref[...].astype(BF16)
            gate_rows.append(as_row(gacc_ref[...]))
        gs_ref[...] = jnp.broadcast_to(jnp.concatenate(gate_rows, axis=1), gs_ref.shape)


def _dispatch(xn, rankt, gatest, g_expert, g_rank0, g_start, g_nspans):
    t = xn.shape[0]
    n_steps = g_expert.shape[0] // DISPATCH_GROUPS
    rows = DISPATCH_GROUPS * SLOT_GROUP
    assert rows == TM_MOE
    res = lambda shape: pl.BlockSpec(shape, lambda g, *_: (0, 0), pipeline_mode=pl.Buffered(1))
    grid_spec = pltpu.PrefetchScalarGridSpec(
        num_scalar_prefetch=4,
        grid=(n_steps,),
        in_specs=[res((t, D_MODEL)), res((N_EXPERTS, t)), res((N_EXPERTS, t))],
        out_specs=(pl.BlockSpec((rows, D_MODEL), lambda g, *_: (g, 0)),
                   pl.BlockSpec((8, rows), lambda g, *_: (g, 0))),
        scratch_shapes=[pltpu.VMEM((SLOT_GROUP, D_MODEL), F32), pltpu.VMEM((SLOT_GROUP, 1), F32)],
    )
    return pl.pallas_call(
        _dispatch_kernel,
        out_shape=(jax.ShapeDtypeStruct((n_steps * rows, D_MODEL), BF16),
                   jax.ShapeDtypeStruct((n_steps * 8, rows), F32)),
        grid_spec=grid_spec,
        compiler_params=_cparams(("arbitrary",)),
        name="moe_dispatch",
    )(g_expert, g_rank0, g_start, g_nspans, xn, rankt, gatest)


def _expert_kernel(te_ref, ta_ref, x_ref, gs_ref, wg_ref, wu_ref, wd_ref, out_ref):
    active = ta_ref[pl.program_id(0)] > 0

    @pl.when(active)
    def _():
        x = x_ref[...]
        acc = None
        for c in range(D_FF // FF_CHUNK):
            cs = slice(c * FF_CHUNK, (c + 1) * FF_CHUNK)
            gt = jnp.dot(x, wg_ref[0, :, cs], preferred_element_type=F32)
            up = jnp.dot(x, wu_ref[0, :, cs], preferred_element_type=F32)
            a = (gt * jax.nn.sigmoid(gt) * up).astype(BF16)
            y = jnp.dot(a, wd_ref[0, cs, :], preferred_element_type=F32)
            acc = y if acc is None else acc + y
        tm = x.shape[0]
        eye = lax.broadcasted_iota(jnp.int32, (tm, tm), 0) == lax.broadcasted_iota(jnp.int32, (tm, tm), 1)
        gate = jnp.sum(jnp.where(eye, gs_ref[0:1, :], 0.0), axis=-1, keepdims=True)
        out_ref[...] = (acc * gate).astype(out_ref.dtype)

    @pl.when(jnp.logical_not(active))
    def _():
        out_ref[...] = jnp.zeros_like(out_ref)


def _experts(x, gate_rows, tile_expert, tile_active, wg, wu, wd):
    n_tiles = tile_expert.shape[0]
    tm = TM_MOE
    wmap = lambda j, te, ta: (te[j], 0, 0)
    tile = lambda j, te, ta: (jnp.where(ta[j] > 0, j, 0), 0)
    grid_spec = pltpu.PrefetchScalarGridSpec(
        num_scalar_prefetch=2,
        grid=(n_tiles,),
        in_specs=[pl.BlockSpec((tm, D_MODEL), tile),
                  pl.BlockSpec((8, tm), tile),
                  pl.BlockSpec((1, D_MODEL, D_FF), wmap),
                  pl.BlockSpec((1, D_MODEL, D_FF), wmap),
                  pl.BlockSpec((1, D_FF, D_MODEL), wmap)],
        out_specs=pl.BlockSpec((tm, D_MODEL), lambda j, te, ta: (j, 0)),
    )
    return pl.pallas_call(
        _expert_kernel,
        out_shape=jax.ShapeDtypeStruct((n_tiles * tm, D_MODEL), BF16),
        grid_spec=grid_spec,
        compiler_params=_cparams(("arbitrary",), EXPERT_VMEM_LIMIT),
        name="moe_experts",
    )(tile_expert, tile_active, x, gate_rows, wg, wu, wd)


def _combine_window_copies(ys_ref, buf_ref, sem_ref, ws_ref, tile, buf_slot, row_offset):
    copies = []
    for e in range(N_EXPERTS):
        start = pl.multiple_of(ws_ref[tile * N_EXPERTS + e] + row_offset, SLOT_ALIGN)
        copies.append(pltpu.make_async_copy(ys_ref.at[pl.ds(start, COMBINE_WIN), :],
                                            buf_ref.at[buf_slot, pl.ds(e * COMBINE_WIN, COMBINE_WIN), :],
                                            sem_ref.at[buf_slot, e]))
    return copies


def _combine_kernel(ws_ref, second_ref, h_ref, rank_ref, base_ref, ys_ref, out_ref, buf_ref, sem_ref):
    i = pl.program_id(0)
    n = pl.num_programs(0)
    tm = h_ref.shape[0]

    @pl.when(i == 0)
    def _():
        for cp in _combine_window_copies(ys_ref, buf_ref, sem_ref, ws_ref, 0, 0, 0):
            cp.start()

    @pl.when(i + 1 < n)
    def _():
        for cp in _combine_window_copies(ys_ref, buf_ref, sem_ref, ws_ref, i + 1, (i + 1) % 2, 0):
            cp.start()

    rank = rank_ref[...]
    slots = jnp.where(rank >= 0.0, rank + base_ref[...], -1.0)
    lane_iota = lax.broadcasted_iota(jnp.int32, (tm, COMBINE_WIN), 1)

    def gathered(buf_slot, row_offset):
        parts = []
        for e in range(N_EXPERTS):
            first = ws_ref[i * N_EXPERTS + e] + row_offset
            hit = slots[:, e:e + 1] == (first + lane_iota).astype(F32)
            parts.append(jnp.where(hit, 1.0, 0.0).astype(BF16))
        return jnp.dot(jnp.concatenate(parts, axis=1), buf_ref[buf_slot], preferred_element_type=F32)

    for cp in _combine_window_copies(ys_ref, buf_ref, sem_ref, ws_ref, i, i % 2, 0):
        cp.wait()
    out_ref[...] = h_ref[...] + gathered(i % 2, 0)

    for extra in range(1, COMBINE_MAX_WINDOWS):
        @pl.when(second_ref[i] > extra)
        def _(extra=extra):
            copies = _combine_window_copies(ys_ref, buf_ref, sem_ref, ws_ref, i, 2, extra * COMBINE_WIN)
            for cp in copies:
                cp.start()
            for cp in copies:
                cp.wait()
            out_ref[...] += gathered(2, extra * COMBINE_WIN)


def _combine(h2d, rank, base_row, ys, win_start, tile_second):
    t = h2d.shape[0]
    tm = TM_PROJ
    row = lambda n: pl.BlockSpec((tm, n), lambda i, *_: (i, 0))
    grid_spec = pltpu.PrefetchScalarGridSpec(
        num_scalar_prefetch=2,
        grid=(t // tm,),
        in_specs=[row(D_MODEL), row(LANE), pl.BlockSpec((1, LANE), lambda i, *_: (0, 0)),
                  pl.BlockSpec(memory_space=pl.ANY)],
        out_specs=row(D_MODEL),
        scratch_shapes=[pltpu.VMEM((3, N_EXPERTS * COMBINE_WIN, D_MODEL), BF16),
                        pltpu.SemaphoreType.DMA((3, N_EXPERTS))],
    )
    return pl.pallas_call(
        _combine_kernel,
        out_shape=jax.ShapeDtypeStruct((t, D_MODEL), F32),
        grid_spec=grid_spec,
        compiler_params=_cparams(("arbitrary",)),
        name="moe_combine",
    )(win_start, tile_second, h2d, rank, base_row, ys)


def _moe_plan(counts, starts, t):
    n_tok_tiles = t // TM_PROJ
    n_tiles = 2 * t // TM_MOE + N_EXPERTS + 1
    n_slots = n_tiles * TM_MOE
    counts = counts.astype(jnp.int32)
    cum = jnp.concatenate([starts.astype(jnp.int32), counts[None, :]], axis=0)
    padded = (counts + TM_MOE - 1) // TM_MOE * TM_MOE
    ends = jnp.cumsum(padded)
    base = ends - padded
    total = ends[-1]
    tile_first = jnp.arange(n_tiles, dtype=jnp.int32) * TM_MOE
    tile_expert = jnp.minimum(jnp.sum(ends[None, :] <= tile_first[:, None], axis=1), N_EXPERTS - 1).astype(jnp.int32)
    tile_active = (tile_first < total).astype(jnp.int32)
    n_groups = n_slots // SLOT_GROUP
    g_first = jnp.arange(n_groups, dtype=jnp.int32) * SLOT_GROUP
    g_expert = tile_expert[g_first // TM_MOE]
    g_r0 = g_first - base[g_expert]
    cum_e = cum.T[g_expert]
    lo = jnp.sum(cum_e <= g_r0[:, None], axis=1) - 1
    hi = jnp.sum(cum_e < (g_r0 + SLOT_GROUP)[:, None], axis=1)
    g_valid = (g_first < total) & (g_r0 < counts[g_expert])
    lo = jnp.clip(lo, 0, n_tok_tiles - 1)
    hi = jnp.clip(hi, 0, n_tok_tiles)
    g_n = jnp.where(g_valid, (jnp.maximum(hi - lo, 0) * TM_PROJ + DISPATCH_K - 1) // DISPATCH_K, 0)
    lo = jnp.minimum(lo * TM_PROJ, t - jnp.maximum(g_n, 1) * DISPATCH_K)
    first = base[None, :] + cum[:-1]
    cnt = cum[1:] - cum[:-1]
    ws = first // SLOT_ALIGN * SLOT_ALIGN
    pieces = jnp.where(cnt > 0, (first + cnt - ws + COMBINE_WIN - 1) // COMBINE_WIN, 0)
    return dict(tile_expert=tile_expert, tile_active=tile_active, g_expert=g_expert, g_r0=g_r0.astype(jnp.int32),
                g_lo=lo.astype(jnp.int32), g_n=g_n.astype(jnp.int32), base=base,
                ws=ws.reshape(-1).astype(jnp.int32), second=jnp.max(pieces, axis=1).astype(jnp.int32))


def _pad_heads_cols(w, heads, dim, pad):
    lead = w.shape[:-1]
    w = w.reshape(lead + (heads, dim))
    w = jnp.pad(w, [(0, 0)] * len(lead) + [(0, 0), (0, pad - dim)])
    return w.reshape(lead + (heads * pad,))


def _row(v):
    return v.reshape(1, -1).astype(F32)


def kernel(x, mem, positions, mem_norm, w_mem_kv, ln_mix0, w_in0, conv_w0, b_i0, b_f0, h_norm0, mq_norm0, mk_norm0, w_out0, ln_ffn0, w_gate0, w_up0, w_down0, ln_mix1, w_in1, cq_norm1, ckv_norm1, w_uq1, w_ukv1, q_norm1, k_norm1, mq_norm1, mk_norm1, w_out1, ln_ffn1, w_router1, we_gate1, we_up1, we_down1):
    batch, seq, _ = x.shape
    t = batch * seq
    h = x.reshape(t, D_MODEL)
    tile4 = lambda g: _row(jnp.tile(g, MEM_HEADS))

    k4_0, k4_1, v4 = _memkv(mem.reshape(batch * MEM_TOKENS, D_MODEL), _row(mem_norm), w_mem_kv.astype(BF16),
                            tile4(mk_norm0), tile4(mk_norm1), batch)

    mw = MAIN_WIDTH
    w_q, w_k, w_v, w_o = (w_in0[:, i * mw:(i + 1) * mw] for i in range(4))
    w_gates = w_in0[:, 4 * mw:4 * mw + 2 * ML_HEADS]
    w_mq = w_in0[:, 4 * mw + 2 * ML_HEADS:]
    padh = lambda w: _pad_heads_cols(w, ML_HEADS, ML_HEAD_DIM, HEAD_PAD)
    wqk = jnp.concatenate([padh(w_q), padh(w_k)], axis=1).astype(BF16)
    wvo = jnp.concatenate([padh(w_v), padh(w_o)], axis=1).astype(BF16)
    wg = jnp.pad(w_gates, ((0, 0), (0, GATE_PAD - 2 * ML_HEADS))).astype(BF16)
    wgt = w_gates.T.astype(BF16)
    cw = jnp.concatenate([padh(conv_w0[:, :mw]), padh(conv_w0[:, mw:])], axis=1).astype(F32)
    bias = jnp.concatenate([b_i0, b_f0]).astype(F32)
    bias_row = jnp.pad(bias, (0, GATE_PAD - 2 * ML_HEADS)).reshape(1, GATE_PAD)
    bias_col = bias.reshape(2 * ML_HEADS, 1)
    q0, kt0, v0, so, mq0, gates, gatest = _in0(h, _row(ln_mix0), wqk, wvo, w_mq.astype(BF16), wg, wgt, cw,
                                               bias_row, bias_col, seq)
    hgain = jnp.pad(h_norm0.astype(F32), (0, HEAD_PAD - ML_HEAD_DIM)).reshape(1, HEAD_PAD)
    ffn_groups = lambda w: w.reshape(N_EXPERTS, w.shape[0] // N_EXPERTS, w.shape[1])
    hn, (wg0, wu0, wd0) = _mlstm(q0, kt0, v0, so, gates, gatest, hgain,
                                 [ffn_groups(w_gate0), ffn_groups(w_up0), ffn_groups(w_down0)], batch, seq)
    wm0 = jnp.pad(w_out0[:mw].reshape(ML_HEADS, ML_HEAD_DIM, D_MODEL),
                  ((0, 0), (0, HEAD_PAD - ML_HEAD_DIM), (0, 0))).reshape(ML_PAD, D_MODEL).astype(BF16)
    h, we_gate_bf = _out_proj(hn, mq0, h, k4_0, v4, tile4(mq_norm0), wm0, w_out0[mw:].astype(BF16), seq,
                              ffn=(_row(ln_ffn0), wg0.reshape(w_gate0.shape), wu0.reshape(w_up0.shape),
                                   wd0.reshape(w_down0.shape), we_gate1))

    w_cq = w_in1[:, :Q_LORA]
    w_ckv = w_in1[:, Q_LORA:Q_LORA + KV_LORA]
    w_kr = jnp.pad(w_in1[:, Q_LORA + KV_LORA:Q_LORA + KV_LORA + MLA_ROPE], ((0, 0), (0, LANE - MLA_ROPE)))
    w_mq1 = w_in1[:, Q_LORA + KV_LORA + MLA_ROPE:]
    w1 = jnp.concatenate([w_cq, w_ckv, w_kr, w_mq1], axis=1).astype(BF16)
    wuq = _pad_heads_cols(w_uq1, MLA_HEADS, MLA_QK, HEAD_PAD).astype(BF16)
    wukv = w_ukv1.reshape(KV_LORA, MLA_HEADS, MLA_NOPE + MLA_V)
    wuk = wukv[:, :, :MLA_NOPE].reshape(KV_LORA, MLA_HEADS * MLA_NOPE).astype(BF16)
    wuvt = wukv[:, :, MLA_NOPE:].reshape(KV_LORA, MLA_HEADS * MLA_V).T.astype(BF16)
    gq = jnp.pad(q_norm1.astype(F32), (0, HEAD_PAD - MLA_QK)).reshape(1, HEAD_PAD)
    gkn = _row(k_norm1[:MLA_NOPE])
    gkr = jnp.pad(k_norm1[MLA_NOPE:].astype(F32), (0, LANE - MLA_ROPE)).reshape(1, LANE)
    half = MLA_ROPE // 2
    inv_freq = ROPE_THETA ** (-jnp.arange(half, dtype=F32) / half)
    invf = jnp.concatenate([inv_freq, inv_freq, jnp.zeros((LANE - MLA_ROPE,), F32)]).reshape(1, LANE)
    q1, k1, v1, mq1 = _in1(h, positions.reshape(1, t).astype(jnp.int32), _row(ln_mix1), w1, _row(cq_norm1),
                           _row(ckv_norm1), wuq, wuk, wuvt, gq, gkn, gkr, invf)
    o1, we_up_bf, we_down_bf = _attention(q1, k1, v1, we_up1, we_down1, batch, seq)
    h = _out_proj(o1, mq1, h, k4_1, v4, tile4(mq_norm1), w_out1[:mw].astype(BF16), w_out1[mw:].astype(BF16), seq)

    wr = jnp.pad(w_router1.astype(F32), ((0, 0), (0, LANE - N_EXPERTS)))
    wr_hi = wr.astype(BF16)
    wr_lo = (wr - wr_hi.astype(F32)).astype(BF16)
    xn, gatest, rank, rankt, starts, counts = _router(h, _row(ln_ffn1), wr_hi, wr_lo)
    plan = _moe_plan(counts[0, :N_EXPERTS], starts[::8, :N_EXPERTS], t)
    xs, gate_rows = _dispatch(xn, rankt, gatest, plan["g_expert"], plan["g_r0"], plan["g_lo"], plan["g_n"])
    ys = _experts(xs, gate_rows, plan["tile_expert"], plan["tile_active"], we_gate_bf, we_up_bf, we_down_bf)
    base_row = jnp.pad(plan["base"].astype(F32), (0, LANE - N_EXPERTS)).reshape(1, LANE)
    h = _combine(h, rank, base_row, ys, plan["ws"], plan["second"])
    return h.reshape(batch, seq, D_MODEL)
```

```python
import functools

import jax
import jax.numpy as jnp
import numpy as np
from jax import lax
from jax.experimental import pallas as pl
from jax.experimental.pallas import tpu as pltpu

F32 = jnp.float32
BF16 = jnp.bfloat16

D_MODEL = 1024
MEM_TOKENS = 256
MEM_HEADS = 4
MEM_WIDTH = 256
MEM_HEAD_DIM = 64
MAIN_WIDTH = 768
ML_HEADS = 4
ML_HEAD_DIM = 192
CONV_WIDTH = 4
MLA_HEADS = 6
MLA_NOPE = 128
MLA_ROPE = 64
MLA_QK = 192
MLA_V = 128
Q_LORA = 384
KV_LORA = 128
ROPE_THETA = 10000.0
D_FF = 3584
N_EXPERTS = 8
EPS = 1e-6

LANE = 128
HEAD_PAD = 256
ML_PAD = ML_HEADS * HEAD_PAD
MLA_PAD = MLA_HEADS * HEAD_PAD
GATE_PAD = 128
ML_CHUNK = 256
TM_PROJ = 256
TM_ROUTER = 2048
TM_IN0 = 1024
TM_IN1 = 1024
TM_OUT = 1024
TM_FFN = 512
FF_CHUNK = 512
TM_MOE = 512
SLOT_GROUP = 128
DISPATCH_K = 1024
DISPATCH_GROUPS = 4
CAST_PARTS = 4
IN0_ROW_SPLIT = 4
IN1_ROW_SPLIT = 4
SLOT_ALIGN = 16
COMBINE_WIN = 128
COMBINE_AHEAD = 2
COMBINE_MAX_WINDOWS = -(-(TM_PROJ + SLOT_ALIGN - 1) // COMBINE_WIN)
TQ = 2048
TK = 512
NEG = -1e30
LOG2E = 1.4426950408889634
VMEM_LIMIT = 56 * 1024 * 1024
EXPERT_VMEM_LIMIT = 62 * 1024 * 1024


def _cparams(sem, vmem=VMEM_LIMIT):
    return pltpu.CompilerParams(dimension_semantics=sem, vmem_limit_bytes=vmem)


def _rms(x, g):
    return x * lax.rsqrt(jnp.mean(x * x, axis=-1, keepdims=True) + EPS) * g


def _split_dot(x, w_bf16):
    hi = x.astype(BF16)
    lo = (x - hi.astype(F32)).astype(BF16)
    return (jnp.dot(hi, w_bf16, preferred_element_type=F32)
            + jnp.dot(lo, w_bf16, preferred_element_type=F32))


def _block_diag_ones(n, blk):
    r = lax.broadcasted_iota(jnp.int32, (n, n), 0) // blk
    c = lax.broadcasted_iota(jnp.int32, (n, n), 1) // blk
    return jnp.where(r == c, 1.0, 0.0).astype(BF16)


def _const_spec(shape):
    nd = len(shape)
    return pl.BlockSpec(shape, lambda *_: (0,) * nd)


def _interleave(stage_generators):
    live = list(stage_generators)
    while live:
        for gen in list(live):
            try:
                next(gen)
            except StopIteration:
                live.remove(gen)


def _memkv_kernel(mem_ref, g_ref, w_ref, gk0_ref, gk1_ref, k0_ref, k1_ref, v_ref):
    xn = _rms(mem_ref[...], g_ref[...]).astype(BF16)
    kv = jnp.dot(xn, w_ref[...], preferred_element_type=F32)
    k = kv[:, :MEM_WIDTH]
    v = kv[:, MEM_WIDTH:]
    bd = _block_diag_ones(MEM_WIDTH, MEM_HEAD_DIM)
    ms = _split_dot(k * k, bd) * (1.0 / MEM_HEAD_DIM)
    kn = k * lax.rsqrt(ms + EPS)
    lane_head = lax.broadcasted_iota(jnp.int32, (MEM_TOKENS, MEM_WIDTH), 1) // MEM_HEAD_DIM
    for h in range(MEM_HEADS):
        sel = lane_head == h
        k0_ref[h] = jnp.where(sel, kn * gk0_ref[...], 0.0).astype(BF16)
        k1_ref[h] = jnp.where(sel, kn * gk1_ref[...], 0.0).astype(BF16)
        v_ref[h] = jnp.where(sel, v, 0.0).astype(BF16)


def _memkv(mem2d, mem_norm, w_kv, gk0, gk1, batch):
    out = jax.ShapeDtypeStruct((batch * MEM_HEADS, MEM_TOKENS, MEM_WIDTH), BF16)
    hspec = pl.BlockSpec((MEM_HEADS, MEM_TOKENS, MEM_WIDTH), lambda b: (b, 0, 0))
    return pl.pallas_call(
        _memkv_kernel,
        out_shape=(out, out, out),
        grid=(batch,),
        in_specs=[pl.BlockSpec((MEM_TOKENS, D_MODEL), lambda b: (b, 0)),
                  _const_spec((1, D_MODEL)), _const_spec((D_MODEL, 2 * MEM_WIDTH)),
                  _const_spec((1, MEM_WIDTH)), _const_spec((1, MEM_WIDTH))],
        out_specs=(hspec, hspec, hspec),
        compiler_params=_cparams(("parallel",)),
        name="memkv",
    )(mem2d, mem_norm, w_kv, gk0, gk1)


def _in0_kernel(tiles_per_seq, h_ref, hp_ref, g_ref, wqk_ref, wvo_ref, wmq_ref, wg_ref, wgt_ref,
                cw_ref, bias_ref, biast_ref,
                q_ref, kt_ref, v_ref, so_ref, mq_ref, gates_ref, gatest_ref):
    i = pl.program_id(0)
    tm = h_ref.shape[0] // IN0_ROW_SPLIT
    g = g_ref[...]
    first = (i % tiles_per_seq) == 0
    row8 = lax.broadcasted_iota(jnp.int32, (8, FF_CHUNK), 0)
    ncol = wqk_ref.shape[1]

    def gate_act(z, is_input_gate):
        logsig = jnp.minimum(z, 0.0) - jnp.log(1.0 + jnp.exp(-jnp.abs(z)))
        return jnp.where(is_input_gate, z, logsig)

    def rows_body(r):
        rows = slice(r * tm, (r + 1) * tm)
        xn = _rms(h_ref[rows, :], g).astype(BF16)
        prev_rows = hp_ref[...] if r == 0 else h_ref[r * tm - 8:r * tm, :]
        xpn = _rms(prev_rows, g).astype(BF16)
        yield
        for c in range(ncol // FF_CHUNK):
            cs = slice(c * FF_CHUNK, (c + 1) * FF_CHUNK)
            w = wqk_ref[:, cs]
            u = jnp.dot(xn, w, preferred_element_type=F32)
            up = jnp.dot(xpn, w, preferred_element_type=F32)
            if r == 0:
                up = jnp.where(first, 0.0, up)
            yield
            acc = u * cw_ref[CONV_WIDTH - 1:CONV_WIDTH, cs]
            for k in range(1, CONV_WIDTH):
                rolled = pltpu.roll(u, k, 0)
                prev = pltpu.roll(up, k, 0)
                head = jnp.where(row8 < k, prev, rolled[:8])
                shifted = jnp.concatenate([head, rolled[8:]], axis=0)
                acc = acc + shifted * cw_ref[CONV_WIDTH - 1 - k:CONV_WIDTH - k, cs]
            y = acc * jax.nn.sigmoid(acc)
            if c < ML_PAD // FF_CHUNK:
                q_ref[rows, cs] = (y * (ML_HEAD_DIM ** -0.5)).astype(BF16)
            else:
                ks = slice(c * FF_CHUNK - ML_PAD, (c + 1) * FF_CHUNK - ML_PAD)
                kt_ref[ks, rows] = y.T.astype(BF16)
            yield
        zvo = jnp.dot(xn, wvo_ref[...], preferred_element_type=F32)
        yield
        v_ref[rows, :] = zvo[:, :ML_PAD].astype(BF16)
        so_ref[rows, :] = jax.nn.sigmoid(zvo[:, ML_PAD:]).astype(BF16)
        yield
        mq_ref[rows, :] = jnp.dot(xn, wmq_ref[...], preferred_element_type=F32).astype(BF16)
        zg = jnp.dot(xn, wg_ref[...], preferred_element_type=F32) + bias_ref[...]
        lane = lax.broadcasted_iota(jnp.int32, (tm, GATE_PAD), 1)
        gates_ref[rows, :] = gate_act(zg, lane < ML_HEADS)
        zgt = lax.dot_general(wgt_ref[...], xn, (((1,), (1,)), ((), ())),
                              preferred_element_type=F32) + biast_ref[...]
        row = lax.broadcasted_iota(jnp.int32, (8, tm), 0)
        gatest_ref[:, rows] = gate_act(zgt, row < ML_HEADS)

    _interleave(rows_body(r) for r in range(IN0_ROW_SPLIT))


def _in0(h2d, ln, wqk, wvo, wmq, wg, wgt, cw, bias, biast, seq):
    t = h2d.shape[0]
    tm = TM_IN0
    tiles_per_seq = seq // tm
    tok = lambda n, dt: jax.ShapeDtypeStruct((t, n), dt)
    row = lambda n: pl.BlockSpec((tm, n), lambda i: (i, 0))
    col = lambda n: pl.BlockSpec((n, tm), lambda i: (0, i))
    return pl.pallas_call(
        functools.partial(_in0_kernel, tiles_per_seq),
        out_shape=(tok(ML_PAD, BF16), jax.ShapeDtypeStruct((ML_PAD, t), BF16), tok(ML_PAD, BF16),
                   tok(ML_PAD, BF16), tok(MEM_WIDTH, BF16), tok(GATE_PAD, F32), jax.ShapeDtypeStruct((8, t), F32)),
        grid=(t // tm,),
        in_specs=[row(D_MODEL),
                  pl.BlockSpec((8, D_MODEL), lambda i: (jnp.maximum(i * (tm // 8) - 1, 0), 0)),
                  _const_spec((1, D_MODEL)), _const_spec(wqk.shape), _const_spec(wvo.shape),
                  _const_spec(wmq.shape), _const_spec(wg.shape), _const_spec(wgt.shape),
                  _const_spec(cw.shape), _const_spec(bias.shape), _const_spec(biast.shape)],
        out_specs=(row(ML_PAD), col(ML_PAD), row(ML_PAD), row(ML_PAD), row(MEM_WIDTH), row(GATE_PAD), col(8)),
        compiler_params=_cparams(("parallel",)),
        name="in_proj0",
    )(h2d, h2d, ln, wqk, wvo, wmq, wg, wgt, cw, bias, biast)


def _mlstm_kernel(batch, cast_blocks, q_ref, v_ref, so_ref, gates_ref, *rest):
    n_cast = len(cast_blocks)
    kt_refs, gatest_refs = rest[:batch], rest[batch:2 * batch]
    hg_ref = rest[2 * batch]
    cast_src = rest[2 * batch + 1:2 * batch + 1 + n_cast]
    out_ref = rest[2 * batch + 1 + n_cast]
    cast_dst = rest[2 * batch + 2 + n_cast:2 * batch + 2 + 2 * n_cast]
    c_ref, m_ref = rest[2 * batch + 2 + 2 * n_cast:]
    L = q_ref.shape[1]
    for src, dst, n_blocks in zip(cast_src, cast_dst, cast_blocks):
        _cast_block(pl.program_id(0), src, dst, n_blocks)

    @pl.when(pl.program_id(0) == 0)
    def _():
        c_ref[...] = jnp.zeros_like(c_ref)
        m_ref[...] = jnp.zeros_like(m_ref)

    r = lax.broadcasted_iota(jnp.int32, (L, L), 0)
    c = lax.broadcasted_iota(jnp.int32, (L, L), 1)
    causal = r >= c
    tri_lo = jnp.where(causal, 1.0, 0.0).astype(BF16)
    tri_up = jnp.where(r <= c, 1.0, 0.0).astype(BF16)
    lane = lax.broadcasted_iota(jnp.int32, (L, HEAD_PAD), 1)
    hg = hg_ref[...]
    def sequence_body(b):
        gts = gates_ref[b]
        gtt = gatest_refs[b][...]
        hi = gts.astype(BF16)
        lo = (gts - hi.astype(F32)).astype(BF16)
        b_cols = (jnp.dot(tri_lo, hi, preferred_element_type=F32)
                  + jnp.dot(tri_lo, lo, preferred_element_type=F32))
        b_rows = _split_dot(gtt, tri_up)
        yield
        heads = range(ML_HEADS)
        hs = [slice(h * HEAD_PAD, (h + 1) * HEAD_PAD) for h in heads]
        state = [b * ML_HEADS + h for h in heads]
        m_h = [m_ref[state[h]][0:1, 0:1] for h in heads]
        b_c = [b_cols[:, ML_HEADS + h:ML_HEADS + h + 1] for h in heads]
        c_r = [b_rows[ML_HEADS + h:ML_HEADS + h + 1, :] - gtt[h:h + 1, :] for h in heads]
        q_h = [q_ref[b, :, hs[h]] for h in heads]
        kt_h = [kt_refs[b][hs[h], :] for h in heads]
        s = [jnp.dot(q_h[h], kt_h[h], preferred_element_type=F32) for h in heads]
        c_old = [c_ref[state[h]] for h in heads]
        qc = [jnp.dot(q_h[h], c_old[h].astype(BF16), preferred_element_type=F32) for h in heads]
        yield
        log_d = [jnp.where(causal, b_c[h] - c_r[h], NEG) for h in heads]
        inter = [b_c[h] + m_h[h] for h in heads]
        m_row = [jnp.maximum(inter[h], jnp.max(log_d[h], axis=-1, keepdims=True)) for h in heads]
        yield
        d = [jnp.exp(log_d[h] - m_row[h]) for h in heads]
        s_inter = [jnp.exp(inter[h] - m_row[h]) for h in heads]
        v_aug = [jnp.where(lane == ML_HEAD_DIM, 1.0, v_ref[b, :, hs[h]].astype(F32)).astype(BF16) for h in heads]
        qkd = [(s[h] * d[h]).astype(BF16) for h in heads]
        yield
        num = [jnp.dot(qkd[h], v_aug[h], preferred_element_type=F32) + s_inter[h] * qc[h] for h in heads]
        yield
        b_last = [b_c[h][L - 1:L, :] for h in heads]
        g_r = [b_last[h] - c_r[h] for h in heads]
        m_new = [jnp.maximum(b_last[h] + m_h[h], jnp.max(g_r[h], axis=-1, keepdims=True)) for h in heads]
        w_r = [jnp.exp(g_r[h] - m_new[h]) for h in heads]
        decay = [jnp.exp(b_last[h] + m_h[h] - m_new[h]) for h in heads]
        ktw = [(kt_h[h].astype(F32) * w_r[h]).astype(BF16) for h in heads]
        yield
        for h in heads:
            c_ref[state[h]] = decay[h] * c_old[h] + jnp.dot(ktw[h], v_aug[h], preferred_element_type=F32)
            m_ref[state[h]] = jnp.broadcast_to(m_new[h], m_ref.shape[1:])
        yield
        den = [jnp.sum(jnp.where(lane == ML_HEAD_DIM, num[h], 0.0), axis=-1, keepdims=True) for h in heads]
        ht = [num[h] * (1.0 / jnp.maximum(jnp.abs(den[h]), jnp.exp(-m_row[h]))) for h in heads]
        ht = [jnp.where(lane < ML_HEAD_DIM, ht[h], 0.0) for h in heads]
        ms = [jnp.sum(ht[h] * ht[h], axis=-1, keepdims=True) * (1.0 / ML_HEAD_DIM) for h in heads]
        yield
        for h in heads:
            hn = ht[h] * lax.rsqrt(ms[h] + EPS) * hg * so_ref[b, :, hs[h]].astype(F32)
            out_ref[b, :, hs[h]] = hn.astype(BF16)

    _interleave(sequence_body(b) for b in range(batch))


def _mlstm(q, kt, v, so, gates, gatest, hgain, casts, batch, seq):
    L = ML_CHUNK
    nc = seq // L
    seq3 = lambda a: a.reshape(batch, seq, a.shape[-1])
    row = lambda n: pl.BlockSpec((batch, L, n), lambda c: (0, c, 0))
    col = lambda rows: [pl.BlockSpec((rows, L), functools.partial(lambda b, c: (0, b * nc + c), b))
                        for b in range(batch)]
    cast_blocks = tuple(a.shape[0] * CAST_PARTS for a in casts)
    assert all(n <= nc for n in cast_blocks)
    cast_specs = [_cast_spec(a.shape, lambda c: c) for a in casts]
    outs = pl.pallas_call(
        functools.partial(_mlstm_kernel, batch, cast_blocks),
        out_shape=[jax.ShapeDtypeStruct((batch, seq, ML_PAD), BF16)]
                  + [jax.ShapeDtypeStruct(a.shape, BF16) for a in casts],
        grid=(nc,),
        in_specs=[row(ML_PAD), row(ML_PAD), row(ML_PAD), row(GATE_PAD)] + col(ML_PAD) + col(8)
                 + [_const_spec((1, HEAD_PAD))] + cast_specs,
        out_specs=[row(ML_PAD)] + cast_specs,
        scratch_shapes=[pltpu.VMEM((batch * ML_HEADS, HEAD_PAD, HEAD_PAD), F32),
                        pltpu.VMEM((batch * ML_HEADS, 8, LANE), F32)],
        compiler_params=_cparams(("arbitrary",)),
        name="mlstm",
    )(seq3(q), seq3(v), seq3(so), seq3(gates), *([kt] * batch), *([gatest] * batch), hgain, *casts)
    return outs[0].reshape(batch * seq, ML_PAD), outs[1:]


def _cast_spec(shape, step_of):
    e, r, c = shape
    n_blocks = e * CAST_PARTS

    def index_map(*grid_idx):
        s = jnp.minimum(step_of(*grid_idx), n_blocks - 1)
        return (s // CAST_PARTS, s % CAST_PARTS, 0)

    return pl.BlockSpec((1, r // CAST_PARTS, c), index_map)


def _cast_block(step, src_ref, dst_ref, n_blocks):
    @pl.when(step < n_blocks)
    def _():
        dst_ref[...] = src_ref[...].astype(dst_ref.dtype)


def _mixer_residual(main_ref, mq_ref, h_ref, k4_ref, v4_ref, gq_ref, wm_ref, wmem_ref):
    heads = range(MEM_HEADS)
    nt_dims = (((1,), (1,)), ((), ()))
    q = mq_ref[...].astype(F32)
    bd = _block_diag_ones(MEM_WIDTH, MEM_HEAD_DIM)
    ms = _split_dot(q * q, bd) * (1.0 / MEM_HEAD_DIM)
    qn = (q * lax.rsqrt(ms + EPS) * gq_ref[...] * (MEM_HEAD_DIM ** -0.5)).astype(BF16)
    ymem = jnp.zeros(q.shape, F32)
    for h in heads:
        s = lax.dot_general(qn, k4_ref[h], nt_dims, preferred_element_type=F32)
        e = jnp.exp(s - jnp.max(s, axis=-1, keepdims=True))
        inv = 1.0 / jnp.sum(e, axis=-1, keepdims=True)
        ymem = ymem + jnp.dot(e.astype(BF16), v4_ref[h], preferred_element_type=F32) * inv
    y = (jnp.dot(main_ref[...], wm_ref[...], preferred_element_type=F32)
         + jnp.dot(ymem.astype(BF16), wmem_ref[...], preferred_element_type=F32))
    return h_ref[...] + y


def _out_kernel(main_ref, mq_ref, h_ref, k4_ref, v4_ref, gq_ref, wm_ref, wmem_ref, out_ref):
    out_ref[...] = _mixer_residual(main_ref, mq_ref, h_ref, k4_ref, v4_ref, gq_ref, wm_ref, wmem_ref)


def _out_ffn_kernel(main_ref, mq_ref, h_ref, k4_ref, v4_ref, gq_ref, wm_ref, wmem_ref,
                    g_ref, wg_ref, wu_ref, wd_ref, cast_src_ref, out_ref, cast_dst_ref):
    _cast_block(pl.program_id(0), cast_src_ref, cast_dst_ref, N_EXPERTS * CAST_PARTS)
    x = _mixer_residual(main_ref, mq_ref, h_ref, k4_ref, v4_ref, gq_ref, wm_ref, wmem_ref)
    xn = _rms(x, g_ref[...]).astype(BF16)
    acc = x
    for c in range(D_FF // FF_CHUNK):
        cs = slice(c * FF_CHUNK, (c + 1) * FF_CHUNK)
        gt = jnp.dot(xn, wg_ref[:, cs], preferred_element_type=F32)
        up = jnp.dot(xn, wu_ref[:, cs], preferred_element_type=F32)
        a = (gt * jax.nn.sigmoid(gt) * up).astype(BF16)
        acc = acc + jnp.dot(a, wd_ref[cs, :], preferred_element_type=F32)
    out_ref[...] = acc


def _out_proj(main, mq, h2d, k4, v4, gq, wm, wmem, seq, ffn=None):
    t = h2d.shape[0]
    tm = TM_OUT if ffn is None else TM_FFN
    tiles_per_seq = seq // tm
    row = lambda n: pl.BlockSpec((tm, n), lambda i: (i, 0))
    memspec = pl.BlockSpec((MEM_HEADS, MEM_TOKENS, MEM_WIDTH), lambda i: (i // tiles_per_seq, 0, 0))
    in_specs = [row(main.shape[1]), row(MEM_WIDTH), row(D_MODEL), memspec, memspec,
                _const_spec((1, MEM_WIDTH)), _const_spec(wm.shape), _const_spec(wmem.shape)]
    if ffn is None:
        return pl.pallas_call(
            _out_kernel,
            out_shape=jax.ShapeDtypeStruct((t, D_MODEL), F32),
            grid=(t // tm,),
            in_specs=in_specs,
            out_specs=row(D_MODEL),
            compiler_params=_cparams(("parallel",)),
            name="out_proj",
        )(main, mq, h2d, k4, v4, gq, wm, wmem)
    ln, wg, wu, wd, cast_src = ffn
    assert t // tm >= cast_src.shape[0] * CAST_PARTS
    res = lambda shape: pl.BlockSpec(shape, lambda i: (0, 0), pipeline_mode=pl.Buffered(1))
    cast = _cast_spec(cast_src.shape, lambda i: i)
    return pl.pallas_call(
        _out_ffn_kernel,
        out_shape=(jax.ShapeDtypeStruct((t, D_MODEL), F32), jax.ShapeDtypeStruct(cast_src.shape, BF16)),
        grid=(t // tm,),
        in_specs=in_specs + [_const_spec((1, D_MODEL)), res(wg.shape), res(wu.shape), res(wd.shape), cast],
        out_specs=(row(D_MODEL), cast),
        compiler_params=_cparams(("arbitrary",)),
        name="out_proj_ffn",
    )(main, mq, h2d, k4, v4, gq, wm, wmem, ln, wg, wu, wd, cast_src)


def _in1_kernel(h_ref, pos_ref, g_ref, w1_ref, gcq_ref, gckv_ref, wuq_ref, wuk_ref, wuvt_ref,
                gq_ref, gkn_ref, gkr_ref, invf_ref,
                q_ref, k_ref, vt_ref, mq_ref):
    tm = h_ref.shape[0] // IN1_ROW_SPLIT
    half = MLA_ROPE // 2
    scale = MLA_QK ** -0.5 * LOG2E
    heads = range(MLA_HEADS)

    def rows_body(rows):
        xn = _rms(h_ref[rows, :], g_ref[...]).astype(BF16)
        yield
        z = jnp.dot(xn, w1_ref[...], preferred_element_type=F32)
        yield
        cq = z[:, :Q_LORA]
        ckv = z[:, Q_LORA:Q_LORA + KV_LORA]
        kr = z[:, Q_LORA + KV_LORA:Q_LORA + KV_LORA + LANE]
        mq_ref[rows, :] = z[:, Q_LORA + KV_LORA + LANE:].astype(BF16)
        cqn = _rms(cq, gcq_ref[...]).astype(BF16)
        ckvn = _rms(ckv, gckv_ref[...]).astype(BF16)
        yield
        q = jnp.dot(cqn, wuq_ref[...], preferred_element_type=F32)
        kn = jnp.dot(ckvn, wuk_ref[...], preferred_element_type=F32)
        vt_ref[:, rows] = lax.dot_general(wuvt_ref[...], ckvn, (((1,), (1,)), ((), ())),
                                          preferred_element_type=F32).astype(BF16)
        yield
        eye = lax.broadcasted_iota(jnp.int32, (tm, tm), 0) == lax.broadcasted_iota(jnp.int32, (tm, tm), 1)
        pos = jnp.sum(jnp.where(eye, pos_ref[:, rows].astype(F32), 0.0), axis=-1, keepdims=True)
        ang = pos * invf_ref[...]
        lane = lax.broadcasted_iota(jnp.int32, (tm, LANE), 1)
        cos = jnp.where(lane < MLA_ROPE, jnp.cos(ang), 0.0)
        sin = jnp.sin(ang)
        sin_hi = jnp.where((lane >= half) & (lane < MLA_ROPE), sin, 0.0)
        sin_lo = jnp.where(lane < half, -sin, 0.0)

        def rope(t):
            return t * cos + pltpu.roll(t, half, 1) * sin_hi + pltpu.roll(t, LANE - half, 1) * sin_lo

        yield
        qh = [q[:, h * HEAD_PAD:(h + 1) * HEAD_PAD] for h in heads]
        kh = [kn[:, h * MLA_NOPE:(h + 1) * MLA_NOPE] for h in heads]
        q_ss = [jnp.sum(qh[h] * qh[h], axis=-1, keepdims=True) for h in heads]
        k_ss = [jnp.sum(kh[h] * kh[h], axis=-1, keepdims=True) for h in heads]
        ss_r = jnp.sum(kr * kr, axis=-1, keepdims=True)
        yield
        krr = rope(kr * gkr_ref[...])
        q_rs = [lax.rsqrt(q_ss[h] * (1.0 / MLA_QK) + EPS) * scale for h in heads]
        k_rs = [lax.rsqrt((k_ss[h] + ss_r) * (1.0 / MLA_QK) + EPS) for h in heads]
        yield
        qn = [qh[h] * q_rs[h] * gq_ref[...] for h in heads]
        q_rot = [rope(qn[h][:, LANE:]) for h in heads]
        yield
        for h in heads:
            q_ref[rows, h * HEAD_PAD:h * HEAD_PAD + LANE] = qn[h][:, :LANE].astype(BF16)
            q_ref[rows, h * HEAD_PAD + LANE:(h + 1) * HEAD_PAD] = q_rot[h].astype(BF16)
        yield
        for h in heads:
            k_ref[rows, h * HEAD_PAD:h * HEAD_PAD + LANE] = (kh[h] * k_rs[h] * gkn_ref[...]).astype(BF16)
            k_ref[rows, h * HEAD_PAD + LANE:(h + 1) * HEAD_PAD] = (krr * k_rs[h]).astype(BF16)

    _interleave(rows_body(slice(r * tm, (r + 1) * tm)) for r in range(IN1_ROW_SPLIT))


def _in1(h2d, pos2d, ln, w1, gcq, gckv, wuq, wuk, wuvt, gq, gkn, gkr, invf):
    t = h2d.shape[0]
    tm = TM_IN1
    tok = lambda n: jax.ShapeDtypeStruct((t, n), BF16)
    row = lambda n: pl.BlockSpec((tm, n), lambda i: (i, 0))
    consts = [ln, w1, gcq, gckv, wuq, wuk, wuvt, gq, gkn, gkr, invf]
    return pl.pallas_call(
        _in1_kernel,
        out_shape=(tok(MLA_PAD), tok(MLA_PAD), jax.ShapeDtypeStruct((MAIN_WIDTH, t), BF16), tok(MEM_WIDTH)),
        grid=(t // tm,),
        in_specs=[row(D_MODEL), pl.BlockSpec((1, tm), lambda i: (0, i))] + [_const_spec(a.shape) for a in consts],
        out_specs=(row(MLA_PAD), row(MLA_PAD), pl.BlockSpec((MAIN_WIDTH, tm), lambda i: (0, i)), row(MEM_WIDTH)),
        compiler_params=_cparams(("parallel",)),
        name="in_proj1",
    )(h2d, pos2d, *consts)


def _attn_kernel(q_ref, k_ref, vt_ref, cast_a_ref, cast_b_ref, o_ref, cast_a_out_ref, cast_b_out_ref,
                 m_ref, l_ref, acc_ref, st_ref, mt_ref):
    qi = pl.program_id(2)
    step = (pl.program_id(0) * pl.num_programs(1) + pl.program_id(1)) * pl.num_programs(2) + qi
    _cast_block(step, cast_a_ref, cast_a_out_ref, N_EXPERTS * CAST_PARTS)
    _cast_block(step, cast_b_ref, cast_b_out_ref, N_EXPERTS * CAST_PARTS)
    m_ref[...] = jnp.full_like(m_ref, NEG)
    l_ref[...] = jnp.zeros_like(l_ref)
    acc_ref[...] = jnp.zeros_like(acc_ref)
    n_diag = TQ // TK
    n_full = qi * n_diag
    nt_dims = (((1,), (1,)), ((), ()))

    def scores(j, slot):
        koff = pl.multiple_of(j * TK, TK)
        st = lax.dot_general(k_ref[pl.ds(koff, TK), :], q_ref[...], nt_dims, preferred_element_type=F32)
        st_ref[slot] = st
        mt_ref[slot] = jnp.max(st, axis=0, keepdims=True)

    def absorb(st, m_tile, vt, cols):
        m_old = m_ref[:, cols]
        m_new = jnp.maximum(m_old, m_tile)
        alpha = jnp.exp2(m_old - m_new)
        e = jnp.exp2(st - m_new)
        l_ref[:, cols] = alpha * l_ref[:, cols] + jnp.sum(e, axis=0, keepdims=True)
        acc_ref[:, cols] = alpha * acc_ref[:, cols] + jnp.dot(vt, e.astype(BF16), preferred_element_type=F32)
        m_ref[:, cols] = m_new

    def finish(j, slot):
        koff = pl.multiple_of(j * TK, TK)
        absorb(st_ref[slot], mt_ref[slot], vt_ref[:, pl.ds(koff, TK)], slice(None))

    strips = [slice(u * TK, (u + 1) * TK) for u in range(n_diag)]

    def scores_stages(j, slot):
        koff = pl.multiple_of(j * TK, TK)
        for cols in strips:
            st = lax.dot_general(k_ref[pl.ds(koff, TK), :], q_ref[cols, :], nt_dims, preferred_element_type=F32)
            st_ref[slot, :, cols] = st
            mt_ref[slot, :, cols] = jnp.max(st, axis=0, keepdims=True)
            yield

    def finish_stages(j, slot):
        koff = pl.multiple_of(j * TK, TK)
        for cols in strips:
            absorb(st_ref[slot, :, cols], mt_ref[slot, :, cols], vt_ref[:, pl.ds(koff, TK)], cols)
            yield

    @pl.when(n_full > 0)
    def _():
        scores(0, 0)
        scores(1, 1)

        def body(p, carry):
            j = 4 * p
            finish(j, 0)
            _interleave([scores_stages(j + 2, 0), finish_stages(j + 1, 1)])
            _interleave([scores_stages(j + 3, 1), finish_stages(j + 2, 0)])
            _interleave([scores_stages(j + 4, 0), finish_stages(j + 3, 1)])
            scores(j + 5, 1)
            return carry

        lax.fori_loop(0, (n_full - 2) // 4, body, 0)
        finish(n_full - 4, 0)
        _interleave([scores_stages(n_full - 2, 0), finish_stages(n_full - 3, 1)])
        _interleave([scores_stages(n_full - 1, 1), finish_stages(n_full - 2, 0)])
        finish(n_full - 1, 1)

    kd0 = pl.multiple_of(n_full * TK, TK)

    def diagonal_block(u):
        nk = (u + 1) * TK
        cols = slice(u * TK, (u + 1) * TK)
        st = lax.dot_general(k_ref[pl.ds(kd0, nk), :], q_ref[cols, :], nt_dims, preferred_element_type=F32)
        yield
        kpos = lax.broadcasted_iota(jnp.int32, st.shape, 0)
        qpos = u * TK + lax.broadcasted_iota(jnp.int32, st.shape, 1)
        st = jnp.where(kpos <= qpos, st, NEG)
        m_tile = jnp.max(st, axis=0, keepdims=True)
        yield
        absorb(st, m_tile, vt_ref[:, pl.ds(kd0, nk)], cols)

    for u in range(n_diag // 2):
        _interleave(diagonal_block(v) for v in (u, n_diag - 1 - u))
    o_ref[...] = (acc_ref[...] * (1.0 / l_ref[...])).T.astype(o_ref.dtype)


def _attention(q, k, v, cast_a, cast_b, batch, seq):
    t = q.shape[0]
    nq = seq // TQ
    assert batch * MLA_HEADS * nq >= cast_a.shape[0] * CAST_PARTS
    assert (TQ // TK) % 4 == 0
    step_of = lambda b, h, i: (b * MLA_HEADS + h) * nq + i
    cast_specs = [_cast_spec(cast_a.shape, step_of), _cast_spec(cast_b.shape, step_of)]
    return pl.pallas_call(
        _attn_kernel,
        out_shape=(jax.ShapeDtypeStruct((t, MAIN_WIDTH), BF16), jax.ShapeDtypeStruct(cast_a.shape, BF16),
                   jax.ShapeDtypeStruct(cast_b.shape, BF16)),
        grid=(batch, MLA_HEADS, nq),
        in_specs=[pl.BlockSpec((TQ, HEAD_PAD), lambda b, h, i: (b * nq + i, h)),
                  pl.BlockSpec((seq, HEAD_PAD), lambda b, h, i: (b, h)),
                  pl.BlockSpec((MLA_V, seq), lambda b, h, i: (h, b))] + cast_specs,
        out_specs=[pl.BlockSpec((TQ, MLA_V), lambda b, h, i: (b * nq + i, h))] + cast_specs,
        scratch_shapes=[pltpu.VMEM((1, TQ), F32), pltpu.VMEM((1, TQ), F32), pltpu.VMEM((MLA_V, TQ), F32),
                        pltpu.VMEM((2, TK, TQ), F32), pltpu.VMEM((2, 1, TQ), F32)],
        compiler_params=_cparams(("arbitrary", "arbitrary", "arbitrary")),
        name="mla_attention",
    )(q, k, v, cast_a, cast_b)


def _router_kernel(h_ref, g_ref, whi_ref, wlo_ref,
                   xn_ref, gatest_ref, rank_ref, rankt_ref, starts_ref, counts_ref, run_ref):
    @pl.when(pl.program_id(0) == 0)
    def _():
        run_ref[...] = jnp.zeros_like(run_ref)

    n_chunks = h_ref.shape[0] // TM_PROJ
    r = lax.broadcasted_iota(jnp.int32, (TM_PROJ, TM_PROJ), 0)
    c = lax.broadcasted_iota(jnp.int32, (TM_PROJ, TM_PROJ), 1)
    before = jnp.where(r > c, 1.0, 0.0).astype(BF16)
    lane = lax.broadcasted_iota(jnp.int32, (TM_PROJ, LANE), 1).astype(F32)
    routed, prefix, count = [None] * n_chunks, [None] * n_chunks, [None] * n_chunks

    def chunk_body(ch):
        rows = slice(ch * TM_PROJ, (ch + 1) * TM_PROJ)
        xn = _rms(h_ref[rows, :], g_ref[...])
        hi = xn.astype(BF16)
        lo = (xn - hi.astype(F32)).astype(BF16)
        xn_ref[rows, :] = hi
        yield
        whi = whi_ref[...]
        logits = (jnp.dot(hi, whi, preferred_element_type=F32) + jnp.dot(lo, whi, preferred_element_type=F32)
                  + jnp.dot(hi, wlo_ref[...], preferred_element_type=F32))
        logits = jnp.where(lane < N_EXPERTS, logits, NEG)
        yield
        m1 = jnp.max(logits, axis=-1, keepdims=True)
        yield
        i1 = jnp.min(jnp.where(logits == m1, lane, float(LANE)), axis=-1, keepdims=True)
        yield
        rest = jnp.where(lane == i1, NEG, logits)
        m2 = jnp.max(rest, axis=-1, keepdims=True)
        yield
        i2 = jnp.min(jnp.where(rest == m2, lane, float(LANE)), axis=-1, keepdims=True)
        yield
        e2 = jnp.exp(m2 - m1)
        w1 = 1.0 / (1.0 + e2)
        w2 = e2 * w1
        gates = jnp.where(lane == i1, w1, jnp.where(lane == i2, w2, 0.0))
        gatest_ref[:, rows] = gates.T[:N_EXPERTS, :]
        routed[ch] = (lane == i1) | (lane == i2)
        oh = jnp.where(routed[ch], 1.0, 0.0)
        yield
        prefix[ch] = jnp.dot(before, oh.astype(BF16), preferred_element_type=F32)
        count[ch] = jnp.sum(oh, axis=0, keepdims=True)

    _interleave(chunk_body(ch) for ch in range(n_chunks))
    run = run_ref[0:1, :]
    for ch in range(n_chunks):
        rows = slice(ch * TM_PROJ, (ch + 1) * TM_PROJ)
        starts_ref[8 * ch:8 * ch + 8, :] = jnp.broadcast_to(run, (8, LANE))
        rank = jnp.where(routed[ch], run + prefix[ch], -1.0)
        rank_ref[rows, :] = rank
        rankt_ref[:, rows] = rank.T[:N_EXPERTS, :]
        run = run + count[ch]
    total = jnp.broadcast_to(run, run_ref.shape)
    run_ref[...] = total
    counts_ref[...] = total


def _router(h2d, ln, whi, wlo):
    t = h2d.shape[0]
    tm = min(TM_ROUTER, t)
    nt = t // tm
    chunk_rows = 8 * (tm // TM_PROJ)
    row = lambda n: pl.BlockSpec((tm, n), lambda i: (i, 0))
    return pl.pallas_call(
        _router_kernel,
        out_shape=(jax.ShapeDtypeStruct((t, D_MODEL), BF16), jax.ShapeDtypeStruct((N_EXPERTS, t), F32),
                   jax.ShapeDtypeStruct((t, LANE), F32), jax.ShapeDtypeStruct((N_EXPERTS, t), F32),
                   jax.ShapeDtypeStruct((nt * chunk_rows, LANE), F32), jax.ShapeDtypeStruct((8, LANE), F32)),
        grid=(nt,),
        in_specs=[row(D_MODEL), _const_spec((1, D_MODEL)), _const_spec(whi.shape), _const_spec(wlo.shape)],
        out_specs=(row(D_MODEL), pl.BlockSpec((N_EXPERTS, tm), lambda i: (0, i)), row(LANE),
                   pl.BlockSpec((N_EXPERTS, tm), lambda i: (0, i)),
                   pl.BlockSpec((chunk_rows, LANE), lambda i: (i, 0)), _const_spec((8, LANE))),
        scratch_shapes=[pltpu.VMEM((8, LANE), F32)],
        compiler_params=_cparams(("arbitrary",)),
        name="moe_router",
    )(h2d, ln, whi, wlo)


def _dispatch_kernel(ge_ref, gr0_ref, gstart_ref, gn_ref, xn_ref, rankt_ref, gatest_ref, xs_ref, gs_ref,
                     acc_ref, gacc_ref):
    groups = [pl.program_id(0) * DISPATCH_GROUPS + u for u in range(DISPATCH_GROUPS)]
    eye = (lax.broadcasted_iota(jnp.int32, (SLOT_GROUP, SLOT_GROUP), 0)
           == lax.broadcasted_iota(jnp.int32, (SLOT_GROUP, SLOT_GROUP), 1))

    def gathered(g, start):
        start = pl.multiple_of(start, TM_PROJ)
        want = (gr0_ref[g] + lax.broadcasted_iota(jnp.int32, (SLOT_GROUP, DISPATCH_K), 0)).astype(F32)
        span = (pl.ds(ge_ref[g], 1), pl.ds(start, DISPATCH_K))
        hit = rankt_ref[span] == want
        rows = jnp.dot(jnp.where(hit, 1.0, 0.0).astype(BF16), xn_ref[pl.ds(start, DISPATCH_K), :],
                       preferred_element_type=F32)
        gate = jnp.sum(jnp.where(hit, gatest_ref[span], 0.0), axis=-1, keepdims=True)
        return rows, gate

    def as_row(col):
        return jnp.sum(jnp.where(eye, col, 0.0), axis=0, keepdims=True)

    single = gn_ref[groups[0]] <= 1
    for g in groups[1:]:
        single = single & (gn_ref[g] <= 1)

    @pl.when(single)
    def _():
        gate_rows = [None] * DISPATCH_GROUPS

        def group_body(u, g):
            rows, gate = gathered(g, gstart_ref[g])
            yield
            xs_ref[u * SLOT_GROUP:(u + 1) * SLOT_GROUP, :] = rows.astype(BF16)
            gate_rows[u] = as_row(gate)

        _interleave(group_body(u, g) for u, g in enumerate(groups))
        gs_ref[...] = jnp.broadcast_to(jnp.concatenate(gate_rows, axis=1), gs_ref.shape)

    @pl.when(jnp.logical_not(single))
    def _():
        gate_rows = []
        for u, g in enumerate(groups):
            acc_ref[...] = jnp.zeros_like(acc_ref)
            gacc_ref[...] = jnp.zeros_like(gacc_ref)

            def body(k, carry, g=g):
                rows, gate = gathered(g, gstart_ref[g] + k * DISPATCH_K)
                acc_ref[...] += rows
                gacc_ref[...] += gate
                return carry

            lax.fori_loop(0, gn_ref[g], body, 0)
            xs_ref[u * SLOT_GROUP:(u + 1) * SLOT_GROUP, :] = acc_ref[...].astype(BF16)
            gate_rows.append(as_row(gacc_ref[...]))
        gs_ref[...] = jnp.broadcast_to(jnp.concatenate(gate_rows, axis=1), gs_ref.shape)


def _dispatch(xn, rankt, gatest, g_expert, g_rank0, g_start, g_nspans):
    t = xn.shape[0]
    n_steps = g_expert.shape[0] // DISPATCH_GROUPS
    rows = DISPATCH_GROUPS * SLOT_GROUP
    assert rows == TM_MOE
    res = lambda shape: pl.BlockSpec(shape, lambda g, *_: (0, 0), pipeline_mode=pl.Buffered(1))
    grid_spec = pltpu.PrefetchScalarGridSpec(
        num_scalar_prefetch=4,
        grid=(n_steps,),
        in_specs=[res((t, D_MODEL)), res((N_EXPERTS, t)), res((N_EXPERTS, t))],
        out_specs=(pl.BlockSpec((rows, D_MODEL), lambda g, *_: (g, 0)),
                   pl.BlockSpec((8, rows), lambda g, *_: (g, 0))),
        scratch_shapes=[pltpu.VMEM((SLOT_GROUP, D_MODEL), F32), pltpu.VMEM((SLOT_GROUP, 1), F32)],
    )
    return pl.pallas_call(
        _dispatch_kernel,
        out_shape=(jax.ShapeDtypeStruct((n_steps * rows, D_MODEL), BF16),
                   jax.ShapeDtypeStruct((n_steps * 8, rows), F32)),
        grid_spec=grid_spec,
        compiler_params=_cparams(("arbitrary",)),
        name="moe_dispatch",
    )(g_expert, g_rank0, g_start, g_nspans, xn, rankt, gatest)


def _expert_kernel(te_ref, ta_ref, x_ref, gs_ref, wg_ref, wu_ref, wd_ref, out_ref):
    active = ta_ref[pl.program_id(0)] > 0

    @pl.when(active)
    def _():
        x = x_ref[...]
        acc = None
        for c in range(D_FF // FF_CHUNK):
            cs = slice(c * FF_CHUNK, (c + 1) * FF_CHUNK)
            gt = jnp.dot(x, wg_ref[0, :, cs], preferred_element_type=F32)
            up = jnp.dot(x, wu_ref[0, :, cs], preferred_element_type=F32)
            a = (gt * jax.nn.sigmoid(gt) * up).astype(BF16)
            y = jnp.dot(a, wd_ref[0, cs, :], preferred_element_type=F32)
            acc = y if acc is None else acc + y
        tm = x.shape[0]
        eye = lax.broadcasted_iota(jnp.int32, (tm, tm), 0) == lax.broadcasted_iota(jnp.int32, (tm, tm), 1)
        gate = jnp.sum(jnp.where(eye, gs_ref[0:1, :], 0.0), axis=-1, keepdims=True)
        out_ref[...] = (acc * gate).astype(out_ref.dtype)

    @pl.when(jnp.logical_not(active))
    def _():
        out_ref[...] = jnp.zeros_like(out_ref)


def _experts(x, gate_rows, tile_expert, tile_active, wg, wu, wd):
    n_tiles = tile_expert.shape[0]
    tm = TM_MOE
    wmap = lambda j, te, ta: (te[j], 0, 0)
    tile = lambda j, te, ta: (jnp.where(ta[j] > 0, j, 0), 0)
    grid_spec = pltpu.PrefetchScalarGridSpec(
        num_scalar_prefetch=2,
        grid=(n_tiles,),
        in_specs=[pl.BlockSpec((tm, D_MODEL), tile),
                  pl.BlockSpec((8, tm), tile),
                  pl.BlockSpec((1, D_MODEL, D_FF), wmap),
                  pl.BlockSpec((1, D_MODEL, D_FF), wmap),
                  pl.BlockSpec((1, D_FF, D_MODEL), wmap)],
        out_specs=pl.BlockSpec((tm, D_MODEL), lambda j, te, ta: (j, 0)),
    )
    return pl.pallas_call(
        _expert_kernel,
        out_shape=jax.ShapeDtypeStruct((n_tiles * tm, D_MODEL), BF16),
        grid_spec=grid_spec,
        compiler_params=_cparams(("arbitrary",), EXPERT_VMEM_LIMIT),
        name="moe_experts",
    )(tile_expert, tile_active, x, gate_rows, wg, wu, wd)


def _combine_window_copies(ys_ref, buf_ref, sem_ref, ws_ref, tile, buf_slot, row_offset):
    copies = []
    for e in range(N_EXPERTS):
        start = pl.multiple_of(ws_ref[tile * N_EXPERTS + e] + row_offset, SLOT_ALIGN)
        copies.append(pltpu.make_async_copy(ys_ref.at[pl.ds(start, COMBINE_WIN), :],
                                            buf_ref.at[buf_slot, pl.ds(e * COMBINE_WIN, COMBINE_WIN), :],
                                            sem_ref.at[buf_slot, e]))
    return copies


def _combine_kernel(ws_ref, second_ref, h_ref, rank_ref, base_ref, ys_ref, out_ref, buf_ref, sem_ref):
    i = pl.program_id(0)
    n = pl.num_programs(0)
    tm = h_ref.shape[0]

    ring = COMBINE_AHEAD + 1

    @pl.when(i == 0)
    def _():
        for ahead in range(COMBINE_AHEAD):
            for cp in _combine_window_copies(ys_ref, buf_ref, sem_ref, ws_ref, ahead, ahead, 0):
                cp.start()

    @pl.when(i + COMBINE_AHEAD < n)
    def _():
        tile = i + COMBINE_AHEAD
        for cp in _combine_window_copies(ys_ref, buf_ref, sem_ref, ws_ref, tile, tile % ring, 0):
            cp.start()

    rank = rank_ref[...]
    slots = jnp.where(rank >= 0.0, rank + base_ref[...], -1.0)
    lane_iota = lax.broadcasted_iota(jnp.int32, (tm, COMBINE_WIN), 1)

    def gathered(buf_slot, row_offset):
        parts = []
        for e in range(N_EXPERTS):
            first = ws_ref[i * N_EXPERTS + e] + row_offset
            hit = slots[:, e:e + 1] == (first + lane_iota).astype(F32)
            parts.append(jnp.where(hit, 1.0, 0.0).astype(BF16))
        return jnp.dot(jnp.concatenate(parts, axis=1), buf_ref[buf_slot], preferred_element_type=F32)

    for cp in _combine_window_copies(ys_ref, buf_ref, sem_ref, ws_ref, i, i % ring, 0):
        cp.wait()
    out_ref[...] = h_ref[...] + gathered(i % ring, 0)

    for extra in range(1, COMBINE_MAX_WINDOWS):
        @pl.when(second_ref[i] > extra)
        def _(extra=extra):
            copies = _combine_window_copies(ys_ref, buf_ref, sem_ref, ws_ref, i, ring, extra * COMBINE_WIN)
            for cp in copies:
                cp.start()
            for cp in copies:
                cp.wait()
            out_ref[...] += gathered(ring, extra * COMBINE_WIN)


def _combine(h2d, rank, base_row, ys, win_start, tile_second):
    t = h2d.shape[0]
    tm = TM_PROJ
    row = lambda n: pl.BlockSpec((tm, n), lambda i, *_: (i, 0))
    grid_spec = pltpu.PrefetchScalarGridSpec(
        num_scalar_prefetch=2,
        grid=(t // tm,),
        in_specs=[row(D_MODEL), row(LANE), pl.BlockSpec((1, LANE), lambda i, *_: (0, 0)),
                  pl.BlockSpec(memory_space=pl.ANY)],
        out_specs=row(D_MODEL),
        scratch_shapes=[pltpu.VMEM((COMBINE_AHEAD + 2, N_EXPERTS * COMBINE_WIN, D_MODEL), BF16),
                        pltpu.SemaphoreType.DMA((COMBINE_AHEAD + 2, N_EXPERTS))],
    )
    return pl.pallas_call(
        _combine_kernel,
        out_shape=jax.ShapeDtypeStruct((t, D_MODEL), F32),
        grid_spec=grid_spec,
        compiler_params=_cparams(("arbitrary",)),
        name="moe_combine",
    )(win_start, tile_second, h2d, rank, base_row, ys)


def _moe_plan(counts, starts, t):
    n_tok_tiles = t // TM_PROJ
    n_tiles = 2 * t // TM_MOE + N_EXPERTS + 1
    n_slots = n_tiles * TM_MOE
    counts = counts.astype(jnp.int32)
    cum = jnp.concatenate([starts.astype(jnp.int32), counts[None, :]], axis=0)
    padded = (counts + TM_MOE - 1) // TM_MOE * TM_MOE
    ends = jnp.cumsum(padded)
    base = ends - padded
    total = ends[-1]
    tile_first = jnp.arange(n_tiles, dtype=jnp.int32) * TM_MOE
    tile_expert = jnp.minimum(jnp.sum(ends[None, :] <= tile_first[:, None], axis=1), N_EXPERTS - 1).astype(jnp.int32)
    tile_active = (tile_first < total).astype(jnp.int32)
    n_groups = n_slots // SLOT_GROUP
    g_first = jnp.arange(n_groups, dtype=jnp.int32) * SLOT_GROUP
    g_expert = tile_expert[g_first // TM_MOE]
    g_r0 = g_first - base[g_expert]
    cum_e = cum.T[g_expert]
    lo = jnp.sum(cum_e <= g_r0[:, None], axis=1) - 1
    hi = jnp.sum(cum_e < (g_r0 + SLOT_GROUP)[:, None], axis=1)
    g_valid = (g_first < total) & (g_r0 < counts[g_expert])
    lo = jnp.clip(lo, 0, n_tok_tiles - 1)
    hi = jnp.clip(hi, 0, n_tok_tiles)
    g_n = jnp.where(g_valid, (jnp.maximum(hi - lo, 0) * TM_PROJ + DISPATCH_K - 1) // DISPATCH_K, 0)
    lo = jnp.minimum(lo * TM_PROJ, t - jnp.maximum(g_n, 1) * DISPATCH_K)
    first = base[None, :] + cum[:-1]
    cnt = cum[1:] - cum[:-1]
    ws = first // SLOT_ALIGN * SLOT_ALIGN
    pieces = jnp.where(cnt > 0, (first + cnt - ws + COMBINE_WIN - 1) // COMBINE_WIN, 0)
    return dict(tile_expert=tile_expert, tile_active=tile_active, g_expert=g_expert, g_r0=g_r0.astype(jnp.int32),
                g_lo=lo.astype(jnp.int32), g_n=g_n.astype(jnp.int32), base=base,
                ws=ws.reshape(-1).astype(jnp.int32), second=jnp.max(pieces, axis=1).astype(jnp.int32))


def _pad_heads_cols(w, heads, dim, pad):
    lead = w.shape[:-1]
    w = w.reshape(lead + (heads, dim))
    w = jnp.pad(w, [(0, 0)] * len(lead) + [(0, 0), (0, pad - dim)])
    return w.reshape(lead + (heads * pad,))


def _row(v):
    return v.reshape(1, -1).astype(F32)


def kernel(x, mem, positions, mem_norm, w_mem_kv, ln_mix0, w_in0, conv_w0, b_i0, b_f0, h_norm0, mq_norm0, mk_norm0, w_out0, ln_ffn0, w_gate0, w_up0, w_down0, ln_mix1, w_in1, cq_norm1, ckv_norm1, w_uq1, w_ukv1, q_norm1, k_norm1, mq_norm1, mk_norm1, w_out1, ln_ffn1, w_router1, we_gate1, we_up1, we_down1):
    batch, seq, _ = x.shape
    t = batch * seq
    h = x.reshape(t, D_MODEL)
    tile4 = lambda g: _row(jnp.tile(g, MEM_HEADS))

    k4_0, k4_1, v4 = _memkv(mem.reshape(batch * MEM_TOKENS, D_MODEL), _row(mem_norm), w_mem_kv.astype(BF16),
                            tile4(mk_norm0), tile4(mk_norm1), batch)

    mw = MAIN_WIDTH
    w_q, w_k, w_v, w_o = (w_in0[:, i * mw:(i + 1) * mw] for i in range(4))
    w_gates = w_in0[:, 4 * mw:4 * mw + 2 * ML_HEADS]
    w_mq = w_in0[:, 4 * mw + 2 * ML_HEADS:]
    padh = lambda w: _pad_heads_cols(w, ML_HEADS, ML_HEAD_DIM, HEAD_PAD)
    wqk = jnp.concatenate([padh(w_q), padh(w_k)], axis=1).astype(BF16)
    wvo = jnp.concatenate([padh(w_v), padh(w_o)], axis=1).astype(BF16)
    wg = jnp.pad(w_gates, ((0, 0), (0, GATE_PAD - 2 * ML_HEADS))).astype(BF16)
    wgt = w_gates.T.astype(BF16)
    cw = jnp.concatenate([padh(conv_w0[:, :mw]), padh(conv_w0[:, mw:])], axis=1).astype(F32)
    bias = jnp.concatenate([b_i0, b_f0]).astype(F32)
    bias_row = jnp.pad(bias, (0, GATE_PAD - 2 * ML_HEADS)).reshape(1, GATE_PAD)
    bias_col = bias.reshape(2 * ML_HEADS, 1)
    q0, kt0, v0, so, mq0, gates, gatest = _in0(h, _row(ln_mix0), wqk, wvo, w_mq.astype(BF16), wg, wgt, cw,
                                               bias_row, bias_col, seq)
    hgain = jnp.pad(h_norm0.astype(F32), (0, HEAD_PAD - ML_HEAD_DIM)).reshape(1, HEAD_PAD)
    ffn_groups = lambda w: w.reshape(N_EXPERTS, w.shape[0] // N_EXPERTS, w.shape[1])
    hn, (wg0, wu0, wd0) = _mlstm(q0, kt0, v0, so, gates, gatest, hgain,
                                 [ffn_groups(w_gate0), ffn_groups(w_up0), ffn_groups(w_down0)], batch, seq)
    wm0 = jnp.pad(w_out0[:mw].reshape(ML_HEADS, ML_HEAD_DIM, D_MODEL),
                  ((0, 0), (0, HEAD_PAD - ML_HEAD_DIM), (0, 0))).reshape(ML_PAD, D_MODEL).astype(BF16)
    h, we_gate_bf = _out_proj(hn, mq0, h, k4_0, v4, tile4(mq_norm0), wm0, w_out0[mw:].astype(BF16), seq,
                              ffn=(_row(ln_ffn0), wg0.reshape(w_gate0.shape), wu0.reshape(w_up0.shape),
                                   wd0.reshape(w_down0.shape), we_gate1))

    w_cq = w_in1[:, :Q_LORA]
    w_ckv = w_in1[:, Q_LORA:Q_LORA + KV_LORA]
    w_kr = jnp.pad(w_in1[:, Q_LORA + KV_LORA:Q_LORA + KV_LORA + MLA_ROPE], ((0, 0), (0, LANE - MLA_ROPE)))
    w_mq1 = w_in1[:, Q_LORA + KV_LORA + MLA_ROPE:]
    w1 = jnp.concatenate([w_cq, w_ckv, w_kr, w_mq1], axis=1).astype(BF16)
    wuq = _pad_heads_cols(w_uq1, MLA_HEADS, MLA_QK, HEAD_PAD).astype(BF16)
    wukv = w_ukv1.reshape(KV_LORA, MLA_HEADS, MLA_NOPE + MLA_V)
    wuk = wukv[:, :, :MLA_NOPE].reshape(KV_LORA, MLA_HEADS * MLA_NOPE).astype(BF16)
    wuvt = wukv[:, :, MLA_NOPE:].reshape(KV_LORA, MLA_HEADS * MLA_V).T.astype(BF16)
    gq = jnp.pad(q_norm1.astype(F32), (0, HEAD_PAD - MLA_QK)).reshape(1, HEAD_PAD)
    gkn = _row(k_norm1[:MLA_NOPE])
    gkr = jnp.pad(k_norm1[MLA_NOPE:].astype(F32), (0, LANE - MLA_ROPE)).reshape(1, LANE)
    half = MLA_ROPE // 2
    inv_freq = ROPE_THETA ** (-jnp.arange(half, dtype=F32) / half)
    invf = jnp.concatenate([inv_freq, inv_freq, jnp.zeros((LANE - MLA_ROPE,), F32)]).reshape(1, LANE)
    q1, k1, v1, mq1 = _in1(h, positions.reshape(1, t).astype(jnp.int32), _row(ln_mix1), w1, _row(cq_norm1),
                           _row(ckv_norm1), wuq, wuk, wuvt, gq, gkn, gkr, invf)
    o1, we_up_bf, we_down_bf = _attention(q1, k1, v1, we_up1, we_down1, batch, seq)
    h = _out_proj(o1, mq1, h, k4_1, v4, tile4(mq_norm1), w_out1[:mw].astype(BF16), w_out1[mw:].astype(BF16), seq)

    wr = jnp.pad(w_router1.astype(F32), ((0, 0), (0, LANE - N_EXPERTS)))
    wr_hi = wr.astype(BF16)
    wr_lo = (wr - wr_hi.astype(F32)).astype(BF16)
    xn, gatest, rank, rankt, starts, counts = _router(h, _row(ln_ffn1), wr_hi, wr_lo)
    plan = _moe_plan(counts[0, :N_EXPERTS], starts[::8, :N_EXPERTS], t)
    xs, gate_rows = _dispatch(xn, rankt, gatest, plan["g_expert"], plan["g_r0"], plan["g_lo"], plan["g_n"])
    ys = _experts(xs, gate_rows, plan["tile_expert"], plan["tile_active"], we_gate_bf, we_up_bf, we_down_bf)
    base_row = jnp.pad(plan["base"].astype(F32), (0, LANE - N_EXPERTS)).reshape(1, LANE)
    h = _combine(h, rank, base_row, ys, plan["ws"], plan["second"])
    return h.reshape(batch, seq, D_MODEL)
```
